```python
import math
import jax
import jax.numpy as jnp
from jax import lax
import numpy as np

D_MODEL = 1024
BATCH = 1
SEQ = 16384
DEPTH = 2

PLE_DIM = 256
ROPE_THETA = 10000.0
Q_BLOCK = 128
EPS = 1e-6
FOX_HEADS = 4
FOX_HD = 64
NSA_HEADS = 8
NSA_KV = 2
NSA_HD = 64
CMP_LEN = 32
CMP_STRIDE = 16
SLC_LEN = 64
SLC_TOPK = 16
WIN = 512
FORCE_BONUS = 1.0e4
DIFF_HEADS = 4
DIFF_QK = 32
DIFF_V = 64
PEER_HEADS = 8
PEER_NKEYS = 128
PEER_EXPERTS = PEER_NKEYS * PEER_NKEYS
PEER_DQ = 256
PEER_TOPK = 16
PEER_TOK_BLOCK = 128

IN_WIDTHS = (FOX_HEADS * FOX_HD, FOX_HEADS * FOX_HD, FOX_HEADS * FOX_HD, FOX_HEADS,
             NSA_HEADS * NSA_HD, NSA_KV * NSA_HD, NSA_KV * NSA_HD, NSA_KV * NSA_HD,
             NSA_KV * NSA_HD, NSA_KV * NSA_HD, NSA_KV * NSA_HD, 3 * NSA_HEADS,
             DIFF_HEADS * 2 * DIFF_QK, DIFF_HEADS * 2 * DIFF_QK, DIFF_HEADS * DIFF_V)
IN_WIDTH = sum(IN_WIDTHS)
MIX_WIDTH = FOX_HEADS * FOX_HD + NSA_HEADS * NSA_HD + DIFF_HEADS * DIFF_V

kernel_name = 'hymba_fox_nsa_diff_peer_trunk'


def rmsnorm(x, w):
    xf = x.astype(jnp.float32)
    y = xf * lax.rsqrt(jnp.mean(xf * xf, axis=-1, keepdims=True) + EPS)
    return (y * w.astype(jnp.float32)).astype(x.dtype)


def rope(x, pos):
    half = x.shape[-1] // 2
    inv_freq = ROPE_THETA ** (-jnp.arange(half, dtype=jnp.float32) / half)
    ang = pos.astype(jnp.float32)[..., None] * inv_freq
    cos = jnp.cos(ang)[:, :, None, :]
    sin = jnp.sin(ang)[:, :, None, :]
    xf = x.astype(jnp.float32)
    x1, x2 = xf[..., :half], xf[..., half:]
    return jnp.concatenate([x1 * cos - x2 * sin, x2 * cos + x1 * sin], axis=-1).astype(x.dtype)


def masked_softmax(s, mask):
    s = jnp.where(mask, s, -jnp.inf)
    m = jnp.max(s, axis=-1, keepdims=True)
    m = jnp.where(jnp.isfinite(m), m, 0.0)
    e = jnp.exp(s - m)
    den = jnp.sum(e, axis=-1, keepdims=True)
    return e / jnp.where(den > 0, den, 1.0)


def to_blocks(a, blk):
    b, s = a.shape[0], a.shape[1]
    return jnp.moveaxis(a.reshape((b, s // blk, blk) + a.shape[2:]), 1, 0)


def from_blocks(a):
    a = jnp.moveaxis(a, 0, 1)
    return a.reshape((a.shape[0], a.shape[1] * a.shape[2]) + a.shape[3:])


def split_proj(proj):
    idx = np.cumsum(np.array(IN_WIDTHS))[:-1].tolist()
    return jnp.split(proj, idx, axis=-1)


def fox_attention(q, k, v, f_logit):
    b, s, nh, hd = q.shape
    scale = hd ** -0.5
    c = jnp.cumsum(jax.nn.log_sigmoid(f_logit.astype(jnp.float32)), axis=1)
    c_keys = jnp.moveaxis(c, 1, 2)
    kpos = jnp.arange(s)

    def block(args):
        qb, cb, bi = args
        qpos = bi * Q_BLOCK + jnp.arange(Q_BLOCK)
        sc = jnp.einsum('bqhd,bkhd->bhqk', qb, k).astype(jnp.float32) * scale
        sc = sc + jnp.moveaxis(cb, 1, 2)[..., None] - c_keys[:, :, None, :]
        pr = masked_softmax(sc, kpos[None, :] <= qpos[:, None])
        return jnp.einsum('bhqk,bkhd->bqhd', pr.astype(v.dtype), v)

    out = lax.map(block, (to_blocks(q, Q_BLOCK), to_blocks(c, Q_BLOCK), jnp.arange(s // Q_BLOCK)))
    return from_blocks(out)


def nsa_compress(kv, pos_emb, w):
    b, s, g, hd = kv.shape
    n_cmp = (s - CMP_LEN) // CMP_STRIDE + 1
    idx = (jnp.arange(n_cmp) * CMP_STRIDE)[:, None] + jnp.arange(CMP_LEN)[None, :]
    blk = kv[:, idx] + pos_emb[:, None, :]
    blk = jnp.moveaxis(blk, 3, 2).reshape(b, n_cmp, g, CMP_LEN * hd)
    return blk @ w


def nsa_attention(q, kc, vc, ks, vs, kw, vw, gates, positions, q_norm_w, k_norm_w, cmp_pos, cmp_w):
    b, s, hq, hd = q.shape
    g = NSA_KV
    hpg = hq // g
    scale = hd ** -0.5
    qn = rmsnorm(q, q_norm_w)
    qr = rope(qn, positions)
    kc_c = rmsnorm(nsa_compress(kc, cmp_pos[0], cmp_w[0]), k_norm_w)
    vc_c = nsa_compress(vc, cmp_pos[1], cmp_w[1])
    ks_r = rope(rmsnorm(ks, k_norm_w), positions)
    kw_r = rope(rmsnorm(kw, k_norm_w), positions)
    n_cmp = kc_c.shape[1]
    ns = s // SLC_LEN
    n_sel = min(SLC_TOPK, ns)
    cmp_end = jnp.arange(n_cmp) * CMP_STRIDE + CMP_LEN - 1
    ci = jnp.arange(n_cmp)[:, None] * CMP_STRIDE
    sj = jnp.arange(ns)[None, :] * SLC_LEN
    overlap = ((ci < sj + SLC_LEN) & (ci + CMP_LEN > sj)).astype(jnp.float32)
    ks_g = jnp.moveaxis(ks_r.reshape(b, ns, SLC_LEN, g, hd), 3, 1)
    vs_g = jnp.moveaxis(vs.reshape(b, ns, SLC_LEN, g, hd), 3, 1)
    kw_p = jnp.pad(kw_r, ((0, 0), (WIN, 0), (0, 0), (0, 0)))
    vw_p = jnp.pad(vw, ((0, 0), (WIN, 0), (0, 0), (0, 0)))
    b_ix = jnp.arange(b)[:, None, None, None]
    g_ix = jnp.arange(g)[None, :, None, None]
    blk_ids = jnp.arange(ns)

    def block(args):
        qn_b, qr_b, g_b, bi = args
        qpos = bi * Q_BLOCK + jnp.arange(Q_BLOCK)
        qn_b = qn_b.reshape(b, Q_BLOCK, g, hpg, hd)
        qr_b = qr_b.reshape(b, Q_BLOCK, g, hpg, hd)
        s_c = jnp.einsum('bqghd,bngd->bghqn', qn_b, kc_c).astype(jnp.float32) * scale
        p_c = masked_softmax(s_c, cmp_end[None, :] <= qpos[:, None])
        o_c = jnp.einsum('bghqn,bngd->bqghd', p_c.astype(vc_c.dtype), vc_c)
        imp = jnp.einsum('bghqn,nm->bgqm', p_c, overlap)
        cur = qpos // SLC_LEN
        valid = blk_ids[None, :] * SLC_LEN <= qpos[:, None]
        forced = (blk_ids[None, :] == 0) | (blk_ids[None, :] == cur[:, None]) | (blk_ids[None, :] == cur[:, None] - 1)
        score = jnp.where(valid, imp + FORCE_BONUS * forced.astype(jnp.float32), -jnp.inf)
        _, sel = lax.top_k(score, n_sel)
        k_sel = ks_g[b_ix, g_ix, sel].reshape(b, g, Q_BLOCK, n_sel * SLC_LEN, hd)
        v_sel = vs_g[b_ix, g_ix, sel].reshape(b, g, Q_BLOCK, n_sel * SLC_LEN, hd)
        kpos_s = (sel[..., None] * SLC_LEN + jnp.arange(SLC_LEN)).reshape(b, g, Q_BLOCK, n_sel * SLC_LEN)
        mask_s = kpos_s <= qpos[None, None, :, None]
        s_s = jnp.einsum('bqghd,bgqkd->bghqk', qr_b, k_sel).astype(jnp.float32) * scale
        p_s = masked_softmax(s_s, mask_s[:, :, None])
        o_s = jnp.einsum('bghqk,bgqkd->bqghd', p_s.astype(v_sel.dtype), v_sel)
        k_w = lax.dynamic_slice_in_dim(kw_p, bi * Q_BLOCK, WIN + Q_BLOCK, axis=1)
        v_w = lax.dynamic_slice_in_dim(vw_p, bi * Q_BLOCK, WIN + Q_BLOCK, axis=1)
        kpos_w = bi * Q_BLOCK - WIN + jnp.arange(WIN + Q_BLOCK)
        dlt = qpos[:, None] - kpos_w[None, :]
        mask_w = (dlt >= 0) & (dlt < WIN) & (kpos_w[None, :] >= 0)
        s_w = jnp.einsum('bqghd,bkgd->bghqk', qr_b, k_w).astype(jnp.float32) * scale
        p_w = masked_softmax(s_w, mask_w)
        o_w = jnp.einsum('bghqk,bkgd->bqghd', p_w.astype(v_w.dtype), v_w)
        gt = jax.nn.sigmoid(g_b.astype(jnp.float32)).reshape(b, Q_BLOCK, g, hpg, 3).astype(o_c.dtype)
        out = gt[..., 0:1] * o_c + gt[..., 1:2] * o_s + gt[..., 2:3] * o_w
        return out.reshape(b, Q_BLOCK, hq, hd)

    out = lax.map(block, (to_blocks(qn, Q_BLOCK), to_blocks(qr, Q_BLOCK), to_blocks(gates, Q_BLOCK),
                          jnp.arange(s // Q_BLOCK)))
    return from_blocks(out)


def diff_lambda_init(layer):
    return 0.8 - 0.6 * math.exp(-0.3 * layer)


def diff_attention(q, k, v, positions, q_norm_w, k_norm_w, lam_vecs, subln_w, lambda_init):
    b, s = q.shape[0], q.shape[1]
    scale = DIFF_QK ** -0.5
    q = rope(rmsnorm(q.reshape(b, s, DIFF_HEADS * 2, DIFF_QK), q_norm_w), positions)
    k = rope(rmsnorm(k.reshape(b, s, DIFF_HEADS * 2, DIFF_QK), k_norm_w), positions)
    q = q.reshape(b, s, DIFF_HEADS, 2, DIFF_QK)
    k = k.reshape(b, s, DIFF_HEADS, 2, DIFF_QK)
    v = v.reshape(b, s, DIFF_HEADS, DIFF_V)
    lv = lam_vecs.astype(jnp.float32)
    lam = jnp.exp(jnp.sum(lv[0] * lv[1])) - jnp.exp(jnp.sum(lv[2] * lv[3])) + lambda_init
    kpos = jnp.arange(s)

    def block(args):
        qb, bi = args
        qpos = bi * Q_BLOCK + jnp.arange(Q_BLOCK)
        sc = jnp.einsum('bqhcd,bkhcd->bhcqk', qb, k).astype(jnp.float32) * scale
        pr = masked_softmax(sc, kpos[None, :] <= qpos[:, None])
        a = pr[:, :, 0] - lam * pr[:, :, 1]
        return jnp.einsum('bhqk,bkhd->bqhd', a.astype(v.dtype), v)

    o = from_blocks(lax.map(block, (to_blocks(q, Q_BLOCK), jnp.arange(s // Q_BLOCK))))
    return rmsnorm(o, subln_w) * (1.0 - lambda_init)


def peer_ffn(x, w_q, sub_keys, u, v):
    b, s, _ = x.shape
    q = (x @ w_q).reshape(b, s, PEER_HEADS, 2, PEER_DQ // 2)
    sc = jnp.einsum('bshcd,hcnd->bshcn', q, sub_keys).astype(jnp.float32)
    s1, i1 = lax.top_k(sc[..., 0, :], PEER_TOPK)
    s2, i2 = lax.top_k(sc[..., 1, :], PEER_TOPK)
    cand = (s1[..., :, None] + s2[..., None, :]).reshape(b, s, PEER_HEADS, PEER_TOPK * PEER_TOPK)
    cidx = (i1[..., :, None] * PEER_NKEYS + i2[..., None, :]).reshape(b, s, PEER_HEADS, PEER_TOPK * PEER_TOPK)
    top_s, pos = lax.top_k(cand, PEER_TOPK)
    eidx = jnp.take_along_axis(cidx, pos, axis=-1)
    gate = jax.nn.softmax(top_s, axis=-1)

    def block(args):
        xb, eb, gb = args
        hid = jax.nn.gelu(jnp.einsum('btd,bthkd->bthk', xb, u[eb]).astype(jnp.float32), approximate=False)
        w = (gb * hid).astype(x.dtype)
        return jnp.einsum('bthk,bthkd->btd', w, v[eb])

    out = lax.map(block, (to_blocks(x, PEER_TOK_BLOCK), to_blocks(eidx, PEER_TOK_BLOCK), to_blocks(gate, PEER_TOK_BLOCK)))
    return from_blocks(out)


def setup_inputs(seed: int = 0) -> dict:
    key = jax.random.key(seed)
    ks = iter(jax.random.split(key, 32))
    L = DEPTH
    D = D_MODEL

    def nrm(shape, scale):
        return jax.random.normal(next(ks), shape, jnp.float32) * scale

    return {
        'x': nrm((BATCH, SEQ, D), 1.0),
        'p': nrm((L, BATCH, SEQ, PLE_DIM), 1.0),
        'positions': jnp.broadcast_to(jnp.arange(SEQ, dtype=jnp.int32), (BATCH, SEQ)),
        'attn_norm_w': 1.0 + nrm((L, D), 0.02),
        'w_in': nrm((L, D, IN_WIDTH), D ** -0.5),
        'fox_f_bias': jnp.linspace(1.0, 6.0, FOX_HEADS, dtype=jnp.float32)[None, :] + nrm((L, FOX_HEADS), 0.1),
        'fox_q_norm_w': 1.0 + nrm((L, FOX_HD), 0.02),
        'fox_k_norm_w': 1.0 + nrm((L, FOX_HD), 0.02),
        'nsa_q_norm_w': 1.0 + nrm((L, NSA_HD), 0.02),
        'nsa_k_norm_w': 1.0 + nrm((L, NSA_HD), 0.02),
        'nsa_cmp_pos': nrm((L, 2, CMP_LEN, NSA_HD), 0.1),
        'nsa_cmp_w': nrm((L, 2, CMP_LEN * NSA_HD, NSA_HD), (CMP_LEN * NSA_HD) ** -0.5),
        'diff_q_norm_w': 1.0 + nrm((L, DIFF_QK), 0.02),
        'diff_k_norm_w': 1.0 + nrm((L, DIFF_QK), 0.02),
        'diff_lambda': nrm((L, 4, DIFF_QK), 0.1),
        'diff_subln_w': 1.0 + nrm((L, DIFF_V), 0.02),
        'w_out': nrm((L, MIX_WIDTH, D), MIX_WIDTH ** -0.5),
        'ffn_norm_w': 1.0 + nrm((L, D), 0.02),
        'peer_w_q': nrm((L, D, PEER_HEADS * PEER_DQ), D ** -0.5),
        'peer_sub_keys': nrm((L, PEER_HEADS, 2, PEER_NKEYS, PEER_DQ // 2), (PEER_DQ // 2) ** -0.5),
        'peer_u': nrm((L, PEER_EXPERTS, D), D ** -0.5),
        'peer_v': nrm((L, PEER_EXPERTS, D), PEER_TOPK ** -0.5),
        'ple_norm_w': 1.0 + nrm((L, D), 0.02),
        'ple_w_gate': nrm((L, D, D), D ** -0.5),
        'ple_w_proj': nrm((L, PLE_DIM, D), PLE_DIM ** -0.5),
    }


def reference(x, p, positions, attn_norm_w, w_in, fox_f_bias, fox_q_norm_w, fox_k_norm_w,
              nsa_q_norm_w, nsa_k_norm_w, nsa_cmp_pos, nsa_cmp_w, diff_q_norm_w, diff_k_norm_w,
              diff_lambda, diff_subln_w, w_out, ffn_norm_w, peer_w_q, peer_sub_keys, peer_u, peer_v,
              ple_norm_w, ple_w_gate, ple_w_proj):
    b, s, _ = x.shape
    h = x
    for i in range(DEPTH):
        a = rmsnorm(h, attn_norm_w[i])
        proj = a @ w_in[i]
        (fq, fk, fv, ff, nq, nkc, nvc, nks, nvs, nkw, nvw, ng, dq, dk, dv) = split_proj(proj)
        fq = rmsnorm(fq.reshape(b, s, FOX_HEADS, FOX_HD), fox_q_norm_w[i])
        fk = rmsnorm(fk.reshape(b, s, FOX_HEADS, FOX_HD), fox_k_norm_w[i])
        fv = fv.reshape(b, s, FOX_HEADS, FOX_HD)
        o_fox = fox_attention(fq, fk, fv, ff + fox_f_bias[i])
        kv_shape = (b, s, NSA_KV, NSA_HD)
        o_nsa = nsa_attention(nq.reshape(b, s, NSA_HEADS, NSA_HD), nkc.reshape(kv_shape), nvc.reshape(kv_shape),
                              nks.reshape(kv_shape), nvs.reshape(kv_shape), nkw.reshape(kv_shape), nvw.reshape(kv_shape),
                              ng.reshape(b, s, NSA_HEADS, 3), positions, nsa_q_norm_w[i], nsa_k_norm_w[i],
                              nsa_cmp_pos[i], nsa_cmp_w[i])
        o_diff = diff_attention(dq, dk, dv, positions, diff_q_norm_w[i], diff_k_norm_w[i], diff_lambda[i],
                                diff_subln_w[i], diff_lambda_init(i))
        mix = jnp.concatenate([o_fox.reshape(b, s, -1), o_nsa.reshape(b, s, -1), o_diff.reshape(b, s, -1)], axis=-1)
        h = h + mix @ w_out[i]
        h = h + peer_ffn(rmsnorm(h, ffn_norm_w[i]), peer_w_q[i], peer_sub_keys[i], peer_u[i], peer_v[i])
        gate = jax.nn.sigmoid((rmsnorm(h, ple_norm_w[i]) @ ple_w_gate[i]).astype(jnp.float32)).astype(h.dtype)
        h = h + gate * (p[i] @ ple_w_proj[i])
    return h
```

```python
import functools
import math

import numpy as np
import jax
import jax.numpy as jnp
from jax import lax
from jax.experimental import pallas as pl
from jax.experimental.pallas import tpu as pltpu

F32 = jnp.float32
BF16 = jnp.bfloat16
HIGHEST = lax.Precision.HIGHEST

D_MODEL = 1024
PLE_DIM = 256
ROPE_THETA = 10000.0
EPS = 1e-6
FOX_HEADS, FOX_HD = 4, 64
NSA_HEADS, NSA_KV, NSA_HD = 8, 2, 64
NSA_G = NSA_HEADS // NSA_KV
CMP_LEN, CMP_STRIDE, SLC_LEN, SLC_TOPK, WIN = 32, 16, 64, 16, 512
FORCE_BONUS = 1.0e4
DIFF_HEADS, DIFF_QK, DIFF_V = 4, 32, 64
PEER_HEADS, PEER_NKEYS, PEER_DQ, PEER_TOPK = 8, 128, 256, 16
PEER_EXPERTS = PEER_NKEYS * PEER_NKEYS

LANES = 128
NEG_INIT = -1.0e30
MASKVAL = -2.0e30
VMEM_LIMIT = 56 * 1024 * 1024

_SEG = dict(fq=(0, 256), fk=(256, 256), fv=(512, 256), nq=(768, 512), nkc=(1280, 128), nvc=(1408, 128),
            nks=(1536, 128), nvs=(1664, 128), nkw=(1792, 128), nvw=(1920, 128),
            dq=(2048, 256), dk=(2304, 256), dv=(2560, 256), misc=(2816, 128))
PROJ_W = 2944


def _cparams(sem):
    return pltpu.CompilerParams(dimension_semantics=sem, vmem_limit_bytes=VMEM_LIMIT)


def _full(shape):
    n = len(shape)
    return pl.BlockSpec(shape, lambda *_: (0,) * n)


def _rope_tab_kernel(pos_ref, f64_ref, g64_ref, f32_ref, g32_ref, c64_o, s64_o, c32_o, s32_o):
    pos = pos_ref[...].astype(F32)
    a64 = pos * f64_ref[...]
    c64_o[...] = jnp.cos(a64)
    s64_o[...] = jnp.sin(a64) * g64_ref[...]
    a32 = pos * f32_ref[...]
    c32_o[...] = jnp.cos(a32)
    s32_o[...] = jnp.sin(a32) * g32_ref[...]


def _rope_tables(positions):
    s = positions.shape[-1]
    pos = positions.reshape(s, 1)
    lane = np.arange(LANES)

    def lanes(half):
        inv = ROPE_THETA ** (-jnp.arange(half, dtype=F32) / half)
        freq = inv[(lane % (2 * half)) % half].reshape(1, LANES)
        sign = np.where((lane % (2 * half)) < half, -1.0, 1.0).astype(np.float32).reshape(1, LANES)
        return freq, jnp.asarray(sign)

    f64, g64 = lanes(NSA_HD // 2)
    f32_, g32 = lanes(DIFF_QK // 2)
    tm = 512
    out = jax.ShapeDtypeStruct((s, LANES), F32)
    row = pl.BlockSpec((tm, LANES), lambda i: (i, 0))
    return pl.pallas_call(
        _rope_tab_kernel,
        grid=(s // tm,),
        in_specs=[pl.BlockSpec((tm, 1), lambda i: (i, 0))] + [_full((1, LANES))] * 4,
        out_specs=[row] * 4,
        out_shape=[out] * 4,
        compiler_params=_cparams(("parallel",)),
        name="rope_tables",
    )(pos, f64, g64, f32_, g32)


def _rms(x, w):
    return x * lax.rsqrt(jnp.mean(x * x, axis=-1, keepdims=True) + EPS) * w


def _norm_matmul_kernel(x_ref, nw_ref, w_ref, o_ref):
    xn = _rms(x_ref[...], nw_ref[...])
    o_ref[...] = jnp.dot(xn.astype(BF16), w_ref[...], preferred_element_type=F32)


def _norm_matmul(x, nw, w_bf16, tm=512):
    s, d = x.shape
    n = w_bf16.shape[1]
    return pl.pallas_call(
        _norm_matmul_kernel,
        grid=(s // tm,),
        in_specs=[pl.BlockSpec((tm, d), lambda i: (i, 0)), _full((1, d)), _full((d, n))],
        out_specs=pl.BlockSpec((tm, n), lambda i: (i, 0)),
        out_shape=jax.ShapeDtypeStruct((s, n), F32),
        compiler_params=_cparams(("parallel",)),
        name="norm_matmul",
    )(x, nw.reshape(1, d), w_bf16)


def _seg_rms(x, bmat, seg):
    outs = []
    for c0 in range(0, x.shape[1], 256):
        w = min(256, x.shape[1] - c0)
        xc = x[:, c0:c0 + w]
        ss = jnp.dot(xc * xc, bmat[:w, :w], precision=HIGHEST, preferred_element_type=F32)
        outs.append(xc * lax.rsqrt(ss * (1.0 / seg) + EPS))
    return outs[0] if len(outs) == 1 else jnp.concatenate(outs, axis=1)


def _tile_lanes(t, width):
    reps = width // LANES
    return t if reps == 1 else jnp.concatenate([t] * reps, axis=1)


def _rope(x, cos, sin, half):
    width = x.shape[1]
    left = pltpu.roll(x, width - half, 1)
    right = pltpu.roll(x, half, 1)
    lane = lax.broadcasted_iota(jnp.int32, x.shape, 1)
    swapped = jnp.where((lane & (2 * half - 1)) < half, left, right)
    return x * _tile_lanes(cos, width) + swapped * _tile_lanes(sin, width)


def _prep_kernel(proj_ref, c64_ref, s64_ref, c32_ref, s32_ref, fb_ref, wfq_ref, wfk_ref, wnq_ref, wnk_ref,
                 wdq_ref, wdk_ref, b64_ref, b32_ref,
                 fq_o, fk_o, fv_o, chi_o, cmid_o, clo_o, nqn_o, nqr_o, kvc_o, ksr_o, vs_o, kwr_o, vw_o,
                 gate_o, dq_o, dk_o, dv_o, carry_sc, *, tm):
    def seg(name):
        c0, w = _SEG[name]
        return proj_ref[:, c0:c0 + w]

    b64 = b64_ref[...]
    b32 = b32_ref[...]
    c64, s64, c32, s32 = c64_ref[...], s64_ref[...], c32_ref[...], s32_ref[...]

    fq_o[...] = (_seg_rms(seg("fq"), b64, FOX_HD) * wfq_ref[...] * (FOX_HD ** -0.5)).astype(BF16)
    fk_o[...] = (_seg_rms(seg("fk"), b64, FOX_HD) * wfk_ref[...]).astype(BF16)
    fv_o[...] = seg("fv").astype(BF16)

    misc = seg("misc")
    gate_o[...] = 1.0 / (1.0 + jnp.exp(-misc))
    t = misc + fb_ref[...]
    logf = jnp.minimum(t, 0.0) - jnp.log1p(jnp.exp(-jnp.abs(t)))

    @pl.when(pl.program_id(0) == 0)
    def _():
        carry_sc[...] = jnp.zeros_like(carry_sc)

    r = lax.broadcasted_iota(jnp.int32, (tm, tm), 0)
    c = lax.broadcasted_iota(jnp.int32, (tm, tm), 1)
    tri = (c <= r).astype(F32)
    csum = jnp.dot(tri, logf, precision=HIGHEST, preferred_element_type=F32) + carry_sc[0:1, :]
    carry_sc[...] = jnp.broadcast_to(csum[tm - 1:tm, :], carry_sc.shape)
    hi = csum.astype(BF16)
    r1 = csum - hi.astype(F32)
    mid = r1.astype(BF16)
    lo = (r1 - mid.astype(F32)).astype(BF16)
    chi_o[...] = hi
    cmid_o[...] = mid
    clo_o[...] = lo

    nqn = _seg_rms(seg("nq"), b64, NSA_HD) * wnq_ref[...]
    nqn_o[...] = (nqn * (NSA_HD ** -0.5)).astype(BF16)
    nqr_o[...] = (_rope(nqn, c64, s64, NSA_HD // 2) * (NSA_HD ** -0.5)).astype(BF16)
    kvc_o[:, 0:128] = seg("nkc").astype(BF16)
    kvc_o[:, 128:256] = seg("nvc").astype(BF16)
    wnk = wnk_ref[...]
    ksr_o[...] = _rope(_seg_rms(seg("nks"), b64, NSA_HD) * wnk, c64, s64, NSA_HD // 2).astype(BF16)
    vs_o[...] = seg("nvs").astype(BF16)
    kwr_o[...] = _rope(_seg_rms(seg("nkw"), b64, NSA_HD) * wnk, c64, s64, NSA_HD // 2).astype(BF16)
    vw_o[...] = seg("nvw").astype(BF16)

    dqn = _rope(_seg_rms(seg("dq"), b32, DIFF_QK) * wdq_ref[...], c32, s32, DIFF_QK // 2)
    dq_o[...] = (dqn * (DIFF_QK ** -0.5)).astype(BF16)
    dk_o[...] = _rope(_seg_rms(seg("dk"), b32, DIFF_QK) * wdk_ref[...], c32, s32, DIFF_QK // 2).astype(BF16)
    dv_o[...] = seg("dv").astype(BF16)


def _block_diag_ones(n, seg):
    i = np.arange(n)
    return jnp.asarray((i[:, None] // seg == i[None, :] // seg).astype(np.float32))


def _prep(proj, tabs, fb, wfq, wfk, wnq, wnk, wdq, wdk, tm=256):
    s = proj.shape[0]
    c64, s64, c32, s32 = tabs
    b64 = _block_diag_ones(256, 64)
    b32 = _block_diag_ones(256, 32)

    def tiled(w, width):
        return jnp.tile(w.reshape(1, -1), (1, width // w.shape[-1]))

    consts = [fb, tiled(wfq, 256), tiled(wfk, 256), tiled(wnq, 512), tiled(wnk, 128), tiled(wdq, 256),
              tiled(wdk, 256), b64, b32]
    widths = [(256, BF16), (256, BF16), (256, BF16), (128, BF16), (128, BF16), (128, BF16), (512, BF16),
              (512, BF16), (256, BF16), (128, BF16), (128, BF16), (128, BF16), (128, BF16), (128, F32),
              (256, BF16), (256, BF16), (256, BF16)]
    row = lambda w: pl.BlockSpec((tm, w), lambda i: (i, 0))
    return pl.pallas_call(
        functools.partial(_prep_kernel, tm=tm),
        grid=(s // tm,),
        in_specs=[row(PROJ_W)] + [row(LANES)] * 4 + [_full(c.shape) for c in consts],
        out_specs=[row(w) for w, _ in widths],
        out_shape=[jax.ShapeDtypeStruct((s, w), dt) for w, dt in widths],
        scratch_shapes=[pltpu.VMEM((8, LANES), F32)],
        compiler_params=_cparams(("arbitrary",)),
        name="head_prep",
    )(proj, c64, s64, c32, s32, *consts)


def _flash_kernel(code_ref, qT_ref, k_ref, vT_ref, *rest, groups, tq, tk, window, has_sel):
    if has_sel:
        sel_ref, o_ref, m_sc, l_sc, acc_sc = rest
    else:
        o_ref, m_sc, l_sc, acc_sc = rest
    code = code_ref[pl.program_id(1)]
    i = code & 0xFFF
    j = (code >> 12) & 0xFFF
    first = (code >> 24) & 1
    last = (code >> 25) & 1
    rows = groups * tq

    @pl.when(first == 1)
    def _():
        m_sc[...] = jnp.full_like(m_sc, NEG_INIT)
        l_sc[...] = jnp.zeros_like(l_sc)
        acc_sc[...] = jnp.zeros_like(acc_sc)

    def step(masked):
        s = jnp.dot(k_ref[0], qT_ref[0, 0], preferred_element_type=F32)
        if has_sel:
            sb = sel_ref[0]
            bias = jnp.concatenate(
                [jnp.broadcast_to(sb[b:b + 1, :], (SLC_LEN, tq)) for b in range(tk // SLC_LEN)], axis=0)
            s = s + _tile_lanes_any(bias, groups)
        if masked:
            kpos = j * tk + lax.broadcasted_iota(jnp.int32, (tk, rows), 0)
            lane = lax.broadcasted_iota(jnp.int32, (tk, rows), 1)
            qpos = i * tq + (lane & (tq - 1))
            keep = kpos <= qpos
            if window is not None:
                keep = keep & (kpos > qpos - window)
            s = jnp.where(keep, s, MASKVAL)
        m_prev = m_sc[...]
        m_new = jnp.maximum(m_prev, jnp.max(s, axis=0, keepdims=True))
        alpha = jnp.exp(m_prev - m_new)
        p = jnp.exp(s - m_new)
        l_sc[...] = alpha * l_sc[...] + jnp.sum(p, axis=0, keepdims=True)
        acc_sc[...] = acc_sc[...] * alpha + jnp.dot(vT_ref[0], p.astype(BF16), preferred_element_type=F32)
        m_sc[...] = m_new

    if window is not None:
        step(True)
    else:
        needs_mask = (j + 1) * tk - 1 > i * tq

        @pl.when(needs_mask)
        def _():
            step(True)

        @pl.when(jnp.logical_not(needs_mask))
        def _():
            step(False)

    @pl.when(last == 1)
    def _():
        l = l_sc[...]
        o_ref[0, 0] = acc_sc[...] / jnp.where(l > 0.0, l, 1.0)


def _tile_lanes_any(t, reps):
    return t if reps == 1 else jnp.concatenate([t] * reps, axis=1)


def _flash(qT, k, vT, sel=None, *, groups, tq, tk, window=None):
    hkv, nq, dk, rows = qT.shape
    dv = vT.shape[1]
    codes = []
    for i in range(nq):
        q_lo, q_hi = i * tq, i * tq + tq - 1
        j_hi = q_hi // tk
        j_lo = 0 if window is None else max(0, (q_lo - window + 1) // tk)
        for j in range(j_lo, j_hi + 1):
            codes.append(i | (j << 12) | (int(j == j_lo) << 24) | (int(j == j_hi) << 25))
    codes = jnp.asarray(np.asarray(codes, dtype=np.int32))
    nsteps = codes.shape[0]

    def ti(c, s):
        return c[s] & 0xFFF

    def tj(c, s):
        return (c[s] >> 12) & 0xFFF

    in_specs = [
        pl.BlockSpec((1, 1, dk, rows), lambda h, s, c: (h, ti(c, s), 0, 0)),
        pl.BlockSpec((1, tk, dk), lambda h, s, c: (h, tj(c, s), 0)),
        pl.BlockSpec((1, dv, tk), lambda h, s, c: (h, 0, tj(c, s))),
    ]
    args = [qT, k, vT]
    if sel is not None:
        in_specs.append(pl.BlockSpec((1, tk // SLC_LEN, tq), lambda h, s, c: (h, tj(c, s), ti(c, s))))
        args.append(sel)
    kern = functools.partial(_flash_kernel, groups=groups, tq=tq, tk=tk, window=window, has_sel=sel is not None)
    return pl.pallas_call(
        kern,
        grid_spec=pltpu.PrefetchScalarGridSpec(
            num_scalar_prefetch=1,
            grid=(hkv, nsteps),
            in_specs=in_specs,
            out_specs=pl.BlockSpec((1, 1, dv, rows), lambda h, s, c: (h, ti(c, s), 0, 0)),
            scratch_shapes=[pltpu.VMEM((1, rows), F32), pltpu.VMEM((1, rows), F32), pltpu.VMEM((dv, rows), F32)],
        ),
        out_shape=jax.ShapeDtypeStruct((hkv, nq, dv, rows), F32),
        compiler_params=_cparams(("parallel", "arbitrary")),
        name="flash_attn",
    )(codes, *args)


def _to_qT(q, heads, groups, tq):
    s = q.shape[0]
    d = q.shape[1] // heads
    x = q.reshape(s // tq, tq, heads // groups, groups, d)
    return x.transpose(2, 0, 4, 3, 1).reshape(heads // groups, s // tq, d, groups * tq)


def _from_oT(oT, groups, tq):
    hkv, nq, d, _ = oT.shape
    x = oT.reshape(hkv, nq, d, groups, tq).transpose(1, 4, 0, 3, 2)
    return x.reshape(nq * tq, hkv * groups * d)


def _to_heads(x, heads):
    s = x.shape[0]
    return x.reshape(s, heads, -1).transpose(1, 0, 2)


def _to_headsT(x, heads):
    s = x.shape[0]
    return x.reshape(s, heads, -1).transpose(1, 2, 0)


def _compress_kernel(r_ref, rn_ref, w_ref, pos_ref, nw_ref, o_ref, *, normalize):
    half = r_ref.shape[2]
    wa = w_ref[0, 0:half, :].astype(BF16)
    wb = w_ref[0, half:2 * half, :].astype(BF16)
    const = jnp.dot(pos_ref[0], w_ref[0], precision=HIGHEST, preferred_element_type=F32)
    out = (jnp.dot(r_ref[0], wa, preferred_element_type=F32)
           + jnp.dot(rn_ref[0], wb, preferred_element_type=F32) + const)
    if normalize:
        out = _rms(out, nw_ref[...])
    o_ref[0] = out.astype(BF16)


def _compress(r, w, pos, nw, normalize):
    g, n, width = r.shape
    hd = w.shape[-1]
    r_next = jnp.concatenate([r[:, 1:], jnp.zeros((g, 1, width), r.dtype)], axis=1)
    rspec = pl.BlockSpec((1, n, width), lambda i: (i, 0, 0))
    return pl.pallas_call(
        functools.partial(_compress_kernel, normalize=normalize),
        grid=(g,),
        in_specs=[rspec, rspec, _full((1,) + w.shape), _full((1,) + pos.shape), _full((1, hd))],
        out_specs=pl.BlockSpec((1, n, hd), lambda i: (i, 0, 0)),
        out_shape=jax.ShapeDtypeStruct((g, n, hd), BF16),
        compiler_params=_cparams(("parallel",)),
        name="nsa_compress",
    )(r, r_next, w[None], pos[None], nw.reshape(1, hd))


def _nsa_cmp_kernel(qT_ref, kc_ref, vcT_ref, ov_ref, o_ref, sel_ref, *, tq, groups):
    i = pl.program_id(1)
    ncmp = kc_ref.shape[1]
    rows = groups * tq
    s = jnp.dot(kc_ref[0], qT_ref[0, 0], preferred_element_type=F32)
    n_idx = lax.broadcasted_iota(jnp.int32, (ncmp, rows), 0)
    lane = lax.broadcasted_iota(jnp.int32, (ncmp, rows), 1)
    qpos = i * tq + (lane & (tq - 1))
    keep = n_idx * CMP_STRIDE + (CMP_LEN - 1) <= qpos
    sm = jnp.where(keep, s, NEG_INIT)
    m = jnp.max(sm, axis=0, keepdims=True)
    e = jnp.where(keep, jnp.exp(sm - m), 0.0)
    den = jnp.sum(e, axis=0, keepdims=True)
    p = e / jnp.where(den > 0.0, den, 1.0)
    o_ref[0, 0] = jnp.dot(vcT_ref[0], p.astype(BF16), preferred_element_type=F32)

    psum = p[:, 0:tq]
    for g in range(1, groups):
        psum = psum + p[:, g * tq:(g + 1) * tq]
    imp = jnp.dot(ov_ref[...], psum, precision=HIGHEST, preferred_element_type=F32)
    ns = imp.shape[0]
    blk = lax.broadcasted_iota(jnp.int32, (ns, tq), 0)
    qp = i * tq + lax.broadcasted_iota(jnp.int32, (ns, tq), 1)
    cur = qp // SLC_LEN
    valid = blk <= cur
    forced = (blk == 0) | (blk == cur) | (blk == cur - 1)
    work = jnp.where(valid, imp + jnp.where(forced, FORCE_BONUS, 0.0), NEG_INIT)
    picked = jnp.zeros((ns, tq), dtype=jnp.bool_)
    for _ in range(SLC_TOPK):
        mx = jnp.max(work, axis=0, keepdims=True)
        idx = jnp.min(jnp.where(work == mx, blk, ns), axis=0, keepdims=True)
        pick = (blk == idx) & (mx > 0.5 * NEG_INIT)
        picked = picked | pick
        work = jnp.where(pick, NEG_INIT, work)
    sel_ref[0] = jnp.where(picked, 0.0, MASKVAL)


def _nsa_cmp(qnT, kcc, vccT, overlapT, tq):
    hkv, nq, hd, rows = qnT.shape
    ncmp = kcc.shape[1]
    ns = overlapT.shape[0]
    s = nq * tq
    return pl.pallas_call(
        functools.partial(_nsa_cmp_kernel, tq=tq, groups=rows // tq),
        grid=(hkv, nq),
        in_specs=[pl.BlockSpec((1, 1, hd, rows), lambda h, i: (h, i, 0, 0)),
                  pl.BlockSpec((1, ncmp, hd), lambda h, i: (h, 0, 0)),
                  pl.BlockSpec((1, hd, ncmp), lambda h, i: (h, 0, 0)),
                  _full(overlapT.shape)],
        out_specs=[pl.BlockSpec((1, 1, hd, rows), lambda h, i: (h, i, 0, 0)),
                   pl.BlockSpec((1, ns, tq), lambda h, i: (h, 0, i))],
        out_shape=[jax.ShapeDtypeStruct((hkv, nq, hd, rows), F32), jax.ShapeDtypeStruct((hkv, ns, s), F32)],
        compiler_params=_cparams(("parallel", "parallel")),
        name="nsa_cmp_select",
    )(qnT, kcc, vccT, overlapT)


def _mix_out_kernel(lam_ref, h_ref, fox_ref, oc_ref, os_ref, ow_ref, gc_ref, gs_ref, gw_ref, d0_ref, d1_ref,
                    sw_ref, b64_ref, w_ref, o_ref, *, diff_scale):
    lam = lam_ref[0]
    nsa = gc_ref[...] * oc_ref[...] + gs_ref[...] * os_ref[...] + gw_ref[...] * ow_ref[...]
    a = d0_ref[...] - lam * d1_ref[...]
    ss = jnp.dot(a * a, b64_ref[...], precision=HIGHEST, preferred_element_type=F32)
    diff = a * lax.rsqrt(ss * (1.0 / DIFF_V) + EPS) * sw_ref[...] * diff_scale
    nf = FOX_HEADS * FOX_HD
    nn = NSA_HEADS * NSA_HD
    acc = h_ref[...]
    acc = acc + jnp.dot(fox_ref[...].astype(BF16), w_ref[0:nf, :], preferred_element_type=F32)
    acc = acc + jnp.dot(nsa.astype(BF16), w_ref[nf:nf + nn, :], preferred_element_type=F32)
    acc = acc + jnp.dot(diff.astype(BF16), w_ref[nf + nn:, :], preferred_element_type=F32)
    o_ref[...] = acc


def _mix_out(lam, h, fox, oc, os_, ow, gc, gs, gw, d0, d1, sw, w_out_bf16, diff_scale, tm=512):
    s, d = h.shape
    row = lambda w: pl.BlockSpec((tm, w), lambda i: (i, 0))
    b64 = _block_diag_ones(256, 64)
    return pl.pallas_call(
        functools.partial(_mix_out_kernel, diff_scale=diff_scale),
        grid=(s // tm,),
        in_specs=[pl.BlockSpec(memory_space=pltpu.SMEM), row(d), row(256)] + [row(512)] * 6 + [row(256), row(256),
                  _full((1, 256)), _full((256, 256)), _full(w_out_bf16.shape)],
        out_specs=row(d),
        out_shape=jax.ShapeDtypeStruct((s, d), F32),
        compiler_params=_cparams(("parallel",)),
        name="mix_out",
    )(lam, h, fox, oc, os_, ow, gc, gs, gw, d0, d1, sw, b64, w_out_bf16)


def _peer_query_kernel(h_ref, nw_ref, wq_ref, keys_ref, xn_o, sc_o):
    xn = _rms(h_ref[...], nw_ref[...]).astype(BF16)
    xn_o[...] = xn
    q = jnp.dot(xn, wq_ref[...], preferred_element_type=F32).astype(BF16)
    half = PEER_DQ // 2
    for b in range(2 * PEER_HEADS):
        sc_o[:, b * PEER_NKEYS:(b + 1) * PEER_NKEYS] = jnp.dot(
            q[:, b * half:(b + 1) * half], keys_ref[b], preferred_element_type=F32)


def _peer_query(h, nw, wq_bf16, keysT_bf16, tm=512):
    s, d = h.shape
    n = wq_bf16.shape[1]
    nsc = 2 * PEER_HEADS * PEER_NKEYS
    return pl.pallas_call(
        _peer_query_kernel,
        grid=(s // tm,),
        in_specs=[pl.BlockSpec((tm, d), lambda i: (i, 0)), _full((1, d)), _full((d, n)), _full(keysT_bf16.shape)],
        out_specs=[pl.BlockSpec((tm, d), lambda i: (i, 0)), pl.BlockSpec((tm, nsc), lambda i: (i, 0))],
        out_shape=[jax.ShapeDtypeStruct((s, d), BF16), jax.ShapeDtypeStruct((s, nsc), F32)],
        compiler_params=_cparams(("parallel",)),
        name="peer_query",
    )(h, nw.reshape(1, d), wq_bf16, keysT_bf16)


def _top16_rows(s):
    n = s.shape[0]
    row = lax.broadcasted_iota(jnp.int32, s.shape, 0)
    rank = jnp.full(s.shape, float(PEER_TOPK), dtype=F32)
    work = s
    vals = []
    for r in range(PEER_TOPK):
        mx = jnp.max(work, axis=0, keepdims=True)
        idx = jnp.min(jnp.where(work == mx, row, n), axis=0, keepdims=True)
        pick = row == idx
        rank = jnp.where(pick, float(r), rank)
        work = jnp.where(pick, NEG_INIT, work)
        vals.append(mx)
    return rank, vals


def _peer_topk_kernel(sc_ref, a_o, c_o, b_o, r_o):
    s1 = sc_ref[0]
    s2 = sc_ref[1]
    rank1, v1 = _top16_rows(s1)
    rank2, v2 = _top16_rows(s2)
    k = PEER_TOPK
    ea = [jnp.exp(v - v1[0]) for v in v1]
    r16 = lax.broadcasted_iota(jnp.int32, (k, s1.shape[1]), 0)
    v2m = jnp.zeros((k, s1.shape[1]), F32)
    for r in range(k):
        v2m = jnp.where(r16 == r, v2[r], v2m)
    eb = jnp.exp(v2m - v2[0])
    cand = jnp.concatenate([v1[r] + v2m for r in range(k)], axis=0)
    gate = jnp.concatenate([ea[r] * eb for r in range(k)], axis=0)
    crow = lax.broadcasted_iota(jnp.int32, cand.shape, 0)
    work = cand
    picked = jnp.zeros(cand.shape, dtype=jnp.bool_)
    for _ in range(k):
        mx = jnp.max(work, axis=0, keepdims=True)
        idx = jnp.min(jnp.where(work == mx, crow, k * k), axis=0, keepdims=True)
        pick = crow == idx
        picked = picked | pick
        work = jnp.where(pick, NEG_INIT, work)
    pf = picked.astype(F32)
    z = jnp.sum(pf * gate, axis=0, keepdims=True)
    cnt = [jnp.sum(pf[r * k:(r + 1) * k, :], axis=0, keepdims=True) for r in range(k)]
    cmap = jnp.zeros(s1.shape, dtype=F32)
    for r in range(k):
        cmap = jnp.where(rank1 == float(r), cnt[r], cmap)
    a_o[0] = jnp.where(rank1 < float(k), jnp.exp(s1 - v1[0]), 0.0) / z
    c_o[0] = cmap
    b_o[0] = jnp.where(rank2 < float(k), jnp.exp(s2 - v2[0]), 0.0)
    r_o[0] = rank2


def _peer_topk(scT, tm=256):
    n2, nk, s = scT.shape
    heads = n2 // 2
    out = jax.ShapeDtypeStruct((heads, nk, s), F32)
    ospec = pl.BlockSpec((1, nk, tm), lambda h, t: (h, 0, t))
    return pl.pallas_call(
        _peer_topk_kernel,
        grid=(heads, s // tm),
        in_specs=[pl.BlockSpec((2, nk, tm), lambda h, t: (h, 0, t))],
        out_specs=[ospec] * 4,
        out_shape=[out] * 4,
        compiler_params=_cparams(("parallel", "parallel")),
        name="peer_topk",
    )(scT)


_ERF_ALPHA = (-2.72614225801306e-10, 2.77068142495902e-08, -2.10102402082508e-06, -5.69250639462346e-05,
              -7.34990630326855e-04, -2.95459980854025e-03, -1.60960333262415e-02)
_ERF_BETA = (-1.45660718464996e-05, -2.13374055278905e-04, -1.68282697438203e-03, -7.37332916720468e-03,
             -1.42647390514189e-02)


def _gelu_exact(x):
    return 0.5 * x * (1.0 + lax.erf(x * (2.0 ** -0.5)))


def _peer_main_kernel(xT_ref, u_ref, vT_ref, a_ref, c_ref, b_ref, r_ref, o_ref, w_sc, *, ec):
    ci = pl.program_id(1)

    @pl.when(ci == 0)
    def _():
        o_ref[...] = jnp.zeros_like(o_ref)

    hid = jnp.dot(u_ref[...], xT_ref[...], preferred_element_type=F32)
    act = _gelu_exact(hid)
    nk = PEER_NKEYS
    for ii in range(ec // nk):
        i1 = ci * (ec // nk) + ii
        w = None
        for h in range(PEER_HEADS):
            a_row = a_ref[h, pl.ds(i1, 1), :]
            c_row = c_ref[h, pl.ds(i1, 1), :]
            term = jnp.where(r_ref[h] < c_row, b_ref[h], 0.0) * a_row
            w = term if w is None else w + term
        w_sc[ii * nk:(ii + 1) * nk, :] = (w * act[ii * nk:(ii + 1) * nk, :]).astype(BF16)
    o_ref[...] += jnp.dot(vT_ref[...], w_sc[...], preferred_element_type=F32)


def _peer_main(xnT, u_bf16, vT_bf16, a, c, b, r, tm=512, ec=1024):
    d, s = xnT.shape
    e = u_bf16.shape[0]
    heads, nk, _ = a.shape
    rt = pl.BlockSpec((heads, nk, tm), lambda t, ci: (0, 0, t))
    return pl.pallas_call(
        functools.partial(_peer_main_kernel, ec=ec),
        grid=(s // tm, e // ec),
        in_specs=[pl.BlockSpec((d, tm), lambda t, ci: (0, t)),
                  pl.BlockSpec((ec, d), lambda t, ci: (ci, 0)),
                  pl.BlockSpec((d, ec), lambda t, ci: (0, ci)),
                  rt, rt, rt, rt],
        out_specs=pl.BlockSpec((d, tm), lambda t, ci: (0, t)),
        out_shape=jax.ShapeDtypeStruct((d, s), F32),
        scratch_shapes=[pltpu.VMEM((ec, tm), BF16)],
        compiler_params=_cparams(("parallel", "arbitrary")),
        name="peer_main",
    )(xnT, u_bf16, vT_bf16, a, c, b, r)


def _ple_kernel(h_ref, peer_ref, p_ref, nw_ref, wg_ref, wp_ref, o_ref):
    h2 = h_ref[...] + peer_ref[...]
    xn = _rms(h2, nw_ref[...]).astype(BF16)
    gate = 1.0 / (1.0 + jnp.exp(-jnp.dot(xn, wg_ref[...], preferred_element_type=F32)))
    emb = jnp.dot(p_ref[...].astype(BF16), wp_ref[...], preferred_element_type=F32)
    o_ref[...] = h2 + gate * emb


def _ple(h, peer, p, nw, wg_bf16, wp_bf16, tm=512):
    s, d = h.shape
    pd = p.shape[1]
    row = lambda w: pl.BlockSpec((tm, w), lambda i: (i, 0))
    return pl.pallas_call(
        _ple_kernel,
        grid=(s // tm,),
        in_specs=[row(d), row(d), row(pd), _full((1, d)), _full((d, d)), _full((pd, d))],
        out_specs=row(d),
        out_shape=jax.ShapeDtypeStruct((s, d), F32),
        compiler_params=_cparams(("parallel",)),
        name="ple",
    )(h, peer, p, nw.reshape(1, d), wg_bf16, wp_bf16)


def _reorder_w_in(w):
    widths = (256, 256, 256, 4, 512, 128, 128, 128, 128, 128, 128, 24, 256, 256, 256)
    offs = np.concatenate([[0], np.cumsum(widths)])
    piece = lambda k: w[:, offs[k]:offs[k + 1]]
    order = [0, 1, 2, 4, 5, 6, 7, 8, 9, 10, 12, 13, 14, 3, 11]
    pad = jnp.zeros((w.shape[0], PROJ_W - int(offs[-1])), w.dtype)
    return jnp.concatenate([piece(k) for k in order] + [pad], axis=1)


def _overlap_T(s):
    n = np.arange(s // CMP_STRIDE)[None, :] * CMP_STRIDE
    m = np.arange(s // SLC_LEN)[:, None] * SLC_LEN
    return jnp.asarray(((n < m + SLC_LEN) & (n + CMP_LEN > m)).astype(np.float32))


def _layer(h, p_i, tabs, layer, attn_norm_w, w_in, fox_f_bias, fox_q_norm_w, fox_k_norm_w, nsa_q_norm_w,
           nsa_k_norm_w, nsa_cmp_pos, nsa_cmp_w, diff_q_norm_w, diff_k_norm_w, diff_lambda, diff_subln_w,
           w_out, ffn_norm_w, peer_w_q, peer_sub_keys, peer_u, peer_v, ple_norm_w, ple_w_gate, ple_w_proj):
    s = h.shape[0]
    proj = _norm_matmul(h, attn_norm_w, _reorder_w_in(w_in).astype(BF16))
    fb = jnp.zeros((1, LANES), F32).at[0, :FOX_HEADS].set(fox_f_bias)
    (fq, fk, fv, chi, cmid, clo, nqn, nqr, kvc, ksr, vs, kwr, vw, gates, dq, dk, dv) = _prep(
        proj, tabs, fb, fox_q_norm_w, fox_k_norm_w, nsa_q_norm_w, nsa_k_norm_w, diff_q_norm_w, diff_k_norm_w)

    tq_f = 512
    one = jnp.ones((s, FOX_HEADS, 3), BF16)
    csplit = jnp.stack([chi[:, :FOX_HEADS], cmid[:, :FOX_HEADS], clo[:, :FOX_HEADS]], axis=-1)
    zpad = jnp.zeros((s, FOX_HEADS, LANES - FOX_HD - 6), BF16)
    q_ext = jnp.concatenate([fq.reshape(s, FOX_HEADS, FOX_HD), one, csplit, zpad], axis=-1)
    k_ext = jnp.concatenate([fk.reshape(s, FOX_HEADS, FOX_HD), -csplit, one, zpad], axis=-1)
    o_fox = _flash(_to_qT(q_ext.reshape(s, -1), FOX_HEADS, 1, tq_f), k_ext.transpose(1, 0, 2),
                   _to_headsT(fv, FOX_HEADS), groups=1, tq=tq_f, tk=512)
    o_fox = _from_oT(o_fox, 1, tq_f)

    tq_n = 128
    kvr = kvc.reshape(s // CMP_STRIDE, CMP_STRIDE, 2, NSA_KV, NSA_HD).transpose(2, 3, 0, 1, 4)
    kvr = kvr.reshape(2, NSA_KV, s // CMP_STRIDE, CMP_STRIDE * NSA_HD)
    posf = nsa_cmp_pos.reshape(2, 1, CMP_LEN * NSA_HD)
    kcc = _compress(kvr[0], nsa_cmp_w[0], posf[0], nsa_k_norm_w, True)
    vcc = _compress(kvr[1], nsa_cmp_w[1], posf[1], nsa_k_norm_w, False)
    o_c, sel = _nsa_cmp(_to_qT(nqn, NSA_HEADS, NSA_G, tq_n), kcc, vcc.transpose(0, 2, 1), _overlap_T(s), tq_n)
    qrT = _to_qT(nqr, NSA_HEADS, NSA_G, tq_n)
    o_s = _flash(qrT, _to_heads(ksr, NSA_KV), _to_headsT(vs, NSA_KV), sel, groups=NSA_G, tq=tq_n, tk=512)
    o_w = _flash(qrT, _to_heads(kwr, NSA_KV), _to_headsT(vw, NSA_KV), groups=NSA_G, tq=tq_n, tk=512, window=WIN)
    o_c, o_s, o_w = (_from_oT(o, NSA_G, tq_n) for o in (o_c, o_s, o_w))
    g3 = gates[:, FOX_HEADS:FOX_HEADS + 3 * NSA_HEADS].reshape(s, NSA_HEADS, 3)
    gc, gs, gw = (jnp.repeat(g3[:, :, b], NSA_HD, axis=1) for b in range(3))

    tq_d = 256
    dq4 = dq.reshape(s, DIFF_HEADS, 2, DIFF_QK)
    zq = jnp.zeros((s, DIFF_HEADS, DIFF_QK), BF16)
    q_maps = jnp.stack([jnp.concatenate([dq4[:, :, 0], zq], axis=-1),
                        jnp.concatenate([zq, dq4[:, :, 1]], axis=-1)], axis=2)
    o_d = _flash(_to_qT(q_maps.reshape(s, -1), 2 * DIFF_HEADS, 2, tq_d), _to_heads(dk, DIFF_HEADS),
                 _to_headsT(dv, DIFF_HEADS), groups=2, tq=tq_d, tk=512)
    o_d = _from_oT(o_d, 2, tq_d).reshape(s, DIFF_HEADS, 2, DIFF_V)
    d0 = o_d[:, :, 0].reshape(s, -1)
    d1 = o_d[:, :, 1].reshape(s, -1)
    lv = diff_lambda.astype(F32)
    lam_init = 0.8 - 0.6 * math.exp(-0.3 * layer)
    lam = (jnp.exp(jnp.sum(lv[0] * lv[1])) - jnp.exp(jnp.sum(lv[2] * lv[3])) + lam_init).reshape(1)
    sw = jnp.tile(diff_subln_w.reshape(1, -1), (1, DIFF_HEADS))
    h1 = _mix_out(lam, h, o_fox, o_c, o_s, o_w, gc, gs, gw, d0, d1, sw, w_out.astype(BF16), 1.0 - lam_init)

    keysT = peer_sub_keys.reshape(2 * PEER_HEADS, PEER_NKEYS, PEER_DQ // 2).transpose(0, 2, 1).astype(BF16)
    xn, sc = _peer_query(h1, ffn_norm_w, peer_w_q.astype(BF16), keysT)
    scT = sc.T.reshape(2 * PEER_HEADS, PEER_NKEYS, s)
    a, c, b, r = _peer_topk(scT)
    peerT = _peer_main(xn.T, peer_u.astype(BF16), peer_v.T.astype(BF16), a, c, b, r)

    return _ple(h1, peerT.T, p_i, ple_norm_w, ple_w_gate.astype(BF16), ple_w_proj.astype(BF16))


def kernel(x, p, positions, attn_norm_w, w_in, fox_f_bias, fox_q_norm_w, fox_k_norm_w, nsa_q_norm_w, nsa_k_norm_w,
           nsa_cmp_pos, nsa_cmp_w, diff_q_norm_w, diff_k_norm_w, diff_lambda, diff_subln_w, w_out, ffn_norm_w,
           peer_w_q, peer_sub_keys, peer_u, peer_v, ple_norm_w, ple_w_gate, ple_w_proj):
    b, s, d = x.shape
    assert b == 1 and d == D_MODEL and s % 512 == 0
    tabs = _rope_tables(positions)
    h = x.reshape(s, d)
    per_layer = (attn_norm_w, w_in, fox_f_bias, fox_q_norm_w, fox_k_norm_w, nsa_q_norm_w, nsa_k_norm_w, nsa_cmp_pos,
                 nsa_cmp_w, diff_q_norm_w, diff_k_norm_w, diff_lambda, diff_subln_w, w_out, ffn_norm_w, peer_w_q,
                 peer_sub_keys, peer_u, peer_v, ple_norm_w, ple_w_gate, ple_w_proj)
    for layer in range(attn_norm_w.shape[0]):
        h = _layer(h, p[layer, 0], tabs, layer, *(w[layer] for w in per_layer))
    return h.reshape(b, s, d)
```

```python
import functools
import math

import numpy as np
import jax
import jax.numpy as jnp
from jax import lax
from jax.experimental import pallas as pl
from jax.experimental.pallas import tpu as pltpu

F32 = jnp.float32
BF16 = jnp.bfloat16
HIGHEST = lax.Precision.HIGHEST

D_MODEL = 1024
PLE_DIM = 256
ROPE_THETA = 10000.0
EPS = 1e-6
FOX_HEADS, FOX_HD = 4, 64
NSA_HEADS, NSA_KV, NSA_HD = 8, 2, 64
NSA_G = NSA_HEADS // NSA_KV
CMP_LEN, CMP_STRIDE, SLC_LEN, SLC_TOPK, WIN = 32, 16, 64, 16, 512
FORCE_BONUS = 1.0e4
DIFF_HEADS, DIFF_QK, DIFF_V = 4, 32, 64
PEER_HEADS, PEER_NKEYS, PEER_DQ, PEER_TOPK = 8, 128, 256, 16
PEER_EXPERTS = PEER_NKEYS * PEER_NKEYS

LANES = 128
LOG2E = math.log2(math.e)
NEG_INIT = -1.0e30
MASKVAL = -2.0e30
VMEM_LIMIT = 56 * 1024 * 1024
TK_CAUSAL = 1024

_SEG = dict(fq=(0, 256), fk=(256, 256), fv=(512, 256), nq=(768, 512), nkc=(1280, 128), nvc=(1408, 128),
            nks=(1536, 128), nvs=(1664, 128), nkw=(1792, 128), nvw=(1920, 128),
            dq=(2048, 256), dk=(2304, 256), dv=(2560, 256), misc=(2816, 128))
PROJ_W = 2944


def _cparams(sem):
    return pltpu.CompilerParams(dimension_semantics=sem, vmem_limit_bytes=VMEM_LIMIT)


def _full(shape):
    n = len(shape)
    return pl.BlockSpec(shape, lambda *_: (0,) * n)


def _rope_tab_kernel(pos_ref, f64_ref, g64_ref, f32_ref, g32_ref, c64_o, s64_o, c32_o, s32_o):
    pos = pos_ref[...].astype(F32)
    a64 = pos * f64_ref[...]
    c64_o[...] = jnp.cos(a64)
    s64_o[...] = jnp.sin(a64) * g64_ref[...]
    a32 = pos * f32_ref[...]
    c32_o[...] = jnp.cos(a32)
    s32_o[...] = jnp.sin(a32) * g32_ref[...]


def _rope_tables(positions):
    s = positions.shape[-1]
    pos = positions.reshape(s, 1)
    lane = np.arange(LANES)

    def lanes(half):
        inv = ROPE_THETA ** (-jnp.arange(half, dtype=F32) / half)
        freq = inv[(lane % (2 * half)) % half].reshape(1, LANES)
        sign = np.where((lane % (2 * half)) < half, -1.0, 1.0).astype(np.float32).reshape(1, LANES)
        return freq, jnp.asarray(sign)

    f64, g64 = lanes(NSA_HD // 2)
    f32_, g32 = lanes(DIFF_QK // 2)
    tm = 512
    out = jax.ShapeDtypeStruct((s, LANES), F32)
    row = pl.BlockSpec((tm, LANES), lambda i: (i, 0))
    return pl.pallas_call(
        _rope_tab_kernel,
        grid=(s // tm,),
        in_specs=[pl.BlockSpec((tm, 1), lambda i: (i, 0))] + [_full((1, LANES))] * 4,
        out_specs=[row] * 4,
        out_shape=[out] * 4,
        compiler_params=_cparams(("parallel",)),
        name="rope_tables",
    )(pos, f64, g64, f32_, g32)


def _rms(x, w):
    return x * lax.rsqrt(jnp.mean(x * x, axis=-1, keepdims=True) + EPS) * w


def _norm_matmul_kernel(x_ref, nw_ref, w_ref, o_ref):
    xn = _rms(x_ref[...], nw_ref[...])
    o_ref[...] = jnp.dot(xn.astype(BF16), w_ref[...], preferred_element_type=F32)


def _norm_matmul(x, nw, w_bf16, tm=512):
    s, d = x.shape
    n = w_bf16.shape[1]
    return pl.pallas_call(
        _norm_matmul_kernel,
        grid=(s // tm,),
        in_specs=[pl.BlockSpec((tm, d), lambda i: (i, 0)), _full((1, d)), _full((d, n))],
        out_specs=pl.BlockSpec((tm, n), lambda i: (i, 0)),
        out_shape=jax.ShapeDtypeStruct((s, n), F32),
        compiler_params=_cparams(("parallel",)),
        name="norm_matmul",
    )(x, nw.reshape(1, d), w_bf16)


def _seg_rms(x, bmat, seg):
    outs = []
    for c0 in range(0, x.shape[1], 256):
        w = min(256, x.shape[1] - c0)
        xc = x[:, c0:c0 + w]
        ss = jnp.dot(xc * xc, bmat[:w, :w], precision=HIGHEST, preferred_element_type=F32)
        outs.append(xc * lax.rsqrt(ss * (1.0 / seg) + EPS))
    return outs[0] if len(outs) == 1 else jnp.concatenate(outs, axis=1)


def _tile_lanes(t, width):
    reps = width // LANES
    return t if reps == 1 else jnp.concatenate([t] * reps, axis=1)


def _rope(x, cos, sin, half):
    width = x.shape[1]
    left = pltpu.roll(x, width - half, 1)
    right = pltpu.roll(x, half, 1)
    lane = lax.broadcasted_iota(jnp.int32, x.shape, 1)
    swapped = jnp.where((lane & (2 * half - 1)) < half, left, right)
    return x * _tile_lanes(cos, width) + swapped * _tile_lanes(sin, width)


def _prep_kernel(proj_ref, c64_ref, s64_ref, c32_ref, s32_ref, fb_ref, wfq_ref, wfk_ref, wnq_ref, wnk_ref,
                 wdq_ref, wdk_ref, b64_ref, b32_ref,
                 fq_o, fk_o, fv_o, chi_o, cmid_o, clo_o, nqn_o, nqr_o, kvc_o, ksr_o, vs_o, kwr_o, vw_o,
                 gate_o, dq_o, dk_o, dv_o, carry_sc, *, tm):
    def seg(name):
        c0, w = _SEG[name]
        return proj_ref[:, c0:c0 + w]

    b64 = b64_ref[...]
    b32 = b32_ref[...]
    c64, s64, c32, s32 = c64_ref[...], s64_ref[...], c32_ref[...], s32_ref[...]

    fq_o[...] = (_seg_rms(seg("fq"), b64, FOX_HD) * wfq_ref[...] * (FOX_HD ** -0.5 * LOG2E)).astype(BF16)
    fk_o[...] = (_seg_rms(seg("fk"), b64, FOX_HD) * wfk_ref[...]).astype(BF16)
    fv_o[...] = seg("fv").astype(BF16)

    misc = seg("misc")
    gate_o[...] = 1.0 / (1.0 + jnp.exp(-misc))
    t = misc + fb_ref[...]
    logf = jnp.minimum(t, 0.0) - jnp.log1p(jnp.exp(-jnp.abs(t)))

    @pl.when(pl.program_id(0) == 0)
    def _():
        carry_sc[...] = jnp.zeros_like(carry_sc)

    r = lax.broadcasted_iota(jnp.int32, (tm, tm), 0)
    c = lax.broadcasted_iota(jnp.int32, (tm, tm), 1)
    tri = (c <= r).astype(F32)
    csum = jnp.dot(tri, logf, precision=HIGHEST, preferred_element_type=F32) + carry_sc[0:1, :]
    carry_sc[...] = jnp.broadcast_to(csum[tm - 1:tm, :], carry_sc.shape)
    csum = csum * LOG2E
    hi = csum.astype(BF16)
    r1 = csum - hi.astype(F32)
    mid = r1.astype(BF16)
    lo = (r1 - mid.astype(F32)).astype(BF16)
    chi_o[...] = hi
    cmid_o[...] = mid
    clo_o[...] = lo

    nqn = _seg_rms(seg("nq"), b64, NSA_HD) * wnq_ref[...]
    nqn_o[...] = (nqn * (NSA_HD ** -0.5 * LOG2E)).astype(BF16)
    nqr_o[...] = (_rope(nqn, c64, s64, NSA_HD // 2) * (NSA_HD ** -0.5 * LOG2E)).astype(BF16)
    kvc_o[:, 0:128] = seg("nkc").astype(BF16)
    kvc_o[:, 128:256] = seg("nvc").astype(BF16)
    wnk = wnk_ref[...]
    ksr_o[...] = _rope(_seg_rms(seg("nks"), b64, NSA_HD) * wnk, c64, s64, NSA_HD // 2).astype(BF16)
    vs_o[...] = seg("nvs").astype(BF16)
    kwr_o[...] = _rope(_seg_rms(seg("nkw"), b64, NSA_HD) * wnk, c64, s64, NSA_HD // 2).astype(BF16)
    vw_o[...] = seg("nvw").astype(BF16)

    dqn = _rope(_seg_rms(seg("dq"), b32, DIFF_QK) * wdq_ref[...], c32, s32, DIFF_QK // 2)
    dq_o[...] = (dqn * (DIFF_QK ** -0.5 * LOG2E)).astype(BF16)
    dk_o[...] = _rope(_seg_rms(seg("dk"), b32, DIFF_QK) * wdk_ref[...], c32, s32, DIFF_QK // 2).astype(BF16)
    dv_o[...] = seg("dv").astype(BF16)


def _block_diag_ones(n, seg):
    i = np.arange(n)
    return jnp.asarray((i[:, None] // seg == i[None, :] // seg).astype(np.float32))


def _prep(proj, tabs, fb, wfq, wfk, wnq, wnk, wdq, wdk, tm=256):
    s = proj.shape[0]
    c64, s64, c32, s32 = tabs
    b64 = _block_diag_ones(256, 64)
    b32 = _block_diag_ones(256, 32)

    def tiled(w, width):
        return jnp.tile(w.reshape(1, -1), (1, width // w.shape[-1]))

    consts = [fb, tiled(wfq, 256), tiled(wfk, 256), tiled(wnq, 512), tiled(wnk, 128), tiled(wdq, 256),
              tiled(wdk, 256), b64, b32]
    widths = [(256, BF16), (256, BF16), (256, BF16), (128, BF16), (128, BF16), (128, BF16), (512, BF16),
              (512, BF16), (256, BF16), (128, BF16), (128, BF16), (128, BF16), (128, BF16), (128, F32),
              (256, BF16), (256, BF16), (256, BF16)]
    row = lambda w: pl.BlockSpec((tm, w), lambda i: (i, 0))
    return pl.pallas_call(
        functools.partial(_prep_kernel, tm=tm),
        grid=(s // tm,),
        in_specs=[row(PROJ_W)] + [row(LANES)] * 4 + [_full(c.shape) for c in consts],
        out_specs=[row(w) for w, _ in widths],
        out_shape=[jax.ShapeDtypeStruct((s, w), dt) for w, dt in widths],
        scratch_shapes=[pltpu.VMEM((8, LANES), F32)],
        compiler_params=_cparams(("arbitrary",)),
        name="head_prep",
    )(proj, c64, s64, c32, s32, *consts)


def _flash_kernel(code_ref, qT_ref, k_ref, vT_ref, *rest, groups, tq, tk, window, has_sel):
    if has_sel:
        sel_ref, o_ref, m_sc, acc_sc = rest
    else:
        o_ref, m_sc, acc_sc = rest
    code = code_ref[pl.program_id(1)]
    i = code & 0xFFF
    j = (code >> 12) & 0xFFF
    first = (code >> 24) & 1
    last = (code >> 25) & 1
    rows = groups * tq
    dv = o_ref.shape[2]

    @pl.when(first == 1)
    def _():
        m_sc[...] = jnp.full_like(m_sc, NEG_INIT)
        acc_sc[...] = jnp.zeros_like(acc_sc)

    def step(masked):
        s = jnp.dot(k_ref[0], qT_ref[0, 0], preferred_element_type=F32)
        if has_sel:
            sb = sel_ref[0]
            bias = jnp.concatenate(
                [jnp.broadcast_to(sb[b:b + 1, :], (SLC_LEN, tq)) for b in range(tk // SLC_LEN)], axis=0)
            s = s + _tile_lanes_any(bias, groups)
        if masked:
            kpos = j * tk + lax.broadcasted_iota(jnp.int32, (tk, rows), 0)
            lane = lax.broadcasted_iota(jnp.int32, (tk, rows), 1)
            qpos = i * tq + (lane & (tq - 1))
            keep = kpos <= qpos
            if window is not None:
                keep = keep & (kpos > qpos - window)
            s = jnp.where(keep, s, MASKVAL)
        m_prev = m_sc[...]
        m_new = jnp.maximum(m_prev, jnp.max(s, axis=0, keepdims=True))
        alpha = jnp.exp2(m_prev - m_new)
        p = jnp.exp2(s - m_new).astype(BF16)
        acc_sc[...] = acc_sc[...] * alpha + jnp.dot(vT_ref[0], p, preferred_element_type=F32)
        m_sc[...] = m_new

    if window is not None:
        step(True)
    else:
        needs_mask = (j + 1) * tk - 1 > i * tq

        @pl.when(needs_mask)
        def _():
            step(True)

        @pl.when(jnp.logical_not(needs_mask))
        def _():
            step(False)

    @pl.when(last == 1)
    def _():
        l = acc_sc[dv:dv + 1, :]
        o_ref[0, 0] = acc_sc[0:dv, :] / jnp.where(l > 0.0, l, 1.0)


def _tile_lanes_any(t, reps):
    return t if reps == 1 else jnp.concatenate([t] * reps, axis=1)


def _flash(qT, k, vT, sel=None, *, groups, tq, tk, window=None):
    hkv, nq, dk, rows = qT.shape
    dv = vT.shape[1]
    dve = dv + 8
    s_len = vT.shape[2]
    vT = jnp.concatenate([vT, jnp.ones((hkv, 1, s_len), vT.dtype), jnp.zeros((hkv, 7, s_len), vT.dtype)], axis=1)
    codes = []
    for i in range(nq):
        q_lo, q_hi = i * tq, i * tq + tq - 1
        j_hi = q_hi // tk
        j_lo = 0 if window is None else max(0, (q_lo - window + 1) // tk)
        for j in range(j_lo, j_hi + 1):
            codes.append(i | (j << 12) | (int(j == j_lo) << 24) | (int(j == j_hi) << 25))
    codes = jnp.asarray(np.asarray(codes, dtype=np.int32))
    nsteps = codes.shape[0]

    def ti(c, s):
        return c[s] & 0xFFF

    def tj(c, s):
        return (c[s] >> 12) & 0xFFF

    in_specs = [
        pl.BlockSpec((1, 1, dk, rows), lambda h, s, c: (h, ti(c, s), 0, 0)),
        pl.BlockSpec((1, tk, dk), lambda h, s, c: (h, tj(c, s), 0)),
        pl.BlockSpec((1, dve, tk), lambda h, s, c: (h, 0, tj(c, s))),
    ]
    args = [qT, k, vT]
    if sel is not None:
        in_specs.append(pl.BlockSpec((1, tk // SLC_LEN, tq), lambda h, s, c: (h, tj(c, s), ti(c, s))))
        args.append(sel)
    kern = functools.partial(_flash_kernel, groups=groups, tq=tq, tk=tk, window=window, has_sel=sel is not None)
    return pl.pallas_call(
        kern,
        grid_spec=pltpu.PrefetchScalarGridSpec(
            num_scalar_prefetch=1,
            grid=(hkv, nsteps),
            in_specs=in_specs,
            out_specs=pl.BlockSpec((1, 1, dv, rows), lambda h, s, c: (h, ti(c, s), 0, 0)),
            scratch_shapes=[pltpu.VMEM((1, rows), F32), pltpu.VMEM((dve, rows), F32)],
        ),
        out_shape=jax.ShapeDtypeStruct((hkv, nq, dv, rows), F32),
        compiler_params=_cparams(("parallel", "arbitrary")),
        name="flash_attn",
    )(codes, *args)


def _to_qT(q, heads, groups, tq):
    s = q.shape[0]
    d = q.shape[1] // heads
    x = q.reshape(s // tq, tq, heads // groups, groups, d)
    return x.transpose(2, 0, 4, 3, 1).reshape(heads // groups, s // tq, d, groups * tq)


def _from_oT(oT, groups, tq):
    hkv, nq, d, _ = oT.shape
    x = oT.reshape(hkv, nq, d, groups, tq).transpose(1, 4, 0, 3, 2)
    return x.reshape(nq * tq, hkv * groups * d)


def _to_heads(x, heads):
    s = x.shape[0]
    return x.reshape(s, heads, -1).transpose(1, 0, 2)


def _to_headsT(x, heads):
    s = x.shape[0]
    return x.reshape(s, heads, -1).transpose(1, 2, 0)


def _compress_kernel(r_ref, rn_ref, w_ref, pos_ref, nw_ref, o_ref, *, normalize):
    half = r_ref.shape[2]
    wa = w_ref[0, 0:half, :].astype(BF16)
    wb = w_ref[0, half:2 * half, :].astype(BF16)
    const = jnp.dot(pos_ref[0], w_ref[0], precision=HIGHEST, preferred_element_type=F32)
    out = (jnp.dot(r_ref[0], wa, preferred_element_type=F32)
           + jnp.dot(rn_ref[0], wb, preferred_element_type=F32) + const)
    if normalize:
        out = _rms(out, nw_ref[...])
    o_ref[0] = out.astype(BF16)


def _compress(r, w, pos, nw, normalize):
    g, n, width = r.shape
    hd = w.shape[-1]
    r_next = jnp.concatenate([r[:, 1:], jnp.zeros((g, 1, width), r.dtype)], axis=1)
    rspec = pl.BlockSpec((1, n, width), lambda i: (i, 0, 0))
    return pl.pallas_call(
        functools.partial(_compress_kernel, normalize=normalize),
        grid=(g,),
        in_specs=[rspec, rspec, _full((1,) + w.shape), _full((1,) + pos.shape), _full((1, hd))],
        out_specs=pl.BlockSpec((1, n, hd), lambda i: (i, 0, 0)),
        out_shape=jax.ShapeDtypeStruct((g, n, hd), BF16),
        compiler_params=_cparams(("parallel",)),
        name="nsa_compress",
    )(r, r_next, w[None], pos[None], nw.reshape(1, hd))


def _nsa_cmp_kernel(qT_ref, kc_ref, vcT_ref, ov_ref, o_ref, sel_ref, *, tq, groups):
    i = pl.program_id(1)
    ncmp = kc_ref.shape[1]
    rows = groups * tq
    s = jnp.dot(kc_ref[0], qT_ref[0, 0], preferred_element_type=F32)
    n_idx = lax.broadcasted_iota(jnp.int32, (ncmp, rows), 0)
    lane = lax.broadcasted_iota(jnp.int32, (ncmp, rows), 1)
    qpos = i * tq + (lane & (tq - 1))
    keep = n_idx * CMP_STRIDE + (CMP_LEN - 1) <= qpos
    sm = jnp.where(keep, s, NEG_INIT)
    m = jnp.max(sm, axis=0, keepdims=True)
    e = jnp.where(keep, jnp.exp2(sm - m), 0.0)
    den = jnp.sum(e, axis=0, keepdims=True)
    p = e / jnp.where(den > 0.0, den, 1.0)
    o_ref[0, 0] = jnp.dot(vcT_ref[0], p.astype(BF16), preferred_element_type=F32)

    psum = p[:, 0:tq]
    for g in range(1, groups):
        psum = psum + p[:, g * tq:(g + 1) * tq]
    imp = jnp.dot(ov_ref[...], psum, precision=HIGHEST, preferred_element_type=F32)
    ns = imp.shape[0]
    blk = lax.broadcasted_iota(jnp.int32, (ns, tq), 0)
    qp = i * tq + lax.broadcasted_iota(jnp.int32, (ns, tq), 1)
    cur = qp // SLC_LEN
    valid = blk <= cur
    forced = (blk == 0) | (blk == cur) | (blk == cur - 1)
    work = jnp.where(valid, imp + jnp.where(forced, FORCE_BONUS, 0.0), NEG_INIT)
    picked = jnp.zeros((ns, tq), dtype=jnp.bool_)
    for _ in range(SLC_TOPK):
        mx = jnp.max(work, axis=0, keepdims=True)
        idx = jnp.min(jnp.where(work == mx, blk, ns), axis=0, keepdims=True)
        pick = (blk == idx) & (mx > 0.5 * NEG_INIT)
        picked = picked | pick
        work = jnp.where(pick, NEG_INIT, work)
    sel_ref[0] = jnp.where(picked, 0.0, MASKVAL)


def _nsa_cmp(qnT, kcc, vccT, overlapT, tq):
    hkv, nq, hd, rows = qnT.shape
    ncmp = kcc.shape[1]
    ns = overlapT.shape[0]
    s = nq * tq
    return pl.pallas_call(
        functools.partial(_nsa_cmp_kernel, tq=tq, groups=rows // tq),
        grid=(hkv, nq),
        in_specs=[pl.BlockSpec((1, 1, hd, rows), lambda h, i: (h, i, 0, 0)),
                  pl.BlockSpec((1, ncmp, hd), lambda h, i: (h, 0, 0)),
                  pl.BlockSpec((1, hd, ncmp), lambda h, i: (h, 0, 0)),
                  _full(overlapT.shape)],
        out_specs=[pl.BlockSpec((1, 1, hd, rows), lambda h, i: (h, i, 0, 0)),
                   pl.BlockSpec((1, ns, tq), lambda h, i: (h, 0, i))],
        out_shape=[jax.ShapeDtypeStruct((hkv, nq, hd, rows), F32), jax.ShapeDtypeStruct((hkv, ns, s), F32)],
        compiler_params=_cparams(("parallel", "parallel")),
        name="nsa_cmp_select",
    )(qnT, kcc, vccT, overlapT)


def _mix_out_kernel(lam_ref, h_ref, fox_ref, oc_ref, os_ref, ow_ref, gc_ref, gs_ref, gw_ref, d0_ref, d1_ref,
                    sw_ref, b64_ref, w_ref, o_ref, *, diff_scale):
    lam = lam_ref[0]
    nsa = gc_ref[...] * oc_ref[...] + gs_ref[...] * os_ref[...] + gw_ref[...] * ow_ref[...]
    a = d0_ref[...] - lam * d1_ref[...]
    ss = jnp.dot(a * a, b64_ref[...], precision=HIGHEST, preferred_element_type=F32)
    diff = a * lax.rsqrt(ss * (1.0 / DIFF_V) + EPS) * sw_ref[...] * diff_scale
    nf = FOX_HEADS * FOX_HD
    nn = NSA_HEADS * NSA_HD
    acc = h_ref[...]
    acc = acc + jnp.dot(fox_ref[...].astype(BF16), w_ref[0:nf, :], preferred_element_type=F32)
    acc = acc + jnp.dot(nsa.astype(BF16), w_ref[nf:nf + nn, :], preferred_element_type=F32)
    acc = acc + jnp.dot(diff.astype(BF16), w_ref[nf + nn:, :], preferred_element_type=F32)
    o_ref[...] = acc


def _mix_out(lam, h, fox, oc, os_, ow, gc, gs, gw, d0, d1, sw, w_out_bf16, diff_scale, tm=512):
    s, d = h.shape
    row = lambda w: pl.BlockSpec((tm, w), lambda i: (i, 0))
    b64 = _block_diag_ones(256, 64)
    return pl.pallas_call(
        functools.partial(_mix_out_kernel, diff_scale=diff_scale),
        grid=(s // tm,),
        in_specs=[pl.BlockSpec(memory_space=pltpu.SMEM), row(d), row(256)] + [row(512)] * 6 + [row(256), row(256),
                  _full((1, 256)), _full((256, 256)), _full(w_out_bf16.shape)],
        out_specs=row(d),
        out_shape=jax.ShapeDtypeStruct((s, d), F32),
        compiler_params=_cparams(("parallel",)),
        name="mix_out",
    )(lam, h, fox, oc, os_, ow, gc, gs, gw, d0, d1, sw, b64, w_out_bf16)


def _peer_query_kernel(h_ref, nw_ref, wq_ref, keys_ref, xn_o, sc_o):
    xn = _rms(h_ref[...], nw_ref[...]).astype(BF16)
    xn_o[...] = xn
    q = jnp.dot(xn, wq_ref[...], preferred_element_type=F32).astype(BF16)
    half = PEER_DQ // 2
    for b in range(2 * PEER_HEADS):
        sc_o[:, b * PEER_NKEYS:(b + 1) * PEER_NKEYS] = jnp.dot(
            q[:, b * half:(b + 1) * half], keys_ref[b], preferred_element_type=F32)


def _peer_query(h, nw, wq_bf16, keysT_bf16, tm=512):
    s, d = h.shape
    n = wq_bf16.shape[1]
    nsc = 2 * PEER_HEADS * PEER_NKEYS
    return pl.pallas_call(
        _peer_query_kernel,
        grid=(s // tm,),
        in_specs=[pl.BlockSpec((tm, d), lambda i: (i, 0)), _full((1, d)), _full((d, n)), _full(keysT_bf16.shape)],
        out_specs=[pl.BlockSpec((tm, d), lambda i: (i, 0)), pl.BlockSpec((tm, nsc), lambda i: (i, 0))],
        out_shape=[jax.ShapeDtypeStruct((s, d), BF16), jax.ShapeDtypeStruct((s, nsc), F32)],
        compiler_params=_cparams(("parallel",)),
        name="peer_query",
    )(h, nw.reshape(1, d), wq_bf16, keysT_bf16)


def _top16_rows(s):
    n = s.shape[0]
    row = lax.broadcasted_iota(jnp.int32, s.shape, 0)
    rank = jnp.full(s.shape, float(PEER_TOPK), dtype=F32)
    work = s
    vals = []
    for r in range(PEER_TOPK):
        mx = jnp.max(work, axis=0, keepdims=True)
        idx = jnp.min(jnp.where(work == mx, row, n), axis=0, keepdims=True)
        pick = row == idx
        rank = jnp.where(pick, float(r), rank)
        work = jnp.where(pick, NEG_INIT, work)
        vals.append(mx)
    return rank, vals


def _peer_topk_kernel(sc_ref, a_o, c_o, b_o, r_o):
    s1 = sc_ref[0]
    s2 = sc_ref[1]
    rank1, v1 = _top16_rows(s1)
    rank2, v2 = _top16_rows(s2)
    k = PEER_TOPK
    ea = [jnp.exp(v - v1[0]) for v in v1]
    r16 = lax.broadcasted_iota(jnp.int32, (k, s1.shape[1]), 0)
    v2m = jnp.zeros((k, s1.shape[1]), F32)
    for r in range(k):
        v2m = jnp.where(r16 == r, v2[r], v2m)
    eb = jnp.exp(v2m - v2[0])
    cand = jnp.concatenate([v1[r] + v2m for r in range(k)], axis=0)
    gate = jnp.concatenate([ea[r] * eb for r in range(k)], axis=0)
    crow = lax.broadcasted_iota(jnp.int32, cand.shape, 0)
    work = cand
    picked = jnp.zeros(cand.shape, dtype=jnp.bool_)
    for _ in range(k):
        mx = jnp.max(work, axis=0, keepdims=True)
        idx = jnp.min(jnp.where(work == mx, crow, k * k), axis=0, keepdims=True)
        pick = crow == idx
        picked = picked | pick
        work = jnp.where(pick, NEG_INIT, work)
    pf = picked.astype(F32)
    z = jnp.sum(pf * gate, axis=0, keepdims=True)
    cnt = [jnp.sum(pf[r * k:(r + 1) * k, :], axis=0, keepdims=True) for r in range(k)]
    cmap = jnp.zeros(s1.shape, dtype=F32)
    for r in range(k):
        cmap = jnp.where(rank1 == float(r), cnt[r], cmap)
    a_o[0] = jnp.where(rank1 < float(k), jnp.exp(s1 - v1[0]), 0.0) / z
    c_o[0] = cmap
    b_o[0] = jnp.where(rank2 < float(k), jnp.exp(s2 - v2[0]), 0.0)
    r_o[0] = rank2


def _peer_topk(scT, tm=256):
    n2, nk, s = scT.shape
    heads = n2 // 2
    out = jax.ShapeDtypeStruct((heads, nk, s), F32)
    ospec = pl.BlockSpec((1, nk, tm), lambda h, t: (h, 0, t))
    return pl.pallas_call(
        _peer_topk_kernel,
        grid=(heads, s // tm),
        in_specs=[pl.BlockSpec((2, nk, tm), lambda h, t: (h, 0, t))],
        out_specs=[ospec] * 4,
        out_shape=[out] * 4,
        compiler_params=_cparams(("parallel", "parallel")),
        name="peer_topk",
    )(scT)


_ERF_ALPHA = (-2.72614225801306e-10, 2.77068142495902e-08, -2.10102402082508e-06, -5.69250639462346e-05,
              -7.34990630326855e-04, -2.95459980854025e-03, -1.60960333262415e-02)
_ERF_BETA = (-1.45660718464996e-05, -2.13374055278905e-04, -1.68282697438203e-03, -7.37332916720468e-03,
             -1.42647390514189e-02)


def _gelu_exact(x):
    return 0.5 * x * (1.0 + lax.erf(x * (2.0 ** -0.5)))


def _peer_main_kernel(xT_ref, u_ref, vT_ref, a_ref, c_ref, b_ref, r_ref, o_ref, w_sc, *, ec):
    ci = pl.program_id(1)

    @pl.when(ci == 0)
    def _():
        o_ref[...] = jnp.zeros_like(o_ref)

    hid = jnp.dot(u_ref[...], xT_ref[...], preferred_element_type=F32)
    act = _gelu_exact(hid)
    nk = PEER_NKEYS
    for ii in range(ec // nk):
        i1 = ci * (ec // nk) + ii
        w = None
        for h in range(PEER_HEADS):
            a_row = a_ref[h, pl.ds(i1, 1), :]
            c_row = c_ref[h, pl.ds(i1, 1), :]
            term = jnp.where(r_ref[h] < c_row, b_ref[h], 0.0) * a_row
            w = term if w is None else w + term
        w_sc[ii * nk:(ii + 1) * nk, :] = (w * act[ii * nk:(ii + 1) * nk, :]).astype(BF16)
    o_ref[...] += jnp.dot(vT_ref[...], w_sc[...], preferred_element_type=F32)


def _peer_main(xnT, u_bf16, vT_bf16, a, c, b, r, tm=512, ec=1024):
    d, s = xnT.shape
    e = u_bf16.shape[0]
    heads, nk, _ = a.shape
    rt = pl.BlockSpec((heads, nk, tm), lambda t, ci: (0, 0, t))
    return pl.pallas_call(
        functools.partial(_peer_main_kernel, ec=ec),
        grid=(s // tm, e // ec),
        in_specs=[pl.BlockSpec((d, tm), lambda t, ci: (0, t)),
                  pl.BlockSpec((ec, d), lambda t, ci: (ci, 0)),
                  pl.BlockSpec((d, ec), lambda t, ci: (0, ci)),
                  rt, rt, rt, rt],
        out_specs=pl.BlockSpec((d, tm), lambda t, ci: (0, t)),
        out_shape=jax.ShapeDtypeStruct((d, s), F32),
        scratch_shapes=[pltpu.VMEM((ec, tm), BF16)],
        compiler_params=_cparams(("parallel", "arbitrary")),
        name="peer_main",
    )(xnT, u_bf16, vT_bf16, a, c, b, r)


def _ple_kernel(h_ref, peer_ref, p_ref, nw_ref, wg_ref, wp_ref, o_ref):
    h2 = h_ref[...] + peer_ref[...]
    xn = _rms(h2, nw_ref[...]).astype(BF16)
    gate = 1.0 / (1.0 + jnp.exp(-jnp.dot(xn, wg_ref[...], preferred_element_type=F32)))
    emb = jnp.dot(p_ref[...].astype(BF16), wp_ref[...], preferred_element_type=F32)
    o_ref[...] = h2 + gate * emb


def _ple(h, peer, p, nw, wg_bf16, wp_bf16, tm=512):
    s, d = h.shape
    pd = p.shape[1]
    row = lambda w: pl.BlockSpec((tm, w), lambda i: (i, 0))
    return pl.pallas_call(
        _ple_kernel,
        grid=(s // tm,),
        in_specs=[row(d), row(d), row(pd), _full((1, d)), _full((d, d)), _full((pd, d))],
        out_specs=row(d),
        out_shape=jax.ShapeDtypeStruct((s, d), F32),
        compiler_params=_cparams(("parallel",)),
        name="ple",
    )(h, peer, p, nw.reshape(1, d), wg_bf16, wp_bf16)


def _reorder_w_in(w):
    widths = (256, 256, 256, 4, 512, 128, 128, 128, 128, 128, 128, 24, 256, 256, 256)
    offs = np.concatenate([[0], np.cumsum(widths)])
    piece = lambda k: w[:, offs[k]:offs[k + 1]]
    order = [0, 1, 2, 4, 5, 6, 7, 8, 9, 10, 12, 13, 14, 3, 11]
    pad = jnp.zeros((w.shape[0], PROJ_W - int(offs[-1])), w.dtype)
    return jnp.concatenate([piece(k) for k in order] + [pad], axis=1)


def _overlap_T(s):
    n = np.arange(s // CMP_STRIDE)[None, :] * CMP_STRIDE
    m = np.arange(s // SLC_LEN)[:, None] * SLC_LEN
    return jnp.asarray(((n < m + SLC_LEN) & (n + CMP_LEN > m)).astype(np.float32))


def _layer(h, p_i, tabs, layer, attn_norm_w, w_in, fox_f_bias, fox_q_norm_w, fox_k_norm_w, nsa_q_norm_w,
           nsa_k_norm_w, nsa_cmp_pos, nsa_cmp_w, diff_q_norm_w, diff_k_norm_w, diff_lambda, diff_subln_w,
           w_out, ffn_norm_w, peer_w_q, peer_sub_keys, peer_u, peer_v, ple_norm_w, ple_w_gate, ple_w_proj):
    s = h.shape[0]
    proj = _norm_matmul(h, attn_norm_w, _reorder_w_in(w_in).astype(BF16))
    fb = jnp.zeros((1, LANES), F32).at[0, :FOX_HEADS].set(fox_f_bias)
    (fq, fk, fv, chi, cmid, clo, nqn, nqr, kvc, ksr, vs, kwr, vw, gates, dq, dk, dv) = _prep(
        proj, tabs, fb, fox_q_norm_w, fox_k_norm_w, nsa_q_norm_w, nsa_k_norm_w, diff_q_norm_w, diff_k_norm_w)

    tq_f = 512
    one = jnp.ones((s, FOX_HEADS, 3), BF16)
    csplit = jnp.stack([chi[:, :FOX_HEADS], cmid[:, :FOX_HEADS], clo[:, :FOX_HEADS]], axis=-1)
    zpad = jnp.zeros((s, FOX_HEADS, LANES - FOX_HD - 6), BF16)
    q_ext = jnp.concatenate([fq.reshape(s, FOX_HEADS, FOX_HD), one, csplit, zpad], axis=-1)
    k_ext = jnp.concatenate([fk.reshape(s, FOX_HEADS, FOX_HD), -csplit, one, zpad], axis=-1)
    o_fox = _flash(_to_qT(q_ext.reshape(s, -1), FOX_HEADS, 1, tq_f), k_ext.transpose(1, 0, 2),
                   _to_headsT(fv, FOX_HEADS), groups=1, tq=tq_f, tk=TK_CAUSAL)
    o_fox = _from_oT(o_fox, 1, tq_f)

    tq_n = 128
    kvr = kvc.reshape(s // CMP_STRIDE, CMP_STRIDE, 2, NSA_KV, NSA_HD).transpose(2, 3, 0, 1, 4)
    kvr = kvr.reshape(2, NSA_KV, s // CMP_STRIDE, CMP_STRIDE * NSA_HD)
    posf = nsa_cmp_pos.reshape(2, 1, CMP_LEN * NSA_HD)
    kcc = _compress(kvr[0], nsa_cmp_w[0], posf[0], nsa_k_norm_w, True)
    vcc = _compress(kvr[1], nsa_cmp_w[1], posf[1], nsa_k_norm_w, False)
    o_c, sel = _nsa_cmp(_to_qT(nqn, NSA_HEADS, NSA_G, tq_n), kcc, vcc.transpose(0, 2, 1), _overlap_T(s), tq_n)
    qrT = _to_qT(nqr, NSA_HEADS, NSA_G, tq_n)
    o_s = _flash(qrT, _to_heads(ksr, NSA_KV), _to_headsT(vs, NSA_KV), sel, groups=NSA_G, tq=tq_n,
                tk=TK_CAUSAL)
    o_w = _flash(qrT, _to_heads(kwr, NSA_KV), _to_headsT(vw, NSA_KV), groups=NSA_G, tq=tq_n, tk=512, window=WIN)
    o_c, o_s, o_w = (_from_oT(o, NSA_G, tq_n) for o in (o_c, o_s, o_w))
    g3 = gates[:, FOX_HEADS:FOX_HEADS + 3 * NSA_HEADS].reshape(s, NSA_HEADS, 3)
    gc, gs, gw = (jnp.repeat(g3[:, :, b], NSA_HD, axis=1) for b in range(3))

    tq_d = 256
    dq4 = dq.reshape(s, DIFF_HEADS, 2, DIFF_QK)
    zq = jnp.zeros((s, DIFF_HEADS, DIFF_QK), BF16)
    q_maps = jnp.stack([jnp.concatenate([dq4[:, :, 0], zq], axis=-1),
                        jnp.concatenate([zq, dq4[:, :, 1]], axis=-1)], axis=2)
    o_d = _flash(_to_qT(q_maps.reshape(s, -1), 2 * DIFF_HEADS, 2, tq_d), _to_heads(dk, DIFF_HEADS),
                 _to_headsT(dv, DIFF_HEADS), groups=2, tq=tq_d, tk=TK_CAUSAL)
    o_d = _from_oT(o_d, 2, tq_d).reshape(s, DIFF_HEADS, 2, DIFF_V)
    d0 = o_d[:, :, 0].reshape(s, -1)
    d1 = o_d[:, :, 1].reshape(s, -1)
    lv = diff_lambda.astype(F32)
    lam_init = 0.8 - 0.6 * math.exp(-0.3 * layer)
    lam = (jnp.exp(jnp.sum(lv[0] * lv[1])) - jnp.exp(jnp.sum(lv[2] * lv[3])) + lam_init).reshape(1)
    sw = jnp.tile(diff_subln_w.reshape(1, -1), (1, DIFF_HEADS))
    h1 = _mix_out(lam, h, o_fox, o_c, o_s, o_w, gc, gs, gw, d0, d1, sw, w_out.astype(BF16), 1.0 - lam_init)

    keysT = peer_sub_keys.reshape(2 * PEER_HEADS, PEER_NKEYS, PEER_DQ // 2).transpose(0, 2, 1).astype(BF16)
    xn, sc = _peer_query(h1, ffn_norm_w, peer_w_q.astype(BF16), keysT)
    scT = sc.T.reshape(2 * PEER_HEADS, PEER_NKEYS, s)
    a, c, b, r = _peer_topk(scT)
    peerT = _peer_main(xn.T, peer_u.astype(BF16), peer_v.T.astype(BF16), a, c, b, r)

    return _ple(h1, peerT.T, p_i, ple_norm_w, ple_w_gate.astype(BF16), ple_w_proj.astype(BF16))


def kernel(x, p, positions, attn_norm_w, w_in, fox_f_bias, fox_q_norm_w, fox_k_norm_w, nsa_q_norm_w, nsa_k_norm_w,
           nsa_cmp_pos, nsa_cmp_w, diff_q_norm_w, diff_k_norm_w, diff_lambda, diff_subln_w, w_out, ffn_norm_w,
           peer_w_q, peer_sub_keys, peer_u, peer_v, ple_norm_w, ple_w_gate, ple_w_proj):
    b, s, d = x.shape
    assert b == 1 and d == D_MODEL and s % 512 == 0
    tabs = _rope_tables(positions)
    h = x.reshape(s, d)
    per_layer = (attn_norm_w, w_in, fox_f_bias, fox_q_norm_w, fox_k_norm_w, nsa_q_norm_w, nsa_k_norm_w, nsa_cmp_pos,
                 nsa_cmp_w, diff_q_norm_w, diff_k_norm_w, diff_lambda, diff_subln_w, w_out, ffn_norm_w, peer_w_q,
                 peer_sub_keys, peer_u, peer_v, ple_norm_w, ple_w_gate, ple_w_proj)
    for layer in range(attn_norm_w.shape[0]):
        h = _layer(h, p[layer, 0], tabs, layer, *(w[layer] for w in per_layer))
    return h.reshape(b, s, d)
```

```python
import functools
import math

import numpy as np
import jax
import jax.numpy as jnp
from jax import lax
from jax.experimental import pallas as pl
from jax.experimental.pallas import tpu as pltpu

F32 = jnp.float32
BF16 = jnp.bfloat16
HIGHEST = lax.Precision.HIGHEST

D_MODEL = 1024
PLE_DIM = 256
ROPE_THETA = 10000.0
EPS = 1e-6
FOX_HEADS, FOX_HD = 4, 64
NSA_HEADS, NSA_KV, NSA_HD = 8, 2, 64
NSA_G = NSA_HEADS // NSA_KV
CMP_LEN, CMP_STRIDE, SLC_LEN, SLC_TOPK, WIN = 32, 16, 64, 16, 512
FORCE_BONUS = 1.0e4
DIFF_HEADS, DIFF_QK, DIFF_V = 4, 32, 64
PEER_HEADS, PEER_NKEYS, PEER_DQ, PEER_TOPK = 8, 128, 256, 16
PEER_EXPERTS = PEER_NKEYS * PEER_NKEYS

LANES = 128
LOG2E = math.log2(math.e)
NEG_INIT = -1.0e30
MASKVAL = -2.0e30
VMEM_LIMIT = 56 * 1024 * 1024
TK_CAUSAL = 1024

_SEG = dict(fq=(0, 256), fk=(256, 256), fv=(512, 256), nq=(768, 512), nkc=(1280, 128), nvc=(1408, 128),
            nks=(1536, 128), nvs=(1664, 128), nkw=(1792, 128), nvw=(1920, 128),
            dq=(2048, 256), dk=(2304, 256), dv=(2560, 256), misc=(2816, 128))
PROJ_W = 2944


def _cparams(sem):
    return pltpu.CompilerParams(dimension_semantics=sem, vmem_limit_bytes=VMEM_LIMIT)


def _full(shape):
    n = len(shape)
    return pl.BlockSpec(shape, lambda *_: (0,) * n)


def _rope_tab_kernel(pos_ref, f64_ref, g64_ref, f32_ref, g32_ref, c64_o, s64_o, c32_o, s32_o):
    pos = pos_ref[...].astype(F32)
    a64 = pos * f64_ref[...]
    c64_o[...] = jnp.cos(a64)
    s64_o[...] = jnp.sin(a64) * g64_ref[...]
    a32 = pos * f32_ref[...]
    c32_o[...] = jnp.cos(a32)
    s32_o[...] = jnp.sin(a32) * g32_ref[...]


def _rope_tables(positions):
    s = positions.shape[-1]
    pos = positions.reshape(s, 1)
    lane = np.arange(LANES)

    def lanes(half):
        inv = ROPE_THETA ** (-jnp.arange(half, dtype=F32) / half)
        freq = inv[(lane % (2 * half)) % half].reshape(1, LANES)
        sign = np.where((lane % (2 * half)) < half, -1.0, 1.0).astype(np.float32).reshape(1, LANES)
        return freq, jnp.asarray(sign)

    f64, g64 = lanes(NSA_HD // 2)
    f32_, g32 = lanes(DIFF_QK // 2)
    tm = 512
    out = jax.ShapeDtypeStruct((s, LANES), F32)
    row = pl.BlockSpec((tm, LANES), lambda i: (i, 0))
    return pl.pallas_call(
        _rope_tab_kernel,
        grid=(s // tm,),
        in_specs=[pl.BlockSpec((tm, 1), lambda i: (i, 0))] + [_full((1, LANES))] * 4,
        out_specs=[row] * 4,
        out_shape=[out] * 4,
        compiler_params=_cparams(("parallel",)),
        name="rope_tables",
    )(pos, f64, g64, f32_, g32)


def _rms(x, w):
    return x * lax.rsqrt(jnp.mean(x * x, axis=-1, keepdims=True) + EPS) * w


def _norm_matmul_kernel(x_ref, nw_ref, w_ref, o_ref):
    xn = _rms(x_ref[...], nw_ref[...])
    o_ref[...] = jnp.dot(xn.astype(BF16), w_ref[...], preferred_element_type=F32)


def _norm_matmul(x, nw, w_bf16, tm=512):
    s, d = x.shape
    n = w_bf16.shape[1]
    return pl.pallas_call(
        _norm_matmul_kernel,
        grid=(s // tm,),
        in_specs=[pl.BlockSpec((tm, d), lambda i: (i, 0)), _full((1, d)), _full((d, n))],
        out_specs=pl.BlockSpec((tm, n), lambda i: (i, 0)),
        out_shape=jax.ShapeDtypeStruct((s, n), F32),
        compiler_params=_cparams(("parallel",)),
        name="norm_matmul",
    )(x, nw.reshape(1, d), w_bf16)


def _seg_rms(x, bmat, seg):
    outs = []
    for c0 in range(0, x.shape[1], 256):
        w = min(256, x.shape[1] - c0)
        xc = x[:, c0:c0 + w]
        ss = jnp.dot(xc * xc, bmat[:w, :w], precision=HIGHEST, preferred_element_type=F32)
        outs.append(xc * lax.rsqrt(ss * (1.0 / seg) + EPS))
    return outs[0] if len(outs) == 1 else jnp.concatenate(outs, axis=1)


def _tile_lanes(t, width):
    reps = width // LANES
    return t if reps == 1 else jnp.concatenate([t] * reps, axis=1)


def _rope(x, cos, sin, half):
    width = x.shape[1]
    left = pltpu.roll(x, width - half, 1)
    right = pltpu.roll(x, half, 1)
    lane = lax.broadcasted_iota(jnp.int32, x.shape, 1)
    swapped = jnp.where((lane & (2 * half - 1)) < half, left, right)
    return x * _tile_lanes(cos, width) + swapped * _tile_lanes(sin, width)


def _prep_kernel(proj_ref, c64_ref, s64_ref, c32_ref, s32_ref, fb_ref, wfq_ref, wfk_ref, wnq_ref, wnk_ref,
                 wdq_ref, wdk_ref, b64_ref, b32_ref,
                 fq_o, fk_o, fv_o, chi_o, cmid_o, clo_o, nqn_o, nqr_o, kvc_o, ksr_o, vs_o, kwr_o, vw_o,
                 gate_o, dq_o, dk_o, dv_o, carry_sc, *, tm):
    def seg(name):
        c0, w = _SEG[name]
        return proj_ref[:, c0:c0 + w]

    b64 = b64_ref[...]
    b32 = b32_ref[...]
    c64, s64, c32, s32 = c64_ref[...], s64_ref[...], c32_ref[...], s32_ref[...]

    fq_o[...] = (_seg_rms(seg("fq"), b64, FOX_HD) * wfq_ref[...] * (FOX_HD ** -0.5 * LOG2E)).astype(BF16)
    fk_o[...] = (_seg_rms(seg("fk"), b64, FOX_HD) * wfk_ref[...]).astype(BF16)
    fv_o[...] = seg("fv").astype(BF16)

    misc = seg("misc")
    gate_o[...] = 1.0 / (1.0 + jnp.exp(-misc))
    t = misc + fb_ref[...]
    logf = jnp.minimum(t, 0.0) - jnp.log1p(jnp.exp(-jnp.abs(t)))

    @pl.when(pl.program_id(0) == 0)
    def _():
        carry_sc[...] = jnp.zeros_like(carry_sc)

    r = lax.broadcasted_iota(jnp.int32, (tm, tm), 0)
    c = lax.broadcasted_iota(jnp.int32, (tm, tm), 1)
    tri = (c <= r).astype(F32)
    csum = jnp.dot(tri, logf, precision=HIGHEST, preferred_element_type=F32) + carry_sc[0:1, :]
    carry_sc[...] = jnp.broadcast_to(csum[tm - 1:tm, :], carry_sc.shape)
    csum = csum * LOG2E
    hi = csum.astype(BF16)
    r1 = csum - hi.astype(F32)
    mid = r1.astype(BF16)
    lo = (r1 - mid.astype(F32)).astype(BF16)
    chi_o[...] = hi
    cmid_o[...] = mid
    clo_o[...] = lo

    nqn = _seg_rms(seg("nq"), b64, NSA_HD) * wnq_ref[...]
    nqn_o[...] = (nqn * (NSA_HD ** -0.5 * LOG2E)).astype(BF16)
    nqr_o[...] = (_rope(nqn, c64, s64, NSA_HD // 2) * (NSA_HD ** -0.5 * LOG2E)).astype(BF16)
    kvc_o[:, 0:128] = seg("nkc").astype(BF16)
    kvc_o[:, 128:256] = seg("nvc").astype(BF16)
    wnk = wnk_ref[...]
    ksr_o[...] = _rope(_seg_rms(seg("nks"), b64, NSA_HD) * wnk, c64, s64, NSA_HD // 2).astype(BF16)
    vs_o[...] = seg("nvs").astype(BF16)
    kwr_o[...] = _rope(_seg_rms(seg("nkw"), b64, NSA_HD) * wnk, c64, s64, NSA_HD // 2).astype(BF16)
    vw_o[...] = seg("nvw").astype(BF16)

    dqn = _rope(_seg_rms(seg("dq"), b32, DIFF_QK) * wdq_ref[...], c32, s32, DIFF_QK // 2)
    dq_o[...] = (dqn * (DIFF_QK ** -0.5 * LOG2E)).astype(BF16)
    dk_o[...] = _rope(_seg_rms(seg("dk"), b32, DIFF_QK) * wdk_ref[...], c32, s32, DIFF_QK // 2).astype(BF16)
    dv_o[...] = seg("dv").astype(BF16)


def _block_diag_ones(n, seg):
    i = np.arange(n)
    return jnp.asarray((i[:, None] // seg == i[None, :] // seg).astype(np.float32))


def _prep(proj, tabs, fb, wfq, wfk, wnq, wnk, wdq, wdk, tm=256):
    s = proj.shape[0]
    c64, s64, c32, s32 = tabs
    b64 = _block_diag_ones(256, 64)
    b32 = _block_diag_ones(256, 32)

    def tiled(w, width):
        return jnp.tile(w.reshape(1, -1), (1, width // w.shape[-1]))

    consts = [fb, tiled(wfq, 256), tiled(wfk, 256), tiled(wnq, 512), tiled(wnk, 128), tiled(wdq, 256),
              tiled(wdk, 256), b64, b32]
    widths = [(256, BF16), (256, BF16), (256, BF16), (128, BF16), (128, BF16), (128, BF16), (512, BF16),
              (512, BF16), (256, BF16), (128, BF16), (128, BF16), (128, BF16), (128, BF16), (128, F32),
              (256, BF16), (256, BF16), (256, BF16)]
    row = lambda w: pl.BlockSpec((tm, w), lambda i: (i, 0))
    return pl.pallas_call(
        functools.partial(_prep_kernel, tm=tm),
        grid=(s // tm,),
        in_specs=[row(PROJ_W)] + [row(LANES)] * 4 + [_full(c.shape) for c in consts],
        out_specs=[row(w) for w, _ in widths],
        out_shape=[jax.ShapeDtypeStruct((s, w), dt) for w, dt in widths],
        scratch_shapes=[pltpu.VMEM((8, LANES), F32)],
        compiler_params=_cparams(("arbitrary",)),
        name="head_prep",
    )(proj, c64, s64, c32, s32, *consts)


QK_DIM = 128
COL_SHIFT = 64
COL_SEL = 80
SCORE_BOUND = 60.0
SHIFT_HEADROOM = 60.0


def _flash_kernel(code_ref, qT_ref, k_ref, vT_ref, *rest, groups, tq, tk, window, has_sel, bounded):
    rest = list(rest)
    sel_ref = rest.pop(0) if has_sel else None
    o_ref = rest.pop(0)
    m_sc = None if bounded else rest.pop(0)
    acc_sc = rest.pop(0)
    code = code_ref[pl.program_id(1)]
    i = code & 0xFFF
    j = (code >> 12) & 0xFFF
    first = (code >> 24) & 1
    last = (code >> 25) & 1
    rows = groups * tq
    dv = o_ref.shape[2]

    @pl.when(first == 1)
    def _():
        if not bounded:
            m_sc[...] = jnp.full_like(m_sc, NEG_INIT)
        acc_sc[...] = jnp.zeros_like(acc_sc)

    def step(masked):
        q = qT_ref[0, 0]
        if has_sel:
            nb = tk // SLC_LEN
            q = jnp.concatenate([q[0:COL_SEL], _tile_lanes_any(sel_ref[0], groups), q[COL_SEL + nb:]], axis=0)
        s = jnp.dot(k_ref[0], q, preferred_element_type=F32)
        if masked:
            kpos = j * tk + lax.broadcasted_iota(jnp.int32, (tk, rows), 0)
            lane = lax.broadcasted_iota(jnp.int32, (tk, rows), 1)
            qpos = i * tq + (lane & (tq - 1))
            keep = kpos <= qpos
            if window is not None:
                keep = keep & (kpos > qpos - window)
            s = jnp.where(keep, s, MASKVAL)
        if bounded:
            acc_sc[...] += jnp.dot(vT_ref[0], jnp.exp2(s).astype(BF16), preferred_element_type=F32)
        else:
            m_prev = m_sc[...]
            m_new = jnp.maximum(m_prev, jnp.max(s, axis=0, keepdims=True))
            p = jnp.exp2(s - m_new).astype(BF16)
            acc_sc[...] = acc_sc[...] * jnp.exp2(m_prev - m_new) + jnp.dot(vT_ref[0], p, preferred_element_type=F32)
            m_sc[...] = m_new

    if window is not None:
        step(True)
    else:
        needs_mask = (j + 1) * tk - 1 > i * tq

        @pl.when(needs_mask)
        def _():
            step(True)

        @pl.when(jnp.logical_not(needs_mask))
        def _():
            step(False)

    @pl.when(last == 1)
    def _():
        l = acc_sc[dv:dv + 1, :]
        o_ref[0, 0] = acc_sc[0:dv, :] / jnp.where(l > 0.0, l, 1.0)


def _tile_lanes_any(t, reps):
    return t if reps == 1 else jnp.concatenate([t] * reps, axis=1)


def _flash_call(qT, k, vT, sel, *, groups, tq, tk, window, bounded):
    hkv, nq, dk, rows = qT.shape
    dve = vT.shape[1]
    dv = dve - 8
    codes = []
    for i in range(nq):
        q_lo, q_hi = i * tq, i * tq + tq - 1
        j_hi = q_hi // tk
        j_lo = 0 if window is None else max(0, (q_lo - window + 1) // tk)
        for j in range(j_lo, j_hi + 1):
            codes.append(i | (j << 12) | (int(j == j_lo) << 24) | (int(j == j_hi) << 25))
    codes = jnp.asarray(np.asarray(codes, dtype=np.int32))
    nsteps = codes.shape[0]

    def ti(c, s):
        return c[s] & 0xFFF

    def tj(c, s):
        return (c[s] >> 12) & 0xFFF

    in_specs = [
        pl.BlockSpec((1, 1, dk, rows), lambda h, s, c: (h, ti(c, s), 0, 0)),
        pl.BlockSpec((1, tk, dk), lambda h, s, c: (h, tj(c, s), 0)),
        pl.BlockSpec((1, dve, tk), lambda h, s, c: (h, 0, tj(c, s))),
    ]
    args = [qT, k, vT]
    if sel is not None:
        in_specs.append(pl.BlockSpec((1, tk // SLC_LEN, tq), lambda h, s, c: (h, tj(c, s), ti(c, s))))
        args.append(sel)
    scratch = [pltpu.VMEM((dve, rows), F32)]
    if not bounded:
        scratch = [pltpu.VMEM((1, rows), F32)] + scratch
    kern = functools.partial(_flash_kernel, groups=groups, tq=tq, tk=tk, window=window, has_sel=sel is not None,
                             bounded=bounded)
    return pl.pallas_call(
        kern,
        grid_spec=pltpu.PrefetchScalarGridSpec(
            num_scalar_prefetch=1,
            grid=(hkv, nsteps),
            in_specs=in_specs,
            out_specs=pl.BlockSpec((1, 1, dv, rows), lambda h, s, c: (h, ti(c, s), 0, 0)),
            scratch_shapes=scratch,
        ),
        out_shape=jax.ShapeDtypeStruct((hkv, nq, dv, rows), F32),
        compiler_params=_cparams(("parallel", "arbitrary")),
        name="flash_bounded" if bounded else "flash_online",
    )(codes, *args)


def _flash(qT, k, vT, sel=None, *, bound_ok, groups, tq, tk, window=None):
    hkv, dv, s_len = vT.shape
    vT = jnp.concatenate([vT, jnp.ones((hkv, 1, s_len), vT.dtype), jnp.zeros((hkv, 7, s_len), vT.dtype)], axis=1)
    call = functools.partial(_flash_call, qT, k, vT, sel, groups=groups, tq=tq, tk=tk, window=window)
    return lax.cond(bound_ok, lambda: call(bounded=True), lambda: call(bounded=False))


def _extend(x, heads, cols):
    s = x.shape[0]
    x = x.reshape(s, heads, -1)
    pieces, at = [x], x.shape[-1]
    for c in sorted(cols):
        if c > at:
            pieces.append(jnp.zeros((s, heads, c - at), BF16))
        v = cols[c]
        v = jnp.broadcast_to(jnp.asarray(v, F32).astype(BF16), (s, heads)) if jnp.ndim(v) == 0 else v.astype(BF16)
        pieces.append(v[:, :, None])
        at = c + 1
    pieces.append(jnp.zeros((s, heads, QK_DIM - at), BF16))
    return jnp.concatenate(pieces, axis=-1)


def _score_bound(wq, wk, seg, scale):
    return 1.02 * seg * jnp.max(jnp.abs(wq)) * jnp.max(jnp.abs(wk)) * scale * LOG2E


def _to_qT(q, heads, groups, tq):
    s = q.shape[0]
    d = q.shape[1] // heads
    x = q.reshape(s // tq, tq, heads // groups, groups, d)
    return x.transpose(2, 0, 4, 3, 1).reshape(heads // groups, s // tq, d, groups * tq)


def _from_oT(oT, groups, tq):
    hkv, nq, d, _ = oT.shape
    x = oT.reshape(hkv, nq, d, groups, tq).transpose(1, 4, 0, 3, 2)
    return x.reshape(nq * tq, hkv * groups * d)


def _to_heads(x, heads):
    s = x.shape[0]
    return x.reshape(s, heads, -1).transpose(1, 0, 2)


def _to_headsT(x, heads):
    s = x.shape[0]
    return x.reshape(s, heads, -1).transpose(1, 2, 0)


def _compress_kernel(r_ref, rn_ref, w_ref, pos_ref, nw_ref, o_ref, *, normalize):
    half = r_ref.shape[2]
    wa = w_ref[0, 0:half, :].astype(BF16)
    wb = w_ref[0, half:2 * half, :].astype(BF16)
    const = jnp.dot(pos_ref[0], w_ref[0], precision=HIGHEST, preferred_element_type=F32)
    out = (jnp.dot(r_ref[0], wa, preferred_element_type=F32)
           + jnp.dot(rn_ref[0], wb, preferred_element_type=F32) + const)
    if normalize:
        out = _rms(out, nw_ref[...])
    o_ref[0] = out.astype(BF16)


def _compress(r, w, pos, nw, normalize):
    g, n, width = r.shape
    hd = w.shape[-1]
    r_next = jnp.concatenate([r[:, 1:], jnp.zeros((g, 1, width), r.dtype)], axis=1)
    rspec = pl.BlockSpec((1, n, width), lambda i: (i, 0, 0))
    return pl.pallas_call(
        functools.partial(_compress_kernel, normalize=normalize),
        grid=(g,),
        in_specs=[rspec, rspec, _full((1,) + w.shape), _full((1,) + pos.shape), _full((1, hd))],
        out_specs=pl.BlockSpec((1, n, hd), lambda i: (i, 0, 0)),
        out_shape=jax.ShapeDtypeStruct((g, n, hd), BF16),
        compiler_params=_cparams(("parallel",)),
        name="nsa_compress",
    )(r, r_next, w[None], pos[None], nw.reshape(1, hd))


def _nsa_cmp_kernel(qT_ref, kc_ref, vcT_ref, ov_ref, o_ref, sel_ref, *, tq, groups):
    i = pl.program_id(1)
    ncmp = kc_ref.shape[1]
    rows = groups * tq
    s = jnp.dot(kc_ref[0], qT_ref[0, 0], preferred_element_type=F32)
    n_idx = lax.broadcasted_iota(jnp.int32, (ncmp, rows), 0)
    lane = lax.broadcasted_iota(jnp.int32, (ncmp, rows), 1)
    qpos = i * tq + (lane & (tq - 1))
    keep = n_idx * CMP_STRIDE + (CMP_LEN - 1) <= qpos
    sm = jnp.where(keep, s, NEG_INIT)
    m = jnp.max(sm, axis=0, keepdims=True)
    e = jnp.where(keep, jnp.exp2(sm - m), 0.0)
    den = jnp.sum(e, axis=0, keepdims=True)
    p = e / jnp.where(den > 0.0, den, 1.0)
    o_ref[0, 0] = jnp.dot(vcT_ref[0], p.astype(BF16), preferred_element_type=F32)

    psum = p[:, 0:tq]
    for g in range(1, groups):
        psum = psum + p[:, g * tq:(g + 1) * tq]
    imp = jnp.dot(ov_ref[...], psum, precision=HIGHEST, preferred_element_type=F32)
    ns = imp.shape[0]
    blk = lax.broadcasted_iota(jnp.int32, (ns, tq), 0)
    qp = i * tq + lax.broadcasted_iota(jnp.int32, (ns, tq), 1)
    cur = qp // SLC_LEN
    valid = blk <= cur
    forced = (blk == 0) | (blk == cur) | (blk == cur - 1)
    work = jnp.where(valid, imp + jnp.where(forced, FORCE_BONUS, 0.0), NEG_INIT)
    picked = jnp.zeros((ns, tq), dtype=jnp.bool_)
    for _ in range(SLC_TOPK):
        mx = jnp.max(work, axis=0, keepdims=True)
        idx = jnp.min(jnp.where(work == mx, blk, ns), axis=0, keepdims=True)
        pick = (blk == idx) & (mx > 0.5 * NEG_INIT)
        picked = picked | pick
        work = jnp.where(pick, NEG_INIT, work)
    sel_ref[0] = jnp.where(picked, 0.0, MASKVAL).astype(BF16)


def _nsa_cmp(qnT, kcc, vccT, overlapT, tq):
    hkv, nq, hd, rows = qnT.shape
    ncmp = kcc.shape[1]
    ns = overlapT.shape[0]
    s = nq * tq
    return pl.pallas_call(
        functools.partial(_nsa_cmp_kernel, tq=tq, groups=rows // tq),
        grid=(hkv, nq),
        in_specs=[pl.BlockSpec((1, 1, hd, rows), lambda h, i: (h, i, 0, 0)),
                  pl.BlockSpec((1, ncmp, hd), lambda h, i: (h, 0, 0)),
                  pl.BlockSpec((1, hd, ncmp), lambda h, i: (h, 0, 0)),
                  _full(overlapT.shape)],
        out_specs=[pl.BlockSpec((1, 1, hd, rows), lambda h, i: (h, i, 0, 0)),
                   pl.BlockSpec((1, ns, tq), lambda h, i: (h, 0, i))],
        out_shape=[jax.ShapeDtypeStruct((hkv, nq, hd, rows), F32), jax.ShapeDtypeStruct((hkv, ns, s), BF16)],
        compiler_params=_cparams(("parallel", "parallel")),
        name="nsa_cmp_select",
    )(qnT, kcc, vccT, overlapT)


def _mix_out_kernel(lam_ref, h_ref, fox_ref, oc_ref, os_ref, ow_ref, gc_ref, gs_ref, gw_ref, d0_ref, d1_ref,
                    sw_ref, b64_ref, w_ref, o_ref, *, diff_scale):
    lam = lam_ref[0]
    nsa = gc_ref[...] * oc_ref[...] + gs_ref[...] * os_ref[...] + gw_ref[...] * ow_ref[...]
    a = d0_ref[...] - lam * d1_ref[...]
    ss = jnp.dot(a * a, b64_ref[...], precision=HIGHEST, preferred_element_type=F32)
    diff = a * lax.rsqrt(ss * (1.0 / DIFF_V) + EPS) * sw_ref[...] * diff_scale
    nf = FOX_HEADS * FOX_HD
    nn = NSA_HEADS * NSA_HD
    acc = h_ref[...]
    acc = acc + jnp.dot(fox_ref[...].astype(BF16), w_ref[0:nf, :], preferred_element_type=F32)
    acc = acc + jnp.dot(nsa.astype(BF16), w_ref[nf:nf + nn, :], preferred_element_type=F32)
    acc = acc + jnp.dot(diff.astype(BF16), w_ref[nf + nn:, :], preferred_element_type=F32)
    o_ref[...] = acc


def _mix_out(lam, h, fox, oc, os_, ow, gc, gs, gw, d0, d1, sw, w_out_bf16, diff_scale, tm=512):
    s, d = h.shape
    row = lambda w: pl.BlockSpec((tm, w), lambda i: (i, 0))
    b64 = _block_diag_ones(256, 64)
    return pl.pallas_call(
        functools.partial(_mix_out_kernel, diff_scale=diff_scale),
        grid=(s // tm,),
        in_specs=[pl.BlockSpec(memory_space=pltpu.SMEM), row(d), row(256)] + [row(512)] * 6 + [row(256), row(256),
                  _full((1, 256)), _full((256, 256)), _full(w_out_bf16.shape)],
        out_specs=row(d),
        out_shape=jax.ShapeDtypeStruct((s, d), F32),
        compiler_params=_cparams(("parallel",)),
        name="mix_out",
    )(lam, h, fox, oc, os_, ow, gc, gs, gw, d0, d1, sw, b64, w_out_bf16)


def _peer_query_kernel(h_ref, nw_ref, wq_ref, keys_ref, xn_o, sc_o):
    xn = _rms(h_ref[...], nw_ref[...]).astype(BF16)
    xn_o[...] = xn
    q = jnp.dot(xn, wq_ref[...], preferred_element_type=F32).astype(BF16)
    half = PEER_DQ // 2
    for b in range(2 * PEER_HEADS):
        sc_o[:, b * PEER_NKEYS:(b + 1) * PEER_NKEYS] = jnp.dot(
            q[:, b * half:(b + 1) * half], keys_ref[b], preferred_element_type=F32)


def _peer_query(h, nw, wq_bf16, keysT_bf16, tm=512):
    s, d = h.shape
    n = wq_bf16.shape[1]
    nsc = 2 * PEER_HEADS * PEER_NKEYS
    return pl.pallas_call(
        _peer_query_kernel,
        grid=(s // tm,),
        in_specs=[pl.BlockSpec((tm, d), lambda i: (i, 0)), _full((1, d)), _full((d, n)), _full(keysT_bf16.shape)],
        out_specs=[pl.BlockSpec((tm, d), lambda i: (i, 0)), pl.BlockSpec((tm, nsc), lambda i: (i, 0))],
        out_shape=[jax.ShapeDtypeStruct((s, d), BF16), jax.ShapeDtypeStruct((s, nsc), F32)],
        compiler_params=_cparams(("parallel",)),
        name="peer_query",
    )(h, nw.reshape(1, d), wq_bf16, keysT_bf16)


def _top16_rows(s):
    n = s.shape[0]
    row = lax.broadcasted_iota(jnp.int32, s.shape, 0)
    rank = jnp.full(s.shape, float(PEER_TOPK), dtype=F32)
    work = s
    vals = []
    for r in range(PEER_TOPK):
        mx = jnp.max(work, axis=0, keepdims=True)
        idx = jnp.min(jnp.where(work == mx, row, n), axis=0, keepdims=True)
        pick = row == idx
        rank = jnp.where(pick, float(r), rank)
        work = jnp.where(pick, NEG_INIT, work)
        vals.append(mx)
    return rank, vals


def _peer_topk_kernel(sc_ref, a_o, c_o, b_o, r_o):
    s1 = sc_ref[0]
    s2 = sc_ref[1]
    rank1, v1 = _top16_rows(s1)
    rank2, v2 = _top16_rows(s2)
    k = PEER_TOPK
    tm = s1.shape[1]
    r16 = lax.broadcasted_iota(jnp.int32, (k, tm), 0)
    v1m = jnp.zeros((k, tm), F32)
    v2m = jnp.zeros((k, tm), F32)
    for r in range(k):
        v1m = jnp.where(r16 == r, v1[r], v1m)
        v2m = jnp.where(r16 == r, v2[r], v2m)
    ea = jnp.exp(v1m - v1[0])
    eb = jnp.exp(v2m - v2[0])
    cands, gates = [v1m[0:1] + v2m], [ea[0:1] * eb]
    for r in range(1, 8):
        cands.append(v1m[r:r + 1] + v2m[0:8])
        gates.append(ea[r:r + 1] * eb[0:8])
    cands.append(v1m[8:16] + v2m[0:1])
    gates.append(ea[8:16] * eb[0:1])
    cand = jnp.concatenate(cands, axis=0)
    gate = jnp.concatenate(gates, axis=0)
    ncand = cand.shape[0]
    crow = lax.broadcasted_iota(jnp.int32, cand.shape, 0)
    for r in range(2, 8):
        start = 16 + 8 * (r - 1)
        cand = jnp.where((crow >= start + k // (r + 1)) & (crow < start + 8), NEG_INIT, cand)
    work = cand
    picked = jnp.zeros(cand.shape, dtype=jnp.bool_)
    for _ in range(k):
        mx = jnp.max(work, axis=0, keepdims=True)
        idx = jnp.min(jnp.where(work == mx, crow, ncand), axis=0, keepdims=True)
        pick = crow == idx
        picked = picked | pick
        work = jnp.where(pick, NEG_INIT, work)
    pf = picked.astype(F32)
    z = jnp.sum(pf * gate, axis=0, keepdims=True)
    cnt = [jnp.sum(pf[0:16, :], axis=0, keepdims=True)]
    cnt += [jnp.sum(pf[8 + 8 * r:16 + 8 * r, :], axis=0, keepdims=True) for r in range(1, 8)]
    cnt += [pf[64 + r:65 + r, :] for r in range(8, 16)]
    cmap = jnp.zeros(s1.shape, dtype=F32)
    for r in range(k):
        cmap = jnp.where(rank1 == float(r), cnt[r], cmap)
    a_o[0] = jnp.where(rank1 < float(k), jnp.exp(s1 - v1[0]), 0.0) / z
    c_o[0] = cmap
    b_o[0] = jnp.where(rank2 < float(k), jnp.exp(s2 - v2[0]), 0.0).astype(BF16)
    r_o[0] = rank2.astype(BF16)


def _peer_topk(scT, tm=256):
    n2, nk, s = scT.shape
    heads = n2 // 2
    ospec = pl.BlockSpec((1, nk, tm), lambda h, t: (h, 0, t))
    return pl.pallas_call(
        _peer_topk_kernel,
        grid=(heads, s // tm),
        in_specs=[pl.BlockSpec((2, nk, tm), lambda h, t: (h, 0, t))],
        out_specs=[ospec] * 4,
        out_shape=[jax.ShapeDtypeStruct((heads, nk, s), dt) for dt in (F32, F32, BF16, BF16)],
        compiler_params=_cparams(("parallel", "parallel")),
        name="peer_topk",
    )(scT)


_ERF_ALPHA = (-2.72614225801306e-10, 2.77068142495902e-08, -2.10102402082508e-06, -5.69250639462346e-05,
              -7.34990630326855e-04, -2.95459980854025e-03, -1.60960333262415e-02)
_ERF_BETA = (-1.45660718464996e-05, -2.13374055278905e-04, -1.68282697438203e-03, -7.37332916720468e-03,
             -1.42647390514189e-02)


def _gelu_exact(x):
    return 0.5 * x * (1.0 + lax.erf(x * (2.0 ** -0.5)))


def _peer_main_kernel(xT_ref, u_ref, vT_ref, a_ref, c_ref, b_ref, r_ref, o_ref, w_sc, *, ec):
    ci = pl.program_id(1)

    @pl.when(ci == 0)
    def _():
        o_ref[...] = jnp.zeros_like(o_ref)

    hid = jnp.dot(u_ref[...], xT_ref[...], preferred_element_type=F32)
    act = _gelu_exact(hid).astype(BF16)
    nk = PEER_NKEYS
    tm = hid.shape[1]

    def row(ref, h, i1):
        x16 = jnp.broadcast_to(ref[h, pl.ds(i1, 1), :], (16, tm)).astype(BF16)
        return jnp.concatenate([x16] * (nk // 16), axis=0)

    for ii in range(ec // nk):
        i1 = ci * (ec // nk) + ii
        w = None
        for h in range(PEER_HEADS):
            term = jnp.where(r_ref[h] < row(c_ref, h, i1), b_ref[h], 0.0) * row(a_ref, h, i1)
            w = term if w is None else w + term
        w_sc[ii * nk:(ii + 1) * nk, :] = w * act[ii * nk:(ii + 1) * nk, :]
    o_ref[...] += jnp.dot(vT_ref[...], w_sc[...], preferred_element_type=F32)


def _peer_main(xnT, u_bf16, vT_bf16, a, c, b, r, tm=512, ec=1024):
    d, s = xnT.shape
    e = u_bf16.shape[0]
    heads, nk, _ = a.shape
    rt = pl.BlockSpec((heads, nk, tm), lambda t, ci: (0, 0, t))
    return pl.pallas_call(
        functools.partial(_peer_main_kernel, ec=ec),
        grid=(s // tm, e // ec),
        in_specs=[pl.BlockSpec((d, tm), lambda t, ci: (0, t)),
                  pl.BlockSpec((ec, d), lambda t, ci: (ci, 0)),
                  pl.BlockSpec((d, ec), lambda t, ci: (0, ci)),
                  rt, rt, rt, rt],
        out_specs=pl.BlockSpec((d, tm), lambda t, ci: (0, t)),
        out_shape=jax.ShapeDtypeStruct((d, s), F32),
        scratch_shapes=[pltpu.VMEM((ec, tm), BF16)],
        compiler_params=_cparams(("parallel", "arbitrary")),
        name="peer_main",
    )(xnT, u_bf16, vT_bf16, a, c, b, r)


def _ple_kernel(h_ref, peer_ref, p_ref, nw_ref, wg_ref, wp_ref, o_ref):
    h2 = h_ref[...] + peer_ref[...]
    xn = _rms(h2, nw_ref[...]).astype(BF16)
    gate = 1.0 / (1.0 + jnp.exp(-jnp.dot(xn, wg_ref[...], preferred_element_type=F32)))
    emb = jnp.dot(p_ref[...].astype(BF16), wp_ref[...], preferred_element_type=F32)
    o_ref[...] = h2 + gate * emb


def _ple(h, peer, p, nw, wg_bf16, wp_bf16, tm=512):
    s, d = h.shape
    pd = p.shape[1]
    row = lambda w: pl.BlockSpec((tm, w), lambda i: (i, 0))
    return pl.pallas_call(
        _ple_kernel,
        grid=(s // tm,),
        in_specs=[row(d), row(d), row(pd), _full((1, d)), _full((d, d)), _full((pd, d))],
        out_specs=row(d),
        out_shape=jax.ShapeDtypeStruct((s, d), F32),
        compiler_params=_cparams(("parallel",)),
        name="ple",
    )(h, peer, p, nw.reshape(1, d), wg_bf16, wp_bf16)


def _reorder_w_in(w):
    widths = (256, 256, 256, 4, 512, 128, 128, 128, 128, 128, 128, 24, 256, 256, 256)
    offs = np.concatenate([[0], np.cumsum(widths)])
    piece = lambda k: w[:, offs[k]:offs[k + 1]]
    order = [0, 1, 2, 4, 5, 6, 7, 8, 9, 10, 12, 13, 14, 3, 11]
    pad = jnp.zeros((w.shape[0], PROJ_W - int(offs[-1])), w.dtype)
    return jnp.concatenate([piece(k) for k in order] + [pad], axis=1)


def _overlap_T(s):
    n = np.arange(s // CMP_STRIDE)[None, :] * CMP_STRIDE
    m = np.arange(s // SLC_LEN)[:, None] * SLC_LEN
    return jnp.asarray(((n < m + SLC_LEN) & (n + CMP_LEN > m)).astype(np.float32))


def _layer(h, p_i, tabs, layer, attn_norm_w, w_in, fox_f_bias, fox_q_norm_w, fox_k_norm_w, nsa_q_norm_w,
           nsa_k_norm_w, nsa_cmp_pos, nsa_cmp_w, diff_q_norm_w, diff_k_norm_w, diff_lambda, diff_subln_w,
           w_out, ffn_norm_w, peer_w_q, peer_sub_keys, peer_u, peer_v, ple_norm_w, ple_w_gate, ple_w_proj):
    s = h.shape[0]
    proj = _norm_matmul(h, attn_norm_w, _reorder_w_in(w_in).astype(BF16))
    fb = jnp.zeros((1, LANES), F32).at[0, :FOX_HEADS].set(fox_f_bias)
    (fq, fk, fv, chi, cmid, clo, nqn, nqr, kvc, ksr, vs, kwr, vw, gates, dq, dk, dv) = _prep(
        proj, tabs, fb, fox_q_norm_w, fox_k_norm_w, nsa_q_norm_w, nsa_k_norm_w, diff_q_norm_w, diff_k_norm_w)

    tq_f = 512
    u_f = _score_bound(fox_q_norm_w, fox_k_norm_w, FOX_HD, FOX_HD ** -0.5)
    c3 = [t[:, :FOX_HEADS] for t in (chi, cmid, clo)]
    q_ext = _extend(fq, FOX_HEADS, {COL_SHIFT: SHIFT_HEADROOM - u_f, 65: 1.0, 66: 1.0, 67: 1.0,
                                    68: c3[0], 69: c3[1], 70: c3[2]})
    k_ext = _extend(fk, FOX_HEADS, {COL_SHIFT: 1.0, 65: -c3[0], 66: -c3[1], 67: -c3[2], 68: 1.0, 69: 1.0, 70: 1.0})
    o_fox = _flash(_to_qT(q_ext.reshape(s, -1), FOX_HEADS, 1, tq_f), k_ext.transpose(1, 0, 2),
                   _to_headsT(fv, FOX_HEADS), bound_ok=u_f <= SCORE_BOUND, groups=1, tq=tq_f, tk=TK_CAUSAL)
    o_fox = _from_oT(o_fox, 1, tq_f)

    tq_n = 128
    kvr = kvc.reshape(s // CMP_STRIDE, CMP_STRIDE, 2, NSA_KV, NSA_HD).transpose(2, 3, 0, 1, 4)
    kvr = kvr.reshape(2, NSA_KV, s // CMP_STRIDE, CMP_STRIDE * NSA_HD)
    posf = nsa_cmp_pos.reshape(2, 1, CMP_LEN * NSA_HD)
    kcc = _compress(kvr[0], nsa_cmp_w[0], posf[0], nsa_k_norm_w, True)
    vcc = _compress(kvr[1], nsa_cmp_w[1], posf[1], nsa_k_norm_w, False)
    o_c, sel = _nsa_cmp(_to_qT(nqn, NSA_HEADS, NSA_G, tq_n), kcc, vcc.transpose(0, 2, 1), _overlap_T(s), tq_n)
    u_n = _score_bound(nsa_q_norm_w, nsa_k_norm_w, NSA_HD, NSA_HD ** -0.5)
    qrT = _to_qT(_extend(nqr, NSA_HEADS, {COL_SHIFT: SHIFT_HEADROOM - u_n}).reshape(s, -1), NSA_HEADS, NSA_G, tq_n)
    blk_in_tile = (jnp.arange(s) % TK_CAUSAL) // SLC_LEN
    onehot = (blk_in_tile[:, None] == jnp.arange(TK_CAUSAL // SLC_LEN)[None, :]).astype(BF16)
    ks_cols = {COL_SHIFT: 1.0}
    ks_cols.update({COL_SEL + b: jnp.broadcast_to(onehot[:, b:b + 1], (s, NSA_KV))
                    for b in range(TK_CAUSAL // SLC_LEN)})
    ks_ext = _extend(ksr, NSA_KV, ks_cols).transpose(1, 0, 2)
    kw_ext = _extend(kwr, NSA_KV, {COL_SHIFT: 1.0}).transpose(1, 0, 2)
    o_s = _flash(qrT, ks_ext, _to_headsT(vs, NSA_KV), sel, bound_ok=u_n <= SCORE_BOUND, groups=NSA_G, tq=tq_n,
                 tk=TK_CAUSAL)
    o_w = _flash(qrT, kw_ext, _to_headsT(vw, NSA_KV), bound_ok=u_n <= SCORE_BOUND, groups=NSA_G, tq=tq_n, tk=512,
                 window=WIN)
    o_c, o_s, o_w = (_from_oT(o, NSA_G, tq_n) for o in (o_c, o_s, o_w))
    g3 = gates[:, FOX_HEADS:FOX_HEADS + 3 * NSA_HEADS].reshape(s, NSA_HEADS, 3)
    gc, gs, gw = (jnp.repeat(g3[:, :, b], NSA_HD, axis=1) for b in range(3))

    tq_d = 256
    dq4 = dq.reshape(s, DIFF_HEADS, 2, DIFF_QK)
    zq = jnp.zeros((s, DIFF_HEADS, DIFF_QK), BF16)
    q_maps = jnp.stack([jnp.concatenate([dq4[:, :, 0], zq], axis=-1),
                        jnp.concatenate([zq, dq4[:, :, 1]], axis=-1)], axis=2)
    u_d = _score_bound(diff_q_norm_w, diff_k_norm_w, DIFF_QK, DIFF_QK ** -0.5)
    qd_ext = _extend(q_maps.reshape(s, -1), 2 * DIFF_HEADS, {COL_SHIFT: SHIFT_HEADROOM - u_d})
    kd_ext = _extend(dk, DIFF_HEADS, {COL_SHIFT: 1.0}).transpose(1, 0, 2)
    o_d = _flash(_to_qT(qd_ext.reshape(s, -1), 2 * DIFF_HEADS, 2, tq_d), kd_ext, _to_headsT(dv, DIFF_HEADS),
                 bound_ok=u_d <= SCORE_BOUND, groups=2, tq=tq_d, tk=TK_CAUSAL)
    o_d = _from_oT(o_d, 2, tq_d).reshape(s, DIFF_HEADS, 2, DIFF_V)
    d0 = o_d[:, :, 0].reshape(s, -1)
    d1 = o_d[:, :, 1].reshape(s, -1)
    lv = diff_lambda.astype(F32)
    lam_init = 0.8 - 0.6 * math.exp(-0.3 * layer)
    lam = (jnp.exp(jnp.sum(lv[0] * lv[1])) - jnp.exp(jnp.sum(lv[2] * lv[3])) + lam_init).reshape(1)
    sw = jnp.tile(diff_subln_w.reshape(1, -1), (1, DIFF_HEADS))
    h1 = _mix_out(lam, h, o_fox, o_c, o_s, o_w, gc, gs, gw, d0, d1, sw, w_out.astype(BF16), 1.0 - lam_init)

    keysT = peer_sub_keys.reshape(2 * PEER_HEADS, PEER_NKEYS, PEER_DQ // 2).transpose(0, 2, 1).astype(BF16)
    xn, sc = _peer_query(h1, ffn_norm_w, peer_w_q.astype(BF16), keysT)
    scT = sc.T.reshape(2 * PEER_HEADS, PEER_NKEYS, s)
    a, c, b, r = _peer_topk(scT)
    peerT = _peer_main(xn.T, peer_u.astype(BF16), peer_v.T.astype(BF16), a, c, b, r)

    return _ple(h1, peerT.T, p_i, ple_norm_w, ple_w_gate.astype(BF16), ple_w_proj.astype(BF16))


def kernel(x, p, positions, attn_norm_w, w_in, fox_f_bias, fox_q_norm_w, fox_k_norm_w, nsa_q_norm_w, nsa_k_norm_w,
           nsa_cmp_pos, nsa_cmp_w, diff_q_norm_w, diff_k_norm_w, diff_lambda, diff_subln_w, w_out, ffn_norm_w,
           peer_w_q, peer_sub_keys, peer_u, peer_v, ple_norm_w, ple_w_gate, ple_w_proj):
    b, s, d = x.shape
    assert b == 1 and d == D_MODEL and s % 512 == 0
    tabs = _rope_tables(positions)
    h = x.reshape(s, d)
    per_layer = (attn_norm_w, w_in, fox_f_bias, fox_q_norm_w, fox_k_norm_w, nsa_q_norm_w, nsa_k_norm_w, nsa_cmp_pos,
                 nsa_cmp_w, diff_q_norm_w, diff_k_norm_w, diff_lambda, diff_subln_w, w_out, ffn_norm_w, peer_w_q,
                 peer_sub_keys, peer_u, peer_v, ple_norm_w, ple_w_gate, ple_w_proj)
    for layer in range(attn_norm_w.shape[0]):
        h = _layer(h, p[layer, 0], tabs, layer, *(w[layer] for w in per_layer))
    return h.reshape(b, s, d)
```

```python
import functools
import math

import numpy as np
import jax
import jax.numpy as jnp
from jax import lax
from jax.experimental import pallas as pl
from jax.experimental.pallas import tpu as pltpu

F32 = jnp.float32
BF16 = jnp.bfloat16
HIGHEST = lax.Precision.HIGHEST

D_MODEL = 1024
PLE_DIM = 256
ROPE_THETA = 10000.0
EPS = 1e-6
FOX_HEADS, FOX_HD = 4, 64
NSA_HEADS, NSA_KV, NSA_HD = 8, 2, 64
NSA_G = NSA_HEADS // NSA_KV
CMP_LEN, CMP_STRIDE, SLC_LEN, SLC_TOPK, WIN = 32, 16, 64, 16, 512
FORCE_BONUS = 1.0e4
DIFF_HEADS, DIFF_QK, DIFF_V = 4, 32, 64
PEER_HEADS, PEER_NKEYS, PEER_DQ, PEER_TOPK = 8, 128, 256, 16
PEER_EXPERTS = PEER_NKEYS * PEER_NKEYS

LANES = 128
LOG2E = math.log2(math.e)
NEG_INIT = -1.0e30
MASKVAL = -2.0e30
VMEM_LIMIT = 56 * 1024 * 1024
TK_CAUSAL = 1024
Q_LANES = 1024

_SEG = dict(fq=(0, 256), fk=(256, 256), fv=(512, 256), nq=(768, 512), nkc=(1280, 128), nvc=(1408, 128),
            nks=(1536, 128), nvs=(1664, 128), nkw=(1792, 128), nvw=(1920, 128),
            dq=(2048, 256), dk=(2304, 256), dv=(2560, 256), misc=(2816, 128))
PROJ_W = 2944


def _cparams(sem):
    return pltpu.CompilerParams(dimension_semantics=sem, vmem_limit_bytes=VMEM_LIMIT)


def _full(shape):
    n = len(shape)
    return pl.BlockSpec(shape, lambda *_: (0,) * n)


def _rope_tab_kernel(pos_ref, f64_ref, g64_ref, f32_ref, g32_ref, c64_o, s64_o, c32_o, s32_o):
    pos = pos_ref[...].astype(F32)
    a64 = pos * f64_ref[...]
    c64_o[...] = jnp.cos(a64)
    s64_o[...] = jnp.sin(a64) * g64_ref[...]
    a32 = pos * f32_ref[...]
    c32_o[...] = jnp.cos(a32)
    s32_o[...] = jnp.sin(a32) * g32_ref[...]


def _rope_tables(positions):
    s = positions.shape[-1]
    pos = positions.reshape(s, 1)
    lane = np.arange(LANES)

    def lanes(half):
        inv = ROPE_THETA ** (-jnp.arange(half, dtype=F32) / half)
        freq = inv[(lane % (2 * half)) % half].reshape(1, LANES)
        sign = np.where((lane % (2 * half)) < half, -1.0, 1.0).astype(np.float32).reshape(1, LANES)
        return freq, jnp.asarray(sign)

    f64, g64 = lanes(NSA_HD // 2)
    f32_, g32 = lanes(DIFF_QK // 2)
    tm = 512
    out = jax.ShapeDtypeStruct((s, LANES), F32)
    row = pl.BlockSpec((tm, LANES), lambda i: (i, 0))
    return pl.pallas_call(
        _rope_tab_kernel,
        grid=(s // tm,),
        in_specs=[pl.BlockSpec((tm, 1), lambda i: (i, 0))] + [_full((1, LANES))] * 4,
        out_specs=[row] * 4,
        out_shape=[out] * 4,
        compiler_params=_cparams(("parallel",)),
        name="rope_tables",
    )(pos, f64, g64, f32_, g32)


def _rms(x, w):
    return x * lax.rsqrt(jnp.mean(x * x, axis=-1, keepdims=True) + EPS) * w


def _norm_matmul_kernel(x_ref, nw_ref, w_ref, o_ref):
    xn = _rms(x_ref[...], nw_ref[...])
    o_ref[...] = jnp.dot(xn.astype(BF16), w_ref[...], preferred_element_type=F32)


def _norm_matmul(x, nw, w_bf16, tm=512):
    s, d = x.shape
    n = w_bf16.shape[1]
    return pl.pallas_call(
        _norm_matmul_kernel,
        grid=(s // tm,),
        in_specs=[pl.BlockSpec((tm, d), lambda i: (i, 0)), _full((1, d)), _full((d, n))],
        out_specs=pl.BlockSpec((tm, n), lambda i: (i, 0)),
        out_shape=jax.ShapeDtypeStruct((s, n), F32),
        compiler_params=_cparams(("parallel",)),
        name="norm_matmul",
    )(x, nw.reshape(1, d), w_bf16)


def _seg_rms(x, bmat, seg):
    outs = []
    for c0 in range(0, x.shape[1], 256):
        w = min(256, x.shape[1] - c0)
        xc = x[:, c0:c0 + w]
        ss = jnp.dot(xc * xc, bmat[:w, :w], precision=HIGHEST, preferred_element_type=F32)
        outs.append(xc * lax.rsqrt(ss * (1.0 / seg) + EPS))
    return outs[0] if len(outs) == 1 else jnp.concatenate(outs, axis=1)


def _tile_lanes(t, width):
    reps = width // LANES
    return t if reps == 1 else jnp.concatenate([t] * reps, axis=1)


def _rope(x, cos, sin, half):
    width = x.shape[1]
    left = pltpu.roll(x, width - half, 1)
    right = pltpu.roll(x, half, 1)
    lane = lax.broadcasted_iota(jnp.int32, x.shape, 1)
    swapped = jnp.where((lane & (2 * half - 1)) < half, left, right)
    return x * _tile_lanes(cos, width) + swapped * _tile_lanes(sin, width)


def _prep_kernel(proj_ref, c64_ref, s64_ref, c32_ref, s32_ref, fb_ref, wfq_ref, wfk_ref, wnq_ref, wnk_ref,
                 wdq_ref, wdk_ref, b64_ref, b32_ref,
                 fq_o, fk_o, fv_o, chi_o, cmid_o, clo_o, nqn_o, nqr_o, kvc_o, ksr_o, vs_o, kwr_o, vw_o,
                 gate_o, dq_o, dk_o, dv_o, carry_sc, *, tm):
    def seg(name):
        c0, w = _SEG[name]
        return proj_ref[:, c0:c0 + w]

    b64 = b64_ref[...]
    b32 = b32_ref[...]
    c64, s64, c32, s32 = c64_ref[...], s64_ref[...], c32_ref[...], s32_ref[...]

    fq_o[...] = (_seg_rms(seg("fq"), b64, FOX_HD) * wfq_ref[...] * (FOX_HD ** -0.5 * LOG2E)).astype(BF16)
    fk_o[...] = (_seg_rms(seg("fk"), b64, FOX_HD) * wfk_ref[...]).astype(BF16)
    fv_o[...] = seg("fv").astype(BF16)

    misc = seg("misc")
    gate_o[...] = 1.0 / (1.0 + jnp.exp(-misc))
    t = misc + fb_ref[...]
    logf = jnp.minimum(t, 0.0) - jnp.log1p(jnp.exp(-jnp.abs(t)))

    @pl.when(pl.program_id(0) == 0)
    def _():
        carry_sc[...] = jnp.zeros_like(carry_sc)

    r = lax.broadcasted_iota(jnp.int32, (tm, tm), 0)
    c = lax.broadcasted_iota(jnp.int32, (tm, tm), 1)
    tri = (c <= r).astype(F32)
    csum = jnp.dot(tri, logf, precision=HIGHEST, preferred_element_type=F32) + carry_sc[0:1, :]
    carry_sc[...] = jnp.broadcast_to(csum[tm - 1:tm, :], carry_sc.shape)
    csum = csum * LOG2E
    hi = csum.astype(BF16)
    r1 = csum - hi.astype(F32)
    mid = r1.astype(BF16)
    lo = (r1 - mid.astype(F32)).astype(BF16)
    chi_o[...] = hi
    cmid_o[...] = mid
    clo_o[...] = lo

    nqn = _seg_rms(seg("nq"), b64, NSA_HD) * wnq_ref[...]
    nqn_o[...] = (nqn * (NSA_HD ** -0.5 * LOG2E)).astype(BF16)
    nqr_o[...] = (_rope(nqn, c64, s64, NSA_HD // 2) * (NSA_HD ** -0.5 * LOG2E)).astype(BF16)
    kvc_o[:, 0:128] = seg("nkc").astype(BF16)
    kvc_o[:, 128:256] = seg("nvc").astype(BF16)
    wnk = wnk_ref[...]
    ksr_o[...] = _rope(_seg_rms(seg("nks"), b64, NSA_HD) * wnk, c64, s64, NSA_HD // 2).astype(BF16)
    vs_o[...] = seg("nvs").astype(BF16)
    kwr_o[...] = _rope(_seg_rms(seg("nkw"), b64, NSA_HD) * wnk, c64, s64, NSA_HD // 2).astype(BF16)
    vw_o[...] = seg("nvw").astype(BF16)

    dqn = _rope(_seg_rms(seg("dq"), b32, DIFF_QK) * wdq_ref[...], c32, s32, DIFF_QK // 2)
    dq_o[...] = (dqn * (DIFF_QK ** -0.5 * LOG2E)).astype(BF16)
    dk_o[...] = _rope(_seg_rms(seg("dk"), b32, DIFF_QK) * wdk_ref[...], c32, s32, DIFF_QK // 2).astype(BF16)
    dv_o[...] = seg("dv").astype(BF16)


def _block_diag_ones(n, seg):
    i = np.arange(n)
    return jnp.asarray((i[:, None] // seg == i[None, :] // seg).astype(np.float32))


def _prep(proj, tabs, fb, wfq, wfk, wnq, wnk, wdq, wdk, tm=256):
    s = proj.shape[0]
    c64, s64, c32, s32 = tabs
    b64 = _block_diag_ones(256, 64)
    b32 = _block_diag_ones(256, 32)

    def tiled(w, width):
        return jnp.tile(w.reshape(1, -1), (1, width // w.shape[-1]))

    consts = [fb, tiled(wfq, 256), tiled(wfk, 256), tiled(wnq, 512), tiled(wnk, 128), tiled(wdq, 256),
              tiled(wdk, 256), b64, b32]
    widths = [(256, BF16), (256, BF16), (256, BF16), (128, BF16), (128, BF16), (128, BF16), (512, BF16),
              (512, BF16), (256, BF16), (128, BF16), (128, BF16), (128, BF16), (128, BF16), (128, F32),
              (256, BF16), (256, BF16), (256, BF16)]
    row = lambda w: pl.BlockSpec((tm, w), lambda i: (i, 0))
    return pl.pallas_call(
        functools.partial(_prep_kernel, tm=tm),
        grid=(s // tm,),
        in_specs=[row(PROJ_W)] + [row(LANES)] * 4 + [_full(c.shape) for c in consts],
        out_specs=[row(w) for w, _ in widths],
        out_shape=[jax.ShapeDtypeStruct((s, w), dt) for w, dt in widths],
        scratch_shapes=[pltpu.VMEM((8, LANES), F32)],
        compiler_params=_cparams(("arbitrary",)),
        name="head_prep",
    )(proj, c64, s64, c32, s32, *consts)


QK_DIM = 128
COL_SHIFT = 64
COL_SEL = 80
SCORE_BOUND = 60.0
SHIFT_HEADROOM = 60.0


def _flash_kernel(code_ref, qT_ref, k_ref, vT_ref, *rest, groups, tq, tk, window, has_sel, bounded):
    rest = list(rest)
    sel_ref = rest.pop(0) if has_sel else None
    o_ref = rest.pop(0)
    m_sc = None if bounded else rest.pop(0)
    acc_sc = rest.pop(0)
    code = code_ref[pl.program_id(1)]
    i = code & 0xFFF
    j = (code >> 12) & 0xFFF
    first = (code >> 24) & 1
    last = (code >> 25) & 1
    rows = groups * tq
    dv = o_ref.shape[2]

    @pl.when(first == 1)
    def _():
        if not bounded:
            m_sc[...] = jnp.full_like(m_sc, NEG_INIT)
        acc_sc[...] = jnp.zeros_like(acc_sc)

    def step(masked):
        q = qT_ref[0, 0]
        if has_sel:
            nb = tk // SLC_LEN
            q = jnp.concatenate([q[0:COL_SEL], _tile_lanes_any(sel_ref[0], groups), q[COL_SEL + nb:]], axis=0)
        s = jnp.dot(k_ref[0], q, preferred_element_type=F32)
        if masked:
            kpos = j * tk + lax.broadcasted_iota(jnp.int32, (tk, rows), 0)
            lane = lax.broadcasted_iota(jnp.int32, (tk, rows), 1)
            qpos = i * tq + (lane & (tq - 1))
            keep = kpos <= qpos
            if window is not None:
                keep = keep & (kpos > qpos - window)
            s = jnp.where(keep, s, MASKVAL)
        if bounded:
            acc_sc[...] += jnp.dot(vT_ref[0], jnp.exp2(s).astype(BF16), preferred_element_type=F32)
        else:
            m_prev = m_sc[...]
            m_new = jnp.maximum(m_prev, jnp.max(s, axis=0, keepdims=True))
            p = jnp.exp2(s - m_new).astype(BF16)
            acc_sc[...] = acc_sc[...] * jnp.exp2(m_prev - m_new) + jnp.dot(vT_ref[0], p, preferred_element_type=F32)
            m_sc[...] = m_new

    if window is not None:
        step(True)
    else:
        needs_mask = (j + 1) * tk - 1 > i * tq

        @pl.when(needs_mask)
        def _():
            step(True)

        @pl.when(jnp.logical_not(needs_mask))
        def _():
            step(False)

    @pl.when(last == 1)
    def _():
        l = acc_sc[dv:dv + 1, :]
        o_ref[0, 0] = acc_sc[0:dv, :] / jnp.where(l > 0.0, l, 1.0)


def _tile_lanes_any(t, reps):
    return t if reps == 1 else jnp.concatenate([t] * reps, axis=1)


def _flash_call(qT, k, vT, sel, *, groups, tq, tk, window, bounded):
    hkv, nq, dk, rows = qT.shape
    dve = vT.shape[1]
    dv = dve - 8
    codes = []
    for i in range(nq):
        q_lo, q_hi = i * tq, i * tq + tq - 1
        j_hi = q_hi // tk
        j_lo = 0 if window is None else max(0, (q_lo - window + 1) // tk)
        for j in range(j_lo, j_hi + 1):
            codes.append(i | (j << 12) | (int(j == j_lo) << 24) | (int(j == j_hi) << 25))
    codes = jnp.asarray(np.asarray(codes, dtype=np.int32))
    nsteps = codes.shape[0]

    def ti(c, s):
        return c[s] & 0xFFF

    def tj(c, s):
        return (c[s] >> 12) & 0xFFF

    in_specs = [
        pl.BlockSpec((1, 1, dk, rows), lambda h, s, c: (h, ti(c, s), 0, 0)),
        pl.BlockSpec((1, tk, dk), lambda h, s, c: (h, tj(c, s), 0)),
        pl.BlockSpec((1, dve, tk), lambda h, s, c: (h, 0, tj(c, s))),
    ]
    args = [qT, k, vT]
    if sel is not None:
        in_specs.append(pl.BlockSpec((1, tk // SLC_LEN, tq), lambda h, s, c: (h, tj(c, s), ti(c, s))))
        args.append(sel)
    scratch = [pltpu.VMEM((dve, rows), F32)]
    if not bounded:
        scratch = [pltpu.VMEM((1, rows), F32)] + scratch
    kern = functools.partial(_flash_kernel, groups=groups, tq=tq, tk=tk, window=window, has_sel=sel is not None,
                             bounded=bounded)
    return pl.pallas_call(
        kern,
        grid_spec=pltpu.PrefetchScalarGridSpec(
            num_scalar_prefetch=1,
            grid=(hkv, nsteps),
            in_specs=in_specs,
            out_specs=pl.BlockSpec((1, 1, dv, rows), lambda h, s, c: (h, ti(c, s), 0, 0)),
            scratch_shapes=scratch,
        ),
        out_shape=jax.ShapeDtypeStruct((hkv, nq, dv, rows), F32),
        compiler_params=_cparams(("parallel", "arbitrary")),
        name="flash_bounded" if bounded else "flash_online",
    )(codes, *args)


def _flash(qT, k, vT, sel=None, *, bound_ok, groups, tq, tk, window=None):
    hkv, dv, s_len = vT.shape
    vT = jnp.concatenate([vT, jnp.ones((hkv, 1, s_len), vT.dtype), jnp.zeros((hkv, 7, s_len), vT.dtype)], axis=1)
    call = functools.partial(_flash_call, qT, k, vT, sel, groups=groups, tq=tq, tk=tk, window=window)
    return lax.cond(bound_ok, lambda: call(bounded=True), lambda: call(bounded=False))


def _extend(x, heads, cols):
    s = x.shape[0]
    x = x.reshape(s, heads, -1)
    pieces, at = [x], x.shape[-1]
    for c in sorted(cols):
        if c > at:
            pieces.append(jnp.zeros((s, heads, c - at), BF16))
        v = cols[c]
        v = jnp.broadcast_to(jnp.asarray(v, F32).astype(BF16), (s, heads)) if jnp.ndim(v) == 0 else v.astype(BF16)
        pieces.append(v[:, :, None])
        at = c + 1
    pieces.append(jnp.zeros((s, heads, QK_DIM - at), BF16))
    return jnp.concatenate(pieces, axis=-1)


def _score_bound(wq, wk, seg, scale):
    return 1.02 * seg * jnp.max(jnp.abs(wq)) * jnp.max(jnp.abs(wk)) * scale * LOG2E


def _to_qT(q, heads, groups, tq):
    s = q.shape[0]
    d = q.shape[1] // heads
    x = q.reshape(s // tq, tq, heads // groups, groups, d)
    return x.transpose(2, 0, 4, 3, 1).reshape(heads // groups, s // tq, d, groups * tq)


def _from_oT(oT, groups, tq):
    hkv, nq, d, _ = oT.shape
    x = oT.reshape(hkv, nq, d, groups, tq).transpose(1, 4, 0, 3, 2)
    return x.reshape(nq * tq, hkv * groups * d)


def _to_heads(x, heads):
    s = x.shape[0]
    return x.reshape(s, heads, -1).transpose(1, 0, 2)


def _to_headsT(x, heads):
    s = x.shape[0]
    return x.reshape(s, heads, -1).transpose(1, 2, 0)


def _compress_kernel(r_ref, rn_ref, w_ref, pos_ref, nw_ref, o_ref, *, normalize):
    half = r_ref.shape[2]
    wa = w_ref[0, 0:half, :].astype(BF16)
    wb = w_ref[0, half:2 * half, :].astype(BF16)
    const = jnp.dot(pos_ref[0], w_ref[0], precision=HIGHEST, preferred_element_type=F32)
    out = (jnp.dot(r_ref[0], wa, preferred_element_type=F32)
           + jnp.dot(rn_ref[0], wb, preferred_element_type=F32) + const)
    if normalize:
        out = _rms(out, nw_ref[...])
    o_ref[0] = out.astype(BF16)


def _compress(r, w, pos, nw, normalize):
    g, n, width = r.shape
    hd = w.shape[-1]
    r_next = jnp.concatenate([r[:, 1:], jnp.zeros((g, 1, width), r.dtype)], axis=1)
    rspec = pl.BlockSpec((1, n, width), lambda i: (i, 0, 0))
    return pl.pallas_call(
        functools.partial(_compress_kernel, normalize=normalize),
        grid=(g,),
        in_specs=[rspec, rspec, _full((1,) + w.shape), _full((1,) + pos.shape), _full((1, hd))],
        out_specs=pl.BlockSpec((1, n, hd), lambda i: (i, 0, 0)),
        out_shape=jax.ShapeDtypeStruct((g, n, hd), BF16),
        compiler_params=_cparams(("parallel",)),
        name="nsa_compress",
    )(r, r_next, w[None], pos[None], nw.reshape(1, hd))


def _nsa_cmp_kernel(qT_ref, kc_ref, vcT_ref, ov_ref, o_ref, sel_ref, *, tq, groups):
    i = pl.program_id(1)
    ncmp = kc_ref.shape[1]
    rows = groups * tq
    s = jnp.dot(kc_ref[0], qT_ref[0, 0], preferred_element_type=F32)
    n_idx = lax.broadcasted_iota(jnp.int32, (ncmp, rows), 0)
    lane = lax.broadcasted_iota(jnp.int32, (ncmp, rows), 1)
    qpos = i * tq + (lane & (tq - 1))
    keep = n_idx * CMP_STRIDE + (CMP_LEN - 1) <= qpos
    sm = jnp.where(keep, s, NEG_INIT)
    m = jnp.max(sm, axis=0, keepdims=True)
    e = jnp.where(keep, jnp.exp2(sm - m), 0.0)
    den = jnp.sum(e, axis=0, keepdims=True)
    p = e / jnp.where(den > 0.0, den, 1.0)
    o_ref[0, 0] = jnp.dot(vcT_ref[0], p.astype(BF16), preferred_element_type=F32)

    psum = p[:, 0:tq]
    for g in range(1, groups):
        psum = psum + p[:, g * tq:(g + 1) * tq]
    imp = jnp.dot(ov_ref[...], psum, precision=HIGHEST, preferred_element_type=F32)
    ns = imp.shape[0]
    blk = lax.broadcasted_iota(jnp.int32, (ns, tq), 0)
    qp = i * tq + lax.broadcasted_iota(jnp.int32, (ns, tq), 1)
    cur = qp // SLC_LEN
    valid = blk <= cur
    forced = (blk == 0) | (blk == cur) | (blk == cur - 1)
    work = jnp.where(valid, imp + jnp.where(forced, FORCE_BONUS, 0.0), NEG_INIT)
    picked = jnp.zeros((ns, tq), dtype=jnp.bool_)
    for _ in range(SLC_TOPK):
        mx = jnp.max(work, axis=0, keepdims=True)
        idx = jnp.min(jnp.where(work == mx, blk, ns), axis=0, keepdims=True)
        pick = (blk == idx) & (mx > 0.5 * NEG_INIT)
        picked = picked | pick
        work = jnp.where(pick, NEG_INIT, work)
    sel_ref[0] = jnp.where(picked, 0.0, MASKVAL).astype(BF16)


def _nsa_cmp(qnT, kcc, vccT, overlapT, tq):
    hkv, nq, hd, rows = qnT.shape
    ncmp = kcc.shape[1]
    ns = overlapT.shape[0]
    s = nq * tq
    return pl.pallas_call(
        functools.partial(_nsa_cmp_kernel, tq=tq, groups=rows // tq),
        grid=(hkv, nq),
        in_specs=[pl.BlockSpec((1, 1, hd, rows), lambda h, i: (h, i, 0, 0)),
                  pl.BlockSpec((1, ncmp, hd), lambda h, i: (h, 0, 0)),
                  pl.BlockSpec((1, hd, ncmp), lambda h, i: (h, 0, 0)),
                  _full(overlapT.shape)],
        out_specs=[pl.BlockSpec((1, 1, hd, rows), lambda h, i: (h, i, 0, 0)),
                   pl.BlockSpec((1, ns, tq), lambda h, i: (h, 0, i))],
        out_shape=[jax.ShapeDtypeStruct((hkv, nq, hd, rows), F32), jax.ShapeDtypeStruct((hkv, ns, s), BF16)],
        compiler_params=_cparams(("parallel", "parallel")),
        name="nsa_cmp_select",
    )(qnT, kcc, vccT, overlapT)


def _mix_out_kernel(lam_ref, h_ref, fox_ref, oc_ref, os_ref, ow_ref, gc_ref, gs_ref, gw_ref, d0_ref, d1_ref,
                    sw_ref, b64_ref, w_ref, o_ref, *, diff_scale):
    lam = lam_ref[0]
    nsa = gc_ref[...] * oc_ref[...] + gs_ref[...] * os_ref[...] + gw_ref[...] * ow_ref[...]
    a = d0_ref[...] - lam * d1_ref[...]
    ss = jnp.dot(a * a, b64_ref[...], precision=HIGHEST, preferred_element_type=F32)
    diff = a * lax.rsqrt(ss * (1.0 / DIFF_V) + EPS) * sw_ref[...] * diff_scale
    nf = FOX_HEADS * FOX_HD
    nn = NSA_HEADS * NSA_HD
    acc = h_ref[...]
    acc = acc + jnp.dot(fox_ref[...].astype(BF16), w_ref[0:nf, :], preferred_element_type=F32)
    acc = acc + jnp.dot(nsa.astype(BF16), w_ref[nf:nf + nn, :], preferred_element_type=F32)
    acc = acc + jnp.dot(diff.astype(BF16), w_ref[nf + nn:, :], preferred_element_type=F32)
    o_ref[...] = acc


def _mix_out(lam, h, fox, oc, os_, ow, gc, gs, gw, d0, d1, sw, w_out_bf16, diff_scale, tm=512):
    s, d = h.shape
    row = lambda w: pl.BlockSpec((tm, w), lambda i: (i, 0))
    b64 = _block_diag_ones(256, 64)
    return pl.pallas_call(
        functools.partial(_mix_out_kernel, diff_scale=diff_scale),
        grid=(s // tm,),
        in_specs=[pl.BlockSpec(memory_space=pltpu.SMEM), row(d), row(256)] + [row(512)] * 6 + [row(256), row(256),
                  _full((1, 256)), _full((256, 256)), _full(w_out_bf16.shape)],
        out_specs=row(d),
        out_shape=jax.ShapeDtypeStruct((s, d), F32),
        compiler_params=_cparams(("parallel",)),
        name="mix_out",
    )(lam, h, fox, oc, os_, ow, gc, gs, gw, d0, d1, sw, b64, w_out_bf16)


def _peer_query_kernel(h_ref, nw_ref, wq_ref, keys_ref, xnT_o, scT_o):
    xn = _rms(h_ref[...], nw_ref[...])
    xnT_o[...] = xn.T.astype(BF16)
    q = jnp.dot(xn.astype(BF16), wq_ref[...], preferred_element_type=F32).astype(BF16)
    half = PEER_DQ // 2
    for b in range(2 * PEER_HEADS):
        sc = jnp.dot(q[:, b * half:(b + 1) * half], keys_ref[b], preferred_element_type=F32)
        scT_o[b] = sc.T


def _peer_query(h, nw, wq_bf16, keysT_bf16, tm=512):
    s, d = h.shape
    n = wq_bf16.shape[1]
    nb = 2 * PEER_HEADS
    return pl.pallas_call(
        _peer_query_kernel,
        grid=(s // tm,),
        in_specs=[pl.BlockSpec((tm, d), lambda i: (i, 0)), _full((1, d)), _full((d, n)), _full(keysT_bf16.shape)],
        out_specs=[pl.BlockSpec((d, tm), lambda i: (0, i)), pl.BlockSpec((nb, PEER_NKEYS, tm), lambda i: (0, 0, i))],
        out_shape=[jax.ShapeDtypeStruct((d, s), BF16), jax.ShapeDtypeStruct((nb, PEER_NKEYS, s), F32)],
        compiler_params=_cparams(("parallel",)),
        name="peer_query",
    )(h, nw.reshape(1, d), wq_bf16, keysT_bf16)


def _top16_rows(s):
    n = s.shape[0]
    row = lax.broadcasted_iota(jnp.int32, s.shape, 0)
    rank = jnp.full(s.shape, float(PEER_TOPK), dtype=F32)
    work = s
    vals = []
    for r in range(PEER_TOPK):
        mx = jnp.max(work, axis=0, keepdims=True)
        idx = jnp.min(jnp.where(work == mx, row, n), axis=0, keepdims=True)
        pick = row == idx
        rank = jnp.where(pick, float(r), rank)
        work = jnp.where(pick, NEG_INIT, work)
        vals.append(mx)
    return rank, vals


def _peer_topk_kernel(sc_ref, a_o, c_o, b_o, r_o):
    s1 = sc_ref[0]
    s2 = sc_ref[1]
    rank1, v1 = _top16_rows(s1)
    rank2, v2 = _top16_rows(s2)
    k = PEER_TOPK
    tm = s1.shape[1]
    r16 = lax.broadcasted_iota(jnp.int32, (k, tm), 0)
    v1m = jnp.zeros((k, tm), F32)
    v2m = jnp.zeros((k, tm), F32)
    for r in range(k):
        v1m = jnp.where(r16 == r, v1[r], v1m)
        v2m = jnp.where(r16 == r, v2[r], v2m)
    ea = jnp.exp(v1m - v1[0])
    eb = jnp.exp(v2m - v2[0])
    cands, gates = [v1m[0:1] + v2m], [ea[0:1] * eb]
    for r in range(1, 8):
        cands.append(v1m[r:r + 1] + v2m[0:8])
        gates.append(ea[r:r + 1] * eb[0:8])
    cands.append(v1m[8:16] + v2m[0:1])
    gates.append(ea[8:16] * eb[0:1])
    cand = jnp.concatenate(cands, axis=0)
    gate = jnp.concatenate(gates, axis=0)
    ncand = cand.shape[0]
    crow = lax.broadcasted_iota(jnp.int32, cand.shape, 0)
    for r in range(2, 8):
        start = 16 + 8 * (r - 1)
        cand = jnp.where((crow >= start + k // (r + 1)) & (crow < start + 8), NEG_INIT, cand)
    work = cand
    picked = jnp.zeros(cand.shape, dtype=jnp.bool_)
    for _ in range(k):
        mx = jnp.max(work, axis=0, keepdims=True)
        idx = jnp.min(jnp.where(work == mx, crow, ncand), axis=0, keepdims=True)
        pick = crow == idx
        picked = picked | pick
        work = jnp.where(pick, NEG_INIT, work)
    pf = picked.astype(F32)
    z = jnp.sum(pf * gate, axis=0, keepdims=True)
    cnt = [jnp.sum(pf[0:16, :], axis=0, keepdims=True)]
    cnt += [jnp.sum(pf[8 + 8 * r:16 + 8 * r, :], axis=0, keepdims=True) for r in range(1, 8)]
    cnt += [pf[64 + r:65 + r, :] for r in range(8, 16)]
    cmap = jnp.zeros(s1.shape, dtype=F32)
    for r in range(k):
        cmap = jnp.where(rank1 == float(r), cnt[r], cmap)
    a_o[0] = jnp.where(rank1 < float(k), jnp.exp(s1 - v1[0]), 0.0) / z
    c_o[0] = cmap
    b_o[0] = jnp.where(rank2 < float(k), jnp.exp(s2 - v2[0]), 0.0).astype(BF16)
    r_o[0] = rank2.astype(BF16)


def _peer_topk(scT, tm=256):
    n2, nk, s = scT.shape
    heads = n2 // 2
    ospec = pl.BlockSpec((1, nk, tm), lambda h, t: (h, 0, t))
    return pl.pallas_call(
        _peer_topk_kernel,
        grid=(heads, s // tm),
        in_specs=[pl.BlockSpec((2, nk, tm), lambda h, t: (h, 0, t))],
        out_specs=[ospec] * 4,
        out_shape=[jax.ShapeDtypeStruct((heads, nk, s), dt) for dt in (F32, F32, BF16, BF16)],
        compiler_params=_cparams(("parallel", "parallel")),
        name="peer_topk",
    )(scT)


_ERF_ALPHA = (-2.72614225801306e-10, 2.77068142495902e-08, -2.10102402082508e-06, -5.69250639462346e-05,
              -7.34990630326855e-04, -2.95459980854025e-03, -1.60960333262415e-02)
_ERF_BETA = (-1.45660718464996e-05, -2.13374055278905e-04, -1.68282697438203e-03, -7.37332916720468e-03,
             -1.42647390514189e-02)


def _gelu_exact(x):
    return 0.5 * x * (1.0 + lax.erf(x * (2.0 ** -0.5)))


def _peer_main_kernel(xT_ref, u_ref, v_ref, a_ref, c_ref, b_ref, r_ref, o_ref, w_sc, *, ec):
    ci = pl.program_id(1)

    @pl.when(ci == 0)
    def _():
        o_ref[...] = jnp.zeros_like(o_ref)

    hid = jnp.dot(u_ref[...], xT_ref[...], preferred_element_type=F32)
    act = _gelu_exact(hid).astype(BF16)
    nk = PEER_NKEYS
    tm = hid.shape[1]

    def row(ref, h, i1):
        x16 = jnp.broadcast_to(ref[h, pl.ds(i1, 1), :], (16, tm)).astype(BF16)
        return jnp.concatenate([x16] * (nk // 16), axis=0)

    for ii in range(ec // nk):
        i1 = ci * (ec // nk) + ii
        w = None
        for h in range(PEER_HEADS):
            term = jnp.where(r_ref[h] < row(c_ref, h, i1), b_ref[h], 0.0) * row(a_ref, h, i1)
            w = term if w is None else w + term
        w_sc[ii * nk:(ii + 1) * nk, :] = w * act[ii * nk:(ii + 1) * nk, :]
    o_ref[...] += jnp.dot(w_sc[...].T, v_ref[...], preferred_element_type=F32)


def _peer_main(xnT, u_bf16, v_bf16, a, c, b, r, tm=512, ec=1024):
    d, s = xnT.shape
    e = u_bf16.shape[0]
    heads, nk, _ = a.shape
    rt = pl.BlockSpec((heads, nk, tm), lambda t, ci: (0, 0, t))
    wt = pl.BlockSpec((ec, d), lambda t, ci: (ci, 0))
    return pl.pallas_call(
        functools.partial(_peer_main_kernel, ec=ec),
        grid=(s // tm, e // ec),
        in_specs=[pl.BlockSpec((d, tm), lambda t, ci: (0, t)), wt, wt, rt, rt, rt, rt],
        out_specs=pl.BlockSpec((tm, d), lambda t, ci: (t, 0)),
        out_shape=jax.ShapeDtypeStruct((s, d), F32),
        scratch_shapes=[pltpu.VMEM((ec, tm), BF16)],
        compiler_params=_cparams(("parallel", "arbitrary")),
        name="peer_main",
    )(xnT, u_bf16, v_bf16, a, c, b, r)


def _ple_kernel(h_ref, peer_ref, p_ref, nw_ref, wg_ref, wp_ref, o_ref):
    h2 = h_ref[...] + peer_ref[...]
    xn = _rms(h2, nw_ref[...]).astype(BF16)
    gate = 1.0 / (1.0 + jnp.exp(-jnp.dot(xn, wg_ref[...], preferred_element_type=F32)))
    emb = jnp.dot(p_ref[...].astype(BF16), wp_ref[...], preferred_element_type=F32)
    o_ref[...] = h2 + gate * emb


def _ple(h, peer, p, nw, wg_bf16, wp_bf16, tm=512):
    s, d = h.shape
    pd = p.shape[1]
    row = lambda w: pl.BlockSpec((tm, w), lambda i: (i, 0))
    return pl.pallas_call(
        _ple_kernel,
        grid=(s // tm,),
        in_specs=[row(d), row(d), row(pd), _full((1, d)), _full((d, d)), _full((pd, d))],
        out_specs=row(d),
        out_shape=jax.ShapeDtypeStruct((s, d), F32),
        compiler_params=_cparams(("parallel",)),
        name="ple",
    )(h, peer, p, nw.reshape(1, d), wg_bf16, wp_bf16)


def _reorder_w_in(w):
    widths = (256, 256, 256, 4, 512, 128, 128, 128, 128, 128, 128, 24, 256, 256, 256)
    offs = np.concatenate([[0], np.cumsum(widths)])
    piece = lambda k: w[:, offs[k]:offs[k + 1]]
    order = [0, 1, 2, 4, 5, 6, 7, 8, 9, 10, 12, 13, 14, 3, 11]
    pad = jnp.zeros((w.shape[0], PROJ_W - int(offs[-1])), w.dtype)
    return jnp.concatenate([piece(k) for k in order] + [pad], axis=1)


def _overlap_T(s):
    n = np.arange(s // CMP_STRIDE)[None, :] * CMP_STRIDE
    m = np.arange(s // SLC_LEN)[:, None] * SLC_LEN
    return jnp.asarray(((n < m + SLC_LEN) & (n + CMP_LEN > m)).astype(np.float32))


def _layer(h, p_i, tabs, layer, attn_norm_w, w_in, fox_f_bias, fox_q_norm_w, fox_k_norm_w, nsa_q_norm_w,
           nsa_k_norm_w, nsa_cmp_pos, nsa_cmp_w, diff_q_norm_w, diff_k_norm_w, diff_lambda, diff_subln_w,
           w_out, ffn_norm_w, peer_w_q, peer_sub_keys, peer_u, peer_v, ple_norm_w, ple_w_gate, ple_w_proj):
    s = h.shape[0]
    proj = _norm_matmul(h, attn_norm_w, _reorder_w_in(w_in).astype(BF16))
    fb = jnp.zeros((1, LANES), F32).at[0, :FOX_HEADS].set(fox_f_bias)
    (fq, fk, fv, chi, cmid, clo, nqn, nqr, kvc, ksr, vs, kwr, vw, gates, dq, dk, dv) = _prep(
        proj, tabs, fb, fox_q_norm_w, fox_k_norm_w, nsa_q_norm_w, nsa_k_norm_w, diff_q_norm_w, diff_k_norm_w)

    tq_f = Q_LANES
    u_f =_score_bound(fox_q_norm_w, fox_k_norm_w, FOX_HD, FOX_HD ** -0.5)
    c3 = [t[:, :FOX_HEADS] for t in (chi, cmid, clo)]
    q_ext = _extend(fq, FOX_HEADS, {COL_SHIFT: SHIFT_HEADROOM - u_f, 65: 1.0, 66: 1.0, 67: 1.0,
                                    68: c3[0], 69: c3[1], 70: c3[2]})
    k_ext = _extend(fk, FOX_HEADS, {COL_SHIFT: 1.0, 65: -c3[0], 66: -c3[1], 67: -c3[2], 68: 1.0, 69: 1.0, 70: 1.0})
    o_fox = _flash(_to_qT(q_ext.reshape(s, -1), FOX_HEADS, 1, tq_f), k_ext.transpose(1, 0, 2),
                   _to_headsT(fv, FOX_HEADS), bound_ok=u_f <= SCORE_BOUND, groups=1, tq=tq_f, tk=TK_CAUSAL)
    o_fox = _from_oT(o_fox, 1, tq_f)

    tq_n = Q_LANES // NSA_G
    kvr =kvc.reshape(s // CMP_STRIDE, CMP_STRIDE, 2, NSA_KV, NSA_HD).transpose(2, 3, 0, 1, 4)
    kvr = kvr.reshape(2, NSA_KV, s // CMP_STRIDE, CMP_STRIDE * NSA_HD)
    posf = nsa_cmp_pos.reshape(2, 1, CMP_LEN * NSA_HD)
    kcc = _compress(kvr[0], nsa_cmp_w[0], posf[0], nsa_k_norm_w, True)
    vcc = _compress(kvr[1], nsa_cmp_w[1], posf[1], nsa_k_norm_w, False)
    o_c, sel = _nsa_cmp(_to_qT(nqn, NSA_HEADS, NSA_G, tq_n), kcc, vcc.transpose(0, 2, 1), _overlap_T(s), tq_n)
    u_n = _score_bound(nsa_q_norm_w, nsa_k_norm_w, NSA_HD, NSA_HD ** -0.5)
    qrT = _to_qT(_extend(nqr, NSA_HEADS, {COL_SHIFT: SHIFT_HEADROOM - u_n}).reshape(s, -1), NSA_HEADS, NSA_G, tq_n)
    blk_in_tile = (jnp.arange(s) % TK_CAUSAL) // SLC_LEN
    onehot = (blk_in_tile[:, None] == jnp.arange(TK_CAUSAL // SLC_LEN)[None, :]).astype(BF16)
    ks_cols = {COL_SHIFT: 1.0}
    ks_cols.update({COL_SEL + b: jnp.broadcast_to(onehot[:, b:b + 1], (s, NSA_KV))
                    for b in range(TK_CAUSAL // SLC_LEN)})
    ks_ext = _extend(ksr, NSA_KV, ks_cols).transpose(1, 0, 2)
    kw_ext = _extend(kwr, NSA_KV, {COL_SHIFT: 1.0}).transpose(1, 0, 2)
    o_s = _flash(qrT, ks_ext, _to_headsT(vs, NSA_KV), sel, bound_ok=u_n <= SCORE_BOUND, groups=NSA_G, tq=tq_n,
                 tk=TK_CAUSAL)
    o_w = _flash(qrT, kw_ext, _to_headsT(vw, NSA_KV), bound_ok=u_n <= SCORE_BOUND, groups=NSA_G, tq=tq_n, tk=512,
                 window=WIN)
    o_c, o_s, o_w = (_from_oT(o, NSA_G, tq_n) for o in (o_c, o_s, o_w))
    g3 = gates[:, FOX_HEADS:FOX_HEADS + 3 * NSA_HEADS].reshape(s, NSA_HEADS, 3)
    gc, gs, gw = (jnp.repeat(g3[:, :, b], NSA_HD, axis=1) for b in range(3))

    tq_d = Q_LANES // 2
    dq4 =dq.reshape(s, DIFF_HEADS, 2, DIFF_QK)
    zq = jnp.zeros((s, DIFF_HEADS, DIFF_QK), BF16)
    q_maps = jnp.stack([jnp.concatenate([dq4[:, :, 0], zq], axis=-1),
                        jnp.concatenate([zq, dq4[:, :, 1]], axis=-1)], axis=2)
    u_d = _score_bound(diff_q_norm_w, diff_k_norm_w, DIFF_QK, DIFF_QK ** -0.5)
    qd_ext = _extend(q_maps.reshape(s, -1), 2 * DIFF_HEADS, {COL_SHIFT: SHIFT_HEADROOM - u_d})
    kd_ext = _extend(dk, DIFF_HEADS, {COL_SHIFT: 1.0}).transpose(1, 0, 2)
    o_d = _flash(_to_qT(qd_ext.reshape(s, -1), 2 * DIFF_HEADS, 2, tq_d), kd_ext, _to_headsT(dv, DIFF_HEADS),
                 bound_ok=u_d <= SCORE_BOUND, groups=2, tq=tq_d, tk=TK_CAUSAL)
    o_d = _from_oT(o_d, 2, tq_d).reshape(s, DIFF_HEADS, 2, DIFF_V)
    d0 = o_d[:, :, 0].reshape(s, -1)
    d1 = o_d[:, :, 1].reshape(s, -1)
    lv = diff_lambda.astype(F32)
    lam_init = 0.8 - 0.6 * math.exp(-0.3 * layer)
    lam = (jnp.exp(jnp.sum(lv[0] * lv[1])) - jnp.exp(jnp.sum(lv[2] * lv[3])) + lam_init).reshape(1)
    sw = jnp.tile(diff_subln_w.reshape(1, -1), (1, DIFF_HEADS))
    h1 = _mix_out(lam, h, o_fox, o_c, o_s, o_w, gc, gs, gw, d0, d1, sw, w_out.astype(BF16), 1.0 - lam_init)

    keysT = peer_sub_keys.reshape(2 * PEER_HEADS, PEER_NKEYS, PEER_DQ // 2).transpose(0, 2, 1).astype(BF16)
    xnT, scT = _peer_query(h1, ffn_norm_w, peer_w_q.astype(BF16), keysT)
    a, c, b, r = _peer_topk(scT)
    peer = _peer_main(xnT, peer_u.astype(BF16), peer_v.astype(BF16), a, c, b, r)

    return _ple(h1, peer, p_i, ple_norm_w, ple_w_gate.astype(BF16), ple_w_proj.astype(BF16))


def kernel(x, p, positions, attn_norm_w, w_in, fox_f_bias, fox_q_norm_w, fox_k_norm_w, nsa_q_norm_w, nsa_k_norm_w,
           nsa_cmp_pos, nsa_cmp_w, diff_q_norm_w, diff_k_norm_w, diff_lambda, diff_subln_w, w_out, ffn_norm_w,
           peer_w_q, peer_sub_keys, peer_u, peer_v, ple_norm_w, ple_w_gate, ple_w_proj):
    b, s, d = x.shape
    assert b == 1 and d == D_MODEL and s % Q_LANES == 0 and s % TK_CAUSAL == 0
    tabs = _rope_tables(positions)
    h = x.reshape(s, d)
    per_layer = (attn_norm_w, w_in, fox_f_bias, fox_q_norm_w, fox_k_norm_w, nsa_q_norm_w, nsa_k_norm_w, nsa_cmp_pos,
                 nsa_cmp_w, diff_q_norm_w, diff_k_norm_w, diff_lambda, diff_subln_w, w_out, ffn_norm_w, peer_w_q,
                 peer_sub_keys, peer_u, peer_v, ple_norm_w, ple_w_gate, ple_w_proj)
    for layer in range(attn_norm_w.shape[0]):
        h = _layer(h, p[layer, 0], tabs, layer, *(w[layer] for w in per_layer))
    return h.reshape(b, s, d)
```

```python
import functools
import math

import numpy as np
import jax
import jax.numpy as jnp
from jax import lax
from jax.experimental import pallas as pl
from jax.experimental.pallas import tpu as pltpu

F32 = jnp.float32
BF16 = jnp.bfloat16
HIGHEST = lax.Precision.HIGHEST

D_MODEL = 1024
PLE_DIM = 256
ROPE_THETA = 10000.0
EPS = 1e-6
FOX_HEADS, FOX_HD = 4, 64
NSA_HEADS, NSA_KV, NSA_HD = 8, 2, 64
NSA_G = NSA_HEADS // NSA_KV
CMP_LEN, CMP_STRIDE, SLC_LEN, SLC_TOPK, WIN = 32, 16, 64, 16, 512
FORCE_BONUS = 1.0e4
DIFF_HEADS, DIFF_QK, DIFF_V = 4, 32, 64
PEER_HEADS, PEER_NKEYS, PEER_DQ, PEER_TOPK = 8, 128, 256, 16
PEER_EXPERTS = PEER_NKEYS * PEER_NKEYS

LANES = 128
LOG2E = math.log2(math.e)
NEG_INIT = -1.0e30
MASKVAL = -2.0e30
VMEM_LIMIT = 56 * 1024 * 1024
TK_CAUSAL = 1024
Q_LANES = 1024

_SEG = dict(fq=(0, 256), fk=(256, 256), fv=(512, 256), nq=(768, 512), nkc=(1280, 128), nvc=(1408, 128),
            nks=(1536, 128), nvs=(1664, 128), nkw=(1792, 128), nvw=(1920, 128),
            dq=(2048, 256), dk=(2304, 256), dv=(2560, 256), misc=(2816, 128))
PROJ_W = 2944


def _cparams(sem):
    return pltpu.CompilerParams(dimension_semantics=sem, vmem_limit_bytes=VMEM_LIMIT)


def _full(shape):
    n = len(shape)
    return pl.BlockSpec(shape, lambda *_: (0,) * n)


def _rope_tab_kernel(pos_ref, f64_ref, g64_ref, f32_ref, g32_ref, c64_o, s64_o, c32_o, s32_o):
    pos = pos_ref[...].astype(F32)
    a64 = pos * f64_ref[...]
    c64_o[...] = jnp.cos(a64)
    s64_o[...] = jnp.sin(a64) * g64_ref[...]
    a32 = pos * f32_ref[...]
    c32_o[...] = jnp.cos(a32)
    s32_o[...] = jnp.sin(a32) * g32_ref[...]


def _rope_tables(positions):
    s = positions.shape[-1]
    pos = positions.reshape(s, 1)
    lane = np.arange(LANES)

    def lanes(half):
        inv = ROPE_THETA ** (-jnp.arange(half, dtype=F32) / half)
        freq = inv[(lane % (2 * half)) % half].reshape(1, LANES)
        sign = np.where((lane % (2 * half)) < half, -1.0, 1.0).astype(np.float32).reshape(1, LANES)
        return freq, jnp.asarray(sign)

    f64, g64 = lanes(NSA_HD // 2)
    f32_, g32 = lanes(DIFF_QK // 2)
    tm = 512
    out = jax.ShapeDtypeStruct((s, LANES), F32)
    row = pl.BlockSpec((tm, LANES), lambda i: (i, 0))
    return pl.pallas_call(
        _rope_tab_kernel,
        grid=(s // tm,),
        in_specs=[pl.BlockSpec((tm, 1), lambda i: (i, 0))] + [_full((1, LANES))] * 4,
        out_specs=[row] * 4,
        out_shape=[out] * 4,
        compiler_params=_cparams(("parallel",)),
        name="rope_tables",
    )(pos, f64, g64, f32_, g32)


def _rms(x, w):
    return x * lax.rsqrt(jnp.mean(x * x, axis=-1, keepdims=True) + EPS) * w


def _norm_matmul_kernel(x_ref, nw_ref, *refs):
    o_ref = refs[-1]
    xn = _rms(x_ref[...], nw_ref[...]).astype(BF16)
    c0 = 0
    for w_ref in refs[:-1]:
        n = w_ref.shape[1]
        o_ref[:, c0:c0 + n] = jnp.dot(xn, w_ref[...], preferred_element_type=F32)
        c0 += n


def _norm_matmul(x, nw, ws_bf16, tm=512):
    s, d = x.shape
    n = sum(w.shape[1] for w in ws_bf16)
    return pl.pallas_call(
        _norm_matmul_kernel,
        grid=(s // tm,),
        in_specs=[pl.BlockSpec((tm, d), lambda i: (i, 0)), _full((1, d))] + [_full(w.shape) for w in ws_bf16],
        out_specs=pl.BlockSpec((tm, n), lambda i: (i, 0)),
        out_shape=jax.ShapeDtypeStruct((s, n), F32),
        compiler_params=_cparams(("parallel",)),
        name="norm_matmul",
    )(x, nw.reshape(1, d), *ws_bf16)


def _seg_rms(x, bmat, seg):
    outs = []
    for c0 in range(0, x.shape[1], 256):
        w = min(256, x.shape[1] - c0)
        xc = x[:, c0:c0 + w]
        ss = jnp.dot(xc * xc, bmat[:w, :w], precision=HIGHEST, preferred_element_type=F32)
        outs.append(xc * lax.rsqrt(ss * (1.0 / seg) + EPS))
    return outs[0] if len(outs) == 1 else jnp.concatenate(outs, axis=1)


def _tile_lanes(t, width):
    reps = width // LANES
    return t if reps == 1 else jnp.concatenate([t] * reps, axis=1)


def _rope(x, cos, sin, half):
    width = x.shape[1]
    left = pltpu.roll(x, width - half, 1)
    right = pltpu.roll(x, half, 1)
    lane = lax.broadcasted_iota(jnp.int32, x.shape, 1)
    swapped = jnp.where((lane & (2 * half - 1)) < half, left, right)
    return x * _tile_lanes(cos, width) + swapped * _tile_lanes(sin, width)


def _prep_kernel(proj_ref, c64_ref, s64_ref, c32_ref, s32_ref, fb_ref, wfq_ref, wfk_ref, wnq_ref, wnk_ref,
                 wdq_ref, wdk_ref, b64_ref, b32_ref,
                 fq_o, fk_o, fv_o, chi_o, cmid_o, clo_o, nqn_o, nqr_o, kvc_o, ksr_o, vs_o, kwr_o, vw_o,
                 gate_o, dq_o, dk_o, dv_o, carry_sc, *, tm):
    def seg(name):
        c0, w = _SEG[name]
        return proj_ref[:, c0:c0 + w]

    b64 = b64_ref[...]
    b32 = b32_ref[...]
    c64, s64, c32, s32 = c64_ref[...], s64_ref[...], c32_ref[...], s32_ref[...]

    fq_o[...] = (_seg_rms(seg("fq"), b64, FOX_HD) * wfq_ref[...] * (FOX_HD ** -0.5 * LOG2E)).astype(BF16)
    fk_o[...] = (_seg_rms(seg("fk"), b64, FOX_HD) * wfk_ref[...]).astype(BF16)
    fv_o[...] = seg("fv").astype(BF16)

    misc = seg("misc")
    gate_o[...] = 1.0 / (1.0 + jnp.exp(-misc))
    t = misc + fb_ref[...]
    logf = jnp.minimum(t, 0.0) - jnp.log1p(jnp.exp(-jnp.abs(t)))

    @pl.when(pl.program_id(0) == 0)
    def _():
        carry_sc[...] = jnp.zeros_like(carry_sc)

    r = lax.broadcasted_iota(jnp.int32, (tm, tm), 0)
    c = lax.broadcasted_iota(jnp.int32, (tm, tm), 1)
    tri = (c <= r).astype(F32)
    csum = jnp.dot(tri, logf, precision=HIGHEST, preferred_element_type=F32) + carry_sc[0:1, :]
    carry_sc[...] = jnp.broadcast_to(csum[tm - 1:tm, :], carry_sc.shape)
    csum = csum * LOG2E
    hi = csum.astype(BF16)
    r1 = csum - hi.astype(F32)
    mid = r1.astype(BF16)
    lo = (r1 - mid.astype(F32)).astype(BF16)
    chi_o[...] = hi
    cmid_o[...] = mid
    clo_o[...] = lo

    nqn = _seg_rms(seg("nq"), b64, NSA_HD) * wnq_ref[...]
    nqn_o[...] = (nqn * (NSA_HD ** -0.5 * LOG2E)).astype(BF16)
    nqr_o[...] = (_rope(nqn, c64, s64, NSA_HD // 2) * (NSA_HD ** -0.5 * LOG2E)).astype(BF16)
    kvc_o[:, 0:128] = seg("nkc").astype(BF16)
    kvc_o[:, 128:256] = seg("nvc").astype(BF16)
    wnk = wnk_ref[...]
    ksr_o[...] = _rope(_seg_rms(seg("nks"), b64, NSA_HD) * wnk, c64, s64, NSA_HD // 2).astype(BF16)
    vs_o[...] = seg("nvs").astype(BF16)
    kwr_o[...] = _rope(_seg_rms(seg("nkw"), b64, NSA_HD) * wnk, c64, s64, NSA_HD // 2).astype(BF16)
    vw_o[...] = seg("nvw").astype(BF16)

    dqn = _rope(_seg_rms(seg("dq"), b32, DIFF_QK) * wdq_ref[...], c32, s32, DIFF_QK // 2)
    dq_o[...] = (dqn * (DIFF_QK ** -0.5 * LOG2E)).astype(BF16)
    dk_o[...] = _rope(_seg_rms(seg("dk"), b32, DIFF_QK) * wdk_ref[...], c32, s32, DIFF_QK // 2).astype(BF16)
    dv_o[...] = seg("dv").astype(BF16)


def _block_diag_ones(n, seg):
    i = np.arange(n)
    return jnp.asarray((i[:, None] // seg == i[None, :] // seg).astype(np.float32))


def _prep(proj, tabs, fb, wfq, wfk, wnq, wnk, wdq, wdk, tm=256):
    s = proj.shape[0]
    c64, s64, c32, s32 = tabs
    b64 = _block_diag_ones(256, 64)
    b32 = _block_diag_ones(256, 32)

    def tiled(w, width):
        return jnp.tile(w.reshape(1, -1), (1, width // w.shape[-1]))

    consts = [fb, tiled(wfq, 256), tiled(wfk, 256), tiled(wnq, 512), tiled(wnk, 128), tiled(wdq, 256),
              tiled(wdk, 256), b64, b32]
    widths = [(256, BF16), (256, BF16), (256, BF16), (128, BF16), (128, BF16), (128, BF16), (512, BF16),
              (512, BF16), (256, BF16), (128, BF16), (128, BF16), (128, BF16), (128, BF16), (128, F32),
              (256, BF16), (256, BF16), (256, BF16)]
    row = lambda w: pl.BlockSpec((tm, w), lambda i: (i, 0))
    return pl.pallas_call(
        functools.partial(_prep_kernel, tm=tm),
        grid=(s // tm,),
        in_specs=[row(PROJ_W)] + [row(LANES)] * 4 + [_full(c.shape) for c in consts],
        out_specs=[row(w) for w, _ in widths],
        out_shape=[jax.ShapeDtypeStruct((s, w), dt) for w, dt in widths],
        scratch_shapes=[pltpu.VMEM((8, LANES), F32)],
        compiler_params=_cparams(("arbitrary",)),
        name="head_prep",
    )(proj, c64, s64, c32, s32, *consts)


QK_DIM = 128
COL_SHIFT = 64
COL_SEL = 80
SCORE_BOUND = 60.0
SHIFT_HEADROOM = 60.0


def _flash_kernel(code_ref, qT_ref, k_ref, vT_ref, *rest, groups, tq, tk, window, has_sel, bounded):
    rest = list(rest)
    sel_ref = rest.pop(0) if has_sel else None
    o_ref = rest.pop(0)
    m_sc = None if bounded else rest.pop(0)
    acc_sc = rest.pop(0)
    code = code_ref[pl.program_id(1)]
    i = code & 0xFFF
    j = (code >> 12) & 0xFFF
    first = (code >> 24) & 1
    last = (code >> 25) & 1
    rows = groups * tq
    dv = o_ref.shape[2]

    @pl.when(first == 1)
    def _():
        if not bounded:
            m_sc[...] = jnp.full_like(m_sc, NEG_INIT)
        acc_sc[...] = jnp.zeros_like(acc_sc)

    def step(masked):
        q = qT_ref[0, 0]
        if has_sel:
            nb = tk // SLC_LEN
            q = jnp.concatenate([q[0:COL_SEL], _tile_lanes_any(sel_ref[0], groups), q[COL_SEL + nb:]], axis=0)
        s = jnp.dot(k_ref[0], q, preferred_element_type=F32)
        if masked:
            kpos = j * tk + lax.broadcasted_iota(jnp.int32, (tk, rows), 0)
            lane = lax.broadcasted_iota(jnp.int32, (tk, rows), 1)
            qpos = i * tq + (lane & (tq - 1))
            keep = kpos <= qpos
            if window is not None:
                keep = keep & (kpos > qpos - window)
            s = jnp.where(keep, s, MASKVAL)
        if bounded:
            acc_sc[...] += jnp.dot(vT_ref[0], jnp.exp2(s).astype(BF16), preferred_element_type=F32)
        else:
            m_prev = m_sc[...]
            m_new = jnp.maximum(m_prev, jnp.max(s, axis=0, keepdims=True))
            p = jnp.exp2(s - m_new).astype(BF16)
            acc_sc[...] = acc_sc[...] * jnp.exp2(m_prev - m_new) + jnp.dot(vT_ref[0], p, preferred_element_type=F32)
            m_sc[...] = m_new

    if window is not None:
        step(True)
    else:
        needs_mask = (j + 1) * tk - 1 > i * tq

        @pl.when(needs_mask)
        def _():
            step(True)

        @pl.when(jnp.logical_not(needs_mask))
        def _():
            step(False)

    @pl.when(last == 1)
    def _():
        l = acc_sc[dv:dv + 1, :]
        o_ref[0, 0] = acc_sc[0:dv, :] / jnp.where(l > 0.0, l, 1.0)


def _tile_lanes_any(t, reps):
    return t if reps == 1 else jnp.concatenate([t] * reps, axis=1)


def _flash_call(qT, k, vT, sel, *, groups, tq, tk, window, bounded):
    hkv, nq, dk, rows = qT.shape
    dve = vT.shape[1]
    dv = dve - 8
    codes = []
    for i in range(nq):
        q_lo, q_hi = i * tq, i * tq + tq - 1
        j_hi = q_hi // tk
        j_lo = 0 if window is None else max(0, (q_lo - window + 1) // tk)
        for j in range(j_lo, j_hi + 1):
            codes.append(i | (j << 12) | (int(j == j_lo) << 24) | (int(j == j_hi) << 25))
    codes = jnp.asarray(np.asarray(codes, dtype=np.int32))
    nsteps = codes.shape[0]

    def ti(c, s):
        return c[s] & 0xFFF

    def tj(c, s):
        return (c[s] >> 12) & 0xFFF

    in_specs = [
        pl.BlockSpec((1, 1, dk, rows), lambda h, s, c: (h, ti(c, s), 0, 0)),
        pl.BlockSpec((1, tk, dk), lambda h, s, c: (h, tj(c, s), 0)),
        pl.BlockSpec((1, dve, tk), lambda h, s, c: (h, 0, tj(c, s))),
    ]
    args = [qT, k, vT]
    if sel is not None:
        in_specs.append(pl.BlockSpec((1, tk // SLC_LEN, tq), lambda h, s, c: (h, tj(c, s), ti(c, s))))
        args.append(sel)
    scratch = [pltpu.VMEM((dve, rows), F32)]
    if not bounded:
        scratch = [pltpu.VMEM((1, rows), F32)] + scratch
    kern = functools.partial(_flash_kernel, groups=groups, tq=tq, tk=tk, window=window, has_sel=sel is not None,
                             bounded=bounded)
    return pl.pallas_call(
        kern,
        grid_spec=pltpu.PrefetchScalarGridSpec(
            num_scalar_prefetch=1,
            grid=(hkv, nsteps),
            in_specs=in_specs,
            out_specs=pl.BlockSpec((1, 1, dv, rows), lambda h, s, c: (h, ti(c, s), 0, 0)),
            scratch_shapes=scratch,
        ),
        out_shape=jax.ShapeDtypeStruct((hkv, nq, dv, rows), F32),
        compiler_params=_cparams(("parallel", "arbitrary")),
        name="flash_bounded" if bounded else "flash_online",
    )(codes, *args)


def _flash(qT, k, vT, sel=None, *, bound_ok, groups, tq, tk, window=None):
    hkv, dv, s_len = vT.shape
    vT = jnp.concatenate([vT, jnp.ones((hkv, 1, s_len), vT.dtype), jnp.zeros((hkv, 7, s_len), vT.dtype)], axis=1)
    call = functools.partial(_flash_call, qT, k, vT, sel, groups=groups, tq=tq, tk=tk, window=window)
    return lax.cond(bound_ok, lambda: call(bounded=True), lambda: call(bounded=False))


def _extend(x, heads, cols):
    s = x.shape[0]
    x = x.reshape(s, heads, -1)
    pieces, at = [x], x.shape[-1]
    for c in sorted(cols):
        if c > at:
            pieces.append(jnp.zeros((s, heads, c - at), BF16))
        v = cols[c]
        v = jnp.broadcast_to(jnp.asarray(v, F32).astype(BF16), (s, heads)) if jnp.ndim(v) == 0 else v.astype(BF16)
        pieces.append(v[:, :, None])
        at = c + 1
    pieces.append(jnp.zeros((s, heads, QK_DIM - at), BF16))
    return jnp.concatenate(pieces, axis=-1)


def _score_bound(wq, wk, seg, scale):
    return 1.02 * seg * jnp.max(jnp.abs(wq)) * jnp.max(jnp.abs(wk)) * scale * LOG2E


def _to_qT(q, heads, groups, tq):
    s = q.shape[0]
    d = q.shape[1] // heads
    x = q.reshape(s // tq, tq, heads // groups, groups, d)
    return x.transpose(2, 0, 4, 3, 1).reshape(heads // groups, s // tq, d, groups * tq)


def _from_oT(oT, groups, tq):
    hkv, nq, d, _ = oT.shape
    x = oT.reshape(hkv, nq, d, groups, tq).transpose(1, 4, 0, 3, 2)
    return x.reshape(nq * tq, hkv * groups * d)


def _to_heads(x, heads):
    s = x.shape[0]
    return x.reshape(s, heads, -1).transpose(1, 0, 2)


def _to_headsT(x, heads):
    s = x.shape[0]
    return x.reshape(s, heads, -1).transpose(1, 2, 0)


def _compress_kernel(r_ref, rn_ref, w_ref, pos_ref, nw_ref, o_ref, *, normalize):
    half = r_ref.shape[2]
    wa = w_ref[0, 0:half, :].astype(BF16)
    wb = w_ref[0, half:2 * half, :].astype(BF16)
    const = jnp.dot(pos_ref[0], w_ref[0], precision=HIGHEST, preferred_element_type=F32)
    out = (jnp.dot(r_ref[0], wa, preferred_element_type=F32)
           + jnp.dot(rn_ref[0], wb, preferred_element_type=F32) + const)
    if normalize:
        out = _rms(out, nw_ref[...])
    o_ref[0] = out.astype(BF16)


def _compress(r, w, pos, nw, normalize):
    g, n, width = r.shape
    hd = w.shape[-1]
    r_next = jnp.concatenate([r[:, 1:], jnp.zeros((g, 1, width), r.dtype)], axis=1)
    rspec = pl.BlockSpec((1, n, width), lambda i: (i, 0, 0))
    return pl.pallas_call(
        functools.partial(_compress_kernel, normalize=normalize),
        grid=(g,),
        in_specs=[rspec, rspec, _full((1,) + w.shape), _full((1,) + pos.shape), _full((1, hd))],
        out_specs=pl.BlockSpec((1, n, hd), lambda i: (i, 0, 0)),
        out_shape=jax.ShapeDtypeStruct((g, n, hd), BF16),
        compiler_params=_cparams(("parallel",)),
        name="nsa_compress",
    )(r, r_next, w[None], pos[None], nw.reshape(1, hd))


def _nsa_cmp_kernel(qT_ref, kc_ref, vcT_ref, ov_ref, o_ref, sel_ref, *, tq, groups):
    i = pl.program_id(1)
    ncmp = kc_ref.shape[1]
    rows = groups * tq
    s = jnp.dot(kc_ref[0], qT_ref[0, 0], preferred_element_type=F32)
    n_idx = lax.broadcasted_iota(jnp.int32, (ncmp, rows), 0)
    lane = lax.broadcasted_iota(jnp.int32, (ncmp, rows), 1)
    qpos = i * tq + (lane & (tq - 1))
    keep = n_idx * CMP_STRIDE + (CMP_LEN - 1) <= qpos
    sm = jnp.where(keep, s, NEG_INIT)
    m = jnp.max(sm, axis=0, keepdims=True)
    e = jnp.where(keep, jnp.exp2(sm - m), 0.0)
    den = jnp.sum(e, axis=0, keepdims=True)
    p = e / jnp.where(den > 0.0, den, 1.0)
    o_ref[0, 0] = jnp.dot(vcT_ref[0], p.astype(BF16), preferred_element_type=F32)

    psum = p[:, 0:tq]
    for g in range(1, groups):
        psum = psum + p[:, g * tq:(g + 1) * tq]
    imp = jnp.dot(ov_ref[...], psum, precision=HIGHEST, preferred_element_type=F32)
    ns = imp.shape[0]
    blk = lax.broadcasted_iota(jnp.int32, (ns, tq), 0)
    qp = i * tq + lax.broadcasted_iota(jnp.int32, (ns, tq), 1)
    cur = qp // SLC_LEN
    valid = blk <= cur
    forced = (blk == 0) | (blk == cur) | (blk == cur - 1)
    work = jnp.where(valid, imp + jnp.where(forced, FORCE_BONUS, 0.0), NEG_INIT)
    picked = jnp.zeros((ns, tq), dtype=jnp.bool_)
    for _ in range(SLC_TOPK):
        mx = jnp.max(work, axis=0, keepdims=True)
        idx = jnp.min(jnp.where(work == mx, blk, ns), axis=0, keepdims=True)
        pick = (blk == idx) & (mx > 0.5 * NEG_INIT)
        picked = picked | pick
        work = jnp.where(pick, NEG_INIT, work)
    sel_ref[0] = jnp.where(picked, 0.0, MASKVAL).astype(BF16)


def _nsa_cmp(qnT, kcc, vccT, overlapT, tq):
    hkv, nq, hd, rows = qnT.shape
    ncmp = kcc.shape[1]
    ns = overlapT.shape[0]
    s = nq * tq
    return pl.pallas_call(
        functools.partial(_nsa_cmp_kernel, tq=tq, groups=rows // tq),
        grid=(hkv, nq),
        in_specs=[pl.BlockSpec((1, 1, hd, rows), lambda h, i: (h, i, 0, 0)),
                  pl.BlockSpec((1, ncmp, hd), lambda h, i: (h, 0, 0)),
                  pl.BlockSpec((1, hd, ncmp), lambda h, i: (h, 0, 0)),
                  _full(overlapT.shape)],
        out_specs=[pl.BlockSpec((1, 1, hd, rows), lambda h, i: (h, i, 0, 0)),
                   pl.BlockSpec((1, ns, tq), lambda h, i: (h, 0, i))],
        out_shape=[jax.ShapeDtypeStruct((hkv, nq, hd, rows), F32), jax.ShapeDtypeStruct((hkv, ns, s), BF16)],
        compiler_params=_cparams(("parallel", "parallel")),
        name="nsa_cmp_select",
    )(qnT, kcc, vccT, overlapT)


def _mix_out_kernel(lam_ref, h_ref, fox_ref, oc_ref, os_ref, ow_ref, gc_ref, gs_ref, gw_ref, d0_ref, d1_ref,
                    sw_ref, b64_ref, w_ref, o_ref, *, diff_scale):
    lam = lam_ref[0]
    nsa = gc_ref[...] * oc_ref[...] + gs_ref[...] * os_ref[...] + gw_ref[...] * ow_ref[...]
    a = d0_ref[...] - lam * d1_ref[...]
    ss = jnp.dot(a * a, b64_ref[...], precision=HIGHEST, preferred_element_type=F32)
    diff = a * lax.rsqrt(ss * (1.0 / DIFF_V) + EPS) * sw_ref[...] * diff_scale
    nf = FOX_HEADS * FOX_HD
    nn = NSA_HEADS * NSA_HD
    acc = h_ref[...]
    acc = acc + jnp.dot(fox_ref[...].astype(BF16), w_ref[0:nf, :], preferred_element_type=F32)
    acc = acc + jnp.dot(nsa.astype(BF16), w_ref[nf:nf + nn, :], preferred_element_type=F32)
    acc = acc + jnp.dot(diff.astype(BF16), w_ref[nf + nn:, :], preferred_element_type=F32)
    o_ref[...] = acc


def _mix_out(lam, h, fox, oc, os_, ow, gc, gs, gw, d0, d1, sw, w_out_bf16, diff_scale, tm=512):
    s, d = h.shape
    row = lambda w: pl.BlockSpec((tm, w), lambda i: (i, 0))
    b64 = _block_diag_ones(256, 64)
    return pl.pallas_call(
        functools.partial(_mix_out_kernel, diff_scale=diff_scale),
        grid=(s // tm,),
        in_specs=[pl.BlockSpec(memory_space=pltpu.SMEM), row(d), row(256)] + [row(512)] * 6 + [row(256), row(256),
                  _full((1, 256)), _full((256, 256)), _full(w_out_bf16.shape)],
        out_specs=row(d),
        out_shape=jax.ShapeDtypeStruct((s, d), F32),
        compiler_params=_cparams(("parallel",)),
        name="mix_out",
    )(lam, h, fox, oc, os_, ow, gc, gs, gw, d0, d1, sw, b64, w_out_bf16)


def _peer_query_kernel(h_ref, nw_ref, wq_ref, keys_ref, xnT_o, scT_o):
    xn = _rms(h_ref[...], nw_ref[...])
    xnT_o[...] = xn.T.astype(BF16)
    q = jnp.dot(xn.astype(BF16), wq_ref[...], preferred_element_type=F32).astype(BF16)
    half = PEER_DQ // 2
    for b in range(2 * PEER_HEADS):
        sc = jnp.dot(q[:, b * half:(b + 1) * half], keys_ref[b], preferred_element_type=F32)
        scT_o[b] = sc.T


def _peer_query(h, nw, wq_bf16, keysT_bf16, tm=512):
    s, d = h.shape
    n = wq_bf16.shape[1]
    nb = 2 * PEER_HEADS
    return pl.pallas_call(
        _peer_query_kernel,
        grid=(s // tm,),
        in_specs=[pl.BlockSpec((tm, d), lambda i: (i, 0)), _full((1, d)), _full((d, n)), _full(keysT_bf16.shape)],
        out_specs=[pl.BlockSpec((d, tm), lambda i: (0, i)), pl.BlockSpec((nb, PEER_NKEYS, tm), lambda i: (0, 0, i))],
        out_shape=[jax.ShapeDtypeStruct((d, s), BF16), jax.ShapeDtypeStruct((nb, PEER_NKEYS, s), F32)],
        compiler_params=_cparams(("parallel",)),
        name="peer_query",
    )(h, nw.reshape(1, d), wq_bf16, keysT_bf16)


def _top16_rows(s, exact_ties):
    n = s.shape[0]
    row = lax.broadcasted_iota(jnp.int32, s.shape, 0)
    rank = jnp.full(s.shape, float(PEER_TOPK), dtype=F32)
    work = s
    vals = []
    for r in range(PEER_TOPK):
        mx = jnp.max(work, axis=0, keepdims=True)
        if exact_ties:
            pick = row == jnp.min(jnp.where(work == mx, row, n), axis=0, keepdims=True)
        else:
            pick = work == mx
        rank = jnp.where(pick, float(r), rank)
        work = jnp.where(pick, NEG_INIT, work)
        vals.append(mx)
    return rank, vals


def _peer_route(s1, s2, exact_ties):
    rank1, v1 = _top16_rows(s1, exact_ties)
    rank2, v2 = _top16_rows(s2, exact_ties)
    k = PEER_TOPK
    tm = s1.shape[1]
    r16 = lax.broadcasted_iota(jnp.int32, (k, tm), 0)
    v1m = jnp.zeros((k, tm), F32)
    v2m = jnp.zeros((k, tm), F32)
    for r in range(k):
        v1m = jnp.where(r16 == r, v1[r], v1m)
        v2m = jnp.where(r16 == r, v2[r], v2m)
    ea = jnp.exp(v1m - v1[0])
    eb = jnp.exp(v2m - v2[0])
    cands, gates = [v1m[0:1] + v2m], [ea[0:1] * eb]
    for r in range(1, 8):
        cands.append(v1m[r:r + 1] + v2m[0:8])
        gates.append(ea[r:r + 1] * eb[0:8])
    cands.append(v1m[8:16] + v2m[0:1])
    gates.append(ea[8:16] * eb[0:1])
    cand = jnp.concatenate(cands, axis=0)
    gate = jnp.concatenate(gates, axis=0)
    ncand = cand.shape[0]
    crow = lax.broadcasted_iota(jnp.int32, cand.shape, 0)
    for r in range(2, 8):
        start = 16 + 8 * (r - 1)
        cand = jnp.where((crow >= start + k // (r + 1)) & (crow < start + 8), NEG_INIT, cand)
    work = cand
    picked = jnp.zeros(cand.shape, dtype=jnp.bool_)
    for _ in range(k):
        mx = jnp.max(work, axis=0, keepdims=True)
        if exact_ties:
            pick = crow == jnp.min(jnp.where(work == mx, crow, ncand), axis=0, keepdims=True)
        else:
            pick = work == mx
        picked = picked | pick
        work = jnp.where(pick, NEG_INIT, work)
    pf = picked.astype(F32)
    z = jnp.sum(pf * gate, axis=0, keepdims=True)
    cnt = [jnp.sum(pf[0:16, :], axis=0, keepdims=True)]
    cnt += [jnp.sum(pf[8 + 8 * r:16 + 8 * r, :], axis=0, keepdims=True) for r in range(1, 8)]
    cnt += [pf[64 + r:65 + r, :] for r in range(8, 16)]
    cmap = jnp.zeros(s1.shape, dtype=F32)
    for r in range(k):
        cmap = jnp.where(rank1 == float(r), cnt[r], cmap)
    in1 = rank1 < float(k)
    in2 = rank2 < float(k)
    a = jnp.where(in1, jnp.exp(s1 - v1[0]), 0.0) / z
    b = jnp.where(in2, jnp.exp(s2 - v2[0]), 0.0)
    excess = (jnp.abs(jnp.sum(in1.astype(F32), axis=0, keepdims=True) - k)
              + jnp.abs(jnp.sum(in2.astype(F32), axis=0, keepdims=True) - k)
              + jnp.abs(jnp.sum(pf, axis=0, keepdims=True) - k))
    return a, cmap, b, rank2, excess


def _peer_topk_kernel(sc_ref, a_o, c_o, b_o, r_o):
    def run(exact_ties):
        a, c, b, rank2, excess = _peer_route(sc_ref[0], sc_ref[1], exact_ties)
        a_o[0] = a
        c_o[0] = c
        b_o[0] = b.astype(BF16)
        r_o[0] = rank2.astype(BF16)
        return excess

    excess = run(False)

    @pl.when(jnp.max(excess) > 0.0)
    def _():
        run(True)


def _peer_topk(scT, tm=256):
    n2, nk, s = scT.shape
    heads = n2 // 2
    ospec = pl.BlockSpec((1, nk, tm), lambda h, t: (h, 0, t))
    return pl.pallas_call(
        _peer_topk_kernel,
        grid=(heads, s // tm),
        in_specs=[pl.BlockSpec((2, nk, tm), lambda h, t: (h, 0, t))],
        out_specs=[ospec] * 4,
        out_shape=[jax.ShapeDtypeStruct((heads, nk, s), dt) for dt in (F32, F32, BF16, BF16)],
        compiler_params=_cparams(("parallel", "parallel")),
        name="peer_topk",
    )(scT)


_ERF_ALPHA = (-2.72614225801306e-10, 2.77068142495902e-08, -2.10102402082508e-06, -5.69250639462346e-05,
              -7.34990630326855e-04, -2.95459980854025e-03, -1.60960333262415e-02)
_ERF_BETA = (-1.45660718464996e-05, -2.13374055278905e-04, -1.68282697438203e-03, -7.37332916720468e-03,
             -1.42647390514189e-02)


def _gelu_exact(x):
    return 0.5 * x * (1.0 + lax.erf(x * (2.0 ** -0.5)))


def _peer_main_kernel(xT_ref, u_ref, v_ref, a_ref, c_ref, b_ref, r_ref, o_ref, w_sc, *, ec):
    ci = pl.program_id(1)

    @pl.when(ci == 0)
    def _():
        o_ref[...] = jnp.zeros_like(o_ref)

    hid = jnp.dot(u_ref[...], xT_ref[...], preferred_element_type=F32)
    act = _gelu_exact(hid).astype(BF16)
    nk = PEER_NKEYS
    tm = hid.shape[1]

    def row(ref, h, i1):
        x16 = jnp.broadcast_to(ref[h, pl.ds(i1, 1), :], (16, tm)).astype(BF16)
        return jnp.concatenate([x16] * (nk // 16), axis=0)

    for ii in range(ec // nk):
        i1 = ci * (ec // nk) + ii
        w = None
        for h in range(PEER_HEADS):
            term = jnp.where(r_ref[h] < row(c_ref, h, i1), b_ref[h], 0.0) * row(a_ref, h, i1)
            w = term if w is None else w + term
        w_sc[ii * nk:(ii + 1) * nk, :] = w * act[ii * nk:(ii + 1) * nk, :]
    o_ref[...] += jnp.dot(w_sc[...].T, v_ref[...], preferred_element_type=F32)


def _peer_main(xnT, u_bf16, v_bf16, a, c, b, r, tm=512, ec=1024):
    d, s = xnT.shape
    e = u_bf16.shape[0]
    heads, nk, _ = a.shape
    rt = pl.BlockSpec((heads, nk, tm), lambda t, ci: (0, 0, t))
    wt = pl.BlockSpec((ec, d), lambda t, ci: (ci, 0))
    return pl.pallas_call(
        functools.partial(_peer_main_kernel, ec=ec),
        grid=(s // tm, e // ec),
        in_specs=[pl.BlockSpec((d, tm), lambda t, ci: (0, t)), wt, wt, rt, rt, rt, rt],
        out_specs=pl.BlockSpec((tm, d), lambda t, ci: (t, 0)),
        out_shape=jax.ShapeDtypeStruct((s, d), F32),
        scratch_shapes=[pltpu.VMEM((ec, tm), BF16)],
        compiler_params=_cparams(("parallel", "arbitrary")),
        name="peer_main",
    )(xnT, u_bf16, v_bf16, a, c, b, r)


def _ple_kernel(h_ref, peer_ref, p_ref, nw_ref, wg_ref, wp_ref, o_ref):
    h2 = h_ref[...] + peer_ref[...]
    xn = _rms(h2, nw_ref[...]).astype(BF16)
    gate = 1.0 / (1.0 + jnp.exp(-jnp.dot(xn, wg_ref[...], preferred_element_type=F32)))
    emb = jnp.dot(p_ref[...].astype(BF16), wp_ref[...], preferred_element_type=F32)
    o_ref[...] = h2 + gate * emb


def _ple(h, peer, p, nw, wg_bf16, wp_bf16, tm=512):
    s, d = h.shape
    pd = p.shape[1]
    row = lambda w: pl.BlockSpec((tm, w), lambda i: (i, 0))
    return pl.pallas_call(
        _ple_kernel,
        grid=(s // tm,),
        in_specs=[row(d), row(d), row(pd), _full((1, d)), _full((d, d)), _full((pd, d))],
        out_specs=row(d),
        out_shape=jax.ShapeDtypeStruct((s, d), F32),
        compiler_params=_cparams(("parallel",)),
        name="ple",
    )(h, peer, p, nw.reshape(1, d), wg_bf16, wp_bf16)


def _split_w_in(w):
    f0 = 3 * FOX_HEADS * FOX_HD
    n0 = f0 + FOX_HEADS
    g0 = n0 + _SEG["dq"][0] - _SEG["nq"][0]
    d0 = g0 + 3 * NSA_HEADS
    misc = jnp.concatenate([w[:, f0:n0], w[:, g0:d0], jnp.zeros((w.shape[0], LANES - (n0 - f0) - (d0 - g0)), w.dtype)],
                           axis=1)
    return [t.astype(BF16) for t in (w[:, :f0], w[:, n0:g0], w[:, d0:], misc)]


def _overlap_T(s):
    n = np.arange(s // CMP_STRIDE)[None, :] * CMP_STRIDE
    m = np.arange(s // SLC_LEN)[:, None] * SLC_LEN
    return jnp.asarray(((n < m + SLC_LEN) & (n + CMP_LEN > m)).astype(np.float32))


def _layer(h, p_i, tabs, layer, attn_norm_w, w_in, fox_f_bias, fox_q_norm_w, fox_k_norm_w, nsa_q_norm_w,
           nsa_k_norm_w, nsa_cmp_pos, nsa_cmp_w, diff_q_norm_w, diff_k_norm_w, diff_lambda, diff_subln_w,
           w_out, ffn_norm_w, peer_w_q, peer_sub_keys, peer_u, peer_v, ple_norm_w, ple_w_gate, ple_w_proj):
    s = h.shape[0]
    proj = _norm_matmul(h, attn_norm_w, _split_w_in(w_in))
    fb = jnp.zeros((1, LANES), F32).at[0, :FOX_HEADS].set(fox_f_bias)
    (fq, fk, fv, chi, cmid, clo, nqn, nqr, kvc, ksr, vs, kwr, vw, gates, dq, dk, dv) = _prep(
        proj, tabs, fb, fox_q_norm_w, fox_k_norm_w, nsa_q_norm_w, nsa_k_norm_w, diff_q_norm_w, diff_k_norm_w)

    tq_f = Q_LANES
    u_f =_score_bound(fox_q_norm_w, fox_k_norm_w, FOX_HD, FOX_HD ** -0.5)
    c3 = [t[:, :FOX_HEADS] for t in (chi, cmid, clo)]
    q_ext = _extend(fq, FOX_HEADS, {COL_SHIFT: SHIFT_HEADROOM - u_f, 65: 1.0, 66: 1.0, 67: 1.0,
                                    68: c3[0], 69: c3[1], 70: c3[2]})
    k_ext = _extend(fk, FOX_HEADS, {COL_SHIFT: 1.0, 65: -c3[0], 66: -c3[1], 67: -c3[2], 68: 1.0, 69: 1.0, 70: 1.0})
    o_fox = _flash(_to_qT(q_ext.reshape(s, -1), FOX_HEADS, 1, tq_f), k_ext.transpose(1, 0, 2),
                   _to_headsT(fv, FOX_HEADS), bound_ok=u_f <= SCORE_BOUND, groups=1, tq=tq_f, tk=TK_CAUSAL)
    o_fox = _from_oT(o_fox, 1, tq_f)

    tq_n = Q_LANES // NSA_G
    kvr =kvc.reshape(s // CMP_STRIDE, CMP_STRIDE, 2, NSA_KV, NSA_HD).transpose(2, 3, 0, 1, 4)
    kvr = kvr.reshape(2, NSA_KV, s // CMP_STRIDE, CMP_STRIDE * NSA_HD)
    posf = nsa_cmp_pos.reshape(2, 1, CMP_LEN * NSA_HD)
    kcc = _compress(kvr[0], nsa_cmp_w[0], posf[0], nsa_k_norm_w, True)
    vcc = _compress(kvr[1], nsa_cmp_w[1], posf[1], nsa_k_norm_w, False)
    o_c, sel = _nsa_cmp(_to_qT(nqn, NSA_HEADS, NSA_G, tq_n), kcc, vcc.transpose(0, 2, 1), _overlap_T(s), tq_n)
    u_n = _score_bound(nsa_q_norm_w, nsa_k_norm_w, NSA_HD, NSA_HD ** -0.5)
    qrT = _to_qT(_extend(nqr, NSA_HEADS, {COL_SHIFT: SHIFT_HEADROOM - u_n}).reshape(s, -1), NSA_HEADS, NSA_G, tq_n)
    blk_in_tile = (jnp.arange(s) % TK_CAUSAL) // SLC_LEN
    onehot = (blk_in_tile[:, None] == jnp.arange(TK_CAUSAL // SLC_LEN)[None, :]).astype(BF16)
    ks_cols = {COL_SHIFT: 1.0}
    ks_cols.update({COL_SEL + b: jnp.broadcast_to(onehot[:, b:b + 1], (s, NSA_KV))
                    for b in range(TK_CAUSAL // SLC_LEN)})
    ks_ext = _extend(ksr, NSA_KV, ks_cols).transpose(1, 0, 2)
    kw_ext = _extend(kwr, NSA_KV, {COL_SHIFT: 1.0}).transpose(1, 0, 2)
    o_s = _flash(qrT, ks_ext, _to_headsT(vs, NSA_KV), sel, bound_ok=u_n <= SCORE_BOUND, groups=NSA_G, tq=tq_n,
                 tk=TK_CAUSAL)
    o_w = _flash(qrT, kw_ext, _to_headsT(vw, NSA_KV), bound_ok=u_n <= SCORE_BOUND, groups=NSA_G, tq=tq_n, tk=512,
                 window=WIN)
    o_c, o_s, o_w = (_from_oT(o, NSA_G, tq_n) for o in (o_c, o_s, o_w))
    g3 = gates[:, FOX_HEADS:FOX_HEADS + 3 * NSA_HEADS].reshape(s, NSA_HEADS, 3)
    gc, gs, gw = (jnp.repeat(g3[:, :, b], NSA_HD, axis=1) for b in range(3))

    tq_d = Q_LANES // 2
    dq4 =dq.reshape(s, DIFF_HEADS, 2, DIFF_QK)
    zq = jnp.zeros((s, DIFF_HEADS, DIFF_QK), BF16)
    q_maps = jnp.stack([jnp.concatenate([dq4[:, :, 0], zq], axis=-1),
                        jnp.concatenate([zq, dq4[:, :, 1]], axis=-1)], axis=2)
    u_d = _score_bound(diff_q_norm_w, diff_k_norm_w, DIFF_QK, DIFF_QK ** -0.5)
    qd_ext = _extend(q_maps.reshape(s, -1), 2 * DIFF_HEADS, {COL_SHIFT: SHIFT_HEADROOM - u_d})
    kd_ext = _extend(dk, DIFF_HEADS, {COL_SHIFT: 1.0}).transpose(1, 0, 2)
    o_d = _flash(_to_qT(qd_ext.reshape(s, -1), 2 * DIFF_HEADS, 2, tq_d), kd_ext, _to_headsT(dv, DIFF_HEADS),
                 bound_ok=u_d <= SCORE_BOUND, groups=2, tq=tq_d, tk=TK_CAUSAL)
    o_d = _from_oT(o_d, 2, tq_d).reshape(s, DIFF_HEADS, 2, DIFF_V)
    d0 = o_d[:, :, 0].reshape(s, -1)
    d1 = o_d[:, :, 1].reshape(s, -1)
    lv = diff_lambda.astype(F32)
    lam_init = 0.8 - 0.6 * math.exp(-0.3 * layer)
    lam = (jnp.exp(jnp.sum(lv[0] * lv[1])) - jnp.exp(jnp.sum(lv[2] * lv[3])) + lam_init).reshape(1)
    sw = jnp.tile(diff_subln_w.reshape(1, -1), (1, DIFF_HEADS))
    h1 = _mix_out(lam, h, o_fox, o_c, o_s, o_w, gc, gs, gw, d0, d1, sw, w_out.astype(BF16), 1.0 - lam_init)

    keysT = peer_sub_keys.reshape(2 * PEER_HEADS, PEER_NKEYS, PEER_DQ // 2).transpose(0, 2, 1).astype(BF16)
    xnT, scT = _peer_query(h1, ffn_norm_w, peer_w_q.astype(BF16), keysT)
    a, c, b, r = _peer_topk(scT)
    peer = _peer_main(xnT, peer_u.astype(BF16), peer_v.astype(BF16), a, c, b, r)

    return _ple(h1, peer, p_i, ple_norm_w, ple_w_gate.astype(BF16), ple_w_proj.astype(BF16))


def kernel(x, p, positions, attn_norm_w, w_in, fox_f_bias, fox_q_norm_w, fox_k_norm_w, nsa_q_norm_w, nsa_k_norm_w,
           nsa_cmp_pos, nsa_cmp_w, diff_q_norm_w, diff_k_norm_w, diff_lambda, diff_subln_w, w_out, ffn_norm_w,
           peer_w_q, peer_sub_keys, peer_u, peer_v, ple_norm_w, ple_w_gate, ple_w_proj):
    b, s, d = x.shape
    assert b == 1 and d == D_MODEL and s % Q_LANES == 0 and s % TK_CAUSAL == 0
    tabs = _rope_tables(positions)
    h = x.reshape(s, d)
    per_layer = (attn_norm_w, w_in, fox_f_bias, fox_q_norm_w, fox_k_norm_w, nsa_q_norm_w, nsa_k_norm_w, nsa_cmp_pos,
                 nsa_cmp_w, diff_q_norm_w, diff_k_norm_w, diff_lambda, diff_subln_w, w_out, ffn_norm_w, peer_w_q,
                 peer_sub_keys, peer_u, peer_v, ple_norm_w, ple_w_gate, ple_w_proj)
    for layer in range(attn_norm_w.shape[0]):
        h = _layer(h, p[layer, 0], tabs, layer, *(w[layer] for w in per_layer))
    return h.reshape(b, s, d)
```

```python
import functools
import math

import numpy as np
import jax
import jax.numpy as jnp
from jax import lax
from jax.experimental import pallas as pl
from jax.experimental.pallas import tpu as pltpu

F32 = jnp.float32
BF16 = jnp.bfloat16
HIGHEST = lax.Precision.HIGHEST

D_MODEL = 1024
PLE_DIM = 256
ROPE_THETA = 10000.0
EPS = 1e-6
FOX_HEADS, FOX_HD = 4, 64
NSA_HEADS, NSA_KV, NSA_HD = 8, 2, 64
NSA_G = NSA_HEADS // NSA_KV
CMP_LEN, CMP_STRIDE, SLC_LEN, SLC_TOPK, WIN = 32, 16, 64, 16, 512
FORCE_BONUS = 1.0e4
DIFF_HEADS, DIFF_QK, DIFF_V = 4, 32, 64
PEER_HEADS, PEER_NKEYS, PEER_DQ, PEER_TOPK = 8, 128, 256, 16
PEER_EXPERTS = PEER_NKEYS * PEER_NKEYS

LANES = 128
LOG2E = math.log2(math.e)
NEG_INIT = -1.0e30
MASKVAL = -2.0e30
VMEM_LIMIT = 56 * 1024 * 1024
TK_CAUSAL = 1024
Q_LANES = 1024

_SEG = dict(fq=(0, 256), fk=(256, 256), fv=(512, 256), nq=(768, 512), nkc=(1280, 128), nvc=(1408, 128),
            nks=(1536, 128), nvs=(1664, 128), nkw=(1792, 128), nvw=(1920, 128),
            dq=(2048, 256), dk=(2304, 256), dv=(2560, 256), misc=(2816, 128))
PROJ_W = 2944


def _cparams(sem):
    return pltpu.CompilerParams(dimension_semantics=sem, vmem_limit_bytes=VMEM_LIMIT)


def _full(shape):
    n = len(shape)
    return pl.BlockSpec(shape, lambda *_: (0,) * n)


def _rope_tab_kernel(pos_ref, f64_ref, g64_ref, f32_ref, g32_ref, c64_o, s64_o, c32_o, s32_o):
    pos = pos_ref[...].astype(F32)
    a64 = pos * f64_ref[...]
    c64_o[...] = jnp.cos(a64)
    s64_o[...] = jnp.sin(a64) * g64_ref[...]
    a32 = pos * f32_ref[...]
    c32_o[...] = jnp.cos(a32)
    s32_o[...] = jnp.sin(a32) * g32_ref[...]


def _rope_tables(positions):
    s = positions.shape[-1]
    pos = positions.reshape(s, 1)
    lane = np.arange(LANES)

    def lanes(half):
        inv = ROPE_THETA ** (-jnp.arange(half, dtype=F32) / half)
        freq = inv[(lane % (2 * half)) % half].reshape(1, LANES)
        sign = np.where((lane % (2 * half)) < half, -1.0, 1.0).astype(np.float32).reshape(1, LANES)
        return freq, jnp.asarray(sign)

    f64, g64 = lanes(NSA_HD // 2)
    f32_, g32 = lanes(DIFF_QK // 2)
    tm = 512
    out = jax.ShapeDtypeStruct((s, LANES), F32)
    row = pl.BlockSpec((tm, LANES), lambda i: (i, 0))
    return pl.pallas_call(
        _rope_tab_kernel,
        grid=(s // tm,),
        in_specs=[pl.BlockSpec((tm, 1), lambda i: (i, 0))] + [_full((1, LANES))] * 4,
        out_specs=[row] * 4,
        out_shape=[out] * 4,
        compiler_params=_cparams(("parallel",)),
        name="rope_tables",
    )(pos, f64, g64, f32_, g32)


def _rms(x, w):
    return x * lax.rsqrt(jnp.mean(x * x, axis=-1, keepdims=True) + EPS) * w


def _norm_matmul_kernel(x_ref, nw_ref, *refs):
    o_ref = refs[-1]
    xn = _rms(x_ref[...], nw_ref[...]).astype(BF16)
    c0 = 0
    for w_ref in refs[:-1]:
        n = w_ref.shape[1]
        o_ref[:, c0:c0 + n] = jnp.dot(xn, w_ref[...], preferred_element_type=F32)
        c0 += n


def _norm_matmul(x, nw, ws_bf16, tm=512):
    s, d = x.shape
    n = sum(w.shape[1] for w in ws_bf16)
    return pl.pallas_call(
        _norm_matmul_kernel,
        grid=(s // tm,),
        in_specs=[pl.BlockSpec((tm, d), lambda i: (i, 0)), _full((1, d))] + [_full(w.shape) for w in ws_bf16],
        out_specs=pl.BlockSpec((tm, n), lambda i: (i, 0)),
        out_shape=jax.ShapeDtypeStruct((s, n), F32),
        compiler_params=_cparams(("parallel",)),
        name="norm_matmul",
    )(x, nw.reshape(1, d), *ws_bf16)


def _seg_rms(x, bmat, seg):
    outs = []
    for c0 in range(0, x.shape[1], 256):
        w = min(256, x.shape[1] - c0)
        xc = x[:, c0:c0 + w]
        ss = jnp.dot(xc * xc, bmat[:w, :w], precision=HIGHEST, preferred_element_type=F32)
        outs.append(xc * lax.rsqrt(ss * (1.0 / seg) + EPS))
    return outs[0] if len(outs) == 1 else jnp.concatenate(outs, axis=1)


def _tile_lanes(t, width):
    reps = width // LANES
    return t if reps == 1 else jnp.concatenate([t] * reps, axis=1)


def _rope(x, cos, sin, half):
    width = x.shape[1]
    left = pltpu.roll(x, width - half, 1)
    right = pltpu.roll(x, half, 1)
    lane = lax.broadcasted_iota(jnp.int32, x.shape, 1)
    swapped = jnp.where((lane & (2 * half - 1)) < half, left, right)
    return x * _tile_lanes(cos, width) + swapped * _tile_lanes(sin, width)


def _lane_fill(tm, cols):
    lane = lax.broadcasted_iota(jnp.int32, (tm, LANES), 1)
    out = jnp.zeros((tm, LANES), F32)
    for l, v in cols.items():
        out = jnp.where(lane == l, v, out)
    return out


def _slots(x, extras):
    lane = lax.broadcasted_iota(jnp.int32, (x.shape[0], LANES), 1)
    outs = []
    for h, ex in enumerate(extras):
        col = x[:, (h // 2) * LANES:(h // 2 + 1) * LANES]
        if h % 2:
            col = pltpu.roll(col, LANES // 2, 1)
        outs.append(jnp.where(lane < LANES // 2, col, ex))
    return jnp.concatenate(outs, axis=1)


def _prep_kernel(shift_ref, proj_ref, c64_ref, s64_ref, c32_ref, s32_ref, fb_ref, wfq_ref, wfk_ref, wnq_ref,
                 wnk_ref, wdq_ref, wdk_ref, b64_ref, b32_ref,
                 fq_o, fk_o, fv_o, nqn_o, nqr_o, kvc_o, ks_o, vs_o, kw_o, vw_o, gate_o, dq_o, dk_o, dv_o,
                 carry_sc, *, tm):
    def seg(name):
        c0, w = _SEG[name]
        return proj_ref[:, c0:c0 + w]

    b64 = b64_ref[...]
    b32 = b32_ref[...]
    c64, s64, c32, s32 = c64_ref[...], s64_ref[...], c32_ref[...], s32_ref[...]
    one_ex = _lane_fill(tm, {COL_SHIFT: 1.0})

    misc = seg("misc")
    gate_o[...] = 1.0 / (1.0 + jnp.exp(-misc))
    t = misc + fb_ref[...]
    logf = jnp.minimum(t, 0.0) - jnp.log1p(jnp.exp(-jnp.abs(t)))

    @pl.when(pl.program_id(0) == 0)
    def _():
        carry_sc[...] = jnp.zeros_like(carry_sc)

    r = lax.broadcasted_iota(jnp.int32, (tm, tm), 0)
    c = lax.broadcasted_iota(jnp.int32, (tm, tm), 1)
    tri = (c <= r).astype(F32)
    csum = jnp.dot(tri, logf, precision=HIGHEST, preferred_element_type=F32) + carry_sc[0:1, :]
    carry_sc[...] = jnp.broadcast_to(csum[tm - 1:tm, :], carry_sc.shape)
    csum = csum * LOG2E
    hi = csum.astype(BF16).astype(F32)
    r1 = csum - hi
    mid = r1.astype(BF16).astype(F32)
    lo = (r1 - mid).astype(BF16).astype(F32)

    fq = _seg_rms(seg("fq"), b64, FOX_HD) * wfq_ref[...] * (FOX_HD ** -0.5 * LOG2E)
    fk = _seg_rms(seg("fk"), b64, FOX_HD) * wfk_ref[...]
    q_ex, k_ex = [], []
    for h in range(FOX_HEADS):
        ch, cm, cl = hi[:, h:h + 1], mid[:, h:h + 1], lo[:, h:h + 1]
        q_ex.append(_lane_fill(tm, {COL_SHIFT: shift_ref[0], 65: 1.0, 66: 1.0, 67: 1.0, 68: ch, 69: cm, 70: cl}))
        k_ex.append(_lane_fill(tm, {COL_SHIFT: 1.0, 65: -ch, 66: -cm, 67: -cl, 68: 1.0, 69: 1.0, 70: 1.0}))
    fq_o[...] = _slots(fq, q_ex)
    fk_o[...] = _slots(fk, k_ex).astype(BF16)
    fv_o[...] = _slots(seg("fv"), [one_ex] * FOX_HEADS).astype(BF16)

    n_ex = [_lane_fill(tm, {COL_SHIFT: shift_ref[1]})] * NSA_HEADS
    nqn = _seg_rms(seg("nq"), b64, NSA_HD) * wnq_ref[...]
    nqn_o[...] = _slots(nqn * (NSA_HD ** -0.5 * LOG2E), n_ex)
    nqr_o[...] = _slots(_rope(nqn, c64, s64, NSA_HD // 2) * (NSA_HD ** -0.5 * LOG2E), n_ex)
    kvc_o[:, 0:128] = seg("nkc").astype(BF16)
    kvc_o[:, 128:256] = seg("nvc").astype(BF16)
    wnk = wnk_ref[...]
    lane = lax.broadcasted_iota(jnp.int32, (tm, LANES), 1)
    row = pl.program_id(0) * tm + lax.broadcasted_iota(jnp.int32, (tm, LANES), 0)
    blk_in_tile = (row & (TK_CAUSAL - 1)) // SLC_LEN
    sel_ex = jnp.where(lane == COL_SEL + blk_in_tile, 1.0, one_ex)
    ks = _rope(_seg_rms(seg("nks"), b64, NSA_HD) * wnk, c64, s64, NSA_HD // 2)
    ks_o[...] = _slots(ks, [sel_ex] * NSA_KV).astype(BF16)
    vs_o[...] = _slots(seg("nvs"), [one_ex] * NSA_KV).astype(BF16)
    kw = _rope(_seg_rms(seg("nkw"), b64, NSA_HD) * wnk, c64, s64, NSA_HD // 2)
    kw_o[...] = _slots(kw, [one_ex] * NSA_KV).astype(BF16)
    vw_o[...] = _slots(seg("nvw"), [one_ex] * NSA_KV).astype(BF16)

    dq = _rope(_seg_rms(seg("dq"), b32, DIFF_QK) * wdq_ref[...], c32, s32, DIFF_QK // 2) * (DIFF_QK ** -0.5 * LOG2E)
    d_ex = _lane_fill(tm, {COL_SHIFT: shift_ref[2]})
    d_slots = []
    for h in range(DIFF_HEADS):
        col = dq[:, (h // 2) * LANES:(h // 2 + 1) * LANES]
        if h % 2:
            col = pltpu.roll(col, LANES // 2, 1)
        d_slots.append(jnp.where(lane < DIFF_QK, col, d_ex))
        d_slots.append(jnp.where((lane >= DIFF_QK) & (lane < 2 * DIFF_QK), col, d_ex))
    dq_o[...] = jnp.concatenate(d_slots, axis=1)
    dk = _rope(_seg_rms(seg("dk"), b32, DIFF_QK) * wdk_ref[...], c32, s32, DIFF_QK // 2)
    dk_o[...] = _slots(dk, [one_ex] * DIFF_HEADS).astype(BF16)
    dv_o[...] = _slots(seg("dv"), [one_ex] * DIFF_HEADS).astype(BF16)


def _block_diag_ones(n, seg):
    i = np.arange(n)
    return jnp.asarray((i[:, None] // seg == i[None, :] // seg).astype(np.float32))


def _prep(shifts, proj, tabs, fb, wfq, wfk, wnq, wnk, wdq, wdk, tm=256):
    s = proj.shape[0]
    c64, s64, c32, s32 = tabs
    b64 = _block_diag_ones(256, 64)
    b32 = _block_diag_ones(256, 32)

    def tiled(w, width):
        return jnp.tile(w.reshape(1, -1), (1, width // w.shape[-1]))

    consts = [fb, tiled(wfq, 256), tiled(wfk, 256), tiled(wnq, 512), tiled(wnk, 128), tiled(wdq, 256),
              tiled(wdk, 256), b64, b32]
    outs = [(FOX_HEADS, F32), (FOX_HEADS, BF16), (FOX_HEADS, BF16),
            (NSA_HEADS, F32), (NSA_HEADS, F32), (2, BF16),
            (NSA_KV, BF16), (NSA_KV, BF16), (NSA_KV, BF16), (NSA_KV, BF16), (1, F32),
            (2 * DIFF_HEADS, F32), (DIFF_HEADS, BF16), (DIFF_HEADS, BF16)]
    row = lambda w: pl.BlockSpec((tm, w), lambda i: (i, 0))
    return pl.pallas_call(
        functools.partial(_prep_kernel, tm=tm),
        grid=(s // tm,),
        in_specs=[pl.BlockSpec(memory_space=pltpu.SMEM), row(PROJ_W)] + [row(LANES)] * 4
                 + [_full(c.shape) for c in consts],
        out_specs=[row(n * LANES) for n, _ in outs],
        out_shape=[jax.ShapeDtypeStruct((s, n * LANES), dt) for n, dt in outs],
        scratch_shapes=[pltpu.VMEM((8, LANES), F32)],
        compiler_params=_cparams(("arbitrary",)),
        name="head_prep",
    )(shifts, proj, c64, s64, c32, s32, *consts)


QK_DIM = 128
COL_SHIFT = 64
COL_SEL = 80
SCORE_BOUND = 60.0
SHIFT_HEADROOM = 60.0


def _flash_kernel(code_ref, q_ref, k_ref, v_ref, *rest, groups, tq, tk, window, has_sel, bounded):
    rest = list(rest)
    sel_ref = rest.pop(0) if has_sel else None
    o_ref = rest.pop(0)
    qT_sc = rest.pop(0)
    m_sc = None if bounded else rest.pop(0)
    acc_sc = rest.pop(0)
    code = code_ref[pl.program_id(1)]
    i = code & 0xFFF
    j = (code >> 12) & 0xFFF
    first = (code >> 24) & 1
    last = (code >> 25) & 1
    rows = groups * tq

    @pl.when(first == 1)
    def _():
        for g in range(groups):
            qT_sc[:, g * tq:(g + 1) * tq] = q_ref[:, g * QK_DIM:(g + 1) * QK_DIM].T.astype(BF16)
        if not bounded:
            m_sc[...] = jnp.full_like(m_sc, NEG_INIT)
        acc_sc[...] = jnp.zeros_like(acc_sc)

    def step(masked):
        if has_sel:
            qT_sc[COL_SEL:COL_SEL + tk // SLC_LEN, :] = _tile_lanes_any(sel_ref[0], groups)
        s = jnp.dot(k_ref[...], qT_sc[...], preferred_element_type=F32)
        if masked:
            kpos = j * tk + lax.broadcasted_iota(jnp.int32, (tk, rows), 0)
            lane = lax.broadcasted_iota(jnp.int32, (tk, rows), 1)
            qpos = i * tq + (lane & (tq - 1))
            keep = kpos <= qpos
            if window is not None:
                keep = keep & (kpos > qpos - window)
            s = jnp.where(keep, s, MASKVAL)
        v_t = (((0,), (0,)), ((), ()))
        if bounded:
            acc_sc[...] += lax.dot_general(v_ref[...], jnp.exp2(s).astype(BF16), v_t, preferred_element_type=F32)
        else:
            m_prev = m_sc[...]
            m_new = jnp.maximum(m_prev, jnp.max(s, axis=0, keepdims=True))
            p = jnp.exp2(s - m_new).astype(BF16)
            acc_sc[...] = (acc_sc[...] * jnp.exp2(m_prev - m_new)
                           + lax.dot_general(v_ref[...], p, v_t, preferred_element_type=F32))
            m_sc[...] = m_new

    if window is not None:
        step(True)
    else:
        needs_mask = (j + 1) * tk - 1 > i * tq

        @pl.when(needs_mask)
        def _():
            step(True)

        @pl.when(jnp.logical_not(needs_mask))
        def _():
            step(False)

    @pl.when(last == 1)
    def _():
        l = acc_sc[COL_SHIFT:COL_SHIFT + 1, :]
        o = acc_sc[...] / jnp.where(l > 0.0, l, 1.0)
        for g in range(groups):
            o_ref[:, g * QK_DIM:(g + 1) * QK_DIM] = o[:, g * tq:(g + 1) * tq].T


def _tile_lanes_any(t, reps):
    return t if reps == 1 else jnp.concatenate([t] * reps, axis=1)


def _flash_call(q, k, v, sel, *, groups, tq, tk, window, bounded):
    s_len = q.shape[0]
    hkv = k.shape[1] // QK_DIM
    nq = s_len // tq
    rows = groups * tq
    codes = []
    for i in range(nq):
        q_lo, q_hi = i * tq, i * tq + tq - 1
        j_hi = q_hi // tk
        j_lo = 0 if window is None else max(0, (q_lo - window + 1) // tk)
        for j in range(j_lo, j_hi + 1):
            codes.append(i | (j << 12) | (int(j == j_lo) << 24) | (int(j == j_hi) << 25))
    codes = jnp.asarray(np.asarray(codes, dtype=np.int32))
    nsteps = codes.shape[0]

    def ti(c, s):
        return c[s] & 0xFFF

    def tj(c, s):
        return (c[s] >> 12) & 0xFFF

    in_specs = [
        pl.BlockSpec((tq, groups * QK_DIM), lambda h, s, c: (ti(c, s), h)),
        pl.BlockSpec((tk, QK_DIM), lambda h, s, c: (tj(c, s), h)),
        pl.BlockSpec((tk, QK_DIM), lambda h, s, c: (tj(c, s), h)),
    ]
    args = [q, k, v]
    if sel is not None:
        in_specs.append(pl.BlockSpec((1, tk // SLC_LEN, tq), lambda h, s, c: (h, tj(c, s), ti(c, s))))
        args.append(sel)
    scratch = [pltpu.VMEM((QK_DIM, rows), BF16)]
    if not bounded:
        scratch.append(pltpu.VMEM((1, rows), F32))
    scratch.append(pltpu.VMEM((QK_DIM, rows), F32))
    kern = functools.partial(_flash_kernel, groups=groups, tq=tq, tk=tk, window=window, has_sel=sel is not None,
                             bounded=bounded)
    return pl.pallas_call(
        kern,
        grid_spec=pltpu.PrefetchScalarGridSpec(
            num_scalar_prefetch=1,
            grid=(hkv, nsteps),
            in_specs=in_specs,
            out_specs=pl.BlockSpec((tq, groups * QK_DIM), lambda h, s, c: (ti(c, s), h)),
            scratch_shapes=scratch,
        ),
        out_shape=jax.ShapeDtypeStruct(q.shape, F32),
        compiler_params=_cparams(("parallel", "arbitrary")),
        name="flash_bounded" if bounded else "flash_online",
    )(codes, *args)


def _flash(q, k, v, sel=None, *, bound_ok, groups, tq, tk, window=None):
    call = functools.partial(_flash_call, q, k, v, sel, groups=groups, tq=tq, tk=tk, window=window)
    return lax.cond(bound_ok, lambda: call(bounded=True), lambda: call(bounded=False))


def _score_bound(wq, wk, seg, scale):
    return 1.02 * seg * jnp.max(jnp.abs(wq)) * jnp.max(jnp.abs(wk)) * scale * LOG2E


def _compress_kernel(r_ref, wa_ref, wb_ref, pa_ref, pb_ref, nw_ref, b64_ref, kx_o, vx_o):
    n = r_ref.shape[0]
    r = r_ref[...]
    wa, wb = wa_ref[...], wb_ref[...]
    first = jnp.dot(r, wa.astype(BF16), preferred_element_type=F32)
    second = jnp.dot(r, wb.astype(BF16), preferred_element_type=F32)
    nxt = pltpu.roll(second, n - 1, 0)
    rowi = lax.broadcasted_iota(jnp.int32, nxt.shape, 0)
    nxt = jnp.where(rowi == n - 1, 0.0, nxt)
    const = (jnp.dot(pa_ref[...], wa, precision=HIGHEST, preferred_element_type=F32)
             + jnp.dot(pb_ref[...], wb, precision=HIGHEST, preferred_element_type=F32))
    out = first + nxt + const
    k, v = out[:, 0:LANES], out[:, LANES:2 * LANES]
    ss = jnp.dot(k * k, b64_ref[...], precision=HIGHEST, preferred_element_type=F32)
    kn = k * lax.rsqrt(ss * (1.0 / NSA_HD) + EPS) * nw_ref[...]
    one_ex = _lane_fill(n, {COL_SHIFT: 1.0})
    kx_o[...] = _slots(kn, [one_ex] * NSA_KV).astype(BF16)
    vx_o[...] = _slots(v, [one_ex] * NSA_KV).astype(BF16)


def _compress(kvc, cmp_w, cmp_pos, nw):
    s = kvc.shape[0]
    n = s // CMP_STRIDE
    width = CMP_STRIDE * 2 * NSA_KV * NSA_HD
    w = cmp_w.reshape(2, 2, CMP_STRIDE, NSA_HD, NSA_HD)
    eye = jnp.eye(2, dtype=w.dtype)
    big = jnp.einsum("khldD,kK,gG->hlkgdKGD", w, eye, eye).reshape(2, width, 2 * NSA_KV * NSA_HD)
    pos = cmp_pos.reshape(2, 2, CMP_STRIDE, 1, NSA_HD)
    pos = jnp.broadcast_to(pos, (2, 2, CMP_STRIDE, NSA_KV, NSA_HD)).transpose(1, 2, 0, 3, 4).reshape(2, 1, width)
    b64 = _block_diag_ones(LANES, NSA_HD)
    nwt = jnp.tile(nw.reshape(1, -1), (1, LANES // NSA_HD))
    out = jax.ShapeDtypeStruct((n, NSA_KV * LANES), BF16)
    return pl.pallas_call(
        _compress_kernel,
        grid=(1,),
        in_specs=[_full((n, width)), _full(big[0].shape), _full(big[1].shape), _full((1, width)), _full((1, width)),
                  _full((1, LANES)), _full((LANES, LANES))],
        out_specs=[_full((n, NSA_KV * LANES))] * 2,
        out_shape=[out, out],
        compiler_params=_cparams(("arbitrary",)),
        name="nsa_compress",
    )(kvc.reshape(n, width), big[0], big[1], pos[0], pos[1], nwt, b64)


def _nsa_cmp_kernel(q_ref, kc_ref, vc_ref, ov_ref, o_ref, sel_ref, *, tq, groups, bounded):
    i = pl.program_id(1)
    ncmp = kc_ref.shape[0]
    rows = groups * tq
    qT = jnp.concatenate([q_ref[:, g * QK_DIM:(g + 1) * QK_DIM].T for g in range(groups)], axis=1).astype(BF16)
    s = jnp.dot(kc_ref[...], qT, preferred_element_type=F32)
    n_idx = lax.broadcasted_iota(jnp.int32, (ncmp, rows), 0)
    lane = lax.broadcasted_iota(jnp.int32, (ncmp, rows), 1)
    qpos = i * tq + (lane & (tq - 1))
    keep = n_idx * CMP_STRIDE + (CMP_LEN - 1) <= qpos
    if bounded:
        e = jnp.where(keep, jnp.exp2(s), 0.0)
    else:
        sm = jnp.where(keep, s, NEG_INIT)
        e = jnp.where(keep, jnp.exp2(sm - jnp.max(sm, axis=0, keepdims=True)), 0.0)
    den = jnp.sum(e, axis=0, keepdims=True)
    p = e / jnp.where(den > 0.0, den, 1.0)
    o = lax.dot_general(vc_ref[...], p.astype(BF16), (((0,), (0,)), ((), ())), preferred_element_type=F32)
    for g in range(groups):
        o_ref[:, g * QK_DIM:(g + 1) * QK_DIM] = o[:, g * tq:(g + 1) * tq].T

    psum = p[:, 0:tq]
    for g in range(1, groups):
        psum = psum + p[:, g * tq:(g + 1) * tq]
    imp = jnp.dot(ov_ref[...], psum, precision=HIGHEST, preferred_element_type=F32)
    ns = imp.shape[0]
    blk = lax.broadcasted_iota(jnp.int32, (ns, tq), 0)
    qp = i * tq + lax.broadcasted_iota(jnp.int32, (ns, tq), 1)
    cur = qp // SLC_LEN
    valid = blk <= cur
    forced = (blk == 0) | (blk == cur) | (blk == cur - 1)
    work = jnp.where(valid, imp + jnp.where(forced, FORCE_BONUS, 0.0), NEG_INIT)
    picked = jnp.zeros((ns, tq), dtype=jnp.bool_)
    for _ in range(SLC_TOPK):
        mx = jnp.max(work, axis=0, keepdims=True)
        idx = jnp.min(jnp.where(work == mx, blk, ns), axis=0, keepdims=True)
        pick = (blk == idx) & (mx > 0.5 * NEG_INIT)
        picked = picked | pick
        work = jnp.where(pick, NEG_INIT, work)
    sel_ref[0] = jnp.where(picked, 0.0, MASKVAL).astype(BF16)


def _nsa_cmp(qn, kcx, vcx, overlapT, tq, bound_ok):
    s = qn.shape[0]
    ncmp = kcx.shape[0]
    ns = overlapT.shape[0]
    groups = NSA_G

    def call(bounded):
        return pl.pallas_call(
            functools.partial(_nsa_cmp_kernel, tq=tq, groups=groups, bounded=bounded),
            grid=(NSA_KV, s // tq),
            in_specs=[pl.BlockSpec((tq, groups * QK_DIM), lambda h, i: (i, h)),
                      pl.BlockSpec((ncmp, QK_DIM), lambda h, i: (0, h)),
                      pl.BlockSpec((ncmp, QK_DIM), lambda h, i: (0, h)),
                      _full(overlapT.shape)],
            out_specs=[pl.BlockSpec((tq, groups * QK_DIM), lambda h, i: (i, h)),
                       pl.BlockSpec((1, ns, tq), lambda h, i: (h, 0, i))],
            out_shape=[jax.ShapeDtypeStruct(qn.shape, F32), jax.ShapeDtypeStruct((NSA_KV, ns, s), BF16)],
            compiler_params=_cparams(("parallel", "parallel")),
            name="nsa_cmp_bounded" if bounded else "nsa_cmp_max",
        )(qn, kcx, vcx, overlapT)

    return lax.cond(bound_ok, lambda: call(True), lambda: call(False))


def _mix_out_kernel(lam_ref, h_ref, fox_ref, oc_ref, os_ref, ow_ref, dd_ref, g_ref, sw_ref, wf_ref, wn_ref, wd_ref,
                    o_ref, *, diff_scale):
    lam = lam_ref[0]
    gates = g_ref[...]
    tm = gates.shape[0]
    lane = lax.broadcasted_iota(jnp.int32, (tm, LANES), 1)
    slot = lambda ref, n: ref[:, n * LANES:(n + 1) * LANES]
    nsa = []
    for h in range(NSA_HEADS):
        c0 = FOX_HEADS + 3 * h
        nsa.append(gates[:, c0:c0 + 1] * slot(oc_ref, h) + gates[:, c0 + 1:c0 + 2] * slot(os_ref, h)
                   + gates[:, c0 + 2:c0 + 3] * slot(ow_ref, h))
    diff = []
    for h in range(DIFF_HEADS):
        a = jnp.where(lane < DIFF_V, slot(dd_ref, 2 * h) - lam * slot(dd_ref, 2 * h + 1), 0.0)
        ss = jnp.sum(a * a, axis=-1, keepdims=True)
        diff.append(a * lax.rsqrt(ss * (1.0 / DIFF_V) + EPS) * sw_ref[...] * diff_scale)
    acc = h_ref[...]
    acc = acc + jnp.dot(fox_ref[...].astype(BF16), wf_ref[...], preferred_element_type=F32)
    acc = acc + jnp.dot(jnp.concatenate(nsa, axis=1).astype(BF16), wn_ref[...], preferred_element_type=F32)
    acc = acc + jnp.dot(jnp.concatenate(diff, axis=1).astype(BF16), wd_ref[...], preferred_element_type=F32)
    o_ref[...] = acc


def _slot_rows(w, heads):
    d = w.shape[1]
    w = w.reshape(heads, -1, d)
    return jnp.concatenate([w, jnp.zeros((heads, LANES - w.shape[1], d), w.dtype)], axis=1).reshape(heads * LANES, d)


def _mix_out(lam, h, fox, oc, os_, ow, dd, gates, subln_w, w_out, diff_scale, tm=512):
    s, d = h.shape
    nf, nn = FOX_HEADS * FOX_HD, NSA_HEADS * NSA_HD
    wf = _slot_rows(w_out[:nf], FOX_HEADS).astype(BF16)
    wn = _slot_rows(w_out[nf:nf + nn], NSA_HEADS).astype(BF16)
    wd = _slot_rows(w_out[nf + nn:], DIFF_HEADS).astype(BF16)
    sw = jnp.concatenate([subln_w.reshape(1, -1), jnp.zeros((1, LANES - DIFF_V), F32)], axis=1)
    row = lambda w: pl.BlockSpec((tm, w), lambda i: (i, 0))
    return pl.pallas_call(
        functools.partial(_mix_out_kernel, diff_scale=diff_scale),
        grid=(s // tm,),
        in_specs=[pl.BlockSpec(memory_space=pltpu.SMEM), row(d), row(fox.shape[1]), row(oc.shape[1]),
                  row(os_.shape[1]), row(ow.shape[1]), row(dd.shape[1]), row(LANES), _full((1, LANES)),
                  _full(wf.shape), _full(wn.shape), _full(wd.shape)],
        out_specs=row(d),
        out_shape=jax.ShapeDtypeStruct((s, d), F32),
        compiler_params=_cparams(("parallel",)),
        name="mix_out",
    )(lam, h, fox, oc, os_, ow, dd, gates, sw, wf, wn, wd)


def _peer_query_kernel(h_ref, nw_ref, wq_ref, keys_ref, xnT_o, scT_o):
    xn = _rms(h_ref[...], nw_ref[...])
    xnT_o[...] = xn.T.astype(BF16)
    q = jnp.dot(xn.astype(BF16), wq_ref[...], preferred_element_type=F32).astype(BF16)
    half = PEER_DQ // 2
    for b in range(2 * PEER_HEADS):
        sc = jnp.dot(q[:, b * half:(b + 1) * half], keys_ref[b], preferred_element_type=F32)
        scT_o[b] = sc.T


def _peer_query(h, nw, wq_bf16, keysT_bf16, tm=512):
    s, d = h.shape
    n = wq_bf16.shape[1]
    nb = 2 * PEER_HEADS
    return pl.pallas_call(
        _peer_query_kernel,
        grid=(s // tm,),
        in_specs=[pl.BlockSpec((tm, d), lambda i: (i, 0)), _full((1, d)), _full((d, n)), _full(keysT_bf16.shape)],
        out_specs=[pl.BlockSpec((d, tm), lambda i: (0, i)), pl.BlockSpec((nb, PEER_NKEYS, tm), lambda i: (0, 0, i))],
        out_shape=[jax.ShapeDtypeStruct((d, s), BF16), jax.ShapeDtypeStruct((nb, PEER_NKEYS, s), F32)],
        compiler_params=_cparams(("parallel",)),
        name="peer_query",
    )(h, nw.reshape(1, d), wq_bf16, keysT_bf16)


def _top16_rows(s, exact_ties):
    n = s.shape[0]
    row = lax.broadcasted_iota(jnp.int32, s.shape, 0)
    rank = jnp.full(s.shape, float(PEER_TOPK), dtype=F32)
    work = s
    vals = []
    for r in range(PEER_TOPK):
        mx = jnp.max(work, axis=0, keepdims=True)
        if exact_ties:
            pick = row == jnp.min(jnp.where(work == mx, row, n), axis=0, keepdims=True)
        else:
            pick = work == mx
        rank = jnp.where(pick, float(r), rank)
        work = jnp.where(pick, NEG_INIT, work)
        vals.append(mx)
    return rank, vals


def _peer_route(s1, s2, exact_ties):
    rank1, v1 = _top16_rows(s1, exact_ties)
    rank2, v2 = _top16_rows(s2, exact_ties)
    k = PEER_TOPK
    tm = s1.shape[1]
    r16 = lax.broadcasted_iota(jnp.int32, (k, tm), 0)
    v1m = jnp.zeros((k, tm), F32)
    v2m = jnp.zeros((k, tm), F32)
    for r in range(k):
        v1m = jnp.where(r16 == r, v1[r], v1m)
        v2m = jnp.where(r16 == r, v2[r], v2m)
    ea = jnp.exp(v1m - v1[0])
    eb = jnp.exp(v2m - v2[0])
    cands, gates = [v1m[0:1] + v2m], [ea[0:1] * eb]
    for r in range(1, 8):
        cands.append(v1m[r:r + 1] + v2m[0:8])
        gates.append(ea[r:r + 1] * eb[0:8])
    cands.append(v1m[8:16] + v2m[0:1])
    gates.append(ea[8:16] * eb[0:1])
    cand = jnp.concatenate(cands, axis=0)
    gate = jnp.concatenate(gates, axis=0)
    ncand = cand.shape[0]
    crow = lax.broadcasted_iota(jnp.int32, cand.shape, 0)
    for r in range(2, 8):
        start = 16 + 8 * (r - 1)
        cand = jnp.where((crow >= start + k // (r + 1)) & (crow < start + 8), NEG_INIT, cand)
    work = cand
    picked = jnp.zeros(cand.shape, dtype=jnp.bool_)
    for _ in range(k):
        mx = jnp.max(work, axis=0, keepdims=True)
        if exact_ties:
            pick = crow == jnp.min(jnp.where(work == mx, crow, ncand), axis=0, keepdims=True)
        else:
            pick = work == mx
        picked = picked | pick
        work = jnp.where(pick, NEG_INIT, work)
    pf = picked.astype(F32)
    z = jnp.sum(pf * gate, axis=0, keepdims=True)
    cnt = [jnp.sum(pf[0:16, :], axis=0, keepdims=True)]
    cnt += [jnp.sum(pf[8 + 8 * r:16 + 8 * r, :], axis=0, keepdims=True) for r in range(1, 8)]
    cnt += [pf[64 + r:65 + r, :] for r in range(8, 16)]
    cmap = jnp.zeros(s1.shape, dtype=F32)
    for r in range(k):
        cmap = jnp.where(rank1 == float(r), cnt[r], cmap)
    in1 = rank1 < float(k)
    in2 = rank2 < float(k)
    a = jnp.where(in1, jnp.exp(s1 - v1[0]), 0.0) / z
    b = jnp.where(in2, jnp.exp(s2 - v2[0]), 0.0)
    excess = (jnp.abs(jnp.sum(in1.astype(F32), axis=0, keepdims=True) - k)
              + jnp.abs(jnp.sum(in2.astype(F32), axis=0, keepdims=True) - k)
              + jnp.abs(jnp.sum(pf, axis=0, keepdims=True) - k))
    return a, cmap, b, rank2, excess


def _peer_topk_kernel(sc_ref, a_o, c_o, b_o, r_o):
    def run(exact_ties):
        a, c, b, rank2, excess = _peer_route(sc_ref[0], sc_ref[1], exact_ties)
        a_o[0] = a
        c_o[0] = c
        b_o[0] = b.astype(BF16)
        r_o[0] = rank2.astype(BF16)
        return excess

    excess = run(False)

    @pl.when(jnp.max(excess) > 0.0)
    def _():
        run(True)


def _peer_topk(scT, tm=256):
    n2, nk, s = scT.shape
    heads = n2 // 2
    ospec = pl.BlockSpec((1, nk, tm), lambda h, t: (h, 0, t))
    return pl.pallas_call(
        _peer_topk_kernel,
        grid=(heads, s // tm),
        in_specs=[pl.BlockSpec((2, nk, tm), lambda h, t: (h, 0, t))],
        out_specs=[ospec] * 4,
        out_shape=[jax.ShapeDtypeStruct((heads, nk, s), dt) for dt in (F32, F32, BF16, BF16)],
        compiler_params=_cparams(("parallel", "parallel")),
        name="peer_topk",
    )(scT)


_ERF_ALPHA = (-2.72614225801306e-10, 2.77068142495902e-08, -2.10102402082508e-06, -5.69250639462346e-05,
              -7.34990630326855e-04, -2.95459980854025e-03, -1.60960333262415e-02)
_ERF_BETA = (-1.45660718464996e-05, -2.13374055278905e-04, -1.68282697438203e-03, -7.37332916720468e-03,
             -1.42647390514189e-02)


def _gelu_exact(x):
    return 0.5 * x * (1.0 + lax.erf(x * (2.0 ** -0.5)))


def _peer_main_kernel(xT_ref, u_ref, v_ref, a_ref, c_ref, b_ref, r_ref, o_ref, w_sc, *, ec, sub):
    ci = pl.program_id(1)

    @pl.when(ci == 0)
    def _():
        o_ref[...] = jnp.zeros_like(o_ref)

    nk = PEER_NKEYS
    tm = xT_ref.shape[1]

    def row(ref, h, i1):
        x16 = jnp.broadcast_to(ref[h, pl.ds(i1, 1), :], (16, tm)).astype(BF16)
        return jnp.concatenate([x16] * (nk // 16), axis=0)

    acc = None
    for sc in range(ec // sub):
        hid = jnp.dot(u_ref[sc * sub:(sc + 1) * sub, :], xT_ref[...], preferred_element_type=F32)
        act = _gelu_exact(hid).astype(BF16)
        for ii in range(sub // nk):
            slab = sc * (sub // nk) + ii
            i1 = ci * (ec // nk) + slab
            w = None
            for h in range(PEER_HEADS):
                term = jnp.where(r_ref[h] < row(c_ref, h, i1), b_ref[h], 0.0) * row(a_ref, h, i1)
                w = term if w is None else w + term
            w_sc[slab * nk:(slab + 1) * nk, :] = w * act[ii * nk:(ii + 1) * nk, :]
        part = jnp.dot(w_sc[sc * sub:(sc + 1) * sub, :].T, v_ref[sc * sub:(sc + 1) * sub, :],
                       preferred_element_type=F32)
        acc = part if acc is None else acc + part
    o_ref[...] += acc


def _peer_main(xnT, u_bf16, v_bf16, a, c, b, r, tm=512, ec=2048, sub=512):
    d, s = xnT.shape
    e = u_bf16.shape[0]
    heads, nk, _ = a.shape
    rt = pl.BlockSpec((heads, nk, tm), lambda t, ci: (0, 0, t))
    wt = pl.BlockSpec((ec, d), lambda t, ci: (ci, 0))
    return pl.pallas_call(
        functools.partial(_peer_main_kernel, ec=ec, sub=sub),
        grid=(s // tm, e // ec),
        in_specs=[pl.BlockSpec((d, tm), lambda t, ci: (0, t)), wt, wt, rt, rt, rt, rt],
        out_specs=pl.BlockSpec((tm, d), lambda t, ci: (t, 0)),
        out_shape=jax.ShapeDtypeStruct((s, d), F32),
        scratch_shapes=[pltpu.VMEM((ec, tm), BF16)],
        compiler_params=_cparams(("parallel", "arbitrary")),
        name="peer_main",
    )(xnT, u_bf16, v_bf16, a, c, b, r)


def _ple_kernel(h_ref, peer_ref, p_ref, nw_ref, wg_ref, wp_ref, o_ref):
    h2 = h_ref[...] + peer_ref[...]
    xn = _rms(h2, nw_ref[...]).astype(BF16)
    gate = 1.0 / (1.0 + jnp.exp(-jnp.dot(xn, wg_ref[...], preferred_element_type=F32)))
    emb = jnp.dot(p_ref[...].astype(BF16), wp_ref[...], preferred_element_type=F32)
    o_ref[...] = h2 + gate * emb


def _ple(h, peer, p, nw, wg_bf16, wp_bf16, tm=512):
    s, d = h.shape
    pd = p.shape[1]
    row = lambda w: pl.BlockSpec((tm, w), lambda i: (i, 0))
    return pl.pallas_call(
        _ple_kernel,
        grid=(s // tm,),
        in_specs=[row(d), row(d), row(pd), _full((1, d)), _full((d, d)), _full((pd, d))],
        out_specs=row(d),
        out_shape=jax.ShapeDtypeStruct((s, d), F32),
        compiler_params=_cparams(("parallel",)),
        name="ple",
    )(h, peer, p, nw.reshape(1, d), wg_bf16, wp_bf16)


def _split_w_in(w):
    f0 = 3 * FOX_HEADS * FOX_HD
    n0 = f0 + FOX_HEADS
    g0 = n0 + _SEG["dq"][0] - _SEG["nq"][0]
    d0 = g0 + 3 * NSA_HEADS
    misc = jnp.concatenate([w[:, f0:n0], w[:, g0:d0], jnp.zeros((w.shape[0], LANES - (n0 - f0) - (d0 - g0)), w.dtype)],
                           axis=1)
    return [t.astype(BF16) for t in (w[:, :f0], w[:, n0:g0], w[:, d0:], misc)]


def _overlap_T(s):
    n = np.arange(s // CMP_STRIDE)[None, :] * CMP_STRIDE
    m = np.arange(s // SLC_LEN)[:, None] * SLC_LEN
    return jnp.asarray(((n < m + SLC_LEN) & (n + CMP_LEN > m)).astype(np.float32))


def _layer(h, p_i, tabs, layer, attn_norm_w, w_in, fox_f_bias, fox_q_norm_w, fox_k_norm_w, nsa_q_norm_w,
           nsa_k_norm_w, nsa_cmp_pos, nsa_cmp_w, diff_q_norm_w, diff_k_norm_w, diff_lambda, diff_subln_w,
           w_out, ffn_norm_w, peer_w_q, peer_sub_keys, peer_u, peer_v, ple_norm_w, ple_w_gate, ple_w_proj):
    s = h.shape[0]
    proj = _norm_matmul(h, attn_norm_w, _split_w_in(w_in))
    fb = jnp.zeros((1, LANES), F32).at[0, :FOX_HEADS].set(fox_f_bias)
    u_f = _score_bound(fox_q_norm_w, fox_k_norm_w, FOX_HD, FOX_HD ** -0.5)
    u_n = _score_bound(nsa_q_norm_w, nsa_k_norm_w, NSA_HD, NSA_HD ** -0.5)
    u_d = _score_bound(diff_q_norm_w, diff_k_norm_w, DIFF_QK, DIFF_QK ** -0.5)
    shifts = SHIFT_HEADROOM - jnp.stack([u_f, u_n, u_d]).astype(F32)
    (fq, fk, fv, nqn, nqr, kvc, ks, vs, kw, vw, gates, dq, dk, dv) = _prep(
        shifts, proj, tabs, fb, fox_q_norm_w, fox_k_norm_w, nsa_q_norm_w, nsa_k_norm_w, diff_q_norm_w, diff_k_norm_w)

    o_fox = _flash(fq, fk, fv, bound_ok=u_f <= SCORE_BOUND, groups=1, tq=Q_LANES, tk=TK_CAUSAL)

    tq_n = Q_LANES // NSA_G
    nsa_ok = u_n <= SCORE_BOUND
    kcx, vcx = _compress(kvc, nsa_cmp_w, nsa_cmp_pos, nsa_k_norm_w)
    o_c, sel = _nsa_cmp(nqn, kcx, vcx, _overlap_T(s), tq_n, nsa_ok)
    o_s = _flash(nqr, ks, vs, sel, bound_ok=nsa_ok, groups=NSA_G, tq=tq_n, tk=TK_CAUSAL)
    o_w = _flash(nqr, kw, vw, bound_ok=nsa_ok, groups=NSA_G, tq=tq_n, tk=512, window=WIN)

    o_d = _flash(dq, dk, dv, bound_ok=u_d <= SCORE_BOUND, groups=2, tq=Q_LANES // 2, tk=TK_CAUSAL)
    lv = diff_lambda.astype(F32)
    lam_init = 0.8 - 0.6 * math.exp(-0.3 * layer)
    lam = (jnp.exp(jnp.sum(lv[0] * lv[1])) - jnp.exp(jnp.sum(lv[2] * lv[3])) + lam_init).reshape(1)
    h1 = _mix_out(lam, h, o_fox, o_c, o_s, o_w, o_d, gates, diff_subln_w, w_out, 1.0 - lam_init)

    keysT = peer_sub_keys.reshape(2 * PEER_HEADS, PEER_NKEYS, PEER_DQ // 2).transpose(0, 2, 1).astype(BF16)
    xnT, scT = _peer_query(h1, ffn_norm_w, peer_w_q.astype(BF16), keysT)
    a, c, b, r = _peer_topk(scT)
    peer = _peer_main(xnT, peer_u.astype(BF16), peer_v.astype(BF16), a, c, b, r)

    return _ple(h1, peer, p_i, ple_norm_w, ple_w_gate.astype(BF16), ple_w_proj.astype(BF16))


def kernel(x, p, positions, attn_norm_w, w_in, fox_f_bias, fox_q_norm_w, fox_k_norm_w, nsa_q_norm_w, nsa_k_norm_w,
           nsa_cmp_pos, nsa_cmp_w, diff_q_norm_w, diff_k_norm_w, diff_lambda, diff_subln_w, w_out, ffn_norm_w,
           peer_w_q, peer_sub_keys, peer_u, peer_v, ple_norm_w, ple_w_gate, ple_w_proj):
    b, s, d = x.shape
    assert b == 1 and d == D_MODEL and s % Q_LANES == 0 and s % TK_CAUSAL == 0
    tabs = _rope_tables(positions)
    h = x.reshape(s, d)
    per_layer = (attn_norm_w, w_in, fox_f_bias, fox_q_norm_w, fox_k_norm_w, nsa_q_norm_w, nsa_k_norm_w, nsa_cmp_pos,
                 nsa_cmp_w, diff_q_norm_w, diff_k_norm_w, diff_lambda, diff_subln_w, w_out, ffn_norm_w, peer_w_q,
                 peer_sub_keys, peer_u, peer_v, ple_norm_w, ple_w_gate, ple_w_proj)
    for layer in range(attn_norm_w.shape[0]):
        h = _layer(h, p[layer, 0], tabs, layer, *(w[layer] for w in per_layer))
    return h.reshape(b, s, d)
```

```python
import functools
import math

import numpy as np
import jax
import jax.numpy as jnp
from jax import lax
from jax.experimental import pallas as pl
from jax.experimental.pallas import tpu as pltpu

F32 = jnp.float32
BF16 = jnp.bfloat16
HIGHEST = lax.Precision.HIGHEST

D_MODEL = 1024
PLE_DIM = 256
ROPE_THETA = 10000.0
EPS = 1e-6
FOX_HEADS, FOX_HD = 4, 64
NSA_HEADS, NSA_KV, NSA_HD = 8, 2, 64
NSA_G = NSA_HEADS // NSA_KV
CMP_LEN, CMP_STRIDE, SLC_LEN, SLC_TOPK, WIN = 32, 16, 64, 16, 512
FORCE_BONUS = 1.0e4
DIFF_HEADS, DIFF_QK, DIFF_V = 4, 32, 64
PEER_HEADS, PEER_NKEYS, PEER_DQ, PEER_TOPK = 8, 128, 256, 16
PEER_EXPERTS = PEER_NKEYS * PEER_NKEYS

LANES = 128
LOG2E = math.log2(math.e)
NEG_INIT = -1.0e30
MASKVAL = -2.0e30
VMEM_LIMIT = 56 * 1024 * 1024
TK_CAUSAL = 1024
Q_LANES = 1024
GATE_LANES = 512

_SEG = dict(fq=(0, 256), fk=(256, 256), fv=(512, 256), nq=(768, 512), nkc=(1280, 128), nvc=(1408, 128),
            nks=(1536, 128), nvs=(1664, 128), nkw=(1792, 128), nvw=(1920, 128),
            dq=(2048, 256), dk=(2304, 256), dv=(2560, 256), misc=(2816, 128))
PROJ_W = 2944


def _cparams(sem):
    return pltpu.CompilerParams(dimension_semantics=sem, vmem_limit_bytes=VMEM_LIMIT)


def _full(shape):
    n = len(shape)
    return pl.BlockSpec(shape, lambda *_: (0,) * n)


def _rope_tab_kernel(pos_ref, f64_ref, g64_ref, f32_ref, g32_ref, c64_o, s64_o, c32_o, s32_o):
    pos = pos_ref[...].astype(F32)
    a64 = pos * f64_ref[...]
    c64_o[...] = jnp.cos(a64)
    s64_o[...] = jnp.sin(a64) * g64_ref[...]
    a32 = pos * f32_ref[...]
    c32_o[...] = jnp.cos(a32)
    s32_o[...] = jnp.sin(a32) * g32_ref[...]


def _rope_tables(positions):
    s = positions.shape[-1]
    pos = positions.reshape(s, 1)
    lane = np.arange(LANES)

    def lanes(half):
        inv = ROPE_THETA ** (-jnp.arange(half, dtype=F32) / half)
        freq = inv[(lane % (2 * half)) % half].reshape(1, LANES)
        sign = np.where((lane % (2 * half)) < half, -1.0, 1.0).astype(np.float32).reshape(1, LANES)
        return freq, jnp.asarray(sign)

    f64, g64 = lanes(NSA_HD // 2)
    f32_, g32 = lanes(DIFF_QK // 2)
    tm = 512
    out = jax.ShapeDtypeStruct((s, LANES), F32)
    row = pl.BlockSpec((tm, LANES), lambda i: (i, 0))
    return pl.pallas_call(
        _rope_tab_kernel,
        grid=(s // tm,),
        in_specs=[pl.BlockSpec((tm, 1), lambda i: (i, 0))] + [_full((1, LANES))] * 4,
        out_specs=[row] * 4,
        out_shape=[out] * 4,
        compiler_params=_cparams(("parallel",)),
        name="rope_tables",
    )(pos, f64, g64, f32_, g32)


def _rms(x, w):
    return x * lax.rsqrt(jnp.mean(x * x, axis=-1, keepdims=True) + EPS) * w


def _seg_rms(x, bmat, seg):
    outs = []
    for c0 in range(0, x.shape[1], 256):
        w = min(256, x.shape[1] - c0)
        xc = x[:, c0:c0 + w]
        ss = jnp.dot(xc * xc, bmat[:w, :w], precision=HIGHEST, preferred_element_type=F32)
        outs.append(xc * lax.rsqrt(ss * (1.0 / seg) + EPS))
    return outs[0] if len(outs) == 1 else jnp.concatenate(outs, axis=1)


def _tile_lanes(t, width):
    reps = width // LANES
    return t if reps == 1 else jnp.concatenate([t] * reps, axis=1)


def _rope(x, cos, sin, half):
    width = x.shape[1]
    left = pltpu.roll(x, width - half, 1)
    right = pltpu.roll(x, half, 1)
    lane = lax.broadcasted_iota(jnp.int32, x.shape, 1)
    swapped = jnp.where((lane & (2 * half - 1)) < half, left, right)
    return x * _tile_lanes(cos, width) + swapped * _tile_lanes(sin, width)


def _lane_fill(tm, cols):
    lane = lax.broadcasted_iota(jnp.int32, (tm, LANES), 1)
    out = jnp.zeros((tm, LANES), F32)
    for l, v in cols.items():
        out = jnp.where(lane == l, v, out)
    return out


def _slots(x, extras):
    lane = lax.broadcasted_iota(jnp.int32, (x.shape[0], LANES), 1)
    outs = []
    for h, ex in enumerate(extras):
        col = x[:, (h // 2) * LANES:(h // 2 + 1) * LANES]
        if h % 2:
            col = pltpu.roll(col, LANES // 2, 1)
        outs.append(jnp.where(lane < LANES // 2, col, ex))
    return jnp.concatenate(outs, axis=1)


def _prep_kernel(shift_ref, proj_ref, c64_ref, s64_ref, c32_ref, s32_ref, fb_ref, wfq_ref, wfk_ref, wnq_ref,
                 wnk_ref, wdq_ref, wdk_ref, b64_ref, b32_ref,
                 fq_o, fk_o, fv_o, nqn_o, nqr_o, kvc_o, ks_o, vs_o, kw_o, vw_o, gate_o, dq_o, dk_o, dv_o,
                 carry_sc, *, tm):
    def seg(name):
        c0, w = _SEG[name]
        return proj_ref[:, c0:c0 + w]

    b64 = b64_ref[...]
    b32 = b32_ref[...]
    c64, s64, c32, s32 = c64_ref[...], s64_ref[...], c32_ref[...], s32_ref[...]
    one_ex = _lane_fill(tm, {COL_SHIFT: 1.0})

    misc = seg("misc")
    gate_o[...] = 1.0 / (1.0 + jnp.exp(-misc))
    t = misc + fb_ref[...]
    logf = jnp.minimum(t, 0.0) - jnp.log1p(jnp.exp(-jnp.abs(t)))

    @pl.when(pl.program_id(0) == 0)
    def _():
        carry_sc[...] = jnp.zeros_like(carry_sc)

    r = lax.broadcasted_iota(jnp.int32, (tm, tm), 0)
    c = lax.broadcasted_iota(jnp.int32, (tm, tm), 1)
    tri = (c <= r).astype(F32)
    csum = jnp.dot(tri, logf, precision=HIGHEST, preferred_element_type=F32) + carry_sc[0:1, :]
    carry_sc[...] = jnp.broadcast_to(csum[tm - 1:tm, :], carry_sc.shape)
    csum = csum * LOG2E
    hi = csum.astype(BF16).astype(F32)
    r1 = csum - hi
    mid = r1.astype(BF16).astype(F32)
    lo = (r1 - mid).astype(BF16).astype(F32)

    fq = _seg_rms(seg("fq"), b64, FOX_HD) * wfq_ref[...] * (FOX_HD ** -0.5 * LOG2E)
    fk = _seg_rms(seg("fk"), b64, FOX_HD) * wfk_ref[...]
    q_ex, k_ex = [], []
    for h in range(FOX_HEADS):
        ch, cm, cl = hi[:, h:h + 1], mid[:, h:h + 1], lo[:, h:h + 1]
        q_ex.append(_lane_fill(tm, {COL_SHIFT: shift_ref[0], 65: 1.0, 66: 1.0, 67: 1.0, 68: ch, 69: cm, 70: cl}))
        k_ex.append(_lane_fill(tm, {COL_SHIFT: 1.0, 65: -ch, 66: -cm, 67: -cl, 68: 1.0, 69: 1.0, 70: 1.0}))
    fq_o[...] = _slots(fq, q_ex)
    fk_o[...] = _slots(fk, k_ex).astype(BF16)
    fv_o[...] = _slots(seg("fv"), [one_ex] * FOX_HEADS).astype(BF16)

    n_ex = [_lane_fill(tm, {COL_SHIFT: shift_ref[1]})] * NSA_HEADS
    nqn = _seg_rms(seg("nq"), b64, NSA_HD) * wnq_ref[...]
    nqn_o[...] = _slots(nqn * (NSA_HD ** -0.5 * LOG2E), n_ex)
    nqr_o[...] = _slots(_rope(nqn, c64, s64, NSA_HD // 2) * (NSA_HD ** -0.5 * LOG2E), n_ex)
    kvc_o[:, 0:128] = seg("nkc").astype(BF16)
    kvc_o[:, 128:256] = seg("nvc").astype(BF16)
    wnk = wnk_ref[...]
    lane = lax.broadcasted_iota(jnp.int32, (tm, LANES), 1)
    row = pl.program_id(0) * tm + lax.broadcasted_iota(jnp.int32, (tm, LANES), 0)
    blk_in_tile = (row & (TK_CAUSAL - 1)) // SLC_LEN
    sel_ex = jnp.where(lane == COL_SEL + blk_in_tile, 1.0, one_ex)
    ks = _rope(_seg_rms(seg("nks"), b64, NSA_HD) * wnk, c64, s64, NSA_HD // 2)
    ks_o[...] = _slots(ks, [sel_ex] * NSA_KV).astype(BF16)
    vs_o[...] = _slots(seg("nvs"), [one_ex] * NSA_KV).astype(BF16)
    kw = _rope(_seg_rms(seg("nkw"), b64, NSA_HD) * wnk, c64, s64, NSA_HD // 2)
    kw_o[...] = _slots(kw, [one_ex] * NSA_KV).astype(BF16)
    vw_o[...] = _slots(seg("nvw"), [one_ex] * NSA_KV).astype(BF16)

    dq = _rope(_seg_rms(seg("dq"), b32, DIFF_QK) * wdq_ref[...], c32, s32, DIFF_QK // 2) * (DIFF_QK ** -0.5 * LOG2E)
    d_ex = _lane_fill(tm, {COL_SHIFT: shift_ref[2]})
    d_slots = []
    for h in range(DIFF_HEADS):
        col = dq[:, (h // 2) * LANES:(h // 2 + 1) * LANES]
        if h % 2:
            col = pltpu.roll(col, LANES // 2, 1)
        d_slots.append(jnp.where(lane < DIFF_QK, col, d_ex))
        d_slots.append(jnp.where((lane >= DIFF_QK) & (lane < 2 * DIFF_QK), col, d_ex))
    dq_o[...] = jnp.concatenate(d_slots, axis=1)
    dk = _rope(_seg_rms(seg("dk"), b32, DIFF_QK) * wdk_ref[...], c32, s32, DIFF_QK // 2)
    dk_o[...] = _slots(dk, [one_ex] * DIFF_HEADS).astype(BF16)
    dv_o[...] = _slots(seg("dv"), [one_ex] * DIFF_HEADS).astype(BF16)


def _block_diag_ones(n, seg):
    i = np.arange(n)
    return jnp.asarray((i[:, None] // seg == i[None, :] // seg).astype(np.float32))


def _proj_prep_kernel(shift_ref, x_ref, nw_ref, wa_ref, wb_ref, wc_ref, wm_ref, *rest, tm):
    proj_sc = rest[-1]
    xn = _rms(x_ref[...], nw_ref[...]).astype(BF16)
    c0 = 0
    for w_ref in (wa_ref, wb_ref, wc_ref, wm_ref):
        n = w_ref.shape[1]
        proj_sc[:, c0:c0 + n] = jnp.dot(xn, w_ref[...], preferred_element_type=F32)
        c0 += n
    _prep_kernel(shift_ref, proj_sc, *rest[:-1], tm=tm)


def _prep(shifts, x, nw, ws_bf16, tabs, fb, wfq, wfk, wnq, wnk, wdq, wdk, tm=256):
    s, d = x.shape
    c64, s64, c32, s32 = tabs
    b64 = _block_diag_ones(256, 64)
    b32 = _block_diag_ones(256, 32)

    def tiled(w, width):
        return jnp.tile(w.reshape(1, -1), (1, width // w.shape[-1]))

    consts = [fb, tiled(wfq, 256), tiled(wfk, 256), tiled(wnq, 512), tiled(wnk, 128), tiled(wdq, 256),
              tiled(wdk, 256), b64, b32]
    outs = [(FOX_HEADS, F32), (FOX_HEADS, BF16), (FOX_HEADS, BF16),
            (NSA_HEADS, F32), (NSA_HEADS, F32), (2, BF16),
            (NSA_KV, BF16), (NSA_KV, BF16), (NSA_KV, BF16), (NSA_KV, BF16), (1, F32),
            (2 * DIFF_HEADS, F32), (DIFF_HEADS, BF16), (DIFF_HEADS, BF16)]
    row = lambda w: pl.BlockSpec((tm, w), lambda i: (i, 0))
    assert sum(w.shape[1] for w in ws_bf16) == PROJ_W
    return pl.pallas_call(
        functools.partial(_proj_prep_kernel, tm=tm),
        grid=(s // tm,),
        in_specs=[pl.BlockSpec(memory_space=pltpu.SMEM), row(d), _full((1, d))] + [_full(w.shape) for w in ws_bf16]
                 + [row(LANES)] * 4 + [_full(c.shape) for c in consts],
        out_specs=[row(n * LANES) for n, _ in outs],
        out_shape=[jax.ShapeDtypeStruct((s, n * LANES), dt) for n, dt in outs],
        scratch_shapes=[pltpu.VMEM((8, LANES), F32), pltpu.VMEM((tm, PROJ_W), F32)],
        compiler_params=_cparams(("arbitrary",)),
        name="proj_head_prep",
    )(shifts, x, nw.reshape(1, d), *ws_bf16, c64, s64, c32, s32, *consts)


QK_DIM = 128
COL_SHIFT = 64
COL_SEL = 80
SCORE_BOUND = 60.0
SHIFT_HEADROOM = 60.0


def _flash_kernel(code_ref, q_ref, k_ref, v_ref, *rest, groups, tq, tk, window, has_sel, bounded):
    rest = list(rest)
    sel_ref = rest.pop(0) if has_sel else None
    o_ref = rest.pop(0)
    qT_sc = rest.pop(0)
    m_sc = None if bounded else rest.pop(0)
    acc_sc = rest.pop(0)
    code = code_ref[pl.program_id(1)]
    i = code & 0xFFF
    j = (code >> 12) & 0xFFF
    first = (code >> 24) & 1
    last = (code >> 25) & 1
    rows = groups * tq

    @pl.when(first == 1)
    def _():
        for g in range(groups):
            qT_sc[:, g * tq:(g + 1) * tq] = q_ref[:, g * QK_DIM:(g + 1) * QK_DIM].T.astype(BF16)
        if not bounded:
            m_sc[...] = jnp.full_like(m_sc, NEG_INIT)
        acc_sc[...] = jnp.zeros_like(acc_sc)

    def step(masked, nk=tk):
        if has_sel:
            qT_sc[COL_SEL:COL_SEL + tk // SLC_LEN, :] = _tile_lanes_any(sel_ref[0], groups)
        s = jnp.dot(k_ref[0:nk, :], qT_sc[...], preferred_element_type=F32)
        if masked:
            kpos = j * tk + lax.broadcasted_iota(jnp.int32, (nk, rows), 0)
            lane = lax.broadcasted_iota(jnp.int32, (nk, rows), 1)
            qpos = i * tq + (lane & (tq - 1))
            keep = kpos <= qpos
            if window is not None:
                keep = keep & (kpos > qpos - window)
            s = jnp.where(keep, s, MASKVAL)
        v_t = (((0,), (0,)), ((), ()))
        v = v_ref[0:nk, :]
        if bounded:
            acc_sc[...] += lax.dot_general(v, jnp.exp2(s).astype(BF16), v_t, preferred_element_type=F32)
        else:
            m_prev = m_sc[...]
            m_new = jnp.maximum(m_prev, jnp.max(s, axis=0, keepdims=True))
            p = jnp.exp2(s - m_new).astype(BF16)
            acc_sc[...] = (acc_sc[...] * jnp.exp2(m_prev - m_new)
                           + lax.dot_general(v, p, v_t, preferred_element_type=F32))
            m_sc[...] = m_new

    if window is not None:
        step(True)
    else:
        needs_mask = (j + 1) * tk - 1 > i * tq
        visible = jnp.minimum((i + 1) * tq - j * tk, tk)

        for nk in range(tq, tk + 1, tq):
            @pl.when(needs_mask & (visible == nk))
            def _(nk=nk):
                step(True, nk)

        @pl.when(jnp.logical_not(needs_mask))
        def _():
            step(False)

    @pl.when(last == 1)
    def _():
        l = acc_sc[COL_SHIFT:COL_SHIFT + 1, :]
        o = acc_sc[...] / jnp.where(l > 0.0, l, 1.0)
        for g in range(groups):
            o_ref[:, g * QK_DIM:(g + 1) * QK_DIM] = o[:, g * tq:(g + 1) * tq].T


def _tile_lanes_any(t, reps):
    return t if reps == 1 else jnp.concatenate([t] * reps, axis=1)


def _flash_call(q, k, v, sel, *, groups, tq, tk, window, bounded):
    s_len = q.shape[0]
    hkv = k.shape[1] // QK_DIM
    nq = s_len // tq
    rows = groups * tq
    codes = []
    for i in range(nq):
        q_lo, q_hi = i * tq, i * tq + tq - 1
        j_hi = q_hi // tk
        j_lo = 0 if window is None else max(0, (q_lo - window + 1) // tk)
        for j in range(j_lo, j_hi + 1):
            codes.append(i | (j << 12) | (int(j == j_lo) << 24) | (int(j == j_hi) << 25))
    codes = jnp.asarray(np.asarray(codes, dtype=np.int32))
    nsteps = codes.shape[0]

    def ti(c, s):
        return c[s] & 0xFFF

    def tj(c, s):
        return (c[s] >> 12) & 0xFFF

    in_specs = [
        pl.BlockSpec((tq, groups * QK_DIM), lambda h, s, c: (ti(c, s), h)),
        pl.BlockSpec((tk, QK_DIM), lambda h, s, c: (tj(c, s), h)),
        pl.BlockSpec((tk, QK_DIM), lambda h, s, c: (tj(c, s), h)),
    ]
    args = [q, k, v]
    if sel is not None:
        in_specs.append(pl.BlockSpec((1, tk // SLC_LEN, tq), lambda h, s, c: (h, tj(c, s), ti(c, s))))
        args.append(sel)
    scratch = [pltpu.VMEM((QK_DIM, rows), BF16)]
    if not bounded:
        scratch.append(pltpu.VMEM((1, rows), F32))
    scratch.append(pltpu.VMEM((QK_DIM, rows), F32))
    kern = functools.partial(_flash_kernel, groups=groups, tq=tq, tk=tk, window=window, has_sel=sel is not None,
                             bounded=bounded)
    return pl.pallas_call(
        kern,
        grid_spec=pltpu.PrefetchScalarGridSpec(
            num_scalar_prefetch=1,
            grid=(hkv, nsteps),
            in_specs=in_specs,
            out_specs=pl.BlockSpec((tq, groups * QK_DIM), lambda h, s, c: (ti(c, s), h)),
            scratch_shapes=scratch,
        ),
        out_shape=jax.ShapeDtypeStruct(q.shape, F32),
        compiler_params=_cparams(("parallel", "arbitrary")),
        name="flash_bounded" if bounded else "flash_online",
    )(codes, *args)


def _flash(q, k, v, sel=None, *, bound_ok, groups, tq, tk, window=None):
    call = functools.partial(_flash_call, q, k, v, sel, groups=groups, tq=tq, tk=tk, window=window)
    return lax.cond(bound_ok, lambda: call(bounded=True), lambda: call(bounded=False))


def _score_bound(wq, wk, seg, scale):
    return 1.02 * seg * jnp.max(jnp.abs(wq)) * jnp.max(jnp.abs(wk)) * scale * LOG2E


def _compress_kernel(r_ref, wa_ref, wb_ref, pa_ref, pb_ref, nw_ref, b64_ref, kx_o, vx_o):
    n = r_ref.shape[0]
    r = r_ref[...]
    wa, wb = wa_ref[...], wb_ref[...]
    first = jnp.dot(r, wa.astype(BF16), preferred_element_type=F32)
    second = jnp.dot(r, wb.astype(BF16), preferred_element_type=F32)
    nxt = pltpu.roll(second, n - 1, 0)
    rowi = lax.broadcasted_iota(jnp.int32, nxt.shape, 0)
    nxt = jnp.where(rowi == n - 1, 0.0, nxt)
    const = (jnp.dot(pa_ref[...], wa, precision=HIGHEST, preferred_element_type=F32)
             + jnp.dot(pb_ref[...], wb, precision=HIGHEST, preferred_element_type=F32))
    out = first + nxt + const
    k, v = out[:, 0:LANES], out[:, LANES:2 * LANES]
    ss = jnp.dot(k * k, b64_ref[...], precision=HIGHEST, preferred_element_type=F32)
    kn = k * lax.rsqrt(ss * (1.0 / NSA_HD) + EPS) * nw_ref[...]
    one_ex = _lane_fill(n, {COL_SHIFT: 1.0})
    kx_o[...] = _slots(kn, [one_ex] * NSA_KV).astype(BF16)
    vx_o[...] = _slots(v, [one_ex] * NSA_KV).astype(BF16)


def _compress(kvc, cmp_w, cmp_pos, nw):
    s = kvc.shape[0]
    n = s // CMP_STRIDE
    width = CMP_STRIDE * 2 * NSA_KV * NSA_HD
    w = cmp_w.reshape(2, 2, CMP_STRIDE, NSA_HD, NSA_HD)
    eye = jnp.eye(2, dtype=w.dtype)
    big = jnp.einsum("khldD,kK,gG->hlkgdKGD", w, eye, eye).reshape(2, width, 2 * NSA_KV * NSA_HD)
    pos = cmp_pos.reshape(2, 2, CMP_STRIDE, 1, NSA_HD)
    pos = jnp.broadcast_to(pos, (2, 2, CMP_STRIDE, NSA_KV, NSA_HD)).transpose(1, 2, 0, 3, 4).reshape(2, 1, width)
    b64 = _block_diag_ones(LANES, NSA_HD)
    nwt = jnp.tile(nw.reshape(1, -1), (1, LANES // NSA_HD))
    out = jax.ShapeDtypeStruct((n, NSA_KV * LANES), BF16)
    return pl.pallas_call(
        _compress_kernel,
        grid=(1,),
        in_specs=[_full((n, width)), _full(big[0].shape), _full(big[1].shape), _full((1, width)), _full((1, width)),
                  _full((1, LANES)), _full((LANES, LANES))],
        out_specs=[_full((n, NSA_KV * LANES))] * 2,
        out_shape=[out, out],
        compiler_params=_cparams(("arbitrary",)),
        name="nsa_compress",
    )(kvc.reshape(n, width), big[0], big[1], pos[0], pos[1], nwt, b64)


def _nsa_cmp_kernel(q_ref, kc_ref, vc_ref, ov_ref, o_ref, sel_ref, *, tq, groups, bounded):
    i = pl.program_id(1)
    ncmp = kc_ref.shape[0]
    rows = groups * tq
    qT = jnp.concatenate([q_ref[:, g * QK_DIM:(g + 1) * QK_DIM].T for g in range(groups)], axis=1).astype(BF16)
    s = jnp.dot(kc_ref[...], qT, preferred_element_type=F32)
    n_idx = lax.broadcasted_iota(jnp.int32, (ncmp, rows), 0)
    lane = lax.broadcasted_iota(jnp.int32, (ncmp, rows), 1)
    qpos = i * tq + (lane & (tq - 1))
    keep = n_idx * CMP_STRIDE + (CMP_LEN - 1) <= qpos
    if bounded:
        e = jnp.where(keep, jnp.exp2(s), 0.0)
    else:
        sm = jnp.where(keep, s, NEG_INIT)
        e = jnp.where(keep, jnp.exp2(sm - jnp.max(sm, axis=0, keepdims=True)), 0.0)
    den = jnp.sum(e, axis=0, keepdims=True)
    p = e / jnp.where(den > 0.0, den, 1.0)
    o = lax.dot_general(vc_ref[...], p.astype(BF16), (((0,), (0,)), ((), ())), preferred_element_type=F32)
    for g in range(groups):
        o_ref[:, g * QK_DIM:(g + 1) * QK_DIM] = o[:, g * tq:(g + 1) * tq].T

    psum = p[:, 0:tq]
    for g in range(1, groups):
        psum = psum + p[:, g * tq:(g + 1) * tq]
    imp = jnp.dot(ov_ref[...], psum, precision=HIGHEST, preferred_element_type=F32)
    ns = imp.shape[0]
    blk = lax.broadcasted_iota(jnp.int32, (ns, tq), 0)
    qp = i * tq + lax.broadcasted_iota(jnp.int32, (ns, tq), 1)
    cur = qp // SLC_LEN
    valid = blk <= cur
    forced = (blk == 0) | (blk == cur) | (blk == cur - 1)
    work = jnp.where(valid, imp + jnp.where(forced, FORCE_BONUS, 0.0), NEG_INIT)
    picked = jnp.zeros((ns, tq), dtype=jnp.bool_)
    for _ in range(SLC_TOPK):
        mx = jnp.max(work, axis=0, keepdims=True)
        idx = jnp.min(jnp.where(work == mx, blk, ns), axis=0, keepdims=True)
        pick = (blk == idx) & (mx > 0.5 * NEG_INIT)
        picked = picked | pick
        work = jnp.where(pick, NEG_INIT, work)
    sel_ref[0] = jnp.where(picked, 0.0, MASKVAL).astype(BF16)


def _nsa_cmp(qn, kcx, vcx, overlapT, tq, bound_ok):
    s = qn.shape[0]
    ncmp = kcx.shape[0]
    ns = overlapT.shape[0]
    groups = NSA_G

    def call(bounded):
        return pl.pallas_call(
            functools.partial(_nsa_cmp_kernel, tq=tq, groups=groups, bounded=bounded),
            grid=(NSA_KV, s // tq),
            in_specs=[pl.BlockSpec((tq, groups * QK_DIM), lambda h, i: (i, h)),
                      pl.BlockSpec((ncmp, QK_DIM), lambda h, i: (0, h)),
                      pl.BlockSpec((ncmp, QK_DIM), lambda h, i: (0, h)),
                      _full(overlapT.shape)],
            out_specs=[pl.BlockSpec((tq, groups * QK_DIM), lambda h, i: (i, h)),
                       pl.BlockSpec((1, ns, tq), lambda h, i: (h, 0, i))],
            out_shape=[jax.ShapeDtypeStruct(qn.shape, F32), jax.ShapeDtypeStruct((NSA_KV, ns, s), BF16)],
            compiler_params=_cparams(("parallel", "parallel")),
            name="nsa_cmp_bounded" if bounded else "nsa_cmp_max",
        )(qn, kcx, vcx, overlapT)

    return lax.cond(bound_ok, lambda: call(True), lambda: call(False))


def _mix_out_kernel(lam_ref, h_ref, fox_ref, oc_ref, os_ref, ow_ref, dd_ref, g_ref, sw_ref, wf_ref, wn_ref, wd_ref,
                    o_ref, *, diff_scale):
    lam = lam_ref[0]
    gates = g_ref[...]
    tm = gates.shape[0]
    lane = lax.broadcasted_iota(jnp.int32, (tm, LANES), 1)
    slot = lambda ref, n: ref[:, n * LANES:(n + 1) * LANES]
    nsa = []
    for h in range(NSA_HEADS):
        c0 = FOX_HEADS + 3 * h
        nsa.append(gates[:, c0:c0 + 1] * slot(oc_ref, h) + gates[:, c0 + 1:c0 + 2] * slot(os_ref, h)
                   + gates[:, c0 + 2:c0 + 3] * slot(ow_ref, h))
    diff = []
    for h in range(DIFF_HEADS):
        a = jnp.where(lane < DIFF_V, slot(dd_ref, 2 * h) - lam * slot(dd_ref, 2 * h + 1), 0.0)
        ss = jnp.sum(a * a, axis=-1, keepdims=True)
        diff.append(a * lax.rsqrt(ss * (1.0 / DIFF_V) + EPS) * sw_ref[...] * diff_scale)
    acc = h_ref[...]
    acc = acc + jnp.dot(fox_ref[...].astype(BF16), wf_ref[...], preferred_element_type=F32)
    acc = acc + jnp.dot(jnp.concatenate(nsa, axis=1).astype(BF16), wn_ref[...], preferred_element_type=F32)
    acc = acc + jnp.dot(jnp.concatenate(diff, axis=1).astype(BF16), wd_ref[...], preferred_element_type=F32)
    o_ref[...] = acc


def _slot_rows(w, heads):
    d = w.shape[1]
    w = w.reshape(heads, -1, d)
    return jnp.concatenate([w, jnp.zeros((heads, LANES - w.shape[1], d), w.dtype)], axis=1).reshape(heads * LANES, d)


def _mix_out(lam, h, fox, oc, os_, ow, dd, gates, subln_w, w_out, diff_scale, tm=512):
    s, d = h.shape
    nf, nn = FOX_HEADS * FOX_HD, NSA_HEADS * NSA_HD
    wf = _slot_rows(w_out[:nf], FOX_HEADS).astype(BF16)
    wn = _slot_rows(w_out[nf:nf + nn], NSA_HEADS).astype(BF16)
    wd = _slot_rows(w_out[nf + nn:], DIFF_HEADS).astype(BF16)
    sw = jnp.concatenate([subln_w.reshape(1, -1), jnp.zeros((1, LANES - DIFF_V), F32)], axis=1)
    row = lambda w: pl.BlockSpec((tm, w), lambda i: (i, 0))
    return pl.pallas_call(
        functools.partial(_mix_out_kernel, diff_scale=diff_scale),
        grid=(s // tm,),
        in_specs=[pl.BlockSpec(memory_space=pltpu.SMEM), row(d), row(fox.shape[1]), row(oc.shape[1]),
                  row(os_.shape[1]), row(ow.shape[1]), row(dd.shape[1]), row(LANES), _full((1, LANES)),
                  _full(wf.shape), _full(wn.shape), _full(wd.shape)],
        out_specs=row(d),
        out_shape=jax.ShapeDtypeStruct((s, d), F32),
        compiler_params=_cparams(("parallel",)),
        name="mix_out",
    )(lam, h, fox, oc, os_, ow, dd, gates, sw, wf, wn, wd)


def _peer_query_kernel(h_ref, nw_ref, wq_ref, keys_ref, xnT_o, scT_o):
    xn = _rms(h_ref[...], nw_ref[...])
    xnT_o[...] = xn.T.astype(BF16)
    q = jnp.dot(xn.astype(BF16), wq_ref[...], preferred_element_type=F32).astype(BF16)
    half = PEER_DQ // 2
    for b in range(2 * PEER_HEADS):
        sc = jnp.dot(q[:, b * half:(b + 1) * half], keys_ref[b], preferred_element_type=F32)
        scT_o[b] = sc.T


def _peer_query(h, nw, wq_bf16, keysT_bf16, tm=512):
    s, d = h.shape
    n = wq_bf16.shape[1]
    nb = 2 * PEER_HEADS
    return pl.pallas_call(
        _peer_query_kernel,
        grid=(s // tm,),
        in_specs=[pl.BlockSpec((tm, d), lambda i: (i, 0)), _full((1, d)), _full((d, n)), _full(keysT_bf16.shape)],
        out_specs=[pl.BlockSpec((d, tm), lambda i: (0, i)), pl.BlockSpec((nb, PEER_NKEYS, tm), lambda i: (0, 0, i))],
        out_shape=[jax.ShapeDtypeStruct((d, s), BF16), jax.ShapeDtypeStruct((nb, PEER_NKEYS, s), F32)],
        compiler_params=_cparams(("parallel",)),
        name="peer_query",
    )(h, nw.reshape(1, d), wq_bf16, keysT_bf16)


def _top16_rows(s, exact_ties):
    n = s.shape[0]
    row = lax.broadcasted_iota(jnp.int32, s.shape, 0)
    rank = jnp.full(s.shape, float(PEER_TOPK), dtype=F32)
    work = s
    vals = []
    for r in range(PEER_TOPK):
        mx = jnp.max(work, axis=0, keepdims=True)
        if exact_ties:
            pick = row == jnp.min(jnp.where(work == mx, row, n), axis=0, keepdims=True)
        else:
            pick = work == mx
        rank = jnp.where(pick, float(r), rank)
        work = jnp.where(pick, NEG_INIT, work)
        vals.append(mx)
    return rank, vals


def _peer_route(s1, s2, exact_ties):
    rank1, v1 = _top16_rows(s1, exact_ties)
    rank2, v2 = _top16_rows(s2, exact_ties)
    k = PEER_TOPK
    tm = s1.shape[1]
    r16 = lax.broadcasted_iota(jnp.int32, (k, tm), 0)
    v1m = jnp.zeros((k, tm), F32)
    v2m = jnp.zeros((k, tm), F32)
    for r in range(k):
        v1m = jnp.where(r16 == r, v1[r], v1m)
        v2m = jnp.where(r16 == r, v2[r], v2m)
    ea = jnp.exp(v1m - v1[0])
    eb = jnp.exp(v2m - v2[0])
    cands, gates = [v1m[0:1] + v2m], [ea[0:1] * eb]
    for r in range(1, 8):
        cands.append(v1m[r:r + 1] + v2m[0:8])
        gates.append(ea[r:r + 1] * eb[0:8])
    cands.append(v1m[8:16] + v2m[0:1])
    gates.append(ea[8:16] * eb[0:1])
    cand = jnp.concatenate(cands, axis=0)
    gate = jnp.concatenate(gates, axis=0)
    ncand = cand.shape[0]
    crow = lax.broadcasted_iota(jnp.int32, cand.shape, 0)
    for r in range(2, 8):
        start = 16 + 8 * (r - 1)
        cand = jnp.where((crow >= start + k // (r + 1)) & (crow < start + 8), NEG_INIT, cand)
    work = cand
    picked = jnp.zeros(cand.shape, dtype=jnp.bool_)
    for _ in range(k):
        mx = jnp.max(work, axis=0, keepdims=True)
        if exact_ties:
            pick = crow == jnp.min(jnp.where(work == mx, crow, ncand), axis=0, keepdims=True)
        else:
            pick = work == mx
        picked = picked | pick
        work = jnp.where(pick, NEG_INIT, work)
    pf = picked.astype(F32)
    z = jnp.sum(pf * gate, axis=0, keepdims=True)
    cnt = [jnp.sum(pf[0:16, :], axis=0, keepdims=True)]
    cnt += [jnp.sum(pf[8 + 8 * r:16 + 8 * r, :], axis=0, keepdims=True) for r in range(1, 8)]
    cnt += [pf[64 + r:65 + r, :] for r in range(8, 16)]
    cmap = jnp.zeros(s1.shape, dtype=F32)
    for r in range(k):
        cmap = jnp.where(rank1 == float(r), cnt[r], cmap)
    in1 = rank1 < float(k)
    in2 = rank2 < float(k)
    a = jnp.where(in1, jnp.exp(s1 - v1[0]), 0.0) / z
    b = jnp.where(in2, jnp.exp(s2 - v2[0]), 0.0)
    excess = (jnp.abs(jnp.sum(in1.astype(F32), axis=0, keepdims=True) - k)
              + jnp.abs(jnp.sum(in2.astype(F32), axis=0, keepdims=True) - k)
              + jnp.abs(jnp.sum(pf, axis=0, keepdims=True) - k))
    return a, cmap, b, rank2, excess


def _peer_topk_kernel(sc_ref, a_o, c_o, b_o, r_o):
    def run(exact_ties):
        a, c, b, rank2, excess = _peer_route(sc_ref[0], sc_ref[1], exact_ties)
        a_o[0] = a
        c_o[0] = c
        b_o[0] = b.astype(BF16)
        r_o[0] = rank2.astype(BF16)
        return excess

    excess = run(False)

    @pl.when(jnp.max(excess) > 0.0)
    def _():
        run(True)


def _peer_topk(scT, tm=256):
    n2, nk, s = scT.shape
    heads = n2 // 2
    ospec = pl.BlockSpec((1, nk, tm), lambda h, t: (h, 0, t))
    return pl.pallas_call(
        _peer_topk_kernel,
        grid=(heads, s // tm),
        in_specs=[pl.BlockSpec((2, nk, tm), lambda h, t: (h, 0, t))],
        out_specs=[ospec] * 4,
        out_shape=[jax.ShapeDtypeStruct((heads, nk, s), dt) for dt in (F32, F32, BF16, BF16)],
        compiler_params=_cparams(("parallel", "parallel")),
        name="peer_topk",
    )(scT)


def _gelu_exact(x):
    return 0.5 * x * (1.0 + lax.erf(x * (2.0 ** -0.5)))


def _peer_main_kernel(xT_ref, u_ref, v_ref, a_ref, c_ref, b_ref, r_ref, o_ref, w_sc, *, ec, sub):
    ci = pl.program_id(1)

    @pl.when(ci == 0)
    def _():
        o_ref[...] = jnp.zeros_like(o_ref)

    nk = PEER_NKEYS
    tm = xT_ref.shape[1]

    def row(ref, h, i1, lanes):
        x16 = jnp.broadcast_to(ref[h, pl.ds(i1, 1), :][:, lanes], (16, GATE_LANES)).astype(BF16)
        return jnp.concatenate([x16] * (nk // 16), axis=0)

    acc = None
    for sc in range(ec // sub):
        hid = jnp.dot(u_ref[sc * sub:(sc + 1) * sub, :], xT_ref[...], preferred_element_type=F32)
        act = _gelu_exact(hid).astype(BF16)
        for ii in range(sub // nk):
            slab = sc * (sub // nk) + ii
            i1 = ci * (ec // nk) + slab
            for l0 in range(0, tm, GATE_LANES):
                lanes = slice(l0, l0 + GATE_LANES)
                w = None
                for h in range(PEER_HEADS):
                    keep = r_ref[h, :, lanes] < row(c_ref, h, i1, lanes)
                    term = jnp.where(keep, b_ref[h, :, lanes], 0.0) * row(a_ref, h, i1, lanes)
                    w = term if w is None else w + term
                w_sc[slab * nk:(slab + 1) * nk, lanes] = w * act[ii * nk:(ii + 1) * nk, lanes]
        part = jnp.dot(w_sc[sc * sub:(sc + 1) * sub, :].T, v_ref[sc * sub:(sc + 1) * sub, :],
                       preferred_element_type=F32)
        acc = part if acc is None else acc + part
    o_ref[...] += acc


def _peer_main(xnT, u_bf16, v_bf16, a, c, b, r, tm=512, ec=2048, sub=512):
    d, s = xnT.shape
    e = u_bf16.shape[0]
    heads, nk, _ = a.shape
    rt = pl.BlockSpec((heads, nk, tm), lambda t, ci: (0, 0, t))
    wt = pl.BlockSpec((ec, d), lambda t, ci: (ci, 0))
    return pl.pallas_call(
        functools.partial(_peer_main_kernel, ec=ec, sub=sub),
        grid=(s // tm, e // ec),
        in_specs=[pl.BlockSpec((d, tm), lambda t, ci: (0, t)), wt, wt, rt, rt, rt, rt],
        out_specs=pl.BlockSpec((tm, d), lambda t, ci: (t, 0)),
        out_shape=jax.ShapeDtypeStruct((s, d), F32),
        scratch_shapes=[pltpu.VMEM((ec, tm), BF16)],
        compiler_params=_cparams(("parallel", "arbitrary")),
        name="peer_main",
    )(xnT, u_bf16, v_bf16, a, c, b, r)


def _ple_kernel(h_ref, peer_ref, p_ref, nw_ref, wg_ref, wp_ref, o_ref):
    h2 = h_ref[...] + peer_ref[...]
    xn = _rms(h2, nw_ref[...]).astype(BF16)
    gate = 1.0 / (1.0 + jnp.exp(-jnp.dot(xn, wg_ref[...], preferred_element_type=F32)))
    emb = jnp.dot(p_ref[...].astype(BF16), wp_ref[...], preferred_element_type=F32)
    o_ref[...] = h2 + gate * emb


def _ple(h, peer, p, nw, wg_bf16, wp_bf16, tm=512):
    s, d = h.shape
    pd = p.shape[1]
    row = lambda w: pl.BlockSpec((tm, w), lambda i: (i, 0))
    return pl.pallas_call(
        _ple_kernel,
        grid=(s // tm,),
        in_specs=[row(d), row(d), row(pd), _full((1, d)), _full((d, d)), _full((pd, d))],
        out_specs=row(d),
        out_shape=jax.ShapeDtypeStruct((s, d), F32),
        compiler_params=_cparams(("parallel",)),
        name="ple",
    )(h, peer, p, nw.reshape(1, d), wg_bf16, wp_bf16)


def _split_w_in(w):
    f0 = 3 * FOX_HEADS * FOX_HD
    n0 = f0 + FOX_HEADS
    g0 = n0 + _SEG["dq"][0] - _SEG["nq"][0]
    d0 = g0 + 3 * NSA_HEADS
    misc = jnp.concatenate([w[:, f0:n0], w[:, g0:d0], jnp.zeros((w.shape[0], LANES - (n0 - f0) - (d0 - g0)), w.dtype)],
                           axis=1)
    return [t.astype(BF16) for t in (w[:, :f0], w[:, n0:g0], w[:, d0:], misc)]


def _overlap_T(s):
    n = np.arange(s // CMP_STRIDE)[None, :] * CMP_STRIDE
    m = np.arange(s // SLC_LEN)[:, None] * SLC_LEN
    return jnp.asarray(((n < m + SLC_LEN) & (n + CMP_LEN > m)).astype(np.float32))


def _layer(h, p_i, tabs, layer, attn_norm_w, w_in, fox_f_bias, fox_q_norm_w, fox_k_norm_w, nsa_q_norm_w,
           nsa_k_norm_w, nsa_cmp_pos, nsa_cmp_w, diff_q_norm_w, diff_k_norm_w, diff_lambda, diff_subln_w,
           w_out, ffn_norm_w, peer_w_q, peer_sub_keys, peer_u, peer_v, ple_norm_w, ple_w_gate, ple_w_proj):
    s = h.shape[0]
    fb =jnp.zeros((1, LANES), F32).at[0, :FOX_HEADS].set(fox_f_bias)
    u_f = _score_bound(fox_q_norm_w, fox_k_norm_w, FOX_HD, FOX_HD ** -0.5)
    u_n = _score_bound(nsa_q_norm_w, nsa_k_norm_w, NSA_HD, NSA_HD ** -0.5)
    u_d = _score_bound(diff_q_norm_w, diff_k_norm_w, DIFF_QK, DIFF_QK ** -0.5)
    shifts = SHIFT_HEADROOM - jnp.stack([u_f, u_n, u_d]).astype(F32)
    (fq, fk, fv, nqn, nqr, kvc, ks, vs, kw, vw, gates, dq, dk, dv) = _prep(
        shifts, h, attn_norm_w, _split_w_in(w_in), tabs, fb, fox_q_norm_w, fox_k_norm_w, nsa_q_norm_w, nsa_k_norm_w, diff_q_norm_w, diff_k_norm_w)

    o_fox = _flash(fq, fk, fv, bound_ok=u_f <= SCORE_BOUND, groups=1, tq=Q_LANES, tk=TK_CAUSAL)

    tq_n = Q_LANES // NSA_G
    nsa_ok = u_n <= SCORE_BOUND
    kcx, vcx = _compress(kvc, nsa_cmp_w, nsa_cmp_pos, nsa_k_norm_w)
    o_c, sel = _nsa_cmp(nqn, kcx, vcx, _overlap_T(s), tq_n, nsa_ok)
    o_s = _flash(nqr, ks, vs, sel, bound_ok=nsa_ok, groups=NSA_G, tq=tq_n, tk=TK_CAUSAL)
    o_w = _flash(nqr, kw, vw, bound_ok=nsa_ok, groups=NSA_G, tq=tq_n, tk=512, window=WIN)

    o_d = _flash(dq, dk, dv, bound_ok=u_d <= SCORE_BOUND, groups=2, tq=Q_LANES // 2, tk=TK_CAUSAL)
    lv = diff_lambda.astype(F32)
    lam_init = 0.8 - 0.6 * math.exp(-0.3 * layer)
    lam = (jnp.exp(jnp.sum(lv[0] * lv[1])) - jnp.exp(jnp.sum(lv[2] * lv[3])) + lam_init).reshape(1)
    h1 = _mix_out(lam, h, o_fox, o_c, o_s, o_w, o_d, gates, diff_subln_w, w_out, 1.0 - lam_init)

    keysT = peer_sub_keys.reshape(2 * PEER_HEADS, PEER_NKEYS, PEER_DQ // 2).transpose(0, 2, 1).astype(BF16)
    xnT, scT = _peer_query(h1, ffn_norm_w, peer_w_q.astype(BF16), keysT)
    a, c, b, r = _peer_topk(scT)
    peer = _peer_main(xnT, peer_u.astype(BF16), peer_v.astype(BF16), a, c, b, r)

    return _ple(h1, peer, p_i, ple_norm_w, ple_w_gate.astype(BF16), ple_w_proj.astype(BF16))


def kernel(x, p, positions, attn_norm_w, w_in, fox_f_bias, fox_q_norm_w, fox_k_norm_w, nsa_q_norm_w, nsa_k_norm_w,
           nsa_cmp_pos, nsa_cmp_w, diff_q_norm_w, diff_k_norm_w, diff_lambda, diff_subln_w, w_out, ffn_norm_w,
           peer_w_q, peer_sub_keys, peer_u, peer_v, ple_norm_w, ple_w_gate, ple_w_proj):
    b, s, d = x.shape
    assert b == 1 and d == D_MODEL and s % Q_LANES == 0 and s % TK_CAUSAL == 0
    tabs = _rope_tables(positions)
    h = x.reshape(s, d)
    per_layer = (attn_norm_w, w_in, fox_f_bias, fox_q_norm_w, fox_k_norm_w, nsa_q_norm_w, nsa_k_norm_w, nsa_cmp_pos,
                 nsa_cmp_w, diff_q_norm_w, diff_k_norm_w, diff_lambda, diff_subln_w, w_out, ffn_norm_w, peer_w_q,
                 peer_sub_keys, peer_u, peer_v, ple_norm_w, ple_w_gate, ple_w_proj)
    for layer in range(attn_norm_w.shape[0]):
        h = _layer(h, p[layer, 0], tabs, layer, *(w[layer] for w in per_layer))
    return h.reshape(b, s, d)
```

```python
import functools
import math

import numpy as np
import jax
import jax.numpy as jnp
from jax import lax
from jax.experimental import pallas as pl
from jax.experimental.pallas import tpu as pltpu

F32 = jnp.float32
BF16 = jnp.bfloat16
HIGHEST = lax.Precision.HIGHEST

D_MODEL = 1024
PLE_DIM = 256
ROPE_THETA = 10000.0
EPS = 1e-6
FOX_HEADS, FOX_HD = 4, 64
NSA_HEADS, NSA_KV, NSA_HD = 8, 2, 64
NSA_G = NSA_HEADS // NSA_KV
CMP_LEN, CMP_STRIDE, SLC_LEN, SLC_TOPK, WIN = 32, 16, 64, 16, 512
FORCE_BONUS = 1.0e4
DIFF_HEADS, DIFF_QK, DIFF_V = 4, 32, 64
PEER_HEADS, PEER_NKEYS, PEER_DQ, PEER_TOPK = 8, 128, 256, 16
PEER_EXPERTS = PEER_NKEYS * PEER_NKEYS

LANES = 128
LOG2E = math.log2(math.e)
NEG_INIT = -1.0e30
MASKVAL = -2.0e30
VMEM_LIMIT = 56 * 1024 * 1024
TK_CAUSAL = 2048
Q_LANES = 1024
GATE_LANES = 512

_SEG = dict(fq=(0, 256), fk=(256, 256), fv=(512, 256), nq=(768, 512), nkc=(1280, 128), nvc=(1408, 128),
            nks=(1536, 128), nvs=(1664, 128), nkw=(1792, 128), nvw=(1920, 128),
            dq=(2048, 256), dk=(2304, 256), dv=(2560, 256), misc=(2816, 128))
PROJ_W = 2944


def _cparams(sem):
    return pltpu.CompilerParams(dimension_semantics=sem, vmem_limit_bytes=VMEM_LIMIT)


def _full(shape):
    n = len(shape)
    return pl.BlockSpec(shape, lambda *_: (0,) * n)


def _rope_tab_kernel(pos_ref, f64_ref, g64_ref, f32_ref, g32_ref, c64_o, s64_o, c32_o, s32_o):
    pos = pos_ref[...].astype(F32)
    a64 = pos * f64_ref[...]
    c64_o[...] = jnp.cos(a64)
    s64_o[...] = jnp.sin(a64) * g64_ref[...]
    a32 = pos * f32_ref[...]
    c32_o[...] = jnp.cos(a32)
    s32_o[...] = jnp.sin(a32) * g32_ref[...]


def _rope_tables(positions):
    s = positions.shape[-1]
    pos = positions.reshape(s, 1)
    lane = np.arange(LANES)

    def lanes(half):
        inv = ROPE_THETA ** (-jnp.arange(half, dtype=F32) / half)
        freq = inv[(lane % (2 * half)) % half].reshape(1, LANES)
        sign = np.where((lane % (2 * half)) < half, -1.0, 1.0).astype(np.float32).reshape(1, LANES)
        return freq, jnp.asarray(sign)

    f64, g64 = lanes(NSA_HD // 2)
    f32_, g32 = lanes(DIFF_QK // 2)
    tm = 512
    out = jax.ShapeDtypeStruct((s, LANES), F32)
    row = pl.BlockSpec((tm, LANES), lambda i: (i, 0))
    return pl.pallas_call(
        _rope_tab_kernel,
        grid=(s // tm,),
        in_specs=[pl.BlockSpec((tm, 1), lambda i: (i, 0))] + [_full((1, LANES))] * 4,
        out_specs=[row] * 4,
        out_shape=[out] * 4,
        compiler_params=_cparams(("parallel",)),
        name="rope_tables",
    )(pos, f64, g64, f32_, g32)


def _rms(x, w):
    return x * lax.rsqrt(jnp.mean(x * x, axis=-1, keepdims=True) + EPS) * w


def _seg_rms(x, bmat, seg):
    outs = []
    for c0 in range(0, x.shape[1], 256):
        w = min(256, x.shape[1] - c0)
        xc = x[:, c0:c0 + w]
        ss = jnp.dot(xc * xc, bmat[:w, :w], precision=HIGHEST, preferred_element_type=F32)
        outs.append(xc * lax.rsqrt(ss * (1.0 / seg) + EPS))
    return outs[0] if len(outs) == 1 else jnp.concatenate(outs, axis=1)


def _tile_lanes(t, width):
    reps = width // LANES
    return t if reps == 1 else jnp.concatenate([t] * reps, axis=1)


def _rope(x, cos, sin, half):
    width = x.shape[1]
    left = pltpu.roll(x, width - half, 1)
    right = pltpu.roll(x, half, 1)
    lane = lax.broadcasted_iota(jnp.int32, x.shape, 1)
    swapped = jnp.where((lane & (2 * half - 1)) < half, left, right)
    return x * _tile_lanes(cos, width) + swapped * _tile_lanes(sin, width)


def _lane_fill(tm, cols):
    lane = lax.broadcasted_iota(jnp.int32, (tm, LANES), 1)
    out = jnp.zeros((tm, LANES), F32)
    for l, v in cols.items():
        out = jnp.where(lane == l, v, out)
    return out


def _slots(x, extras):
    lane = lax.broadcasted_iota(jnp.int32, (x.shape[0], LANES), 1)
    outs = []
    for h, ex in enumerate(extras):
        col = x[:, (h // 2) * LANES:(h // 2 + 1) * LANES]
        if h % 2:
            col = pltpu.roll(col, LANES // 2, 1)
        outs.append(jnp.where(lane < LANES // 2, col, ex))
    return jnp.concatenate(outs, axis=1)


def _prep_kernel(shift_ref, proj_ref, c64_ref, s64_ref, c32_ref, s32_ref, fb_ref, wfq_ref, wfk_ref, wnq_ref,
                 wnk_ref, wdq_ref, wdk_ref, b64_ref, b32_ref,
                 fq_o, fk_o, fv_o, nqn_o, nqr_o, kvc_o, ks_o, vs_o, kw_o, vw_o, gate_o, dq_o, dk_o, dv_o,
                 carry_sc, *, tm):
    def seg(name):
        c0, w = _SEG[name]
        return proj_ref[:, c0:c0 + w]

    b64 = b64_ref[...]
    b32 = b32_ref[...]
    c64, s64, c32, s32 = c64_ref[...], s64_ref[...], c32_ref[...], s32_ref[...]
    one_ex = _lane_fill(tm, {COL_SHIFT: 1.0})

    misc = seg("misc")
    gate_o[...] = 1.0 / (1.0 + jnp.exp(-misc))
    t = misc + fb_ref[...]
    logf = jnp.minimum(t, 0.0) - jnp.log1p(jnp.exp(-jnp.abs(t)))

    @pl.when(pl.program_id(0) == 0)
    def _():
        carry_sc[...] = jnp.zeros_like(carry_sc)

    r = lax.broadcasted_iota(jnp.int32, (tm, tm), 0)
    c = lax.broadcasted_iota(jnp.int32, (tm, tm), 1)
    tri = (c <= r).astype(F32)
    csum = jnp.dot(tri, logf, precision=HIGHEST, preferred_element_type=F32) + carry_sc[0:1, :]
    carry_sc[...] = jnp.broadcast_to(csum[tm - 1:tm, :], carry_sc.shape)
    csum = csum * LOG2E
    hi = csum.astype(BF16).astype(F32)
    r1 = csum - hi
    mid = r1.astype(BF16).astype(F32)
    lo = (r1 - mid).astype(BF16).astype(F32)

    fq = _seg_rms(seg("fq"), b64, FOX_HD) * wfq_ref[...] * (FOX_HD ** -0.5 * LOG2E)
    fk = _seg_rms(seg("fk"), b64, FOX_HD) * wfk_ref[...]
    q_ex, k_ex = [], []
    for h in range(FOX_HEADS):
        ch, cm, cl = hi[:, h:h + 1], mid[:, h:h + 1], lo[:, h:h + 1]
        q_ex.append(_lane_fill(tm, {COL_SHIFT: shift_ref[0], 65: 1.0, 66: 1.0, 67: 1.0, 68: ch, 69: cm, 70: cl}))
        k_ex.append(_lane_fill(tm, {COL_SHIFT: 1.0, 65: -ch, 66: -cm, 67: -cl, 68: 1.0, 69: 1.0, 70: 1.0}))
    fq_o[...] = _slots(fq, q_ex)
    fk_o[...] = _slots(fk, k_ex).astype(BF16)
    fv_o[...] = _slots(seg("fv"), [one_ex] * FOX_HEADS).astype(BF16)

    n_ex = [_lane_fill(tm, {COL_SHIFT: shift_ref[1]})] * NSA_HEADS
    nqn = _seg_rms(seg("nq"), b64, NSA_HD) * wnq_ref[...]
    nqn_o[...] = _slots(nqn * (NSA_HD ** -0.5 * LOG2E), n_ex)
    nqr_o[...] = _slots(_rope(nqn, c64, s64, NSA_HD // 2) * (NSA_HD ** -0.5 * LOG2E), n_ex)
    kvc_o[:, 0:128] = seg("nkc").astype(BF16)
    kvc_o[:, 128:256] = seg("nvc").astype(BF16)
    wnk = wnk_ref[...]
    lane = lax.broadcasted_iota(jnp.int32, (tm, LANES), 1)
    row = pl.program_id(0) * tm + lax.broadcasted_iota(jnp.int32, (tm, LANES), 0)
    blk_in_tile = (row & (TK_CAUSAL - 1)) // SLC_LEN
    sel_ex = jnp.where(lane == COL_SEL + blk_in_tile, 1.0, one_ex)
    ks = _rope(_seg_rms(seg("nks"), b64, NSA_HD) * wnk, c64, s64, NSA_HD // 2)
    ks_o[...] = _slots(ks, [sel_ex] * NSA_KV).astype(BF16)
    vs_o[...] = _slots(seg("nvs"), [one_ex] * NSA_KV).astype(BF16)
    kw = _rope(_seg_rms(seg("nkw"), b64, NSA_HD) * wnk, c64, s64, NSA_HD // 2)
    kw_o[...] = _slots(kw, [one_ex] * NSA_KV).astype(BF16)
    vw_o[...] = _slots(seg("nvw"), [one_ex] * NSA_KV).astype(BF16)

    dq = _rope(_seg_rms(seg("dq"), b32, DIFF_QK) * wdq_ref[...], c32, s32, DIFF_QK // 2) * (DIFF_QK ** -0.5 * LOG2E)
    d_ex = _lane_fill(tm, {COL_SHIFT: shift_ref[2]})
    d_slots = []
    for h in range(DIFF_HEADS):
        col = dq[:, (h // 2) * LANES:(h // 2 + 1) * LANES]
        if h % 2:
            col = pltpu.roll(col, LANES // 2, 1)
        d_slots.append(jnp.where(lane < DIFF_QK, col, d_ex))
        d_slots.append(jnp.where((lane >= DIFF_QK) & (lane < 2 * DIFF_QK), col, d_ex))
    dq_o[...] = jnp.concatenate(d_slots, axis=1)
    dk = _rope(_seg_rms(seg("dk"), b32, DIFF_QK) * wdk_ref[...], c32, s32, DIFF_QK // 2)
    dk_o[...] = _slots(dk, [one_ex] * DIFF_HEADS).astype(BF16)
    dv_o[...] = _slots(seg("dv"), [one_ex] * DIFF_HEADS).astype(BF16)


def _block_diag_ones(n, seg):
    i = np.arange(n)
    return jnp.asarray((i[:, None] // seg == i[None, :] // seg).astype(np.float32))


def _proj_prep_kernel(shift_ref, x_ref, nw_ref, wa_ref, wb_ref, wc_ref, wm_ref, *rest, tm):
    proj_sc = rest[-1]
    xn = _rms(x_ref[...], nw_ref[...]).astype(BF16)
    c0 = 0
    for w_ref in (wa_ref, wb_ref, wc_ref, wm_ref):
        n = w_ref.shape[1]
        proj_sc[:, c0:c0 + n] = jnp.dot(xn, w_ref[...], preferred_element_type=F32)
        c0 += n
    _prep_kernel(shift_ref, proj_sc, *rest[:-1], tm=tm)


def _prep(shifts, x, nw, ws_bf16, tabs, fb, wfq, wfk, wnq, wnk, wdq, wdk, tm=256):
    s, d = x.shape
    c64, s64, c32, s32 = tabs
    b64 = _block_diag_ones(256, 64)
    b32 = _block_diag_ones(256, 32)

    def tiled(w, width):
        return jnp.tile(w.reshape(1, -1), (1, width // w.shape[-1]))

    consts = [fb, tiled(wfq, 256), tiled(wfk, 256), tiled(wnq, 512), tiled(wnk, 128), tiled(wdq, 256),
              tiled(wdk, 256), b64, b32]
    outs = [(FOX_HEADS, F32), (FOX_HEADS, BF16), (FOX_HEADS, BF16),
            (NSA_HEADS, F32), (NSA_HEADS, F32), (2, BF16),
            (NSA_KV, BF16), (NSA_KV, BF16), (NSA_KV, BF16), (NSA_KV, BF16), (1, F32),
            (2 * DIFF_HEADS, F32), (DIFF_HEADS, BF16), (DIFF_HEADS, BF16)]
    row = lambda w: pl.BlockSpec((tm, w), lambda i: (i, 0))
    assert sum(w.shape[1] for w in ws_bf16) == PROJ_W
    return pl.pallas_call(
        functools.partial(_proj_prep_kernel, tm=tm),
        grid=(s // tm,),
        in_specs=[pl.BlockSpec(memory_space=pltpu.SMEM), row(d), _full((1, d))] + [_full(w.shape) for w in ws_bf16]
                 + [row(LANES)] * 4 + [_full(c.shape) for c in consts],
        out_specs=[row(n * LANES) for n, _ in outs],
        out_shape=[jax.ShapeDtypeStruct((s, n * LANES), dt) for n, dt in outs],
        scratch_shapes=[pltpu.VMEM((8, LANES), F32), pltpu.VMEM((tm, PROJ_W), F32)],
        compiler_params=_cparams(("arbitrary",)),
        name="proj_head_prep",
    )(shifts, x, nw.reshape(1, d), *ws_bf16, c64, s64, c32, s32, *consts)


QK_DIM = 128
COL_SHIFT = 64
COL_SEL = 80
SCORE_BOUND = 60.0
SHIFT_HEADROOM = 60.0


def _flash_kernel(code_ref, q_ref, k_ref, v_ref, *rest, groups, tq, tk, window, has_sel, bounded):
    rest = list(rest)
    sel_ref = rest.pop(0) if has_sel else None
    o_ref = rest.pop(0)
    qT_sc = rest.pop(0)
    m_sc = None if bounded else rest.pop(0)
    acc_sc = rest.pop(0)
    code = code_ref[pl.program_id(1)]
    i = code & 0xFFF
    j = (code >> 12) & 0xFFF
    first = (code >> 24) & 1
    last = (code >> 25) & 1
    rows = groups * tq

    @pl.when(first == 1)
    def _():
        for g in range(groups):
            qT_sc[:, g * tq:(g + 1) * tq] = q_ref[:, g * QK_DIM:(g + 1) * QK_DIM].T.astype(BF16)
        if not bounded:
            m_sc[...] = jnp.full_like(m_sc, NEG_INIT)
        acc_sc[...] = jnp.zeros_like(acc_sc)

    def step(masked):
        if has_sel:
            qT_sc[COL_SEL:COL_SEL + tk // SLC_LEN, :] = _tile_lanes_any(sel_ref[0], groups)
        s = jnp.dot(k_ref[...], qT_sc[...], preferred_element_type=F32)
        if masked:
            kpos = j * tk + lax.broadcasted_iota(jnp.int32, (tk, rows), 0)
            lane = lax.broadcasted_iota(jnp.int32, (tk, rows), 1)
            qpos = i * tq + (lane & (tq - 1))
            keep = kpos <= qpos
            if window is not None:
                keep = keep & (kpos > qpos - window)
            s = jnp.where(keep, s, MASKVAL)
        v_t = (((0,), (0,)), ((), ()))
        v = v_ref[...]
        if bounded:
            acc_sc[...] += lax.dot_general(v, jnp.exp2(s).astype(BF16), v_t, preferred_element_type=F32)
        else:
            m_prev = m_sc[...]
            m_new = jnp.maximum(m_prev, jnp.max(s, axis=0, keepdims=True))
            p = jnp.exp2(s - m_new).astype(BF16)
            acc_sc[...] = (acc_sc[...] * jnp.exp2(m_prev - m_new)
                           + lax.dot_general(v, p, v_t, preferred_element_type=F32))
            m_sc[...] = m_new

    if window is not None:
        step(True)
    else:
        needs_mask = (j + 1) * tk - 1 > i * tq

        @pl.when(needs_mask)
        def _():
            step(True)

        @pl.when(jnp.logical_not(needs_mask))
        def _():
            step(False)

    @pl.when(last == 1)
    def _():
        l = acc_sc[COL_SHIFT:COL_SHIFT + 1, :]
        o = acc_sc[...] / jnp.where(l > 0.0, l, 1.0)
        for g in range(groups):
            o_ref[:, g * QK_DIM:(g + 1) * QK_DIM] = o[:, g * tq:(g + 1) * tq].T


def _tile_lanes_any(t, reps):
    return t if reps == 1 else jnp.concatenate([t] * reps, axis=1)


def _flash_call(q, k, v, sel, *, groups, tq, tk, window, bounded):
    s_len = q.shape[0]
    hkv = k.shape[1] // QK_DIM
    nq = s_len // tq
    rows = groups * tq
    codes = []
    for i in range(nq):
        q_lo, q_hi = i * tq, i * tq + tq - 1
        j_hi = q_hi // tk
        j_lo = 0 if window is None else max(0, (q_lo - window + 1) // tk)
        for j in range(j_lo, j_hi + 1):
            codes.append(i | (j << 12) | (int(j == j_lo) << 24) | (int(j == j_hi) << 25))
    codes = jnp.asarray(np.asarray(codes, dtype=np.int32))
    nsteps = codes.shape[0]

    def ti(c, s):
        return c[s] & 0xFFF

    def tj(c, s):
        return (c[s] >> 12) & 0xFFF

    in_specs = [
        pl.BlockSpec((tq, groups * QK_DIM), lambda h, s, c: (ti(c, s), h)),
        pl.BlockSpec((tk, QK_DIM), lambda h, s, c: (tj(c, s), h)),
        pl.BlockSpec((tk, QK_DIM), lambda h, s, c: (tj(c, s), h)),
    ]
    args = [q, k, v]
    if sel is not None:
        in_specs.append(pl.BlockSpec((1, tk // SLC_LEN, tq), lambda h, s, c: (h, tj(c, s), ti(c, s))))
        args.append(sel)
    scratch = [pltpu.VMEM((QK_DIM, rows), BF16)]
    if not bounded:
        scratch.append(pltpu.VMEM((1, rows), F32))
    scratch.append(pltpu.VMEM((QK_DIM, rows), F32))
    kern = functools.partial(_flash_kernel, groups=groups, tq=tq, tk=tk, window=window, has_sel=sel is not None,
                             bounded=bounded)
    return pl.pallas_call(
        kern,
        grid_spec=pltpu.PrefetchScalarGridSpec(
            num_scalar_prefetch=1,
            grid=(hkv, nsteps),
            in_specs=in_specs,
            out_specs=pl.BlockSpec((tq, groups * QK_DIM), lambda h, s, c: (ti(c, s), h)),
            scratch_shapes=scratch,
        ),
        out_shape=jax.ShapeDtypeStruct(q.shape, F32),
        compiler_params=_cparams(("parallel", "arbitrary")),
        name="flash_bounded" if bounded else "flash_online",
    )(codes, *args)


def _flash(q, k, v, sel=None, *, bound_ok, groups, tq, tk, window=None):
    call = functools.partial(_flash_call, q, k, v, sel, groups=groups, tq=tq, tk=tk, window=window)
    return lax.cond(bound_ok, lambda: call(bounded=True), lambda: call(bounded=False))


def _score_bound(wq, wk, seg, scale):
    return 1.02 * seg * jnp.max(jnp.abs(wq)) * jnp.max(jnp.abs(wk)) * scale * LOG2E


def _compress_kernel(r_ref, wa_ref, wb_ref, pa_ref, pb_ref, nw_ref, b64_ref, kx_o, vx_o):
    n = r_ref.shape[0]
    r = r_ref[...]
    wa, wb = wa_ref[...], wb_ref[...]
    first = jnp.dot(r, wa.astype(BF16), preferred_element_type=F32)
    second = jnp.dot(r, wb.astype(BF16), preferred_element_type=F32)
    nxt = pltpu.roll(second, n - 1, 0)
    rowi = lax.broadcasted_iota(jnp.int32, nxt.shape, 0)
    nxt = jnp.where(rowi == n - 1, 0.0, nxt)
    const = (jnp.dot(pa_ref[...], wa, precision=HIGHEST, preferred_element_type=F32)
             + jnp.dot(pb_ref[...], wb, precision=HIGHEST, preferred_element_type=F32))
    out = first + nxt + const
    k, v = out[:, 0:LANES], out[:, LANES:2 * LANES]
    ss = jnp.dot(k * k, b64_ref[...], precision=HIGHEST, preferred_element_type=F32)
    kn = k * lax.rsqrt(ss * (1.0 / NSA_HD) + EPS) * nw_ref[...]
    one_ex = _lane_fill(n, {COL_SHIFT: 1.0})
    kx_o[...] = _slots(kn, [one_ex] * NSA_KV).astype(BF16)
    vx_o[...] = _slots(v, [one_ex] * NSA_KV).astype(BF16)


def _compress(kvc, cmp_w, cmp_pos, nw):
    s = kvc.shape[0]
    n = s // CMP_STRIDE
    width = CMP_STRIDE * 2 * NSA_KV * NSA_HD
    w = cmp_w.reshape(2, 2, CMP_STRIDE, NSA_HD, NSA_HD)
    eye = jnp.eye(2, dtype=w.dtype)
    big = jnp.einsum("khldD,kK,gG->hlkgdKGD", w, eye, eye).reshape(2, width, 2 * NSA_KV * NSA_HD)
    pos = cmp_pos.reshape(2, 2, CMP_STRIDE, 1, NSA_HD)
    pos = jnp.broadcast_to(pos, (2, 2, CMP_STRIDE, NSA_KV, NSA_HD)).transpose(1, 2, 0, 3, 4).reshape(2, 1, width)
    b64 = _block_diag_ones(LANES, NSA_HD)
    nwt = jnp.tile(nw.reshape(1, -1), (1, LANES // NSA_HD))
    out = jax.ShapeDtypeStruct((n, NSA_KV * LANES), BF16)
    return pl.pallas_call(
        _compress_kernel,
        grid=(1,),
        in_specs=[_full((n, width)), _full(big[0].shape), _full(big[1].shape), _full((1, width)), _full((1, width)),
                  _full((1, LANES)), _full((LANES, LANES))],
        out_specs=[_full((n, NSA_KV * LANES))] * 2,
        out_shape=[out, out],
        compiler_params=_cparams(("arbitrary",)),
        name="nsa_compress",
    )(kvc.reshape(n, width), big[0], big[1], pos[0], pos[1], nwt, b64)


def _nsa_cmp_kernel(q_ref, kc_ref, vc_ref, ov_ref, o_ref, sel_ref, *, tq, groups, bounded):
    i = pl.program_id(1)
    ncmp = kc_ref.shape[0]
    rows = groups * tq
    qT = jnp.concatenate([q_ref[:, g * QK_DIM:(g + 1) * QK_DIM].T for g in range(groups)], axis=1).astype(BF16)
    s = jnp.dot(kc_ref[...], qT, preferred_element_type=F32)
    n_idx = lax.broadcasted_iota(jnp.int32, (ncmp, rows), 0)
    lane = lax.broadcasted_iota(jnp.int32, (ncmp, rows), 1)
    qpos = i * tq + (lane & (tq - 1))
    keep = n_idx * CMP_STRIDE + (CMP_LEN - 1) <= qpos
    if bounded:
        e = jnp.where(keep, jnp.exp2(s), 0.0)
    else:
        sm = jnp.where(keep, s, NEG_INIT)
        e = jnp.where(keep, jnp.exp2(sm - jnp.max(sm, axis=0, keepdims=True)), 0.0)
    den = jnp.sum(e, axis=0, keepdims=True)
    p = e / jnp.where(den > 0.0, den, 1.0)
    o = lax.dot_general(vc_ref[...], p.astype(BF16), (((0,), (0,)), ((), ())), preferred_element_type=F32)
    for g in range(groups):
        o_ref[:, g * QK_DIM:(g + 1) * QK_DIM] = o[:, g * tq:(g + 1) * tq].T

    psum = p[:, 0:tq]
    for g in range(1, groups):
        psum = psum + p[:, g * tq:(g + 1) * tq]
    imp = jnp.dot(ov_ref[...], psum, precision=HIGHEST, preferred_element_type=F32)
    ns = imp.shape[0]
    blk = lax.broadcasted_iota(jnp.int32, (ns, tq), 0)
    qp = i * tq + lax.broadcasted_iota(jnp.int32, (ns, tq), 1)
    cur = qp // SLC_LEN
    valid = blk <= cur
    forced = (blk == 0) | (blk == cur) | (blk == cur - 1)
    work = jnp.where(valid, imp + jnp.where(forced, FORCE_BONUS, 0.0), NEG_INIT)
    picked = jnp.zeros((ns, tq), dtype=jnp.bool_)
    for _ in range(SLC_TOPK):
        mx = jnp.max(work, axis=0, keepdims=True)
        idx = jnp.min(jnp.where(work == mx, blk, ns), axis=0, keepdims=True)
        pick = (blk == idx) & (mx > 0.5 * NEG_INIT)
        picked = picked | pick
        work = jnp.where(pick, NEG_INIT, work)
    sel_ref[0] = jnp.where(picked, 0.0, MASKVAL).astype(BF16)


def _nsa_cmp(qn, kcx, vcx, overlapT, tq, bound_ok):
    s = qn.shape[0]
    ncmp = kcx.shape[0]
    ns = overlapT.shape[0]
    groups = NSA_G

    def call(bounded):
        return pl.pallas_call(
            functools.partial(_nsa_cmp_kernel, tq=tq, groups=groups, bounded=bounded),
            grid=(NSA_KV, s // tq),
            in_specs=[pl.BlockSpec((tq, groups * QK_DIM), lambda h, i: (i, h)),
                      pl.BlockSpec((ncmp, QK_DIM), lambda h, i: (0, h)),
                      pl.BlockSpec((ncmp, QK_DIM), lambda h, i: (0, h)),
                      _full(overlapT.shape)],
            out_specs=[pl.BlockSpec((tq, groups * QK_DIM), lambda h, i: (i, h)),
                       pl.BlockSpec((1, ns, tq), lambda h, i: (h, 0, i))],
            out_shape=[jax.ShapeDtypeStruct(qn.shape, F32), jax.ShapeDtypeStruct((NSA_KV, ns, s), BF16)],
            compiler_params=_cparams(("parallel", "parallel")),
            name="nsa_cmp_bounded" if bounded else "nsa_cmp_max",
        )(qn, kcx, vcx, overlapT)

    return lax.cond(bound_ok, lambda: call(True), lambda: call(False))


def _mix_out_kernel(lam_ref, h_ref, fox_ref, oc_ref, os_ref, ow_ref, dd_ref, g_ref, sw_ref, wf_ref, wn_ref, wd_ref,
                    o_ref, *, diff_scale):
    lam = lam_ref[0]
    gates = g_ref[...]
    tm = gates.shape[0]
    lane = lax.broadcasted_iota(jnp.int32, (tm, LANES), 1)
    slot = lambda ref, n: ref[:, n * LANES:(n + 1) * LANES]
    nsa = []
    for h in range(NSA_HEADS):
        c0 = FOX_HEADS + 3 * h
        nsa.append(gates[:, c0:c0 + 1] * slot(oc_ref, h) + gates[:, c0 + 1:c0 + 2] * slot(os_ref, h)
                   + gates[:, c0 + 2:c0 + 3] * slot(ow_ref, h))
    diff = []
    for h in range(DIFF_HEADS):
        a = jnp.where(lane < DIFF_V, slot(dd_ref, 2 * h) - lam * slot(dd_ref, 2 * h + 1), 0.0)
        ss = jnp.sum(a * a, axis=-1, keepdims=True)
        diff.append(a * lax.rsqrt(ss * (1.0 / DIFF_V) + EPS) * sw_ref[...] * diff_scale)
    acc = h_ref[...]
    acc = acc + jnp.dot(fox_ref[...].astype(BF16), wf_ref[...], preferred_element_type=F32)
    acc = acc + jnp.dot(jnp.concatenate(nsa, axis=1).astype(BF16), wn_ref[...], preferred_element_type=F32)
    acc = acc + jnp.dot(jnp.concatenate(diff, axis=1).astype(BF16), wd_ref[...], preferred_element_type=F32)
    o_ref[...] = acc


def _slot_rows(w, heads):
    d = w.shape[1]
    w = w.reshape(heads, -1, d)
    return jnp.concatenate([w, jnp.zeros((heads, LANES - w.shape[1], d), w.dtype)], axis=1).reshape(heads * LANES, d)


def _mix_out(lam, h, fox, oc, os_, ow, dd, gates, subln_w, w_out, diff_scale, tm=512):
    s, d = h.shape
    nf, nn = FOX_HEADS * FOX_HD, NSA_HEADS * NSA_HD
    wf = _slot_rows(w_out[:nf], FOX_HEADS).astype(BF16)
    wn = _slot_rows(w_out[nf:nf + nn], NSA_HEADS).astype(BF16)
    wd = _slot_rows(w_out[nf + nn:], DIFF_HEADS).astype(BF16)
    sw = jnp.concatenate([subln_w.reshape(1, -1), jnp.zeros((1, LANES - DIFF_V), F32)], axis=1)
    row = lambda w: pl.BlockSpec((tm, w), lambda i: (i, 0))
    return pl.pallas_call(
        functools.partial(_mix_out_kernel, diff_scale=diff_scale),
        grid=(s // tm,),
        in_specs=[pl.BlockSpec(memory_space=pltpu.SMEM), row(d), row(fox.shape[1]), row(oc.shape[1]),
                  row(os_.shape[1]), row(ow.shape[1]), row(dd.shape[1]), row(LANES), _full((1, LANES)),
                  _full(wf.shape), _full(wn.shape), _full(wd.shape)],
        out_specs=row(d),
        out_shape=jax.ShapeDtypeStruct((s, d), F32),
        compiler_params=_cparams(("parallel",)),
        name="mix_out",
    )(lam, h, fox, oc, os_, ow, dd, gates, sw, wf, wn, wd)


def _peer_query_kernel(h_ref, nw_ref, wq_ref, keys_ref, xnT_o, scT_o):
    xn = _rms(h_ref[...], nw_ref[...])
    xnT_o[...] = xn.T.astype(BF16)
    q = jnp.dot(xn.astype(BF16), wq_ref[...], preferred_element_type=F32).astype(BF16)
    half = PEER_DQ // 2
    for b in range(2 * PEER_HEADS):
        sc = jnp.dot(q[:, b * half:(b + 1) * half], keys_ref[b], preferred_element_type=F32)
        scT_o[b] = sc.T


def _peer_query(h, nw, wq_bf16, keysT_bf16, tm=512):
    s, d = h.shape
    n = wq_bf16.shape[1]
    nb = 2 * PEER_HEADS
    return pl.pallas_call(
        _peer_query_kernel,
        grid=(s // tm,),
        in_specs=[pl.BlockSpec((tm, d), lambda i: (i, 0)), _full((1, d)), _full((d, n)), _full(keysT_bf16.shape)],
        out_specs=[pl.BlockSpec((d, tm), lambda i: (0, i)), pl.BlockSpec((nb, PEER_NKEYS, tm), lambda i: (0, 0, i))],
        out_shape=[jax.ShapeDtypeStruct((d, s), BF16), jax.ShapeDtypeStruct((nb, PEER_NKEYS, s), F32)],
        compiler_params=_cparams(("parallel",)),
        name="peer_query",
    )(h, nw.reshape(1, d), wq_bf16, keysT_bf16)


def _top16_rows(s, exact_ties):
    n = s.shape[0]
    row = lax.broadcasted_iota(jnp.int32, s.shape, 0)
    rank = jnp.full(s.shape, float(PEER_TOPK), dtype=F32)
    work = s
    vals = []
    for r in range(PEER_TOPK):
        mx = jnp.max(work, axis=0, keepdims=True)
        if exact_ties:
            pick = row == jnp.min(jnp.where(work == mx, row, n), axis=0, keepdims=True)
        else:
            pick = work == mx
        rank = jnp.where(pick, float(r), rank)
        work = jnp.where(pick, NEG_INIT, work)
        vals.append(mx)
    return rank, vals


def _peer_route(s1, s2, exact_ties):
    rank1, v1 = _top16_rows(s1, exact_ties)
    rank2, v2 = _top16_rows(s2, exact_ties)
    k = PEER_TOPK
    tm = s1.shape[1]
    r16 = lax.broadcasted_iota(jnp.int32, (k, tm), 0)
    v1m = jnp.zeros((k, tm), F32)
    v2m = jnp.zeros((k, tm), F32)
    for r in range(k):
        v1m = jnp.where(r16 == r, v1[r], v1m)
        v2m = jnp.where(r16 == r, v2[r], v2m)
    ea = jnp.exp(v1m - v1[0])
    eb = jnp.exp(v2m - v2[0])
    cands, gates = [v1m[0:1] + v2m], [ea[0:1] * eb]
    for r in range(1, 8):
        cands.append(v1m[r:r + 1] + v2m[0:8])
        gates.append(ea[r:r + 1] * eb[0:8])
    cands.append(v1m[8:16] + v2m[0:1])
    gates.append(ea[8:16] * eb[0:1])
    cand = jnp.concatenate(cands, axis=0)
    gate = jnp.concatenate(gates, axis=0)
    ncand = cand.shape[0]
    crow = lax.broadcasted_iota(jnp.int32, cand.shape, 0)
    for r in range(2, 8):
        start = 16 + 8 * (r - 1)
        cand = jnp.where((crow >= start + k // (r + 1)) & (crow < start + 8), NEG_INIT, cand)
    work = cand
    picked = jnp.zeros(cand.shape, dtype=jnp.bool_)
    for _ in range(k):
        mx = jnp.max(work, axis=0, keepdims=True)
        if exact_ties:
            pick = crow == jnp.min(jnp.where(work == mx, crow, ncand), axis=0, keepdims=True)
        else:
            pick = work == mx
        picked = picked | pick
        work = jnp.where(pick, NEG_INIT, work)
    pf = picked.astype(F32)
    z = jnp.sum(pf * gate, axis=0, keepdims=True)
    cnt = [jnp.sum(pf[0:16, :], axis=0, keepdims=True)]
    cnt += [jnp.sum(pf[8 + 8 * r:16 + 8 * r, :], axis=0, keepdims=True) for r in range(1, 8)]
    cnt += [pf[64 + r:65 + r, :] for r in range(8, 16)]
    cmap = jnp.zeros(s1.shape, dtype=F32)
    for r in range(k):
        cmap = jnp.where(rank1 == float(r), cnt[r], cmap)
    in1 = rank1 < float(k)
    in2 = rank2 < float(k)
    a = jnp.where(in1, jnp.exp(s1 - v1[0]), 0.0) / z
    b = jnp.where(in2, jnp.exp(s2 - v2[0]), 0.0)
    excess = (jnp.abs(jnp.sum(in1.astype(F32), axis=0, keepdims=True) - k)
              + jnp.abs(jnp.sum(in2.astype(F32), axis=0, keepdims=True) - k)
              + jnp.abs(jnp.sum(pf, axis=0, keepdims=True) - k))
    return a, cmap, b, rank2, excess


def _peer_topk_kernel(sc_ref, a_o, c_o, b_o, r_o):
    def run(exact_ties):
        a, c, b, rank2, excess = _peer_route(sc_ref[0], sc_ref[1], exact_ties)
        a_o[0] = a
        c_o[0] = c
        b_o[0] = b.astype(BF16)
        r_o[0] = rank2.astype(BF16)
        return excess

    excess = run(False)

    @pl.when(jnp.max(excess) > 0.0)
    def _():
        run(True)


def _peer_topk(scT, tm=256):
    n2, nk, s = scT.shape
    heads = n2 // 2
    ospec = pl.BlockSpec((1, nk, tm), lambda h, t: (h, 0, t))
    return pl.pallas_call(
        _peer_topk_kernel,
        grid=(heads, s // tm),
        in_specs=[pl.BlockSpec((2, nk, tm), lambda h, t: (h, 0, t))],
        out_specs=[ospec] * 4,
        out_shape=[jax.ShapeDtypeStruct((heads, nk, s), dt) for dt in (F32, F32, BF16, BF16)],
        compiler_params=_cparams(("parallel", "parallel")),
        name="peer_topk",
    )(scT)


def _gelu_exact(x):
    return 0.5 * x * (1.0 + lax.erf(x * (2.0 ** -0.5)))


def _peer_main_kernel(xT_ref, u_ref, v_ref, a_ref, c_ref, b_ref, r_ref, o_ref, w_sc, *, ec, sub):
    ci = pl.program_id(1)

    @pl.when(ci == 0)
    def _():
        o_ref[...] = jnp.zeros_like(o_ref)

    nk = PEER_NKEYS
    tm = xT_ref.shape[1]

    def row(ref, h, i1, lanes):
        x16 = jnp.broadcast_to(ref[h, pl.ds(i1, 1), :][:, lanes], (16, GATE_LANES)).astype(BF16)
        return jnp.concatenate([x16] * (nk // 16), axis=0)

    acc = None
    for sc in range(ec // sub):
        hid = jnp.dot(u_ref[sc * sub:(sc + 1) * sub, :], xT_ref[...], preferred_element_type=F32)
        act = _gelu_exact(hid).astype(BF16)
        for ii in range(sub // nk):
            slab = sc * (sub // nk) + ii
            i1 = ci * (ec // nk) + slab
            for l0 in range(0, tm, GATE_LANES):
                lanes = slice(l0, l0 + GATE_LANES)
                w = None
                for h in range(PEER_HEADS):
                    keep = r_ref[h, :, lanes] < row(c_ref, h, i1, lanes)
                    term = jnp.where(keep, b_ref[h, :, lanes], 0.0) * row(a_ref, h, i1, lanes)
                    w = term if w is None else w + term
                w_sc[slab * nk:(slab + 1) * nk, lanes] = w * act[ii * nk:(ii + 1) * nk, lanes]
        part = jnp.dot(w_sc[sc * sub:(sc + 1) * sub, :].T, v_ref[sc * sub:(sc + 1) * sub, :],
                       preferred_element_type=F32)
        acc = part if acc is None else acc + part
    o_ref[...] += acc


def _peer_main(xnT, u_bf16, v_bf16, a, c, b, r, tm=512, ec=2048, sub=512):
    d, s = xnT.shape
    e = u_bf16.shape[0]
    heads, nk, _ = a.shape
    rt = pl.BlockSpec((heads, nk, tm), lambda t, ci: (0, 0, t))
    wt = pl.BlockSpec((ec, d), lambda t, ci: (ci, 0))
    return pl.pallas_call(
        functools.partial(_peer_main_kernel, ec=ec, sub=sub),
        grid=(s // tm, e // ec),
        in_specs=[pl.BlockSpec((d, tm), lambda t, ci: (0, t)), wt, wt, rt, rt, rt, rt],
        out_specs=pl.BlockSpec((tm, d), lambda t, ci: (t, 0)),
        out_shape=jax.ShapeDtypeStruct((s, d), F32),
        scratch_shapes=[pltpu.VMEM((ec, tm), BF16)],
        compiler_params=_cparams(("parallel", "arbitrary")),
        name="peer_main",
    )(xnT, u_bf16, v_bf16, a, c, b, r)


def _ple_kernel(h_ref, peer_ref, p_ref, nw_ref, wg_ref, wp_ref, o_ref):
    h2 = h_ref[...] + peer_ref[...]
    xn = _rms(h2, nw_ref[...]).astype(BF16)
    gate = 1.0 / (1.0 + jnp.exp(-jnp.dot(xn, wg_ref[...], preferred_element_type=F32)))
    emb = jnp.dot(p_ref[...].astype(BF16), wp_ref[...], preferred_element_type=F32)
    o_ref[...] = h2 + gate * emb


def _ple(h, peer, p, nw, wg_bf16, wp_bf16, tm=512):
    s, d = h.shape
    pd = p.shape[1]
    row = lambda w: pl.BlockSpec((tm, w), lambda i: (i, 0))
    return pl.pallas_call(
        _ple_kernel,
        grid=(s // tm,),
        in_specs=[row(d), row(d), row(pd), _full((1, d)), _full((d, d)), _full((pd, d))],
        out_specs=row(d),
        out_shape=jax.ShapeDtypeStruct((s, d), F32),
        compiler_params=_cparams(("parallel",)),
        name="ple",
    )(h, peer, p, nw.reshape(1, d), wg_bf16, wp_bf16)


def _split_w_in(w):
    f0 = 3 * FOX_HEADS * FOX_HD
    n0 = f0 + FOX_HEADS
    g0 = n0 + _SEG["dq"][0] - _SEG["nq"][0]
    d0 = g0 + 3 * NSA_HEADS
    misc = jnp.concatenate([w[:, f0:n0], w[:, g0:d0], jnp.zeros((w.shape[0], LANES - (n0 - f0) - (d0 - g0)), w.dtype)],
                           axis=1)
    return [t.astype(BF16) for t in (w[:, :f0], w[:, n0:g0], w[:, d0:], misc)]


def _overlap_T(s):
    n = np.arange(s // CMP_STRIDE)[None, :] * CMP_STRIDE
    m = np.arange(s // SLC_LEN)[:, None] * SLC_LEN
    return jnp.asarray(((n < m + SLC_LEN) & (n + CMP_LEN > m)).astype(np.float32))


def _layer(h, p_i, tabs, layer, attn_norm_w, w_in, fox_f_bias, fox_q_norm_w, fox_k_norm_w, nsa_q_norm_w,
           nsa_k_norm_w, nsa_cmp_pos, nsa_cmp_w, diff_q_norm_w, diff_k_norm_w, diff_lambda, diff_subln_w,
           w_out, ffn_norm_w, peer_w_q, peer_sub_keys, peer_u, peer_v, ple_norm_w, ple_w_gate, ple_w_proj):
    s = h.shape[0]
    fb =jnp.zeros((1, LANES), F32).at[0, :FOX_HEADS].set(fox_f_bias)
    u_f = _score_bound(fox_q_norm_w, fox_k_norm_w, FOX_HD, FOX_HD ** -0.5)
    u_n = _score_bound(nsa_q_norm_w, nsa_k_norm_w, NSA_HD, NSA_HD ** -0.5)
    u_d = _score_bound(diff_q_norm_w, diff_k_norm_w, DIFF_QK, DIFF_QK ** -0.5)
    shifts = SHIFT_HEADROOM - jnp.stack([u_f, u_n, u_d]).astype(F32)
    (fq, fk, fv, nqn, nqr, kvc, ks, vs, kw, vw, gates, dq, dk, dv) = _prep(
        shifts, h, attn_norm_w, _split_w_in(w_in), tabs, fb, fox_q_norm_w, fox_k_norm_w, nsa_q_norm_w, nsa_k_norm_w, diff_q_norm_w, diff_k_norm_w)

    o_fox = _flash(fq, fk, fv, bound_ok=u_f <= SCORE_BOUND, groups=1, tq=Q_LANES, tk=TK_CAUSAL)

    tq_n = Q_LANES // NSA_G
    nsa_ok = u_n <= SCORE_BOUND
    kcx, vcx = _compress(kvc, nsa_cmp_w, nsa_cmp_pos, nsa_k_norm_w)
    o_c, sel = _nsa_cmp(nqn, kcx, vcx, _overlap_T(s), tq_n, nsa_ok)
    o_s = _flash(nqr, ks, vs, sel, bound_ok=nsa_ok, groups=NSA_G, tq=tq_n, tk=TK_CAUSAL)
    o_w = _flash(nqr, kw, vw, bound_ok=nsa_ok, groups=NSA_G, tq=tq_n, tk=512, window=WIN)

    o_d = _flash(dq, dk, dv, bound_ok=u_d <= SCORE_BOUND, groups=2, tq=Q_LANES // 2, tk=TK_CAUSAL)
    lv = diff_lambda.astype(F32)
    lam_init = 0.8 - 0.6 * math.exp(-0.3 * layer)
    lam = (jnp.exp(jnp.sum(lv[0] * lv[1])) - jnp.exp(jnp.sum(lv[2] * lv[3])) + lam_init).reshape(1)
    h1 = _mix_out(lam, h, o_fox, o_c, o_s, o_w, o_d, gates, diff_subln_w, w_out, 1.0 - lam_init)

    keysT = peer_sub_keys.reshape(2 * PEER_HEADS, PEER_NKEYS, PEER_DQ // 2).transpose(0, 2, 1).astype(BF16)
    xnT, scT = _peer_query(h1, ffn_norm_w, peer_w_q.astype(BF16), keysT)
    a, c, b, r = _peer_topk(scT)
    peer = _peer_main(xnT, peer_u.astype(BF16), peer_v.astype(BF16), a, c, b, r)

    return _ple(h1, peer, p_i, ple_norm_w, ple_w_gate.astype(BF16), ple_w_proj.astype(BF16))


def kernel(x, p, positions, attn_norm_w, w_in, fox_f_bias, fox_q_norm_w, fox_k_norm_w, nsa_q_norm_w, nsa_k_norm_w,
           nsa_cmp_pos, nsa_cmp_w, diff_q_norm_w, diff_k_norm_w, diff_lambda, diff_subln_w, w_out, ffn_norm_w,
           peer_w_q, peer_sub_keys, peer_u, peer_v, ple_norm_w, ple_w_gate, ple_w_proj):
    b, s, d = x.shape
    assert b == 1 and d == D_MODEL and s % Q_LANES == 0 and s % TK_CAUSAL == 0
    tabs = _rope_tables(positions)
    h = x.reshape(s, d)
    per_layer = (attn_norm_w, w_in, fox_f_bias, fox_q_norm_w, fox_k_norm_w, nsa_q_norm_w, nsa_k_norm_w, nsa_cmp_pos,
                 nsa_cmp_w, diff_q_norm_w, diff_k_norm_w, diff_lambda, diff_subln_w, w_out, ffn_norm_w, peer_w_q,
                 peer_sub_keys, peer_u, peer_v, ple_norm_w, ple_w_gate, ple_w_proj)
    for layer in range(attn_norm_w.shape[0]):
        h = _layer(h, p[layer, 0], tabs, layer, *(w[layer] for w in per_layer))
    return h.reshape(b, s, d)
```

```python
import functools
import math

import numpy as np
import jax
import jax.numpy as jnp
from jax import lax
from jax.experimental import pallas as pl
from jax.experimental.pallas import tpu as pltpu

F32 = jnp.float32
BF16 = jnp.bfloat16
HIGHEST = lax.Precision.HIGHEST

D_MODEL = 1024
PLE_DIM = 256
ROPE_THETA = 10000.0
EPS = 1e-6
FOX_HEADS, FOX_HD = 4, 64
NSA_HEADS, NSA_KV, NSA_HD = 8, 2, 64
NSA_G = NSA_HEADS // NSA_KV
CMP_LEN, CMP_STRIDE, SLC_LEN, SLC_TOPK, WIN = 32, 16, 64, 16, 512
FORCE_BONUS = 1.0e4
DIFF_HEADS, DIFF_QK, DIFF_V = 4, 32, 64
PEER_HEADS, PEER_NKEYS, PEER_DQ, PEER_TOPK = 8, 128, 256, 16
PEER_EXPERTS = PEER_NKEYS * PEER_NKEYS

LANES = 128
LOG2E = math.log2(math.e)
NEG_INIT = -1.0e30
MASKVAL = -2.0e30
VMEM_LIMIT = 56 * 1024 * 1024
TK_CAUSAL = 2048
Q_LANES = 2048
GATE_LANES = 512

_SEG = dict(fq=(0, 256), fk=(256, 256), fv=(512, 256), nq=(768, 512), nkc=(1280, 128), nvc=(1408, 128),
            nks=(1536, 128), nvs=(1664, 128), nkw=(1792, 128), nvw=(1920, 128),
            dq=(2048, 256), dk=(2304, 256), dv=(2560, 256), misc=(2816, 128))
PROJ_W = 2944


def _cparams(sem):
    return pltpu.CompilerParams(dimension_semantics=sem, vmem_limit_bytes=VMEM_LIMIT)


def _full(shape):
    n = len(shape)
    return pl.BlockSpec(shape, lambda *_: (0,) * n)


def _rope_tab_kernel(pos_ref, f64_ref, g64_ref, f32_ref, g32_ref, c64_o, s64_o, c32_o, s32_o):
    pos = pos_ref[...].astype(F32)
    a64 = pos * f64_ref[...]
    c64_o[...] = jnp.cos(a64)
    s64_o[...] = jnp.sin(a64) * g64_ref[...]
    a32 = pos * f32_ref[...]
    c32_o[...] = jnp.cos(a32)
    s32_o[...] = jnp.sin(a32) * g32_ref[...]


def _rope_tables(positions):
    s = positions.shape[-1]
    pos = positions.reshape(s, 1)
    lane = np.arange(LANES)

    def lanes(half):
        inv = ROPE_THETA ** (-jnp.arange(half, dtype=F32) / half)
        freq = inv[(lane % (2 * half)) % half].reshape(1, LANES)
        sign = np.where((lane % (2 * half)) < half, -1.0, 1.0).astype(np.float32).reshape(1, LANES)
        return freq, jnp.asarray(sign)

    f64, g64 = lanes(NSA_HD // 2)
    f32_, g32 = lanes(DIFF_QK // 2)
    tm = 512
    out = jax.ShapeDtypeStruct((s, LANES), F32)
    row = pl.BlockSpec((tm, LANES), lambda i: (i, 0))
    return pl.pallas_call(
        _rope_tab_kernel,
        grid=(s // tm,),
        in_specs=[pl.BlockSpec((tm, 1), lambda i: (i, 0))] + [_full((1, LANES))] * 4,
        out_specs=[row] * 4,
        out_shape=[out] * 4,
        compiler_params=_cparams(("parallel",)),
        name="rope_tables",
    )(pos, f64, g64, f32_, g32)


def _rms(x, w):
    return x * lax.rsqrt(jnp.mean(x * x, axis=-1, keepdims=True) + EPS) * w


def _seg_rms(x, bmat, seg):
    outs = []
    for c0 in range(0, x.shape[1], 256):
        w = min(256, x.shape[1] - c0)
        xc = x[:, c0:c0 + w]
        ss = jnp.dot(xc * xc, bmat[:w, :w], precision=HIGHEST, preferred_element_type=F32)
        outs.append(xc * lax.rsqrt(ss * (1.0 / seg) + EPS))
    return outs[0] if len(outs) == 1 else jnp.concatenate(outs, axis=1)


def _tile_lanes(t, width):
    reps = width // LANES
    return t if reps == 1 else jnp.concatenate([t] * reps, axis=1)


def _rope(x, cos, sin, half):
    width = x.shape[1]
    left = pltpu.roll(x, width - half, 1)
    right = pltpu.roll(x, half, 1)
    lane = lax.broadcasted_iota(jnp.int32, x.shape, 1)
    swapped = jnp.where((lane & (2 * half - 1)) < half, left, right)
    return x * _tile_lanes(cos, width) + swapped * _tile_lanes(sin, width)


def _lane_fill(tm, cols):
    lane = lax.broadcasted_iota(jnp.int32, (tm, LANES), 1)
    out = jnp.zeros((tm, LANES), F32)
    for l, v in cols.items():
        out = jnp.where(lane == l, v, out)
    return out


def _slots(x, extras):
    lane = lax.broadcasted_iota(jnp.int32, (x.shape[0], LANES), 1)
    outs = []
    for h, ex in enumerate(extras):
        col = x[:, (h // 2) * LANES:(h // 2 + 1) * LANES]
        if h % 2:
            col = pltpu.roll(col, LANES // 2, 1)
        outs.append(jnp.where(lane < LANES // 2, col, ex))
    return jnp.concatenate(outs, axis=1)


def _prep_kernel(shift_ref, proj_ref, c64_ref, s64_ref, c32_ref, s32_ref, fb_ref, wfq_ref, wfk_ref, wnq_ref,
                 wnk_ref, wdq_ref, wdk_ref, b64_ref, b32_ref,
                 fq_o, fk_o, fv_o, nqn_o, nqr_o, kvc_o, ks_o, vs_o, kw_o, vw_o, gate_o, dq_o, dk_o, dv_o,
                 carry_sc, *, tm):
    def seg(name):
        c0, w = _SEG[name]
        return proj_ref[:, c0:c0 + w]

    b64 = b64_ref[...]
    b32 = b32_ref[...]
    c64, s64, c32, s32 = c64_ref[...], s64_ref[...], c32_ref[...], s32_ref[...]
    one_ex = _lane_fill(tm, {COL_SHIFT: 1.0})

    misc = seg("misc")
    gate_o[...] = 1.0 / (1.0 + jnp.exp(-misc))
    t = misc + fb_ref[...]
    logf = jnp.minimum(t, 0.0) - jnp.log1p(jnp.exp(-jnp.abs(t)))

    @pl.when(pl.program_id(0) == 0)
    def _():
        carry_sc[...] = jnp.zeros_like(carry_sc)

    r = lax.broadcasted_iota(jnp.int32, (tm, tm), 0)
    c = lax.broadcasted_iota(jnp.int32, (tm, tm), 1)
    tri = (c <= r).astype(F32)
    csum = jnp.dot(tri, logf, precision=HIGHEST, preferred_element_type=F32) + carry_sc[0:1, :]
    carry_sc[...] = jnp.broadcast_to(csum[tm - 1:tm, :], carry_sc.shape)
    csum = csum * LOG2E
    hi = csum.astype(BF16).astype(F32)
    r1 = csum - hi
    mid = r1.astype(BF16).astype(F32)
    lo = (r1 - mid).astype(BF16).astype(F32)

    fq = _seg_rms(seg("fq"), b64, FOX_HD) * wfq_ref[...] * (FOX_HD ** -0.5 * LOG2E)
    fk = _seg_rms(seg("fk"), b64, FOX_HD) * wfk_ref[...]
    q_ex, k_ex = [], []
    for h in range(FOX_HEADS):
        ch, cm, cl = hi[:, h:h + 1], mid[:, h:h + 1], lo[:, h:h + 1]
        q_ex.append(_lane_fill(tm, {COL_SHIFT: shift_ref[0], 65: 1.0, 66: 1.0, 67: 1.0, 68: ch, 69: cm, 70: cl}))
        k_ex.append(_lane_fill(tm, {COL_SHIFT: 1.0, 65: -ch, 66: -cm, 67: -cl, 68: 1.0, 69: 1.0, 70: 1.0}))
    fq_o[...] = _slots(fq, q_ex)
    fk_o[...] = _slots(fk, k_ex).astype(BF16)
    fv_o[...] = _slots(seg("fv"), [one_ex] * FOX_HEADS).astype(BF16)

    n_ex = [_lane_fill(tm, {COL_SHIFT: shift_ref[1]})] * NSA_HEADS
    nqn = _seg_rms(seg("nq"), b64, NSA_HD) * wnq_ref[...]
    nqn_o[...] = _slots(nqn * (NSA_HD ** -0.5 * LOG2E), n_ex)
    nqr_o[...] = _slots(_rope(nqn, c64, s64, NSA_HD // 2) * (NSA_HD ** -0.5 * LOG2E), n_ex)
    kvc_o[:, 0:128] = seg("nkc").astype(BF16)
    kvc_o[:, 128:256] = seg("nvc").astype(BF16)
    wnk = wnk_ref[...]
    lane = lax.broadcasted_iota(jnp.int32, (tm, LANES), 1)
    row = pl.program_id(0) * tm + lax.broadcasted_iota(jnp.int32, (tm, LANES), 0)
    blk_in_tile = (row & (TK_CAUSAL - 1)) // SLC_LEN
    sel_ex = jnp.where(lane == COL_SEL + blk_in_tile, 1.0, one_ex)
    ks = _rope(_seg_rms(seg("nks"), b64, NSA_HD) * wnk, c64, s64, NSA_HD // 2)
    ks_o[...] = _slots(ks, [sel_ex] * NSA_KV).astype(BF16)
    vs_o[...] = _slots(seg("nvs"), [one_ex] * NSA_KV).astype(BF16)
    kw = _rope(_seg_rms(seg("nkw"), b64, NSA_HD) * wnk, c64, s64, NSA_HD // 2)
    kw_o[...] = _slots(kw, [one_ex] * NSA_KV).astype(BF16)
    vw_o[...] = _slots(seg("nvw"), [one_ex] * NSA_KV).astype(BF16)

    dq = _rope(_seg_rms(seg("dq"), b32, DIFF_QK) * wdq_ref[...], c32, s32, DIFF_QK // 2) * (DIFF_QK ** -0.5 * LOG2E)
    d_ex = _lane_fill(tm, {COL_SHIFT: shift_ref[2]})
    d_slots = []
    for h in range(DIFF_HEADS):
        col = dq[:, (h // 2) * LANES:(h // 2 + 1) * LANES]
        if h % 2:
            col = pltpu.roll(col, LANES // 2, 1)
        d_slots.append(jnp.where(lane < DIFF_QK, col, d_ex))
        d_slots.append(jnp.where((lane >= DIFF_QK) & (lane < 2 * DIFF_QK), col, d_ex))
    dq_o[...] = jnp.concatenate(d_slots, axis=1)
    dk = _rope(_seg_rms(seg("dk"), b32, DIFF_QK) * wdk_ref[...], c32, s32, DIFF_QK // 2)
    dk_o[...] = _slots(dk, [one_ex] * DIFF_HEADS).astype(BF16)
    dv_o[...] = _slots(seg("dv"), [one_ex] * DIFF_HEADS).astype(BF16)


def _block_diag_ones(n, seg):
    i = np.arange(n)
    return jnp.asarray((i[:, None] // seg == i[None, :] // seg).astype(np.float32))


def _proj_prep_kernel(shift_ref, x_ref, nw_ref, wa_ref, wb_ref, wc_ref, wm_ref, *rest, tm):
    proj_sc = rest[-1]
    xn = _rms(x_ref[...], nw_ref[...]).astype(BF16)
    c0 = 0
    for w_ref in (wa_ref, wb_ref, wc_ref, wm_ref):
        n = w_ref.shape[1]
        proj_sc[:, c0:c0 + n] = jnp.dot(xn, w_ref[...], preferred_element_type=F32)
        c0 += n
    _prep_kernel(shift_ref, proj_sc, *rest[:-1], tm=tm)


def _prep(shifts, x, nw, ws_bf16, tabs, fb, wfq, wfk, wnq, wnk, wdq, wdk, tm=256):
    s, d = x.shape
    c64, s64, c32, s32 = tabs
    b64 = _block_diag_ones(256, 64)
    b32 = _block_diag_ones(256, 32)

    def tiled(w, width):
        return jnp.tile(w.reshape(1, -1), (1, width // w.shape[-1]))

    consts = [fb, tiled(wfq, 256), tiled(wfk, 256), tiled(wnq, 512), tiled(wnk, 128), tiled(wdq, 256),
              tiled(wdk, 256), b64, b32]
    outs = [(FOX_HEADS, F32), (FOX_HEADS, BF16), (FOX_HEADS, BF16),
            (NSA_HEADS, F32), (NSA_HEADS, F32), (2, BF16),
            (NSA_KV, BF16), (NSA_KV, BF16), (NSA_KV, BF16), (NSA_KV, BF16), (1, F32),
            (2 * DIFF_HEADS, F32), (DIFF_HEADS, BF16), (DIFF_HEADS, BF16)]
    row = lambda w: pl.BlockSpec((tm, w), lambda i: (i, 0))
    assert sum(w.shape[1] for w in ws_bf16) == PROJ_W
    return pl.pallas_call(
        functools.partial(_proj_prep_kernel, tm=tm),
        grid=(s // tm,),
        in_specs=[pl.BlockSpec(memory_space=pltpu.SMEM), row(d), _full((1, d))] + [_full(w.shape) for w in ws_bf16]
                 + [row(LANES)] * 4 + [_full(c.shape) for c in consts],
        out_specs=[row(n * LANES) for n, _ in outs],
        out_shape=[jax.ShapeDtypeStruct((s, n * LANES), dt) for n, dt in outs],
        scratch_shapes=[pltpu.VMEM((8, LANES), F32), pltpu.VMEM((tm, PROJ_W), F32)],
        compiler_params=_cparams(("arbitrary",)),
        name="proj_head_prep",
    )(shifts, x, nw.reshape(1, d), *ws_bf16, c64, s64, c32, s32, *consts)


QK_DIM = 128
COL_SHIFT = 64
COL_SEL = 80
SCORE_BOUND = 60.0
SHIFT_HEADROOM = 60.0


def _flash_kernel(code_ref, q_ref, k_ref, v_ref, *rest, groups, tq, tk, window, has_sel, bounded):
    rest = list(rest)
    sel_ref = rest.pop(0) if has_sel else None
    o_ref = rest.pop(0)
    qT_sc = rest.pop(0)
    m_sc = None if bounded else rest.pop(0)
    acc_sc = rest.pop(0)
    code = code_ref[pl.program_id(1)]
    i = code & 0xFFF
    j = (code >> 12) & 0xFFF
    first = (code >> 24) & 1
    last = (code >> 25) & 1
    rows = groups * tq

    @pl.when(first == 1)
    def _():
        for g in range(groups):
            qT_sc[:, g * tq:(g + 1) * tq] = q_ref[:, g * QK_DIM:(g + 1) * QK_DIM].T.astype(BF16)
        if not bounded:
            m_sc[...] = jnp.full_like(m_sc, NEG_INIT)
        acc_sc[...] = jnp.zeros_like(acc_sc)

    def step(masked):
        if has_sel:
            qT_sc[COL_SEL:COL_SEL + tk // SLC_LEN, :] = _tile_lanes_any(sel_ref[0], groups)
        s = jnp.dot(k_ref[...], qT_sc[...], preferred_element_type=F32)
        if masked:
            kpos = j * tk + lax.broadcasted_iota(jnp.int32, (tk, rows), 0)
            lane = lax.broadcasted_iota(jnp.int32, (tk, rows), 1)
            qpos = i * tq + (lane & (tq - 1))
            keep = kpos <= qpos
            if window is not None:
                keep = keep & (kpos > qpos - window)
            s = jnp.where(keep, s, MASKVAL)
        v_t = (((0,), (0,)), ((), ()))
        v = v_ref[...]
        if bounded:
            acc_sc[...] += lax.dot_general(v, jnp.exp2(s).astype(BF16), v_t, preferred_element_type=F32)
        else:
            m_prev = m_sc[...]
            m_new = jnp.maximum(m_prev, jnp.max(s, axis=0, keepdims=True))
            p = jnp.exp2(s - m_new).astype(BF16)
            acc_sc[...] = (acc_sc[...] * jnp.exp2(m_prev - m_new)
                           + lax.dot_general(v, p, v_t, preferred_element_type=F32))
            m_sc[...] = m_new

    if window is not None:
        step(True)
    else:
        needs_mask = (j + 1) * tk - 1 > i * tq

        @pl.when(needs_mask)
        def _():
            step(True)

        @pl.when(jnp.logical_not(needs_mask))
        def _():
            step(False)

    @pl.when(last == 1)
    def _():
        l = acc_sc[COL_SHIFT:COL_SHIFT + 1, :]
        o = acc_sc[...] / jnp.where(l > 0.0, l, 1.0)
        for g in range(groups):
            o_ref[:, g * QK_DIM:(g + 1) * QK_DIM] = o[:, g * tq:(g + 1) * tq].T


def _tile_lanes_any(t, reps):
    return t if reps == 1 else jnp.concatenate([t] * reps, axis=1)


def _flash_call(q, k, v, sel, *, groups, tq, tk, window, bounded):
    s_len = q.shape[0]
    hkv = k.shape[1] // QK_DIM
    nq = s_len // tq
    rows = groups * tq
    codes = []
    for i in range(nq):
        q_lo, q_hi = i * tq, i * tq + tq - 1
        j_hi = q_hi // tk
        j_lo = 0 if window is None else max(0, (q_lo - window + 1) // tk)
        for j in range(j_lo, j_hi + 1):
            codes.append(i | (j << 12) | (int(j == j_lo) << 24) | (int(j == j_hi) << 25))
    codes = jnp.asarray(np.asarray(codes, dtype=np.int32))
    nsteps = codes.shape[0]

    def ti(c, s):
        return c[s] & 0xFFF

    def tj(c, s):
        return (c[s] >> 12) & 0xFFF

    in_specs = [
        pl.BlockSpec((tq, groups * QK_DIM), lambda h, s, c: (ti(c, s), h)),
        pl.BlockSpec((tk, QK_DIM), lambda h, s, c: (tj(c, s), h)),
        pl.BlockSpec((tk, QK_DIM), lambda h, s, c: (tj(c, s), h)),
    ]
    args = [q, k, v]
    if sel is not None:
        in_specs.append(pl.BlockSpec((1, tk // SLC_LEN, tq), lambda h, s, c: (h, tj(c, s), ti(c, s))))
        args.append(sel)
    scratch = [pltpu.VMEM((QK_DIM, rows), BF16)]
    if not bounded:
        scratch.append(pltpu.VMEM((1, rows), F32))
    scratch.append(pltpu.VMEM((QK_DIM, rows), F32))
    kern = functools.partial(_flash_kernel, groups=groups, tq=tq, tk=tk, window=window, has_sel=sel is not None,
                             bounded=bounded)
    return pl.pallas_call(
        kern,
        grid_spec=pltpu.PrefetchScalarGridSpec(
            num_scalar_prefetch=1,
            grid=(hkv, nsteps),
            in_specs=in_specs,
            out_specs=pl.BlockSpec((tq, groups * QK_DIM), lambda h, s, c: (ti(c, s), h)),
            scratch_shapes=scratch,
        ),
        out_shape=jax.ShapeDtypeStruct(q.shape, F32),
        compiler_params=_cparams(("parallel", "arbitrary")),
        name="flash_bounded" if bounded else "flash_online",
    )(codes, *args)


def _flash(q, k, v, sel=None, *, bound_ok, groups, tq, tk, window=None):
    call = functools.partial(_flash_call, q, k, v, sel, groups=groups, tq=tq, tk=tk, window=window)
    return lax.cond(bound_ok, lambda: call(bounded=True), lambda: call(bounded=False))


def _score_bound(wq, wk, seg, scale):
    return 1.02 * seg * jnp.max(jnp.abs(wq)) * jnp.max(jnp.abs(wk)) * scale * LOG2E


def _compress_kernel(r_ref, wa_ref, wb_ref, pa_ref, pb_ref, nw_ref, b64_ref, kx_o, vx_o):
    n = r_ref.shape[0]
    r = r_ref[...]
    wa, wb = wa_ref[...], wb_ref[...]
    first = jnp.dot(r, wa.astype(BF16), preferred_element_type=F32)
    second = jnp.dot(r, wb.astype(BF16), preferred_element_type=F32)
    nxt = pltpu.roll(second, n - 1, 0)
    rowi = lax.broadcasted_iota(jnp.int32, nxt.shape, 0)
    nxt = jnp.where(rowi == n - 1, 0.0, nxt)
    const = (jnp.dot(pa_ref[...], wa, precision=HIGHEST, preferred_element_type=F32)
             + jnp.dot(pb_ref[...], wb, precision=HIGHEST, preferred_element_type=F32))
    out = first + nxt + const
    k, v = out[:, 0:LANES], out[:, LANES:2 * LANES]
    ss = jnp.dot(k * k, b64_ref[...], precision=HIGHEST, preferred_element_type=F32)
    kn = k * lax.rsqrt(ss * (1.0 / NSA_HD) + EPS) * nw_ref[...]
    one_ex = _lane_fill(n, {COL_SHIFT: 1.0})
    kx_o[...] = _slots(kn, [one_ex] * NSA_KV).astype(BF16)
    vx_o[...] = _slots(v, [one_ex] * NSA_KV).astype(BF16)


def _compress(kvc, cmp_w, cmp_pos, nw):
    s = kvc.shape[0]
    n = s // CMP_STRIDE
    width = CMP_STRIDE * 2 * NSA_KV * NSA_HD
    w = cmp_w.reshape(2, 2, CMP_STRIDE, NSA_HD, NSA_HD)
    eye = jnp.eye(2, dtype=w.dtype)
    big = jnp.einsum("khldD,kK,gG->hlkgdKGD", w, eye, eye).reshape(2, width, 2 * NSA_KV * NSA_HD)
    pos = cmp_pos.reshape(2, 2, CMP_STRIDE, 1, NSA_HD)
    pos = jnp.broadcast_to(pos, (2, 2, CMP_STRIDE, NSA_KV, NSA_HD)).transpose(1, 2, 0, 3, 4).reshape(2, 1, width)
    b64 = _block_diag_ones(LANES, NSA_HD)
    nwt = jnp.tile(nw.reshape(1, -1), (1, LANES // NSA_HD))
    out = jax.ShapeDtypeStruct((n, NSA_KV * LANES), BF16)
    return pl.pallas_call(
        _compress_kernel,
        grid=(1,),
        in_specs=[_full((n, width)), _full(big[0].shape), _full(big[1].shape), _full((1, width)), _full((1, width)),
                  _full((1, LANES)), _full((LANES, LANES))],
        out_specs=[_full((n, NSA_KV * LANES))] * 2,
        out_shape=[out, out],
        compiler_params=_cparams(("arbitrary",)),
        name="nsa_compress",
    )(kvc.reshape(n, width), big[0], big[1], pos[0], pos[1], nwt, b64)


def _nsa_cmp_kernel(q_ref, kc_ref, vc_ref, ov_ref, o_ref, sel_ref, *, tq, groups, bounded):
    i = pl.program_id(1)
    ncmp = kc_ref.shape[0]
    rows = groups * tq
    qT = jnp.concatenate([q_ref[:, g * QK_DIM:(g + 1) * QK_DIM].T for g in range(groups)], axis=1).astype(BF16)
    s = jnp.dot(kc_ref[...], qT, preferred_element_type=F32)
    n_idx = lax.broadcasted_iota(jnp.int32, (ncmp, rows), 0)
    lane = lax.broadcasted_iota(jnp.int32, (ncmp, rows), 1)
    qpos = i * tq + (lane & (tq - 1))
    keep = n_idx * CMP_STRIDE + (CMP_LEN - 1) <= qpos
    if bounded:
        e = jnp.where(keep, jnp.exp2(s), 0.0)
    else:
        sm = jnp.where(keep, s, NEG_INIT)
        e = jnp.where(keep, jnp.exp2(sm - jnp.max(sm, axis=0, keepdims=True)), 0.0)
    den = jnp.sum(e, axis=0, keepdims=True)
    p = e / jnp.where(den > 0.0, den, 1.0)
    o = lax.dot_general(vc_ref[...], p.astype(BF16), (((0,), (0,)), ((), ())), preferred_element_type=F32)
    for g in range(groups):
        o_ref[:, g * QK_DIM:(g + 1) * QK_DIM] = o[:, g * tq:(g + 1) * tq].T

    psum = p[:, 0:tq]
    for g in range(1, groups):
        psum = psum + p[:, g * tq:(g + 1) * tq]
    imp = jnp.dot(ov_ref[...], psum, precision=HIGHEST, preferred_element_type=F32)
    ns = imp.shape[0]
    blk = lax.broadcasted_iota(jnp.int32, (ns, tq), 0)
    qp = i * tq + lax.broadcasted_iota(jnp.int32, (ns, tq), 1)
    cur = qp // SLC_LEN
    valid = blk <= cur
    forced = (blk == 0) | (blk == cur) | (blk == cur - 1)
    work = jnp.where(valid, imp + jnp.where(forced, FORCE_BONUS, 0.0), NEG_INIT)
    picked = jnp.zeros((ns, tq), dtype=jnp.bool_)
    for _ in range(SLC_TOPK):
        mx = jnp.max(work, axis=0, keepdims=True)
        idx = jnp.min(jnp.where(work == mx, blk, ns), axis=0, keepdims=True)
        pick = (blk == idx) & (mx > 0.5 * NEG_INIT)
        picked = picked | pick
        work = jnp.where(pick, NEG_INIT, work)
    sel_ref[0] = jnp.where(picked, 0.0, MASKVAL).astype(BF16)


def _nsa_cmp(qn, kcx, vcx, overlapT, tq, bound_ok):
    s = qn.shape[0]
    ncmp = kcx.shape[0]
    ns = overlapT.shape[0]
    groups = NSA_G

    def call(bounded):
        return pl.pallas_call(
            functools.partial(_nsa_cmp_kernel, tq=tq, groups=groups, bounded=bounded),
            grid=(NSA_KV, s // tq),
            in_specs=[pl.BlockSpec((tq, groups * QK_DIM), lambda h, i: (i, h)),
                      pl.BlockSpec((ncmp, QK_DIM), lambda h, i: (0, h)),
                      pl.BlockSpec((ncmp, QK_DIM), lambda h, i: (0, h)),
                      _full(overlapT.shape)],
            out_specs=[pl.BlockSpec((tq, groups * QK_DIM), lambda h, i: (i, h)),
                       pl.BlockSpec((1, ns, tq), lambda h, i: (h, 0, i))],
            out_shape=[jax.ShapeDtypeStruct(qn.shape, F32), jax.ShapeDtypeStruct((NSA_KV, ns, s), BF16)],
            compiler_params=_cparams(("parallel", "parallel")),
            name="nsa_cmp_bounded" if bounded else "nsa_cmp_max",
        )(qn, kcx, vcx, overlapT)

    return lax.cond(bound_ok, lambda: call(True), lambda: call(False))


def _mix_out_kernel(lam_ref, h_ref, fox_ref, oc_ref, os_ref, ow_ref, dd_ref, g_ref, sw_ref, wf_ref, wn_ref, wd_ref,
                    o_ref, *, diff_scale):
    lam = lam_ref[0]
    gates = g_ref[...]
    tm = gates.shape[0]
    lane = lax.broadcasted_iota(jnp.int32, (tm, LANES), 1)
    slot = lambda ref, n: ref[:, n * LANES:(n + 1) * LANES]
    nsa = []
    for h in range(NSA_HEADS):
        c0 = FOX_HEADS + 3 * h
        nsa.append(gates[:, c0:c0 + 1] * slot(oc_ref, h) + gates[:, c0 + 1:c0 + 2] * slot(os_ref, h)
                   + gates[:, c0 + 2:c0 + 3] * slot(ow_ref, h))
    diff = []
    for h in range(DIFF_HEADS):
        a = jnp.where(lane < DIFF_V, slot(dd_ref, 2 * h) - lam * slot(dd_ref, 2 * h + 1), 0.0)
        ss = jnp.sum(a * a, axis=-1, keepdims=True)
        diff.append(a * lax.rsqrt(ss * (1.0 / DIFF_V) + EPS) * sw_ref[...] * diff_scale)
    acc = h_ref[...]
    acc = acc + jnp.dot(fox_ref[...].astype(BF16), wf_ref[...], preferred_element_type=F32)
    acc = acc + jnp.dot(jnp.concatenate(nsa, axis=1).astype(BF16), wn_ref[...], preferred_element_type=F32)
    acc = acc + jnp.dot(jnp.concatenate(diff, axis=1).astype(BF16), wd_ref[...], preferred_element_type=F32)
    o_ref[...] = acc


def _slot_rows(w, heads):
    d = w.shape[1]
    w = w.reshape(heads, -1, d)
    return jnp.concatenate([w, jnp.zeros((heads, LANES - w.shape[1], d), w.dtype)], axis=1).reshape(heads * LANES, d)


def _mix_out(lam, h, fox, oc, os_, ow, dd, gates, subln_w, w_out, diff_scale, tm=512):
    s, d = h.shape
    nf, nn = FOX_HEADS * FOX_HD, NSA_HEADS * NSA_HD
    wf = _slot_rows(w_out[:nf], FOX_HEADS).astype(BF16)
    wn = _slot_rows(w_out[nf:nf + nn], NSA_HEADS).astype(BF16)
    wd = _slot_rows(w_out[nf + nn:], DIFF_HEADS).astype(BF16)
    sw = jnp.concatenate([subln_w.reshape(1, -1), jnp.zeros((1, LANES - DIFF_V), F32)], axis=1)
    row = lambda w: pl.BlockSpec((tm, w), lambda i: (i, 0))
    return pl.pallas_call(
        functools.partial(_mix_out_kernel, diff_scale=diff_scale),
        grid=(s // tm,),
        in_specs=[pl.BlockSpec(memory_space=pltpu.SMEM), row(d), row(fox.shape[1]), row(oc.shape[1]),
                  row(os_.shape[1]), row(ow.shape[1]), row(dd.shape[1]), row(LANES), _full((1, LANES)),
                  _full(wf.shape), _full(wn.shape), _full(wd.shape)],
        out_specs=row(d),
        out_shape=jax.ShapeDtypeStruct((s, d), F32),
        compiler_params=_cparams(("parallel",)),
        name="mix_out",
    )(lam, h, fox, oc, os_, ow, dd, gates, sw, wf, wn, wd)


def _peer_query_kernel(h_ref, nw_ref, wq_ref, keys_ref, xnT_o, scT_o):
    xn = _rms(h_ref[...], nw_ref[...])
    xnT_o[...] = xn.T.astype(BF16)
    q = jnp.dot(xn.astype(BF16), wq_ref[...], preferred_element_type=F32).astype(BF16)
    half = PEER_DQ // 2
    for b in range(2 * PEER_HEADS):
        sc = jnp.dot(q[:, b * half:(b + 1) * half], keys_ref[b], preferred_element_type=F32)
        scT_o[b] = sc.T


def _peer_query(h, nw, wq_bf16, keysT_bf16, tm=512):
    s, d = h.shape
    n = wq_bf16.shape[1]
    nb = 2 * PEER_HEADS
    return pl.pallas_call(
        _peer_query_kernel,
        grid=(s // tm,),
        in_specs=[pl.BlockSpec((tm, d), lambda i: (i, 0)), _full((1, d)), _full((d, n)), _full(keysT_bf16.shape)],
        out_specs=[pl.BlockSpec((d, tm), lambda i: (0, i)), pl.BlockSpec((nb, PEER_NKEYS, tm), lambda i: (0, 0, i))],
        out_shape=[jax.ShapeDtypeStruct((d, s), BF16), jax.ShapeDtypeStruct((nb, PEER_NKEYS, s), F32)],
        compiler_params=_cparams(("parallel",)),
        name="peer_query",
    )(h, nw.reshape(1, d), wq_bf16, keysT_bf16)


def _top16_rows(s, exact_ties):
    n = s.shape[0]
    row = lax.broadcasted_iota(jnp.int32, s.shape, 0)
    rank = jnp.full(s.shape, float(PEER_TOPK), dtype=F32)
    work = s
    vals = []
    for r in range(PEER_TOPK):
        mx = jnp.max(work, axis=0, keepdims=True)
        if exact_ties:
            pick = row == jnp.min(jnp.where(work == mx, row, n), axis=0, keepdims=True)
        else:
            pick = work == mx
        rank = jnp.where(pick, float(r), rank)
        work = jnp.where(pick, NEG_INIT, work)
        vals.append(mx)
    return rank, vals


def _peer_route(s1, s2, exact_ties):
    rank1, v1 = _top16_rows(s1, exact_ties)
    rank2, v2 = _top16_rows(s2, exact_ties)
    k = PEER_TOPK
    tm = s1.shape[1]
    r16 = lax.broadcasted_iota(jnp.int32, (k, tm), 0)
    v1m = jnp.zeros((k, tm), F32)
    v2m = jnp.zeros((k, tm), F32)
    for r in range(k):
        v1m = jnp.where(r16 == r, v1[r], v1m)
        v2m = jnp.where(r16 == r, v2[r], v2m)
    ea = jnp.exp(v1m - v1[0])
    eb = jnp.exp(v2m - v2[0])
    cands, gates = [v1m[0:1] + v2m], [ea[0:1] * eb]
    for r in range(1, 8):
        cands.append(v1m[r:r + 1] + v2m[0:8])
        gates.append(ea[r:r + 1] * eb[0:8])
    cands.append(v1m[8:16] + v2m[0:1])
    gates.append(ea[8:16] * eb[0:1])
    cand = jnp.concatenate(cands, axis=0)
    gate = jnp.concatenate(gates, axis=0)
    ncand = cand.shape[0]
    crow = lax.broadcasted_iota(jnp.int32, cand.shape, 0)
    for r in range(2, 8):
        start = 16 + 8 * (r - 1)
        cand = jnp.where((crow >= start + k // (r + 1)) & (crow < start + 8), NEG_INIT, cand)
    work = cand
    picked = jnp.zeros(cand.shape, dtype=jnp.bool_)
    for _ in range(k):
        mx = jnp.max(work, axis=0, keepdims=True)
        if exact_ties:
            pick = crow == jnp.min(jnp.where(work == mx, crow, ncand), axis=0, keepdims=True)
        else:
            pick = work == mx
        picked = picked | pick
        work = jnp.where(pick, NEG_INIT, work)
    pf = picked.astype(F32)
    z = jnp.sum(pf * gate, axis=0, keepdims=True)
    cnt = [jnp.sum(pf[0:16, :], axis=0, keepdims=True)]
    cnt += [jnp.sum(pf[8 + 8 * r:16 + 8 * r, :], axis=0, keepdims=True) for r in range(1, 8)]
    cnt += [pf[64 + r:65 + r, :] for r in range(8, 16)]
    cmap = jnp.zeros(s1.shape, dtype=F32)
    for r in range(k):
        cmap = jnp.where(rank1 == float(r), cnt[r], cmap)
    in1 = rank1 < float(k)
    in2 = rank2 < float(k)
    a = jnp.where(in1, jnp.exp(s1 - v1[0]), 0.0) / z
    b = jnp.where(in2, jnp.exp(s2 - v2[0]), 0.0)
    excess = (jnp.abs(jnp.sum(in1.astype(F32), axis=0, keepdims=True) - k)
              + jnp.abs(jnp.sum(in2.astype(F32), axis=0, keepdims=True) - k)
              + jnp.abs(jnp.sum(pf, axis=0, keepdims=True) - k))
    return a, cmap, b, rank2, excess


def _peer_topk_kernel(sc_ref, a_o, c_o, b_o, r_o):
    def run(exact_ties):
        a, c, b, rank2, excess = _peer_route(sc_ref[0], sc_ref[1], exact_ties)
        a_o[0] = a
        c_o[0] = c
        b_o[0] = b.astype(BF16)
        r_o[0] = rank2.astype(BF16)
        return excess

    excess = run(False)

    @pl.when(jnp.max(excess) > 0.0)
    def _():
        run(True)


def _peer_topk(scT, tm=512):
    n2, nk, s = scT.shape
    heads = n2 // 2
    ospec = pl.BlockSpec((1, nk, tm), lambda h, t: (h, 0, t))
    return pl.pallas_call(
        _peer_topk_kernel,
        grid=(heads, s // tm),
        in_specs=[pl.BlockSpec((2, nk, tm), lambda h, t: (h, 0, t))],
        out_specs=[ospec] * 4,
        out_shape=[jax.ShapeDtypeStruct((heads, nk, s), dt) for dt in (F32, F32, BF16, BF16)],
        compiler_params=_cparams(("parallel", "parallel")),
        name="peer_topk",
    )(scT)


def _gelu_exact(x):
    return 0.5 * x * (1.0 + lax.erf(x * (2.0 ** -0.5)))


def _peer_main_kernel(xT_ref, u_ref, v_ref, a_ref, c_ref, b_ref, r_ref, o_ref, w_sc, *, ec, sub):
    ci = pl.program_id(1)

    @pl.when(ci == 0)
    def _():
        o_ref[...] = jnp.zeros_like(o_ref)

    nk = PEER_NKEYS
    tm = xT_ref.shape[1]

    def row(ref, h, i1, lanes):
        x16 = jnp.broadcast_to(ref[h, pl.ds(i1, 1), :][:, lanes], (16, GATE_LANES)).astype(BF16)
        return jnp.concatenate([x16] * (nk // 16), axis=0)

    acc = None
    for sc in range(ec // sub):
        hid = jnp.dot(u_ref[sc * sub:(sc + 1) * sub, :], xT_ref[...], preferred_element_type=F32)
        act = _gelu_exact(hid).astype(BF16)
        for ii in range(sub // nk):
            slab = sc * (sub // nk) + ii
            i1 = ci * (ec // nk) + slab
            for l0 in range(0, tm, GATE_LANES):
                lanes = slice(l0, l0 + GATE_LANES)
                w = None
                for h in range(PEER_HEADS):
                    keep = r_ref[h, :, lanes] < row(c_ref, h, i1, lanes)
                    term = jnp.where(keep, b_ref[h, :, lanes], 0.0) * row(a_ref, h, i1, lanes)
                    w = term if w is None else w + term
                w_sc[slab * nk:(slab + 1) * nk, lanes] = w * act[ii * nk:(ii + 1) * nk, lanes]
        part = jnp.dot(w_sc[sc * sub:(sc + 1) * sub, :].T, v_ref[sc * sub:(sc + 1) * sub, :],
                       preferred_element_type=F32)
        acc = part if acc is None else acc + part
    o_ref[...] += acc


def _peer_main(xnT, u_bf16, v_bf16, a, c, b, r, tm=512, ec=2048, sub=512):
    d, s = xnT.shape
    e = u_bf16.shape[0]
    heads, nk, _ = a.shape
    rt = pl.BlockSpec((heads, nk, tm), lambda t, ci: (0, 0, t))
    wt = pl.BlockSpec((ec, d), lambda t, ci: (ci, 0))
    return pl.pallas_call(
        functools.partial(_peer_main_kernel, ec=ec, sub=sub),
        grid=(s // tm, e // ec),
        in_specs=[pl.BlockSpec((d, tm), lambda t, ci: (0, t)), wt, wt, rt, rt, rt, rt],
        out_specs=pl.BlockSpec((tm, d), lambda t, ci: (t, 0)),
        out_shape=jax.ShapeDtypeStruct((s, d), F32),
        scratch_shapes=[pltpu.VMEM((ec, tm), BF16)],
        compiler_params=_cparams(("parallel", "arbitrary")),
        name="peer_main",
    )(xnT, u_bf16, v_bf16, a, c, b, r)


def _ple_kernel(h_ref, peer_ref, p_ref, nw_ref, wg_ref, wp_ref, o_ref):
    h2 = h_ref[...] + peer_ref[...]
    xn = _rms(h2, nw_ref[...]).astype(BF16)
    gate = 1.0 / (1.0 + jnp.exp(-jnp.dot(xn, wg_ref[...], preferred_element_type=F32)))
    emb = jnp.dot(p_ref[...].astype(BF16), wp_ref[...], preferred_element_type=F32)
    o_ref[...] = h2 + gate * emb


def _ple(h, peer, p, nw, wg_bf16, wp_bf16, tm=512):
    s, d = h.shape
    pd = p.shape[1]
    row = lambda w: pl.BlockSpec((tm, w), lambda i: (i, 0))
    return pl.pallas_call(
        _ple_kernel,
        grid=(s // tm,),
        in_specs=[row(d), row(d), row(pd), _full((1, d)), _full((d, d)), _full((pd, d))],
        out_specs=row(d),
        out_shape=jax.ShapeDtypeStruct((s, d), F32),
        compiler_params=_cparams(("parallel",)),
        name="ple",
    )(h, peer, p, nw.reshape(1, d), wg_bf16, wp_bf16)


def _split_w_in(w):
    f0 = 3 * FOX_HEADS * FOX_HD
    n0 = f0 + FOX_HEADS
    g0 = n0 + _SEG["dq"][0] - _SEG["nq"][0]
    d0 = g0 + 3 * NSA_HEADS
    misc = jnp.concatenate([w[:, f0:n0], w[:, g0:d0], jnp.zeros((w.shape[0], LANES - (n0 - f0) - (d0 - g0)), w.dtype)],
                           axis=1)
    return [t.astype(BF16) for t in (w[:, :f0], w[:, n0:g0], w[:, d0:], misc)]


def _overlap_T(s):
    n = np.arange(s // CMP_STRIDE)[None, :] * CMP_STRIDE
    m = np.arange(s // SLC_LEN)[:, None] * SLC_LEN
    return jnp.asarray(((n < m + SLC_LEN) & (n + CMP_LEN > m)).astype(np.float32))


def _layer(h, p_i, tabs, layer, attn_norm_w, w_in, fox_f_bias, fox_q_norm_w, fox_k_norm_w, nsa_q_norm_w,
           nsa_k_norm_w, nsa_cmp_pos, nsa_cmp_w, diff_q_norm_w, diff_k_norm_w, diff_lambda, diff_subln_w,
           w_out, ffn_norm_w, peer_w_q, peer_sub_keys, peer_u, peer_v, ple_norm_w, ple_w_gate, ple_w_proj):
    s = h.shape[0]
    fb =jnp.zeros((1, LANES), F32).at[0, :FOX_HEADS].set(fox_f_bias)
    u_f = _score_bound(fox_q_norm_w, fox_k_norm_w, FOX_HD, FOX_HD ** -0.5)
    u_n = _score_bound(nsa_q_norm_w, nsa_k_norm_w, NSA_HD, NSA_HD ** -0.5)
    u_d = _score_bound(diff_q_norm_w, diff_k_norm_w, DIFF_QK, DIFF_QK ** -0.5)
    shifts = SHIFT_HEADROOM - jnp.stack([u_f, u_n, u_d]).astype(F32)
    (fq, fk, fv, nqn, nqr, kvc, ks, vs, kw, vw, gates, dq, dk, dv) = _prep(
        shifts, h, attn_norm_w, _split_w_in(w_in), tabs, fb, fox_q_norm_w, fox_k_norm_w, nsa_q_norm_w, nsa_k_norm_w, diff_q_norm_w, diff_k_norm_w)

    o_fox = _flash(fq, fk, fv, bound_ok=u_f <= SCORE_BOUND, groups=1, tq=Q_LANES, tk=TK_CAUSAL)

    tq_n = Q_LANES // NSA_G
    nsa_ok = u_n <= SCORE_BOUND
    kcx, vcx = _compress(kvc, nsa_cmp_w, nsa_cmp_pos, nsa_k_norm_w)
    o_c, sel = _nsa_cmp(nqn, kcx, vcx, _overlap_T(s), tq_n, nsa_ok)
    o_s = _flash(nqr, ks, vs, sel, bound_ok=nsa_ok, groups=NSA_G, tq=tq_n, tk=TK_CAUSAL)
    o_w = _flash(nqr, kw, vw, bound_ok=nsa_ok, groups=NSA_G, tq=tq_n, tk=512, window=WIN)

    o_d = _flash(dq, dk, dv, bound_ok=u_d <= SCORE_BOUND, groups=2, tq=Q_LANES // 2, tk=TK_CAUSAL)
    lv = diff_lambda.astype(F32)
    lam_init = 0.8 - 0.6 * math.exp(-0.3 * layer)
    lam = (jnp.exp(jnp.sum(lv[0] * lv[1])) - jnp.exp(jnp.sum(lv[2] * lv[3])) + lam_init).reshape(1)
    h1 = _mix_out(lam, h, o_fox, o_c, o_s, o_w, o_d, gates, diff_subln_w, w_out, 1.0 - lam_init)

    keysT = peer_sub_keys.reshape(2 * PEER_HEADS, PEER_NKEYS, PEER_DQ // 2).transpose(0, 2, 1).astype(BF16)
    xnT, scT = _peer_query(h1, ffn_norm_w, peer_w_q.astype(BF16), keysT)
    a, c, b, r = _peer_topk(scT)
    peer = _peer_main(xnT, peer_u.astype(BF16), peer_v.astype(BF16), a, c, b, r)

    return _ple(h1, peer, p_i, ple_norm_w, ple_w_gate.astype(BF16), ple_w_proj.astype(BF16))


def kernel(x, p, positions, attn_norm_w, w_in, fox_f_bias, fox_q_norm_w, fox_k_norm_w, nsa_q_norm_w, nsa_k_norm_w,
           nsa_cmp_pos, nsa_cmp_w, diff_q_norm_w, diff_k_norm_w, diff_lambda, diff_subln_w, w_out, ffn_norm_w,
           peer_w_q, peer_sub_keys, peer_u, peer_v, ple_norm_w, ple_w_gate, ple_w_proj):
    b, s, d = x.shape
    assert b == 1 and d == D_MODEL and s % Q_LANES == 0 and s % TK_CAUSAL == 0
    tabs = _rope_tables(positions)
    h = x.reshape(s, d)
    per_layer = (attn_norm_w, w_in, fox_f_bias, fox_q_norm_w, fox_k_norm_w, nsa_q_norm_w, nsa_k_norm_w, nsa_cmp_pos,
                 nsa_cmp_w, diff_q_norm_w, diff_k_norm_w, diff_lambda, diff_subln_w, w_out, ffn_norm_w, peer_w_q,
                 peer_sub_keys, peer_u, peer_v, ple_norm_w, ple_w_gate, ple_w_proj)
    for layer in range(attn_norm_w.shape[0]):
        h = _layer(h, p[layer, 0], tabs, layer, *(w[layer] for w in per_layer))
    return h.reshape(b, s, d)
```

```python
import functools
import math

import numpy as np
import jax
import jax.numpy as jnp
from jax import lax
from jax.experimental import pallas as pl
from jax.experimental.pallas import tpu as pltpu

F32 = jnp.float32
BF16 = jnp.bfloat16
HIGHEST = lax.Precision.HIGHEST

D_MODEL = 1024
PLE_DIM = 256
ROPE_THETA = 10000.0
EPS = 1e-6
FOX_HEADS, FOX_HD = 4, 64
NSA_HEADS, NSA_KV, NSA_HD = 8, 2, 64
NSA_G = NSA_HEADS // NSA_KV
CMP_LEN, CMP_STRIDE, SLC_LEN, SLC_TOPK, WIN = 32, 16, 64, 16, 512
FORCE_BONUS = 1.0e4
DIFF_HEADS, DIFF_QK, DIFF_V = 4, 32, 64
PEER_HEADS, PEER_NKEYS, PEER_DQ, PEER_TOPK = 8, 128, 256, 16
PEER_EXPERTS = PEER_NKEYS * PEER_NKEYS

LANES = 128
LOG2E = math.log2(math.e)
NEG_INIT = -1.0e30
MASKVAL = -2.0e30
VMEM_LIMIT = 56 * 1024 * 1024
TK_CAUSAL = 2048
Q_LANES = 2048
GATE_LANES = 512

_SEG = dict(fq=(0, 256), fk=(256, 256), fv=(512, 256), nq=(768, 512), nkc=(1280, 128), nvc=(1408, 128),
            nks=(1536, 128), nvs=(1664, 128), nkw=(1792, 128), nvw=(1920, 128),
            dq=(2048, 256), dk=(2304, 256), dv=(2560, 256), misc=(2816, 128))
PROJ_W = 2944


def _cparams(sem):
    return pltpu.CompilerParams(dimension_semantics=sem, vmem_limit_bytes=VMEM_LIMIT)


def _full(shape):
    n = len(shape)
    return pl.BlockSpec(shape, lambda *_: (0,) * n)


def _rope_tab_kernel(pos_ref, f64_ref, g64_ref, f32_ref, g32_ref, c64_o, s64_o, c32_o, s32_o):
    pos = pos_ref[...].astype(F32)
    a64 = pos * f64_ref[...]
    c64_o[...] = jnp.cos(a64)
    s64_o[...] = jnp.sin(a64) * g64_ref[...]
    a32 = pos * f32_ref[...]
    c32_o[...] = jnp.cos(a32)
    s32_o[...] = jnp.sin(a32) * g32_ref[...]


def _rope_tables(positions):
    s = positions.shape[-1]
    pos = positions.reshape(s, 1)
    lane = np.arange(LANES)

    def lanes(half):
        inv = ROPE_THETA ** (-jnp.arange(half, dtype=F32) / half)
        freq = inv[(lane % (2 * half)) % half].reshape(1, LANES)
        sign = np.where((lane % (2 * half)) < half, -1.0, 1.0).astype(np.float32).reshape(1, LANES)
        return freq, jnp.asarray(sign)

    f64, g64 = lanes(NSA_HD // 2)
    f32_, g32 = lanes(DIFF_QK // 2)
    tm = 512
    out = jax.ShapeDtypeStruct((s, LANES), F32)
    row = pl.BlockSpec((tm, LANES), lambda i: (i, 0))
    return pl.pallas_call(
        _rope_tab_kernel,
        grid=(s // tm,),
        in_specs=[pl.BlockSpec((tm, 1), lambda i: (i, 0))] + [_full((1, LANES))] * 4,
        out_specs=[row] * 4,
        out_shape=[out] * 4,
        compiler_params=_cparams(("parallel",)),
        name="rope_tables",
    )(pos, f64, g64, f32_, g32)


def _rms(x, w):
    return x * lax.rsqrt(jnp.mean(x * x, axis=-1, keepdims=True) + EPS) * w


def _seg_rms(x, bmat, seg):
    outs = []
    for c0 in range(0, x.shape[1], 256):
        w = min(256, x.shape[1] - c0)
        xc = x[:, c0:c0 + w]
        sq = xc * xc
        hi = sq.astype(BF16)
        lo = (sq - hi.astype(F32)).astype(BF16)
        b = bmat[:w, :w].astype(BF16)
        ss = jnp.dot(hi, b, preferred_element_type=F32) + jnp.dot(lo, b, preferred_element_type=F32)
        outs.append(xc * lax.rsqrt(ss * (1.0 / seg) + EPS))
    return outs[0] if len(outs) == 1 else jnp.concatenate(outs, axis=1)


def _tile_lanes(t, width):
    reps = width // LANES
    return t if reps == 1 else jnp.concatenate([t] * reps, axis=1)


def _rope(x, cos, sin, half):
    width = x.shape[1]
    left = pltpu.roll(x, width - half, 1)
    right = pltpu.roll(x, half, 1)
    lane = lax.broadcasted_iota(jnp.int32, x.shape, 1)
    swapped = jnp.where((lane & (2 * half - 1)) < half, left, right)
    return x * _tile_lanes(cos, width) + swapped * _tile_lanes(sin, width)


def _lane_fill(tm, cols):
    lane = lax.broadcasted_iota(jnp.int32, (tm, LANES), 1)
    out = jnp.zeros((tm, LANES), F32)
    for l, v in cols.items():
        out = jnp.where(lane == l, v, out)
    return out


def _slots(x, extras):
    lane = lax.broadcasted_iota(jnp.int32, (x.shape[0], LANES), 1)
    outs = []
    for h, ex in enumerate(extras):
        col = x[:, (h // 2) * LANES:(h // 2 + 1) * LANES]
        if h % 2:
            col = pltpu.roll(col, LANES // 2, 1)
        outs.append(jnp.where(lane < LANES // 2, col, ex))
    return jnp.concatenate(outs, axis=1)


def _prep_kernel(shift_ref, proj_ref, c64_ref, s64_ref, c32_ref, s32_ref, fb_ref, wfq_ref, wfk_ref, wnq_ref,
                 wnk_ref, wdq_ref, wdk_ref, b64_ref, b32_ref,
                 fq_o, fk_o, fv_o, nqn_o, nqr_o, kvc_o, ks_o, vs_o, kw_o, vw_o, gate_o, dq_o, dk_o, dv_o,
                 carry_sc, *, tm):
    def seg(name):
        c0, w = _SEG[name]
        return proj_ref[:, c0:c0 + w]

    b64 = b64_ref[...]
    b32 = b32_ref[...]
    c64, s64, c32, s32 = c64_ref[...], s64_ref[...], c32_ref[...], s32_ref[...]
    one_ex = _lane_fill(tm, {COL_SHIFT: 1.0})

    misc = seg("misc")
    gate_o[...] = 1.0 / (1.0 + jnp.exp(-misc))
    t = misc + fb_ref[...]
    logf = jnp.minimum(t, 0.0) - jnp.log1p(jnp.exp(-jnp.abs(t)))

    @pl.when(pl.program_id(0) == 0)
    def _():
        carry_sc[...] = jnp.zeros_like(carry_sc)

    r = lax.broadcasted_iota(jnp.int32, (tm, tm), 0)
    c = lax.broadcasted_iota(jnp.int32, (tm, tm), 1)
    tri = (c <= r).astype(F32)
    csum = jnp.dot(tri, logf, precision=HIGHEST, preferred_element_type=F32) + carry_sc[0:1, :]
    carry_sc[...] = jnp.broadcast_to(csum[tm - 1:tm, :], carry_sc.shape)
    csum = csum * LOG2E
    hi = csum.astype(BF16).astype(F32)
    r1 = csum - hi
    mid = r1.astype(BF16).astype(F32)
    lo = (r1 - mid).astype(BF16).astype(F32)

    fq = _seg_rms(seg("fq"), b64, FOX_HD) * wfq_ref[...] * (FOX_HD ** -0.5 * LOG2E)
    fk = _seg_rms(seg("fk"), b64, FOX_HD) * wfk_ref[...]
    q_ex, k_ex = [], []
    for h in range(FOX_HEADS):
        ch, cm, cl = hi[:, h:h + 1], mid[:, h:h + 1], lo[:, h:h + 1]
        q_ex.append(_lane_fill(tm, {COL_SHIFT: shift_ref[0], 65: 1.0, 66: 1.0, 67: 1.0, 68: ch, 69: cm, 70: cl}))
        k_ex.append(_lane_fill(tm, {COL_SHIFT: 1.0, 65: -ch, 66: -cm, 67: -cl, 68: 1.0, 69: 1.0, 70: 1.0}))
    fq_o[...] = _slots(fq, q_ex)
    fk_o[...] = _slots(fk, k_ex).astype(BF16)
    fv_o[...] = _slots(seg("fv"), [one_ex] * FOX_HEADS).astype(BF16)

    n_ex = [_lane_fill(tm, {COL_SHIFT: shift_ref[1]})] * NSA_HEADS
    nqn = _seg_rms(seg("nq"), b64, NSA_HD) * wnq_ref[...]
    nqn_o[...] = _slots(nqn * (NSA_HD ** -0.5 * LOG2E), n_ex)
    nqr_o[...] = _slots(_rope(nqn, c64, s64, NSA_HD // 2) * (NSA_HD ** -0.5 * LOG2E), n_ex)
    kvc_o[:, 0:128] = seg("nkc").astype(BF16)
    kvc_o[:, 128:256] = seg("nvc").astype(BF16)
    wnk = wnk_ref[...]
    lane = lax.broadcasted_iota(jnp.int32, (tm, LANES), 1)
    row = pl.program_id(0) * tm + lax.broadcasted_iota(jnp.int32, (tm, LANES), 0)
    blk_in_tile = (row & (TK_CAUSAL - 1)) // SLC_LEN
    sel_ex = jnp.where(lane == COL_SEL + blk_in_tile, 1.0, one_ex)
    ks = _rope(_seg_rms(seg("nks"), b64, NSA_HD) * wnk, c64, s64, NSA_HD // 2)
    ks_o[...] = _slots(ks, [sel_ex] * NSA_KV).astype(BF16)
    vs_o[...] = _slots(seg("nvs"), [one_ex] * NSA_KV).astype(BF16)
    kw = _rope(_seg_rms(seg("nkw"), b64, NSA_HD) * wnk, c64, s64, NSA_HD // 2)
    kw_o[...] = _slots(kw, [one_ex] * NSA_KV).astype(BF16)
    vw_o[...] = _slots(seg("nvw"), [one_ex] * NSA_KV).astype(BF16)

    dq = _rope(_seg_rms(seg("dq"), b32, DIFF_QK) * wdq_ref[...], c32, s32, DIFF_QK // 2) * (DIFF_QK ** -0.5 * LOG2E)
    d_ex = _lane_fill(tm, {COL_SHIFT: shift_ref[2]})
    d_slots = []
    for h in range(DIFF_HEADS):
        col = dq[:, (h // 2) * LANES:(h // 2 + 1) * LANES]
        if h % 2:
            col = pltpu.roll(col, LANES // 2, 1)
        d_slots.append(jnp.where(lane < DIFF_QK, col, d_ex))
        d_slots.append(jnp.where((lane >= DIFF_QK) & (lane < 2 * DIFF_QK), col, d_ex))
    dq_o[...] = jnp.concatenate(d_slots, axis=1)
    dk = _rope(_seg_rms(seg("dk"), b32, DIFF_QK) * wdk_ref[...], c32, s32, DIFF_QK // 2)
    dk_o[...] = _slots(dk, [one_ex] * DIFF_HEADS).astype(BF16)
    dv_o[...] = _slots(seg("dv"), [one_ex] * DIFF_HEADS).astype(BF16)


def _block_diag_ones(n, seg):
    i = np.arange(n)
    return jnp.asarray((i[:, None] // seg == i[None, :] // seg).astype(np.float32))


def _proj_prep_kernel(shift_ref, x_ref, nw_ref, wa_ref, wb_ref, wc_ref, wm_ref, *rest, tm):
    proj_sc = rest[-1]
    xn = _rms(x_ref[...], nw_ref[...]).astype(BF16)
    c0 = 0
    for w_ref in (wa_ref, wb_ref, wc_ref, wm_ref):
        n = w_ref.shape[1]
        proj_sc[:, c0:c0 + n] = jnp.dot(xn, w_ref[...], preferred_element_type=F32)
        c0 += n
    _prep_kernel(shift_ref, proj_sc, *rest[:-1], tm=tm)


def _prep(shifts, x, nw, ws_bf16, tabs, fb, wfq, wfk, wnq, wnk, wdq, wdk, tm=256):
    s, d = x.shape
    c64, s64, c32, s32 = tabs
    b64 = _block_diag_ones(256, 64)
    b32 = _block_diag_ones(256, 32)

    def tiled(w, width):
        return jnp.tile(w.reshape(1, -1), (1, width // w.shape[-1]))

    consts = [fb, tiled(wfq, 256), tiled(wfk, 256), tiled(wnq, 512), tiled(wnk, 128), tiled(wdq, 256),
              tiled(wdk, 256), b64, b32]
    outs = [(FOX_HEADS, F32), (FOX_HEADS, BF16), (FOX_HEADS, BF16),
            (NSA_HEADS, F32), (NSA_HEADS, F32), (2, BF16),
            (NSA_KV, BF16), (NSA_KV, BF16), (NSA_KV, BF16), (NSA_KV, BF16), (1, F32),
            (2 * DIFF_HEADS, F32), (DIFF_HEADS, BF16), (DIFF_HEADS, BF16)]
    row = lambda w: pl.BlockSpec((tm, w), lambda i: (i, 0))
    assert sum(w.shape[1] for w in ws_bf16) == PROJ_W
    return pl.pallas_call(
        functools.partial(_proj_prep_kernel, tm=tm),
        grid=(s // tm,),
        in_specs=[pl.BlockSpec(memory_space=pltpu.SMEM), row(d), _full((1, d))] + [_full(w.shape) for w in ws_bf16]
                 + [row(LANES)] * 4 + [_full(c.shape) for c in consts],
        out_specs=[row(n * LANES) for n, _ in outs],
        out_shape=[jax.ShapeDtypeStruct((s, n * LANES), dt) for n, dt in outs],
        scratch_shapes=[pltpu.VMEM((8, LANES), F32), pltpu.VMEM((tm, PROJ_W), F32)],
        compiler_params=_cparams(("arbitrary",)),
        name="proj_head_prep",
    )(shifts, x, nw.reshape(1, d), *ws_bf16, c64, s64, c32, s32, *consts)


QK_DIM = 128
COL_SHIFT = 64
COL_SEL = 80
SCORE_BOUND = 60.0
SHIFT_HEADROOM = 60.0


def _flash_kernel(code_ref, q_ref, k_ref, v_ref, *rest, groups, tq, tk, window, has_sel, bounded):
    rest = list(rest)
    sel_ref = rest.pop(0) if has_sel else None
    o_ref = rest.pop(0)
    qT_sc = rest.pop(0)
    m_sc = None if bounded else rest.pop(0)
    acc_sc = rest.pop(0)
    code = code_ref[pl.program_id(1)]
    i = code & 0xFFF
    j = (code >> 12) & 0xFFF
    first = (code >> 24) & 1
    last = (code >> 25) & 1
    rows = groups * tq

    @pl.when(first == 1)
    def _():
        for g in range(groups):
            qT_sc[:, g * tq:(g + 1) * tq] = q_ref[:, g * QK_DIM:(g + 1) * QK_DIM].T.astype(BF16)
        if not bounded:
            m_sc[...] = jnp.full_like(m_sc, NEG_INIT)
        acc_sc[...] = jnp.zeros_like(acc_sc)

    def step(masked):
        if has_sel:
            qT_sc[COL_SEL:COL_SEL + tk // SLC_LEN, :] = _tile_lanes_any(sel_ref[0], groups)
        s = jnp.dot(k_ref[...], qT_sc[...], preferred_element_type=F32)
        if masked:
            kpos = j * tk + lax.broadcasted_iota(jnp.int32, (tk, rows), 0)
            lane = lax.broadcasted_iota(jnp.int32, (tk, rows), 1)
            qpos = i * tq + (lane & (tq - 1))
            keep = kpos <= qpos
            if window is not None:
                keep = keep & (kpos > qpos - window)
            s = jnp.where(keep, s, MASKVAL)
        v_t = (((0,), (0,)), ((), ()))
        v = v_ref[...]
        if bounded:
            acc_sc[...] += lax.dot_general(v, jnp.exp2(s).astype(BF16), v_t, preferred_element_type=F32)
        else:
            m_prev = m_sc[...]
            m_new = jnp.maximum(m_prev, jnp.max(s, axis=0, keepdims=True))
            p = jnp.exp2(s - m_new).astype(BF16)
            acc_sc[...] = (acc_sc[...] * jnp.exp2(m_prev - m_new)
                           + lax.dot_general(v, p, v_t, preferred_element_type=F32))
            m_sc[...] = m_new

    if window is not None:
        step(True)
    else:
        needs_mask = (j + 1) * tk - 1 > i * tq

        @pl.when(needs_mask)
        def _():
            step(True)

        @pl.when(jnp.logical_not(needs_mask))
        def _():
            step(False)

    @pl.when(last == 1)
    def _():
        l = acc_sc[COL_SHIFT:COL_SHIFT + 1, :]
        o = acc_sc[...] / jnp.where(l > 0.0, l, 1.0)
        for g in range(groups):
            o_ref[:, g * QK_DIM:(g + 1) * QK_DIM] = o[:, g * tq:(g + 1) * tq].T


def _tile_lanes_any(t, reps):
    return t if reps == 1 else jnp.concatenate([t] * reps, axis=1)


def _flash_call(q, k, v, sel, *, groups, tq, tk, window, bounded):
    s_len = q.shape[0]
    hkv = k.shape[1] // QK_DIM
    nq = s_len // tq
    rows = groups * tq
    codes = []
    for i in range(nq):
        q_lo, q_hi = i * tq, i * tq + tq - 1
        j_hi = q_hi // tk
        j_lo = 0 if window is None else max(0, (q_lo - window + 1) // tk)
        for j in range(j_lo, j_hi + 1):
            codes.append(i | (j << 12) | (int(j == j_lo) << 24) | (int(j == j_hi) << 25))
    codes = jnp.asarray(np.asarray(codes, dtype=np.int32))
    nsteps = codes.shape[0]

    def ti(c, s):
        return c[s] & 0xFFF

    def tj(c, s):
        return (c[s] >> 12) & 0xFFF

    in_specs = [
        pl.BlockSpec((tq, groups * QK_DIM), lambda h, s, c: (ti(c, s), h)),
        pl.BlockSpec((tk, QK_DIM), lambda h, s, c: (tj(c, s), h)),
        pl.BlockSpec((tk, QK_DIM), lambda h, s, c: (tj(c, s), h)),
    ]
    args = [q, k, v]
    if sel is not None:
        in_specs.append(pl.BlockSpec((1, tk // SLC_LEN, tq), lambda h, s, c: (h, tj(c, s), ti(c, s))))
        args.append(sel)
    scratch = [pltpu.VMEM((QK_DIM, rows), BF16)]
    if not bounded:
        scratch.append(pltpu.VMEM((1, rows), F32))
    scratch.append(pltpu.VMEM((QK_DIM, rows), F32))
    kern = functools.partial(_flash_kernel, groups=groups, tq=tq, tk=tk, window=window, has_sel=sel is not None,
                             bounded=bounded)
    return pl.pallas_call(
        kern,
        grid_spec=pltpu.PrefetchScalarGridSpec(
            num_scalar_prefetch=1,
            grid=(hkv, nsteps),
            in_specs=in_specs,
            out_specs=pl.BlockSpec((tq, groups * QK_DIM), lambda h, s, c: (ti(c, s), h)),
            scratch_shapes=scratch,
        ),
        out_shape=jax.ShapeDtypeStruct(q.shape, F32),
        compiler_params=_cparams(("parallel", "arbitrary")),
        name="flash_bounded" if bounded else "flash_online",
    )(codes, *args)


def _flash(q, k, v, sel=None, *, bound_ok, groups, tq, tk, window=None):
    call = functools.partial(_flash_call, q, k, v, sel, groups=groups, tq=tq, tk=tk, window=window)
    return lax.cond(bound_ok, lambda: call(bounded=True), lambda: call(bounded=False))


def _score_bound(wq, wk, seg, scale):
    return 1.02 * seg * jnp.max(jnp.abs(wq)) * jnp.max(jnp.abs(wk)) * scale * LOG2E


def _compress_kernel(r_ref, wa_ref, wb_ref, pa_ref, pb_ref, nw_ref, b64_ref, kx_o, vx_o):
    n = r_ref.shape[0]
    r = r_ref[...]
    wa, wb = wa_ref[...], wb_ref[...]
    first = jnp.dot(r, wa.astype(BF16), preferred_element_type=F32)
    second = jnp.dot(r, wb.astype(BF16), preferred_element_type=F32)
    nxt = pltpu.roll(second, n - 1, 0)
    rowi = lax.broadcasted_iota(jnp.int32, nxt.shape, 0)
    nxt = jnp.where(rowi == n - 1, 0.0, nxt)
    const = (jnp.dot(pa_ref[...], wa, precision=HIGHEST, preferred_element_type=F32)
             + jnp.dot(pb_ref[...], wb, precision=HIGHEST, preferred_element_type=F32))
    out = first + nxt + const
    k, v = out[:, 0:LANES], out[:, LANES:2 * LANES]
    ss = jnp.dot(k * k, b64_ref[...], precision=HIGHEST, preferred_element_type=F32)
    kn = k * lax.rsqrt(ss * (1.0 / NSA_HD) + EPS) * nw_ref[...]
    one_ex = _lane_fill(n, {COL_SHIFT: 1.0})
    kx_o[...] = _slots(kn, [one_ex] * NSA_KV).astype(BF16)
    vx_o[...] = _slots(v, [one_ex] * NSA_KV).astype(BF16)


def _compress(kvc, cmp_w, cmp_pos, nw):
    s = kvc.shape[0]
    n = s // CMP_STRIDE
    width = CMP_STRIDE * 2 * NSA_KV * NSA_HD
    w = cmp_w.reshape(2, 2, CMP_STRIDE, NSA_HD, NSA_HD)
    eye = jnp.eye(2, dtype=w.dtype)
    big = jnp.einsum("khldD,kK,gG->hlkgdKGD", w, eye, eye).reshape(2, width, 2 * NSA_KV * NSA_HD)
    pos = cmp_pos.reshape(2, 2, CMP_STRIDE, 1, NSA_HD)
    pos = jnp.broadcast_to(pos, (2, 2, CMP_STRIDE, NSA_KV, NSA_HD)).transpose(1, 2, 0, 3, 4).reshape(2, 1, width)
    b64 = _block_diag_ones(LANES, NSA_HD)
    nwt = jnp.tile(nw.reshape(1, -1), (1, LANES // NSA_HD))
    out = jax.ShapeDtypeStruct((n, NSA_KV * LANES), BF16)
    return pl.pallas_call(
        _compress_kernel,
        grid=(1,),
        in_specs=[_full((n, width)), _full(big[0].shape), _full(big[1].shape), _full((1, width)), _full((1, width)),
                  _full((1, LANES)), _full((LANES, LANES))],
        out_specs=[_full((n, NSA_KV * LANES))] * 2,
        out_shape=[out, out],
        compiler_params=_cparams(("arbitrary",)),
        name="nsa_compress",
    )(kvc.reshape(n, width), big[0], big[1], pos[0], pos[1], nwt, b64)


def _nsa_cmp_kernel(q_ref, kc_ref, vc_ref, ov_ref, o_ref, sel_ref, *, tq, groups, bounded):
    i = pl.program_id(1)
    ncmp = kc_ref.shape[0]
    rows = groups * tq
    qT = jnp.concatenate([q_ref[:, g * QK_DIM:(g + 1) * QK_DIM].T for g in range(groups)], axis=1).astype(BF16)
    s = jnp.dot(kc_ref[...], qT, preferred_element_type=F32)
    n_idx = lax.broadcasted_iota(jnp.int32, (ncmp, rows), 0)
    lane = lax.broadcasted_iota(jnp.int32, (ncmp, rows), 1)
    qpos = i * tq + (lane & (tq - 1))
    keep = n_idx * CMP_STRIDE + (CMP_LEN - 1) <= qpos
    if bounded:
        e = jnp.where(keep, jnp.exp2(s), 0.0)
    else:
        sm = jnp.where(keep, s, NEG_INIT)
        e = jnp.where(keep, jnp.exp2(sm - jnp.max(sm, axis=0, keepdims=True)), 0.0)
    den = jnp.sum(e, axis=0, keepdims=True)
    p = e / jnp.where(den > 0.0, den, 1.0)
    o = lax.dot_general(vc_ref[...], p.astype(BF16), (((0,), (0,)), ((), ())), preferred_element_type=F32)
    for g in range(groups):
        o_ref[:, g * QK_DIM:(g + 1) * QK_DIM] = o[:, g * tq:(g + 1) * tq].T

    psum = p[:, 0:tq]
    for g in range(1, groups):
        psum = psum + p[:, g * tq:(g + 1) * tq]
    hi = psum.astype(BF16)
    lo = (psum - hi.astype(F32)).astype(BF16)
    ov = ov_ref[...]
    imp = (jnp.dot(ov, hi, preferred_element_type=F32) + jnp.dot(ov, lo, preferred_element_type=F32))
    ns = imp.shape[0]
    blk = lax.broadcasted_iota(jnp.int32, (ns, tq), 0)
    qp = i * tq + lax.broadcasted_iota(jnp.int32, (ns, tq), 1)
    cur = qp // SLC_LEN
    valid = blk <= cur
    forced = (blk == 0) | (blk == cur) | (blk == cur - 1)
    work = jnp.where(valid, imp + jnp.where(forced, FORCE_BONUS, 0.0), NEG_INIT)
    picked = jnp.zeros((ns, tq), dtype=jnp.bool_)
    for _ in range(SLC_TOPK):
        mx = jnp.max(work, axis=0, keepdims=True)
        idx = jnp.min(jnp.where(work == mx, blk, ns), axis=0, keepdims=True)
        pick = (blk == idx) & (mx > 0.5 * NEG_INIT)
        picked = picked | pick
        work = jnp.where(pick, NEG_INIT, work)
    sel_ref[0] = jnp.where(picked, 0.0, MASKVAL).astype(BF16)


def _nsa_cmp(qn, kcx, vcx, overlapT, tq, bound_ok):
    s = qn.shape[0]
    ncmp = kcx.shape[0]
    ns = overlapT.shape[0]
    groups = NSA_G

    def call(bounded):
        return pl.pallas_call(
            functools.partial(_nsa_cmp_kernel, tq=tq, groups=groups, bounded=bounded),
            grid=(NSA_KV, s // tq),
            in_specs=[pl.BlockSpec((tq, groups * QK_DIM), lambda h, i: (i, h)),
                      pl.BlockSpec((ncmp, QK_DIM), lambda h, i: (0, h)),
                      pl.BlockSpec((ncmp, QK_DIM), lambda h, i: (0, h)),
                      _full(overlapT.shape)],
            out_specs=[pl.BlockSpec((tq, groups * QK_DIM), lambda h, i: (i, h)),
                       pl.BlockSpec((1, ns, tq), lambda h, i: (h, 0, i))],
            out_shape=[jax.ShapeDtypeStruct(qn.shape, F32), jax.ShapeDtypeStruct((NSA_KV, ns, s), BF16)],
            compiler_params=_cparams(("parallel", "parallel")),
            name="nsa_cmp_bounded" if bounded else "nsa_cmp_max",
        )(qn, kcx, vcx, overlapT)

    return lax.cond(bound_ok, lambda: call(True), lambda: call(False))


def _mix_out_kernel(lam_ref, h_ref, fox_ref, oc_ref, os_ref, ow_ref, dd_ref, g_ref, sw_ref, wf_ref, wn_ref, wd_ref,
                    o_ref, *, diff_scale):
    lam = lam_ref[0]
    gates = g_ref[...]
    tm = gates.shape[0]
    lane = lax.broadcasted_iota(jnp.int32, (tm, LANES), 1)
    slot = lambda ref, n: ref[:, n * LANES:(n + 1) * LANES]
    nsa = []
    for h in range(NSA_HEADS):
        c0 = FOX_HEADS + 3 * h
        nsa.append(gates[:, c0:c0 + 1] * slot(oc_ref, h) + gates[:, c0 + 1:c0 + 2] * slot(os_ref, h)
                   + gates[:, c0 + 2:c0 + 3] * slot(ow_ref, h))
    diff = []
    for h in range(DIFF_HEADS):
        a = jnp.where(lane < DIFF_V, slot(dd_ref, 2 * h) - lam * slot(dd_ref, 2 * h + 1), 0.0)
        ss = jnp.sum(a * a, axis=-1, keepdims=True)
        diff.append(a * lax.rsqrt(ss * (1.0 / DIFF_V) + EPS) * sw_ref[...] * diff_scale)
    acc = h_ref[...]
    acc = acc + jnp.dot(fox_ref[...].astype(BF16), wf_ref[...], preferred_element_type=F32)
    acc = acc + jnp.dot(jnp.concatenate(nsa, axis=1).astype(BF16), wn_ref[...], preferred_element_type=F32)
    acc = acc + jnp.dot(jnp.concatenate(diff, axis=1).astype(BF16), wd_ref[...], preferred_element_type=F32)
    o_ref[...] = acc


def _slot_rows(w, heads):
    d = w.shape[1]
    w = w.reshape(heads, -1, d)
    return jnp.concatenate([w, jnp.zeros((heads, LANES - w.shape[1], d), w.dtype)], axis=1).reshape(heads * LANES, d)


def _mix_out(lam, h, fox, oc, os_, ow, dd, gates, subln_w, w_out, diff_scale, tm=512):
    s, d = h.shape
    nf, nn = FOX_HEADS * FOX_HD, NSA_HEADS * NSA_HD
    wf = _slot_rows(w_out[:nf], FOX_HEADS).astype(BF16)
    wn = _slot_rows(w_out[nf:nf + nn], NSA_HEADS).astype(BF16)
    wd = _slot_rows(w_out[nf + nn:], DIFF_HEADS).astype(BF16)
    sw = jnp.concatenate([subln_w.reshape(1, -1), jnp.zeros((1, LANES - DIFF_V), F32)], axis=1)
    row = lambda w: pl.BlockSpec((tm, w), lambda i: (i, 0))
    return pl.pallas_call(
        functools.partial(_mix_out_kernel, diff_scale=diff_scale),
        grid=(s // tm,),
        in_specs=[pl.BlockSpec(memory_space=pltpu.SMEM), row(d), row(fox.shape[1]), row(oc.shape[1]),
                  row(os_.shape[1]), row(ow.shape[1]), row(dd.shape[1]), row(LANES), _full((1, LANES)),
                  _full(wf.shape), _full(wn.shape), _full(wd.shape)],
        out_specs=row(d),
        out_shape=jax.ShapeDtypeStruct((s, d), F32),
        compiler_params=_cparams(("parallel",)),
        name="mix_out",
    )(lam, h, fox, oc, os_, ow, dd, gates, sw, wf, wn, wd)


def _peer_query_kernel(h_ref, nw_ref, wq_ref, keys_ref, xnT_o, scT_o):
    xn = _rms(h_ref[...], nw_ref[...])
    xnT_o[...] = xn.T.astype(BF16)
    q = jnp.dot(xn.astype(BF16), wq_ref[...], preferred_element_type=F32).astype(BF16)
    half = PEER_DQ // 2
    for b in range(2 * PEER_HEADS):
        sc = jnp.dot(q[:, b * half:(b + 1) * half], keys_ref[b], preferred_element_type=F32)
        scT_o[b] = sc.T


def _peer_query(h, nw, wq_bf16, keysT_bf16, tm=512):
    s, d = h.shape
    n = wq_bf16.shape[1]
    nb = 2 * PEER_HEADS
    return pl.pallas_call(
        _peer_query_kernel,
        grid=(s // tm,),
        in_specs=[pl.BlockSpec((tm, d), lambda i: (i, 0)), _full((1, d)), _full((d, n)), _full(keysT_bf16.shape)],
        out_specs=[pl.BlockSpec((d, tm), lambda i: (0, i)), pl.BlockSpec((nb, PEER_NKEYS, tm), lambda i: (0, 0, i))],
        out_shape=[jax.ShapeDtypeStruct((d, s), BF16), jax.ShapeDtypeStruct((nb, PEER_NKEYS, s), F32)],
        compiler_params=_cparams(("parallel",)),
        name="peer_query",
    )(h, nw.reshape(1, d), wq_bf16, keysT_bf16)


def _top16_rows(s, exact_ties):
    n = s.shape[0]
    row = lax.broadcasted_iota(jnp.int32, s.shape, 0)
    rank = jnp.full(s.shape, float(PEER_TOPK), dtype=F32)
    work = s
    vals = []
    for r in range(PEER_TOPK):
        mx = jnp.max(work, axis=0, keepdims=True)
        if exact_ties:
            pick = row == jnp.min(jnp.where(work == mx, row, n), axis=0, keepdims=True)
        else:
            pick = work == mx
        rank = jnp.where(pick, float(r), rank)
        work = jnp.where(pick, NEG_INIT, work)
        vals.append(mx)
    return rank, vals


def _peer_route(s1, s2, exact_ties):
    rank1, v1 = _top16_rows(s1, exact_ties)
    rank2, v2 = _top16_rows(s2, exact_ties)
    k = PEER_TOPK
    tm = s1.shape[1]
    r16 = lax.broadcasted_iota(jnp.int32, (k, tm), 0)
    v1m = jnp.zeros((k, tm), F32)
    v2m = jnp.zeros((k, tm), F32)
    for r in range(k):
        v1m = jnp.where(r16 == r, v1[r], v1m)
        v2m = jnp.where(r16 == r, v2[r], v2m)
    ea = jnp.exp(v1m - v1[0])
    eb = jnp.exp(v2m - v2[0])
    cands, gates = [v1m[0:1] + v2m], [ea[0:1] * eb]
    for r in range(1, 8):
        cands.append(v1m[r:r + 1] + v2m[0:8])
        gates.append(ea[r:r + 1] * eb[0:8])
    cands.append(v1m[8:16] + v2m[0:1])
    gates.append(ea[8:16] * eb[0:1])
    cand = jnp.concatenate(cands, axis=0)
    gate = jnp.concatenate(gates, axis=0)
    ncand = cand.shape[0]
    crow = lax.broadcasted_iota(jnp.int32, cand.shape, 0)
    for r in range(2, 8):
        start = 16 + 8 * (r - 1)
        cand = jnp.where((crow >= start + k // (r + 1)) & (crow < start + 8), NEG_INIT, cand)
    work = cand
    picked = jnp.zeros(cand.shape, dtype=jnp.bool_)
    for _ in range(k):
        mx = jnp.max(work, axis=0, keepdims=True)
        if exact_ties:
            pick = crow == jnp.min(jnp.where(work == mx, crow, ncand), axis=0, keepdims=True)
        else:
            pick = work == mx
        picked = picked | pick
        work = jnp.where(pick, NEG_INIT, work)
    pf = picked.astype(F32)
    z = jnp.sum(pf * gate, axis=0, keepdims=True)
    cnt = [jnp.sum(pf[0:16, :], axis=0, keepdims=True)]
    cnt += [jnp.sum(pf[8 + 8 * r:16 + 8 * r, :], axis=0, keepdims=True) for r in range(1, 8)]
    cnt += [pf[64 + r:65 + r, :] for r in range(8, 16)]
    cmap = jnp.zeros(s1.shape, dtype=F32)
    for r in range(k):
        cmap = jnp.where(rank1 == float(r), cnt[r], cmap)
    in1 = rank1 < float(k)
    in2 = rank2 < float(k)
    a = jnp.where(in1, jnp.exp(s1 - v1[0]), 0.0) / z
    b = jnp.where(in2, jnp.exp(s2 - v2[0]), 0.0)
    excess = (jnp.abs(jnp.sum(in1.astype(F32), axis=0, keepdims=True) - k)
              + jnp.abs(jnp.sum(in2.astype(F32), axis=0, keepdims=True) - k)
              + jnp.abs(jnp.sum(pf, axis=0, keepdims=True) - k))
    return a, cmap, b, rank2, excess


def _peer_topk_kernel(sc_ref, a_o, c_o, b_o, r_o):
    def run(exact_ties):
        a, c, b, rank2, excess = _peer_route(sc_ref[0], sc_ref[1], exact_ties)
        a_o[0] = a
        c_o[0] = c
        b_o[0] = b.astype(BF16)
        r_o[0] = rank2.astype(BF16)
        return excess

    excess = run(False)

    @pl.when(jnp.max(excess) > 0.0)
    def _():
        run(True)


def _peer_topk(scT, tm=256):
    n2, nk, s = scT.shape
    heads = n2 // 2
    ospec = pl.BlockSpec((1, nk, tm), lambda h, t: (h, 0, t))
    return pl.pallas_call(
        _peer_topk_kernel,
        grid=(heads, s // tm),
        in_specs=[pl.BlockSpec((2, nk, tm), lambda h, t: (h, 0, t))],
        out_specs=[ospec] * 4,
        out_shape=[jax.ShapeDtypeStruct((heads, nk, s), dt) for dt in (F32, F32, BF16, BF16)],
        compiler_params=_cparams(("parallel", "parallel")),
        name="peer_topk",
    )(scT)


def _gelu_exact(x):
    return 0.5 * x * (1.0 + lax.erf(x * (2.0 ** -0.5)))


def _peer_main_kernel(xT_ref, u_ref, v_ref, a_ref, c_ref, b_ref, r_ref, o_ref, w_sc, *, ec, sub):
    ci = pl.program_id(1)

    @pl.when(ci == 0)
    def _():
        o_ref[...] = jnp.zeros_like(o_ref)

    nk = PEER_NKEYS
    tm = xT_ref.shape[1]

    def row(ref, h, i1, lanes):
        x16 = jnp.broadcast_to(ref[h, pl.ds(i1, 1), :][:, lanes], (16, GATE_LANES)).astype(BF16)
        return jnp.concatenate([x16] * (nk // 16), axis=0)

    acc = None
    for sc in range(ec // sub):
        hid = jnp.dot(u_ref[sc * sub:(sc + 1) * sub, :], xT_ref[...], preferred_element_type=F32)
        act = _gelu_exact(hid).astype(BF16)
        for ii in range(sub // nk):
            slab = sc * (sub // nk) + ii
            i1 = ci * (ec // nk) + slab
            for l0 in range(0, tm, GATE_LANES):
                lanes = slice(l0, l0 + GATE_LANES)
                w = None
                for h in range(PEER_HEADS):
                    keep = r_ref[h, :, lanes] < row(c_ref, h, i1, lanes)
                    term = jnp.where(keep, b_ref[h, :, lanes], 0.0) * row(a_ref, h, i1, lanes)
                    w = term if w is None else w + term
                w_sc[slab * nk:(slab + 1) * nk, lanes] = w * act[ii * nk:(ii + 1) * nk, lanes]
        part = jnp.dot(w_sc[sc * sub:(sc + 1) * sub, :].T, v_ref[sc * sub:(sc + 1) * sub, :],
                       preferred_element_type=F32)
        acc = part if acc is None else acc + part
    o_ref[...] += acc


def _peer_main(xnT, u_bf16, v_bf16, a, c, b, r, tm=512, ec=2048, sub=512):
    d, s = xnT.shape
    e = u_bf16.shape[0]
    heads, nk, _ = a.shape
    rt = pl.BlockSpec((heads, nk, tm), lambda t, ci: (0, 0, t))
    wt = pl.BlockSpec((ec, d), lambda t, ci: (ci, 0))
    return pl.pallas_call(
        functools.partial(_peer_main_kernel, ec=ec, sub=sub),
        grid=(s // tm, e // ec),
        in_specs=[pl.BlockSpec((d, tm), lambda t, ci: (0, t)), wt, wt, rt, rt, rt, rt],
        out_specs=pl.BlockSpec((tm, d), lambda t, ci: (t, 0)),
        out_shape=jax.ShapeDtypeStruct((s, d), F32),
        scratch_shapes=[pltpu.VMEM((ec, tm), BF16)],
        compiler_params=_cparams(("parallel", "arbitrary")),
        name="peer_main",
    )(xnT, u_bf16, v_bf16, a, c, b, r)


def _ple_kernel(h_ref, peer_ref, p_ref, nw_ref, wg_ref, wp_ref, o_ref):
    h2 = h_ref[...] + peer_ref[...]
    xn = _rms(h2, nw_ref[...]).astype(BF16)
    gate = 1.0 / (1.0 + jnp.exp(-jnp.dot(xn, wg_ref[...], preferred_element_type=F32)))
    emb = jnp.dot(p_ref[...].astype(BF16), wp_ref[...], preferred_element_type=F32)
    o_ref[...] = h2 + gate * emb


def _ple(h, peer, p, nw, wg_bf16, wp_bf16, tm=512):
    s, d = h.shape
    pd = p.shape[1]
    row = lambda w: pl.BlockSpec((tm, w), lambda i: (i, 0))
    return pl.pallas_call(
        _ple_kernel,
        grid=(s // tm,),
        in_specs=[row(d), row(d), row(pd), _full((1, d)), _full((d, d)), _full((pd, d))],
        out_specs=row(d),
        out_shape=jax.ShapeDtypeStruct((s, d), F32),
        compiler_params=_cparams(("parallel",)),
        name="ple",
    )(h, peer, p, nw.reshape(1, d), wg_bf16, wp_bf16)


def _split_w_in(w):
    f0 = 3 * FOX_HEADS * FOX_HD
    n0 = f0 + FOX_HEADS
    g0 = n0 + _SEG["dq"][0] - _SEG["nq"][0]
    d0 = g0 + 3 * NSA_HEADS
    misc = jnp.concatenate([w[:, f0:n0], w[:, g0:d0], jnp.zeros((w.shape[0], LANES - (n0 - f0) - (d0 - g0)), w.dtype)],
                           axis=1)
    return [t.astype(BF16) for t in (w[:, :f0], w[:, n0:g0], w[:, d0:], misc)]


def _overlap_T(s):
    n = np.arange(s // CMP_STRIDE)[None, :] * CMP_STRIDE
    m = np.arange(s // SLC_LEN)[:, None] * SLC_LEN
    return jnp.asarray(((n < m + SLC_LEN) & (n + CMP_LEN > m)).astype(np.float32), dtype=BF16)


def _layer(h, p_i, tabs, layer, attn_norm_w, w_in, fox_f_bias, fox_q_norm_w, fox_k_norm_w, nsa_q_norm_w,
           nsa_k_norm_w, nsa_cmp_pos, nsa_cmp_w, diff_q_norm_w, diff_k_norm_w, diff_lambda, diff_subln_w,
           w_out, ffn_norm_w, peer_w_q, peer_sub_keys, peer_u, peer_v, ple_norm_w, ple_w_gate, ple_w_proj):
    s = h.shape[0]
    fb =jnp.zeros((1, LANES), F32).at[0, :FOX_HEADS].set(fox_f_bias)
    u_f = _score_bound(fox_q_norm_w, fox_k_norm_w, FOX_HD, FOX_HD ** -0.5)
    u_n = _score_bound(nsa_q_norm_w, nsa_k_norm_w, NSA_HD, NSA_HD ** -0.5)
    u_d = _score_bound(diff_q_norm_w, diff_k_norm_w, DIFF_QK, DIFF_QK ** -0.5)
    shifts = SHIFT_HEADROOM - jnp.stack([u_f, u_n, u_d]).astype(F32)
    (fq, fk, fv, nqn, nqr, kvc, ks, vs, kw, vw, gates, dq, dk, dv) = _prep(
        shifts, h, attn_norm_w, _split_w_in(w_in), tabs, fb, fox_q_norm_w, fox_k_norm_w, nsa_q_norm_w, nsa_k_norm_w, diff_q_norm_w, diff_k_norm_w)

    o_fox = _flash(fq, fk, fv, bound_ok=u_f <= SCORE_BOUND, groups=1, tq=Q_LANES, tk=TK_CAUSAL)

    tq_n = Q_LANES // NSA_G
    nsa_ok = u_n <= SCORE_BOUND
    kcx, vcx = _compress(kvc, nsa_cmp_w, nsa_cmp_pos, nsa_k_norm_w)
    o_c, sel = _nsa_cmp(nqn, kcx, vcx, _overlap_T(s), tq_n, nsa_ok)
    o_s = _flash(nqr, ks, vs, sel, bound_ok=nsa_ok, groups=NSA_G, tq=tq_n, tk=TK_CAUSAL)
    o_w = _flash(nqr, kw, vw, bound_ok=nsa_ok, groups=NSA_G, tq=tq_n, tk=512, window=WIN)

    o_d = _flash(dq, dk, dv, bound_ok=u_d <= SCORE_BOUND, groups=2, tq=Q_LANES // 2, tk=TK_CAUSAL)
    lv = diff_lambda.astype(F32)
    lam_init = 0.8 - 0.6 * math.exp(-0.3 * layer)
    lam = (jnp.exp(jnp.sum(lv[0] * lv[1])) - jnp.exp(jnp.sum(lv[2] * lv[3])) + lam_init).reshape(1)
    h1 = _mix_out(lam, h, o_fox, o_c, o_s, o_w, o_d, gates, diff_subln_w, w_out, 1.0 - lam_init)

    keysT = peer_sub_keys.reshape(2 * PEER_HEADS, PEER_NKEYS, PEER_DQ // 2).transpose(0, 2, 1).astype(BF16)
    xnT, scT = _peer_query(h1, ffn_norm_w, peer_w_q.astype(BF16), keysT)
    a, c, b, r = _peer_topk(scT)
    peer = _peer_main(xnT, peer_u.astype(BF16), peer_v.astype(BF16), a, c, b, r)

    return _ple(h1, peer, p_i, ple_norm_w, ple_w_gate.astype(BF16), ple_w_proj.astype(BF16))


def kernel(x, p, positions, attn_norm_w, w_in, fox_f_bias, fox_q_norm_w, fox_k_norm_w, nsa_q_norm_w, nsa_k_norm_w,
           nsa_cmp_pos, nsa_cmp_w, diff_q_norm_w, diff_k_norm_w, diff_lambda, diff_subln_w, w_out, ffn_norm_w,
           peer_w_q, peer_sub_keys, peer_u, peer_v, ple_norm_w, ple_w_gate, ple_w_proj):
    b, s, d = x.shape
    assert b == 1 and d == D_MODEL and s % Q_LANES == 0 and s % TK_CAUSAL == 0
    tabs = _rope_tables(positions)
    h = x.reshape(s, d)
    per_layer = (attn_norm_w, w_in, fox_f_bias, fox_q_norm_w, fox_k_norm_w, nsa_q_norm_w, nsa_k_norm_w, nsa_cmp_pos,
                 nsa_cmp_w, diff_q_norm_w, diff_k_norm_w, diff_lambda, diff_subln_w, w_out, ffn_norm_w, peer_w_q,
                 peer_sub_keys, peer_u, peer_v, ple_norm_w, ple_w_gate, ple_w_proj)
    for layer in range(attn_norm_w.shape[0]):
        h = _layer(h, p[layer, 0], tabs, layer, *(w[layer] for w in per_layer))
    return h.reshape(b, s, d)
```

```python
import functools
import math

import numpy as np
import jax
import jax.numpy as jnp
from jax import lax
from jax.experimental import pallas as pl
from jax.experimental.pallas import tpu as pltpu

F32 = jnp.float32
BF16 = jnp.bfloat16
HIGHEST = lax.Precision.HIGHEST

D_MODEL = 1024
PLE_DIM = 256
ROPE_THETA = 10000.0
EPS = 1e-6
FOX_HEADS, FOX_HD = 4, 64
NSA_HEADS, NSA_KV, NSA_HD = 8, 2, 64
NSA_G = NSA_HEADS // NSA_KV
CMP_LEN, CMP_STRIDE, SLC_LEN, SLC_TOPK, WIN = 32, 16, 64, 16, 512
FORCE_BONUS = 1.0e4
DIFF_HEADS, DIFF_QK, DIFF_V = 4, 32, 64
PEER_HEADS, PEER_NKEYS, PEER_DQ, PEER_TOPK = 8, 128, 256, 16
PEER_EXPERTS = PEER_NKEYS * PEER_NKEYS

LANES = 128
LOG2E = math.log2(math.e)
NEG_INIT = -1.0e30
MASKVAL = -2.0e30
VMEM_LIMIT = 56 * 1024 * 1024
TK_CAUSAL = 2048
Q_LANES = 2048
GATE_LANES = 512

_SEG = dict(fq=(0, 256), fk=(256, 256), fv=(512, 256), nq=(768, 512), nkc=(1280, 128), nvc=(1408, 128),
            nks=(1536, 128), nvs=(1664, 128), nkw=(1792, 128), nvw=(1920, 128),
            dq=(2048, 256), dk=(2304, 256), dv=(2560, 256), misc=(2816, 128))
PROJ_W = 2944


def _cparams(sem):
    return pltpu.CompilerParams(dimension_semantics=sem, vmem_limit_bytes=VMEM_LIMIT)


def _full(shape):
    n = len(shape)
    return pl.BlockSpec(shape, lambda *_: (0,) * n)


def _rope_tab_kernel(pos_ref, f64_ref, g64_ref, f32_ref, g32_ref, c64_o, s64_o, c32_o, s32_o):
    pos = pos_ref[...].astype(F32)
    a64 = pos * f64_ref[...]
    c64_o[...] = jnp.cos(a64)
    s64_o[...] = jnp.sin(a64) * g64_ref[...]
    a32 = pos * f32_ref[...]
    c32_o[...] = jnp.cos(a32)
    s32_o[...] = jnp.sin(a32) * g32_ref[...]


def _rope_tables(positions):
    s = positions.shape[-1]
    pos = positions.reshape(s, 1)
    lane = np.arange(LANES)

    def lanes(half):
        inv = ROPE_THETA ** (-jnp.arange(half, dtype=F32) / half)
        freq = inv[(lane % (2 * half)) % half].reshape(1, LANES)
        sign = np.where((lane % (2 * half)) < half, -1.0, 1.0).astype(np.float32).reshape(1, LANES)
        return freq, jnp.asarray(sign)

    f64, g64 = lanes(NSA_HD // 2)
    f32_, g32 = lanes(DIFF_QK // 2)
    tm = 512
    out = jax.ShapeDtypeStruct((s, LANES), F32)
    row = pl.BlockSpec((tm, LANES), lambda i: (i, 0))
    return pl.pallas_call(
        _rope_tab_kernel,
        grid=(s // tm,),
        in_specs=[pl.BlockSpec((tm, 1), lambda i: (i, 0))] + [_full((1, LANES))] * 4,
        out_specs=[row] * 4,
        out_shape=[out] * 4,
        compiler_params=_cparams(("parallel",)),
        name="rope_tables",
    )(pos, f64, g64, f32_, g32)


def _rms(x, w):
    return x * lax.rsqrt(jnp.mean(x * x, axis=-1, keepdims=True) + EPS) * w


def _seg_rms(x, bmat, seg):
    outs = []
    for c0 in range(0, x.shape[1], 256):
        w = min(256, x.shape[1] - c0)
        xc = x[:, c0:c0 + w]
        sq = xc * xc
        hi = sq.astype(BF16)
        lo = (sq - hi.astype(F32)).astype(BF16)
        b = bmat[:w, :w].astype(BF16)
        ss = jnp.dot(hi, b, preferred_element_type=F32) + jnp.dot(lo, b, preferred_element_type=F32)
        outs.append(xc * lax.rsqrt(ss * (1.0 / seg) + EPS))
    return outs[0] if len(outs) == 1 else jnp.concatenate(outs, axis=1)


def _tile_lanes(t, width):
    reps = width // LANES
    return t if reps == 1 else jnp.concatenate([t] * reps, axis=1)


def _rope(x, cos, sin, half):
    width = x.shape[1]
    left = pltpu.roll(x, width - half, 1)
    right = pltpu.roll(x, half, 1)
    lane = lax.broadcasted_iota(jnp.int32, x.shape, 1)
    swapped = jnp.where((lane & (2 * half - 1)) < half, left, right)
    return x * _tile_lanes(cos, width) + swapped * _tile_lanes(sin, width)


def _lane_fill(tm, cols):
    lane = lax.broadcasted_iota(jnp.int32, (tm, LANES), 1)
    out = jnp.zeros((tm, LANES), F32)
    for l, v in cols.items():
        out = jnp.where(lane == l, v, out)
    return out


def _slots(x, extras):
    lane = lax.broadcasted_iota(jnp.int32, (x.shape[0], LANES), 1)
    outs = []
    for h, ex in enumerate(extras):
        col = x[:, (h // 2) * LANES:(h // 2 + 1) * LANES]
        if h % 2:
            col = pltpu.roll(col, LANES // 2, 1)
        outs.append(jnp.where(lane < LANES // 2, col, ex))
    return jnp.concatenate(outs, axis=1)


def _prep_kernel(shift_ref, proj_ref, c64_ref, s64_ref, c32_ref, s32_ref, fb_ref, wfq_ref, wfk_ref, wnq_ref,
                 wnk_ref, wdq_ref, wdk_ref, b64_ref, b32_ref,
                 fq_o, fk_o, fv_o, nqn_o, nqr_o, kvc_o, ks_o, vs_o, kw_o, vw_o, gate_o, dq_o, dk_o, dv_o, cs_o,
                 carry_sc, *, tm):
    def seg(name):
        c0, w = _SEG[name]
        return proj_ref[:, c0:c0 + w]

    b64 = b64_ref[...]
    b32 = b32_ref[...]
    c64, s64, c32, s32 = c64_ref[...], s64_ref[...], c32_ref[...], s32_ref[...]
    one_ex = _lane_fill(tm, {COL_SHIFT: 1.0})

    misc = seg("misc")
    gate_o[...] = 1.0 / (1.0 + jnp.exp(-misc))
    t = misc + fb_ref[...]
    logf = jnp.minimum(t, 0.0) - jnp.log1p(jnp.exp(-jnp.abs(t)))

    @pl.when(pl.program_id(0) == 0)
    def _():
        carry_sc[...] = jnp.zeros_like(carry_sc)

    r = lax.broadcasted_iota(jnp.int32, (tm, tm), 0)
    c = lax.broadcasted_iota(jnp.int32, (tm, tm), 1)
    tri = (c <= r).astype(F32)
    csum = jnp.dot(tri, logf, precision=HIGHEST, preferred_element_type=F32) + carry_sc[0:1, :]
    carry_sc[...] = jnp.broadcast_to(csum[tm - 1:tm, :], carry_sc.shape)
    csum = csum * LOG2E
    hi = csum.astype(BF16).astype(F32)
    r1 = csum - hi
    mid = r1.astype(BF16).astype(F32)
    lo = (r1 - mid).astype(BF16).astype(F32)
    cs_o[...] = csum

    fq = _seg_rms(seg("fq"), b64, FOX_HD) * wfq_ref[...] * (FOX_HD ** -0.5 * LOG2E)
    fk = _seg_rms(seg("fk"), b64, FOX_HD) * wfk_ref[...]
    q_ex, k_ex = [], []
    for h in range(FOX_HEADS):
        ch, cm, cl = hi[:, h:h + 1], mid[:, h:h + 1], lo[:, h:h + 1]
        q_ex.append(_lane_fill(tm, {COL_SHIFT: shift_ref[0], 65: 1.0, 66: 1.0, 67: 1.0, 68: ch, 69: cm, 70: cl}))
        k_ex.append(_lane_fill(tm, {COL_SHIFT: 1.0, 65: -ch, 66: -cm, 67: -cl, 68: 1.0, 69: 1.0, 70: 1.0}))
    fq_o[...] = _slots(fq, q_ex)
    fk_o[...] = _slots(fk, k_ex).astype(BF16)
    fv_o[...] = _slots(seg("fv"), [one_ex] * FOX_HEADS).astype(BF16)

    n_ex = [_lane_fill(tm, {COL_SHIFT: shift_ref[1]})] * NSA_HEADS
    nqn = _seg_rms(seg("nq"), b64, NSA_HD) * wnq_ref[...]
    nqn_o[...] = _slots(nqn * (NSA_HD ** -0.5 * LOG2E), n_ex)
    nqr_o[...] = _slots(_rope(nqn, c64, s64, NSA_HD // 2) * (NSA_HD ** -0.5 * LOG2E), n_ex)
    kvc_o[:, 0:128] = seg("nkc").astype(BF16)
    kvc_o[:, 128:256] = seg("nvc").astype(BF16)
    wnk = wnk_ref[...]
    lane = lax.broadcasted_iota(jnp.int32, (tm, LANES), 1)
    row = pl.program_id(0) * tm + lax.broadcasted_iota(jnp.int32, (tm, LANES), 0)
    blk_in_tile = (row & (TK_CAUSAL - 1)) // SLC_LEN
    sel_ex = jnp.where(lane == COL_SEL + blk_in_tile, 1.0, one_ex)
    ks = _rope(_seg_rms(seg("nks"), b64, NSA_HD) * wnk, c64, s64, NSA_HD // 2)
    ks_o[...] = _slots(ks, [sel_ex] * NSA_KV).astype(BF16)
    vs_o[...] = _slots(seg("nvs"), [one_ex] * NSA_KV).astype(BF16)
    kw = _rope(_seg_rms(seg("nkw"), b64, NSA_HD) * wnk, c64, s64, NSA_HD // 2)
    kw_o[...] = _slots(kw, [one_ex] * NSA_KV).astype(BF16)
    vw_o[...] = _slots(seg("nvw"), [one_ex] * NSA_KV).astype(BF16)

    dq = _rope(_seg_rms(seg("dq"), b32, DIFF_QK) * wdq_ref[...], c32, s32, DIFF_QK // 2) * (DIFF_QK ** -0.5 * LOG2E)
    d_ex = _lane_fill(tm, {COL_SHIFT: shift_ref[2]})
    d_slots = []
    for h in range(DIFF_HEADS):
        col = dq[:, (h // 2) * LANES:(h // 2 + 1) * LANES]
        if h % 2:
            col = pltpu.roll(col, LANES // 2, 1)
        d_slots.append(jnp.where(lane < DIFF_QK, col, d_ex))
        d_slots.append(jnp.where((lane >= DIFF_QK) & (lane < 2 * DIFF_QK), col, d_ex))
    dq_o[...] = jnp.concatenate(d_slots, axis=1)
    dk = _rope(_seg_rms(seg("dk"), b32, DIFF_QK) * wdk_ref[...], c32, s32, DIFF_QK // 2)
    dk_o[...] = _slots(dk, [one_ex] * DIFF_HEADS).astype(BF16)
    dv_o[...] = _slots(seg("dv"), [one_ex] * DIFF_HEADS).astype(BF16)


def _block_diag_ones(n, seg):
    i = np.arange(n)
    return jnp.asarray((i[:, None] // seg == i[None, :] // seg).astype(np.float32))


def _proj_prep_kernel(shift_ref, x_ref, nw_ref, wa_ref, wb_ref, wc_ref, wm_ref, *rest, tm):
    proj_sc = rest[-1]
    xn = _rms(x_ref[...], nw_ref[...]).astype(BF16)
    c0 = 0
    for w_ref in (wa_ref, wb_ref, wc_ref, wm_ref):
        n = w_ref.shape[1]
        proj_sc[:, c0:c0 + n] = jnp.dot(xn, w_ref[...], preferred_element_type=F32)
        c0 += n
    _prep_kernel(shift_ref, proj_sc, *rest[:-1], tm=tm)


def _prep(shifts, x, nw, ws_bf16, tabs, fb, wfq, wfk, wnq, wnk, wdq, wdk, tm=256):
    s, d = x.shape
    c64, s64, c32, s32 = tabs
    b64 = _block_diag_ones(256, 64)
    b32 = _block_diag_ones(256, 32)

    def tiled(w, width):
        return jnp.tile(w.reshape(1, -1), (1, width // w.shape[-1]))

    consts = [fb, tiled(wfq, 256), tiled(wfk, 256), tiled(wnq, 512), tiled(wnk, 128), tiled(wdq, 256),
              tiled(wdk, 256), b64, b32]
    outs = [(FOX_HEADS, F32), (FOX_HEADS, BF16), (FOX_HEADS, BF16),
            (NSA_HEADS, F32), (NSA_HEADS, F32), (2, BF16),
            (NSA_KV, BF16), (NSA_KV, BF16), (NSA_KV, BF16), (NSA_KV, BF16), (1, F32),
            (2 * DIFF_HEADS, F32), (DIFF_HEADS, BF16), (DIFF_HEADS, BF16),
            (1, F32)]
    row = lambda w: pl.BlockSpec((tm, w), lambda i: (i, 0))
    assert sum(w.shape[1] for w in ws_bf16) == PROJ_W
    return pl.pallas_call(
        functools.partial(_proj_prep_kernel, tm=tm),
        grid=(s // tm,),
        in_specs=[pl.BlockSpec(memory_space=pltpu.SMEM), row(d), _full((1, d))] + [_full(w.shape) for w in ws_bf16]
                 + [row(LANES)] * 4 + [_full(c.shape) for c in consts],
        out_specs=[row(n * LANES) for n, _ in outs],
        out_shape=[jax.ShapeDtypeStruct((s, n * LANES), dt) for n, dt in outs],
        scratch_shapes=[pltpu.VMEM((8, LANES), F32), pltpu.VMEM((tm, PROJ_W), F32)],
        compiler_params=_cparams(("arbitrary",)),
        name="proj_head_prep",
    )(shifts, x, nw.reshape(1, d), *ws_bf16, c64, s64, c32, s32, *consts)


QK_DIM = 128
COL_SHIFT = 64
COL_SEL = 80
SCORE_BOUND = 60.0
SHIFT_HEADROOM = 60.0
EXP2_FLUSH = 200.0


def _flash_kernel(code_ref, q_ref, k_ref, v_ref, *rest, groups, tq, tk, window, has_sel, bounded, decay_tiles):
    rest = list(rest)
    sel_ref = rest.pop(0) if has_sel else None
    decay_ref = rest.pop(0) if decay_tiles else None
    o_ref = rest.pop(0)
    qT_sc = rest.pop(0)
    m_sc = None if bounded else rest.pop(0)
    acc_sc = rest.pop(0)
    code = code_ref[pl.program_id(1)]
    i = code & 0xFFF
    j = (code >> 12) & 0xFFF
    first = (code >> 24) & 1
    last = (code >> 25) & 1
    rows = groups * tq

    @pl.when(first == 1)
    def _():
        for g in range(groups):
            qT_sc[:, g * tq:(g + 1) * tq] = q_ref[:, g * QK_DIM:(g + 1) * QK_DIM].T.astype(BF16)
        if not bounded:
            m_sc[...] = jnp.full_like(m_sc, NEG_INIT)
        acc_sc[...] = jnp.zeros_like(acc_sc)

    def step(masked):
        if has_sel:
            qT_sc[COL_SEL:COL_SEL + tk // SLC_LEN, :] = _tile_lanes_any(sel_ref[0], groups)
        s = jnp.dot(k_ref[...], qT_sc[...], preferred_element_type=F32)
        if masked:
            kpos = j * tk + lax.broadcasted_iota(jnp.int32, (tk, rows), 0)
            lane = lax.broadcasted_iota(jnp.int32, (tk, rows), 1)
            qpos = i * tq + (lane & (tq - 1))
            keep = kpos <= qpos
            if window is not None:
                keep = keep & (kpos > qpos - window)
            s = jnp.where(keep, s, MASKVAL)
        v_t = (((0,), (0,)), ((), ()))
        v = v_ref[...]
        if bounded:
            acc_sc[...] += lax.dot_general(v, jnp.exp2(s).astype(BF16), v_t, preferred_element_type=F32)
        else:
            m_prev = m_sc[...]
            m_new = jnp.maximum(m_prev, jnp.max(s, axis=0, keepdims=True))
            p = jnp.exp2(s - m_new).astype(BF16)
            acc_sc[...] = (acc_sc[...] * jnp.exp2(m_prev - m_new)
                           + lax.dot_general(v, p, v_t, preferred_element_type=F32))
            m_sc[...] = m_new

    if window is not None:
        step(True)
    else:
        needs_mask = (j + 1) * tk - 1 > i * tq

        @pl.when(needs_mask)
        def _():
            step(True)

        @pl.when(jnp.logical_not(needs_mask))
        def _():
            if decay_tiles:
                nq, nk = decay_tiles
                bias_max = decay_ref[(pl.program_id(0) * nq + i) * nk + j]

                @pl.when(SHIFT_HEADROOM + bias_max > -EXP2_FLUSH)
                def _():
                    step(False)
            else:
                step(False)

    @pl.when(last == 1)
    def _():
        l = acc_sc[COL_SHIFT:COL_SHIFT + 1, :]
        o = acc_sc[...] / jnp.where(l > 0.0, l, 1.0)
        for g in range(groups):
            o_ref[:, g * QK_DIM:(g + 1) * QK_DIM] = o[:, g * tq:(g + 1) * tq].T


def _tile_lanes_any(t, reps):
    return t if reps == 1 else jnp.concatenate([t] * reps, axis=1)


def _flash_call(q, k, v, sel, decay, *, groups, tq, tk, window, bounded):
    s_len = q.shape[0]
    hkv = k.shape[1] // QK_DIM
    nq = s_len // tq
    rows = groups * tq
    codes = []
    for i in range(nq):
        q_lo, q_hi = i * tq, i * tq + tq - 1
        j_hi = q_hi // tk
        j_lo = 0 if window is None else max(0, (q_lo - window + 1) // tk)
        for j in range(j_lo, j_hi + 1):
            codes.append(i | (j << 12) | (int(j == j_lo) << 24) | (int(j == j_hi) << 25))
    codes = jnp.asarray(np.asarray(codes, dtype=np.int32))
    nsteps = codes.shape[0]

    def ti(c, s):
        return c[s] & 0xFFF

    def tj(c, s):
        return (c[s] >> 12) & 0xFFF

    in_specs = [
        pl.BlockSpec((tq, groups * QK_DIM), lambda h, s, c: (ti(c, s), h)),
        pl.BlockSpec((tk, QK_DIM), lambda h, s, c: (tj(c, s), h)),
        pl.BlockSpec((tk, QK_DIM), lambda h, s, c: (tj(c, s), h)),
    ]
    args = [q, k, v]
    if sel is not None:
        in_specs.append(pl.BlockSpec((1, tk // SLC_LEN, tq), lambda h, s, c: (h, tj(c, s), ti(c, s))))
        args.append(sel)
    use_decay = decay is not None and bounded
    if use_decay:
        in_specs.append(pl.BlockSpec(memory_space=pltpu.SMEM))
        args.append(decay)
    scratch = [pltpu.VMEM((QK_DIM, rows), BF16)]
    if not bounded:
        scratch.append(pltpu.VMEM((1, rows), F32))
    scratch.append(pltpu.VMEM((QK_DIM, rows), F32))
    kern = functools.partial(_flash_kernel, groups=groups, tq=tq, tk=tk, window=window, has_sel=sel is not None,
                             bounded=bounded, decay_tiles=(nq, s_len // tk) if use_decay else None)
    return pl.pallas_call(
        kern,
        grid_spec=pltpu.PrefetchScalarGridSpec(
            num_scalar_prefetch=1,
            grid=(hkv, nsteps),
            in_specs=in_specs,
            out_specs=pl.BlockSpec((tq, groups * QK_DIM), lambda h, s, c: (ti(c, s), h)),
            scratch_shapes=scratch,
        ),
        out_shape=jax.ShapeDtypeStruct(q.shape, F32),
        compiler_params=_cparams(("parallel", "arbitrary")),
        name="flash_bounded" if bounded else "flash_online",
    )(codes, *args)


def _flash(q, k, v, sel=None, decay=None, *, bound_ok, groups, tq, tk, window=None):
    call = functools.partial(_flash_call, q, k, v, sel, decay, groups=groups, tq=tq, tk=tk, window=window)
    return lax.cond(bound_ok, lambda: call(bounded=True), lambda: call(bounded=False))


def _score_bound(wq, wk, seg, scale):
    return 1.02 * seg * jnp.max(jnp.abs(wq)) * jnp.max(jnp.abs(wk)) * scale * LOG2E


def _compress_kernel(r_ref, wa_ref, wb_ref, pa_ref, pb_ref, nw_ref, b64_ref, kx_o, vx_o):
    n = r_ref.shape[0]
    r = r_ref[...]
    wa, wb = wa_ref[...], wb_ref[...]
    first = jnp.dot(r, wa.astype(BF16), preferred_element_type=F32)
    second = jnp.dot(r, wb.astype(BF16), preferred_element_type=F32)
    nxt = pltpu.roll(second, n - 1, 0)
    rowi = lax.broadcasted_iota(jnp.int32, nxt.shape, 0)
    nxt = jnp.where(rowi == n - 1, 0.0, nxt)
    const = (jnp.dot(pa_ref[...], wa, precision=HIGHEST, preferred_element_type=F32)
             + jnp.dot(pb_ref[...], wb, precision=HIGHEST, preferred_element_type=F32))
    out = first + nxt + const
    k, v = out[:, 0:LANES], out[:, LANES:2 * LANES]
    ss = jnp.dot(k * k, b64_ref[...], precision=HIGHEST, preferred_element_type=F32)
    kn = k * lax.rsqrt(ss * (1.0 / NSA_HD) + EPS) * nw_ref[...]
    one_ex = _lane_fill(n, {COL_SHIFT: 1.0})
    kx_o[...] = _slots(kn, [one_ex] * NSA_KV).astype(BF16)
    vx_o[...] = _slots(v, [one_ex] * NSA_KV).astype(BF16)


def _compress(kvc, cmp_w, cmp_pos, nw):
    s = kvc.shape[0]
    n = s // CMP_STRIDE
    width = CMP_STRIDE * 2 * NSA_KV * NSA_HD
    w = cmp_w.reshape(2, 2, CMP_STRIDE, NSA_HD, NSA_HD)
    eye = jnp.eye(2, dtype=w.dtype)
    big = jnp.einsum("khldD,kK,gG->hlkgdKGD", w, eye, eye).reshape(2, width, 2 * NSA_KV * NSA_HD)
    pos = cmp_pos.reshape(2, 2, CMP_STRIDE, 1, NSA_HD)
    pos = jnp.broadcast_to(pos, (2, 2, CMP_STRIDE, NSA_KV, NSA_HD)).transpose(1, 2, 0, 3, 4).reshape(2, 1, width)
    b64 = _block_diag_ones(LANES, NSA_HD)
    nwt = jnp.tile(nw.reshape(1, -1), (1, LANES // NSA_HD))
    out = jax.ShapeDtypeStruct((n, NSA_KV * LANES), BF16)
    return pl.pallas_call(
        _compress_kernel,
        grid=(1,),
        in_specs=[_full((n, width)), _full(big[0].shape), _full(big[1].shape), _full((1, width)), _full((1, width)),
                  _full((1, LANES)), _full((LANES, LANES))],
        out_specs=[_full((n, NSA_KV * LANES))] * 2,
        out_shape=[out, out],
        compiler_params=_cparams(("arbitrary",)),
        name="nsa_compress",
    )(kvc.reshape(n, width), big[0], big[1], pos[0], pos[1], nwt, b64)


def _nsa_cmp_kernel(q_ref, kc_ref, vc_ref, ov_ref, o_ref, sel_ref, *, tq, groups, bounded):
    i = pl.program_id(1)
    ncmp = kc_ref.shape[0]
    rows = groups * tq
    qT = jnp.concatenate([q_ref[:, g * QK_DIM:(g + 1) * QK_DIM].T for g in range(groups)], axis=1).astype(BF16)
    s = jnp.dot(kc_ref[...], qT, preferred_element_type=F32)
    n_idx = lax.broadcasted_iota(jnp.int32, (ncmp, rows), 0)
    lane = lax.broadcasted_iota(jnp.int32, (ncmp, rows), 1)
    qpos = i * tq + (lane & (tq - 1))
    keep = n_idx * CMP_STRIDE + (CMP_LEN - 1) <= qpos
    if bounded:
        e = jnp.where(keep, jnp.exp2(s), 0.0)
    else:
        sm = jnp.where(keep, s, NEG_INIT)
        e = jnp.where(keep, jnp.exp2(sm - jnp.max(sm, axis=0, keepdims=True)), 0.0)
    den = jnp.sum(e, axis=0, keepdims=True)
    p = e / jnp.where(den > 0.0, den, 1.0)
    o = lax.dot_general(vc_ref[...], p.astype(BF16), (((0,), (0,)), ((), ())), preferred_element_type=F32)
    for g in range(groups):
        o_ref[:, g * QK_DIM:(g + 1) * QK_DIM] = o[:, g * tq:(g + 1) * tq].T

    psum = p[:, 0:tq]
    for g in range(1, groups):
        psum = psum + p[:, g * tq:(g + 1) * tq]
    hi = psum.astype(BF16)
    lo = (psum - hi.astype(F32)).astype(BF16)
    ov = ov_ref[...]
    imp = (jnp.dot(ov, hi, preferred_element_type=F32) + jnp.dot(ov, lo, preferred_element_type=F32))
    ns = imp.shape[0]
    blk = lax.broadcasted_iota(jnp.int32, (ns, tq), 0)
    qp = i * tq + lax.broadcasted_iota(jnp.int32, (ns, tq), 1)
    cur = qp // SLC_LEN
    valid = blk <= cur
    forced = (blk == 0) | (blk == cur) | (blk == cur - 1)
    work = jnp.where(valid, imp + jnp.where(forced, FORCE_BONUS, 0.0), NEG_INIT)
    picked = jnp.zeros((ns, tq), dtype=jnp.bool_)
    for _ in range(SLC_TOPK):
        mx = jnp.max(work, axis=0, keepdims=True)
        idx = jnp.min(jnp.where(work == mx, blk, ns), axis=0, keepdims=True)
        pick = (blk == idx) & (mx > 0.5 * NEG_INIT)
        picked = picked | pick
        work = jnp.where(pick, NEG_INIT, work)
    sel_ref[0] = jnp.where(picked, 0.0, MASKVAL).astype(BF16)


def _nsa_cmp(qn, kcx, vcx, overlapT, tq, bound_ok):
    s = qn.shape[0]
    ncmp = kcx.shape[0]
    ns = overlapT.shape[0]
    groups = NSA_G

    def call(bounded):
        return pl.pallas_call(
            functools.partial(_nsa_cmp_kernel, tq=tq, groups=groups, bounded=bounded),
            grid=(NSA_KV, s // tq),
            in_specs=[pl.BlockSpec((tq, groups * QK_DIM), lambda h, i: (i, h)),
                      pl.BlockSpec((ncmp, QK_DIM), lambda h, i: (0, h)),
                      pl.BlockSpec((ncmp, QK_DIM), lambda h, i: (0, h)),
                      _full(overlapT.shape)],
            out_specs=[pl.BlockSpec((tq, groups * QK_DIM), lambda h, i: (i, h)),
                       pl.BlockSpec((1, ns, tq), lambda h, i: (h, 0, i))],
            out_shape=[jax.ShapeDtypeStruct(qn.shape, F32), jax.ShapeDtypeStruct((NSA_KV, ns, s), BF16)],
            compiler_params=_cparams(("parallel", "parallel")),
            name="nsa_cmp_bounded" if bounded else "nsa_cmp_max",
        )(qn, kcx, vcx, overlapT)

    return lax.cond(bound_ok, lambda: call(True), lambda: call(False))


def _mix_out_kernel(lam_ref, h_ref, fox_ref, oc_ref, os_ref, ow_ref, dd_ref, g_ref, sw_ref, wf_ref, wn_ref, wd_ref,
                    o_ref, *, diff_scale):
    lam = lam_ref[0]
    gates = g_ref[...]
    tm = gates.shape[0]
    lane = lax.broadcasted_iota(jnp.int32, (tm, LANES), 1)
    slot = lambda ref, n: ref[:, n * LANES:(n + 1) * LANES]
    nsa = []
    for h in range(NSA_HEADS):
        c0 = FOX_HEADS + 3 * h
        nsa.append(gates[:, c0:c0 + 1] * slot(oc_ref, h) + gates[:, c0 + 1:c0 + 2] * slot(os_ref, h)
                   + gates[:, c0 + 2:c0 + 3] * slot(ow_ref, h))
    diff = []
    for h in range(DIFF_HEADS):
        a = jnp.where(lane < DIFF_V, slot(dd_ref, 2 * h) - lam * slot(dd_ref, 2 * h + 1), 0.0)
        ss = jnp.sum(a * a, axis=-1, keepdims=True)
        diff.append(a * lax.rsqrt(ss * (1.0 / DIFF_V) + EPS) * sw_ref[...] * diff_scale)
    acc = h_ref[...]
    acc = acc + jnp.dot(fox_ref[...].astype(BF16), wf_ref[...], preferred_element_type=F32)
    acc = acc + jnp.dot(jnp.concatenate(nsa, axis=1).astype(BF16), wn_ref[...], preferred_element_type=F32)
    acc = acc + jnp.dot(jnp.concatenate(diff, axis=1).astype(BF16), wd_ref[...], preferred_element_type=F32)
    o_ref[...] = acc


def _slot_rows(w, heads):
    d = w.shape[1]
    w = w.reshape(heads, -1, d)
    return jnp.concatenate([w, jnp.zeros((heads, LANES - w.shape[1], d), w.dtype)], axis=1).reshape(heads * LANES, d)


def _mix_out(lam, h, fox, oc, os_, ow, dd, gates, subln_w, w_out, diff_scale, tm=512):
    s, d = h.shape
    nf, nn = FOX_HEADS * FOX_HD, NSA_HEADS * NSA_HD
    wf = _slot_rows(w_out[:nf], FOX_HEADS).astype(BF16)
    wn = _slot_rows(w_out[nf:nf + nn], NSA_HEADS).astype(BF16)
    wd = _slot_rows(w_out[nf + nn:], DIFF_HEADS).astype(BF16)
    sw = jnp.concatenate([subln_w.reshape(1, -1), jnp.zeros((1, LANES - DIFF_V), F32)], axis=1)
    row = lambda w: pl.BlockSpec((tm, w), lambda i: (i, 0))
    return pl.pallas_call(
        functools.partial(_mix_out_kernel, diff_scale=diff_scale),
        grid=(s // tm,),
        in_specs=[pl.BlockSpec(memory_space=pltpu.SMEM), row(d), row(fox.shape[1]), row(oc.shape[1]),
                  row(os_.shape[1]), row(ow.shape[1]), row(dd.shape[1]), row(LANES), _full((1, LANES)),
                  _full(wf.shape), _full(wn.shape), _full(wd.shape)],
        out_specs=row(d),
        out_shape=jax.ShapeDtypeStruct((s, d), F32),
        compiler_params=_cparams(("parallel",)),
        name="mix_out",
    )(lam, h, fox, oc, os_, ow, dd, gates, sw, wf, wn, wd)


def _peer_query_kernel(h_ref, nw_ref, wq_ref, keys_ref, xnT_o, scT_o):
    xn = _rms(h_ref[...], nw_ref[...])
    xnT_o[...] = xn.T.astype(BF16)
    q = jnp.dot(xn.astype(BF16), wq_ref[...], preferred_element_type=F32).astype(BF16)
    half = PEER_DQ // 2
    for b in range(2 * PEER_HEADS):
        sc = jnp.dot(q[:, b * half:(b + 1) * half], keys_ref[b], preferred_element_type=F32)
        scT_o[b] = sc.T


def _peer_query(h, nw, wq_bf16, keysT_bf16, tm=512):
    s, d = h.shape
    n = wq_bf16.shape[1]
    nb = 2 * PEER_HEADS
    return pl.pallas_call(
        _peer_query_kernel,
        grid=(s // tm,),
        in_specs=[pl.BlockSpec((tm, d), lambda i: (i, 0)), _full((1, d)), _full((d, n)), _full(keysT_bf16.shape)],
        out_specs=[pl.BlockSpec((d, tm), lambda i: (0, i)), pl.BlockSpec((nb, PEER_NKEYS, tm), lambda i: (0, 0, i))],
        out_shape=[jax.ShapeDtypeStruct((d, s), BF16), jax.ShapeDtypeStruct((nb, PEER_NKEYS, s), F32)],
        compiler_params=_cparams(("parallel",)),
        name="peer_query",
    )(h, nw.reshape(1, d), wq_bf16, keysT_bf16)


def _top16_rows(s, exact_ties):
    n = s.shape[0]
    row = lax.broadcasted_iota(jnp.int32, s.shape, 0)
    rank = jnp.full(s.shape, float(PEER_TOPK), dtype=F32)
    work = s
    vals = []
    for r in range(PEER_TOPK):
        mx = jnp.max(work, axis=0, keepdims=True)
        if exact_ties:
            pick = row == jnp.min(jnp.where(work == mx, row, n), axis=0, keepdims=True)
        else:
            pick = work == mx
        rank = jnp.where(pick, float(r), rank)
        work = jnp.where(pick, NEG_INIT, work)
        vals.append(mx)
    return rank, vals


def _peer_route(s1, s2, exact_ties):
    rank1, v1 = _top16_rows(s1, exact_ties)
    rank2, v2 = _top16_rows(s2, exact_ties)
    k = PEER_TOPK
    tm = s1.shape[1]
    r16 = lax.broadcasted_iota(jnp.int32, (k, tm), 0)
    v1m = jnp.zeros((k, tm), F32)
    v2m = jnp.zeros((k, tm), F32)
    for r in range(k):
        v1m = jnp.where(r16 == r, v1[r], v1m)
        v2m = jnp.where(r16 == r, v2[r], v2m)
    ea = jnp.exp(v1m - v1[0])
    eb = jnp.exp(v2m - v2[0])
    cands, gates = [v1m[0:1] + v2m], [ea[0:1] * eb]
    for r in range(1, 8):
        cands.append(v1m[r:r + 1] + v2m[0:8])
        gates.append(ea[r:r + 1] * eb[0:8])
    cands.append(v1m[8:16] + v2m[0:1])
    gates.append(ea[8:16] * eb[0:1])
    cand = jnp.concatenate(cands, axis=0)
    gate = jnp.concatenate(gates, axis=0)
    ncand = cand.shape[0]
    crow = lax.broadcasted_iota(jnp.int32, cand.shape, 0)
    for r in range(2, 8):
        start = 16 + 8 * (r - 1)
        cand = jnp.where((crow >= start + k // (r + 1)) & (crow < start + 8), NEG_INIT, cand)
    work = cand
    picked = jnp.zeros(cand.shape, dtype=jnp.bool_)
    for _ in range(k):
        mx = jnp.max(work, axis=0, keepdims=True)
        if exact_ties:
            pick = crow == jnp.min(jnp.where(work == mx, crow, ncand), axis=0, keepdims=True)
        else:
            pick = work == mx
        picked = picked | pick
        work = jnp.where(pick, NEG_INIT, work)
    pf = picked.astype(F32)
    z = jnp.sum(pf * gate, axis=0, keepdims=True)
    cnt = [jnp.sum(pf[0:16, :], axis=0, keepdims=True)]
    cnt += [jnp.sum(pf[8 + 8 * r:16 + 8 * r, :], axis=0, keepdims=True) for r in range(1, 8)]
    cnt += [pf[64 + r:65 + r, :] for r in range(8, 16)]
    cmap = jnp.zeros(s1.shape, dtype=F32)
    for r in range(k):
        cmap = jnp.where(rank1 == float(r), cnt[r], cmap)
    in1 = rank1 < float(k)
    in2 = rank2 < float(k)
    a = jnp.where(in1, jnp.exp(s1 - v1[0]), 0.0) / z
    b = jnp.where(in2, jnp.exp(s2 - v2[0]), 0.0)
    excess = (jnp.abs(jnp.sum(in1.astype(F32), axis=0, keepdims=True) - k)
              + jnp.abs(jnp.sum(in2.astype(F32), axis=0, keepdims=True) - k)
              + jnp.abs(jnp.sum(pf, axis=0, keepdims=True) - k))
    return a, cmap, b, rank2, excess


def _peer_topk_kernel(sc_ref, a_o, c_o, b_o, r_o):
    def run(exact_ties):
        a, c, b, rank2, excess = _peer_route(sc_ref[0], sc_ref[1], exact_ties)
        a_o[0] = a
        c_o[0] = c
        b_o[0] = b.astype(BF16)
        r_o[0] = rank2.astype(BF16)
        return excess

    excess = run(False)

    @pl.when(jnp.max(excess) > 0.0)
    def _():
        run(True)


def _peer_topk(scT, tm=256):
    n2, nk, s = scT.shape
    heads = n2 // 2
    ospec = pl.BlockSpec((1, nk, tm), lambda h, t: (h, 0, t))
    return pl.pallas_call(
        _peer_topk_kernel,
        grid=(heads, s // tm),
        in_specs=[pl.BlockSpec((2, nk, tm), lambda h, t: (h, 0, t))],
        out_specs=[ospec] * 4,
        out_shape=[jax.ShapeDtypeStruct((heads, nk, s), dt) for dt in (F32, F32, BF16, BF16)],
        compiler_params=_cparams(("parallel", "parallel")),
        name="peer_topk",
    )(scT)


def _gelu_exact(x):
    return 0.5 * x * (1.0 + lax.erf(x * (2.0 ** -0.5)))


def _peer_main_kernel(xT_ref, u_ref, v_ref, a_ref, c_ref, b_ref, r_ref, o_ref, w_sc, *, ec, sub):
    ci = pl.program_id(1)

    @pl.when(ci == 0)
    def _():
        o_ref[...] = jnp.zeros_like(o_ref)

    nk = PEER_NKEYS
    tm = xT_ref.shape[1]

    def row(ref, h, i1, lanes):
        x16 = jnp.broadcast_to(ref[h, pl.ds(i1, 1), :][:, lanes], (16, GATE_LANES)).astype(BF16)
        return jnp.concatenate([x16] * (nk // 16), axis=0)

    acc = None
    for sc in range(ec // sub):
        hid = jnp.dot(u_ref[sc * sub:(sc + 1) * sub, :], xT_ref[...], preferred_element_type=F32)
        act = _gelu_exact(hid).astype(BF16)
        for ii in range(sub // nk):
            slab = sc * (sub // nk) + ii
            i1 = ci * (ec // nk) + slab
            for l0 in range(0, tm, GATE_LANES):
                lanes = slice(l0, l0 + GATE_LANES)
                w = None
                for h in range(PEER_HEADS):
                    keep = r_ref[h, :, lanes] < row(c_ref, h, i1, lanes)
                    term = jnp.where(keep, b_ref[h, :, lanes], 0.0) * row(a_ref, h, i1, lanes)
                    w = term if w is None else w + term
                w_sc[slab * nk:(slab + 1) * nk, lanes] = w * act[ii * nk:(ii + 1) * nk, lanes]
        part = jnp.dot(w_sc[sc * sub:(sc + 1) * sub, :].T, v_ref[sc * sub:(sc + 1) * sub, :],
                       preferred_element_type=F32)
        acc = part if acc is None else acc + part
    o_ref[...] += acc


def _peer_main(xnT, u_bf16, v_bf16, a, c, b, r, tm=512, ec=2048, sub=512):
    d, s = xnT.shape
    e = u_bf16.shape[0]
    heads, nk, _ = a.shape
    rt = pl.BlockSpec((heads, nk, tm), lambda t, ci: (0, 0, t))
    wt = pl.BlockSpec((ec, d), lambda t, ci: (ci, 0))
    return pl.pallas_call(
        functools.partial(_peer_main_kernel, ec=ec, sub=sub),
        grid=(s // tm, e // ec),
        in_specs=[pl.BlockSpec((d, tm), lambda t, ci: (0, t)), wt, wt, rt, rt, rt, rt],
        out_specs=pl.BlockSpec((tm, d), lambda t, ci: (t, 0)),
        out_shape=jax.ShapeDtypeStruct((s, d), F32),
        scratch_shapes=[pltpu.VMEM((ec, tm), BF16)],
        compiler_params=_cparams(("parallel", "arbitrary")),
        name="peer_main",
    )(xnT, u_bf16, v_bf16, a, c, b, r)


def _ple_kernel(h_ref, peer_ref, p_ref, nw_ref, wg_ref, wp_ref, o_ref):
    h2 = h_ref[...] + peer_ref[...]
    xn = _rms(h2, nw_ref[...]).astype(BF16)
    gate = 1.0 / (1.0 + jnp.exp(-jnp.dot(xn, wg_ref[...], preferred_element_type=F32)))
    emb = jnp.dot(p_ref[...].astype(BF16), wp_ref[...], preferred_element_type=F32)
    o_ref[...] = h2 + gate * emb


def _ple(h, peer, p, nw, wg_bf16, wp_bf16, tm=512):
    s, d = h.shape
    pd = p.shape[1]
    row = lambda w: pl.BlockSpec((tm, w), lambda i: (i, 0))
    return pl.pallas_call(
        _ple_kernel,
        grid=(s // tm,),
        in_specs=[row(d), row(d), row(pd), _full((1, d)), _full((d, d)), _full((pd, d))],
        out_specs=row(d),
        out_shape=jax.ShapeDtypeStruct((s, d), F32),
        compiler_params=_cparams(("parallel",)),
        name="ple",
    )(h, peer, p, nw.reshape(1, d), wg_bf16, wp_bf16)


def _split_w_in(w):
    f0 = 3 * FOX_HEADS * FOX_HD
    n0 = f0 + FOX_HEADS
    g0 = n0 + _SEG["dq"][0] - _SEG["nq"][0]
    d0 = g0 + 3 * NSA_HEADS
    misc = jnp.concatenate([w[:, f0:n0], w[:, g0:d0], jnp.zeros((w.shape[0], LANES - (n0 - f0) - (d0 - g0)), w.dtype)],
                           axis=1)
    return [t.astype(BF16) for t in (w[:, :f0], w[:, n0:g0], w[:, d0:], misc)]


def _overlap_T(s):
    n = np.arange(s // CMP_STRIDE)[None, :] * CMP_STRIDE
    m = np.arange(s // SLC_LEN)[:, None] * SLC_LEN
    return jnp.asarray(((n < m + SLC_LEN) & (n + CMP_LEN > m)).astype(np.float32), dtype=BF16)


def _layer(h, p_i, tabs, layer, attn_norm_w, w_in, fox_f_bias, fox_q_norm_w, fox_k_norm_w, nsa_q_norm_w,
           nsa_k_norm_w, nsa_cmp_pos, nsa_cmp_w, diff_q_norm_w, diff_k_norm_w, diff_lambda, diff_subln_w,
           w_out, ffn_norm_w, peer_w_q, peer_sub_keys, peer_u, peer_v, ple_norm_w, ple_w_gate, ple_w_proj):
    s = h.shape[0]
    fb =jnp.zeros((1, LANES), F32).at[0, :FOX_HEADS].set(fox_f_bias)
    u_f = _score_bound(fox_q_norm_w, fox_k_norm_w, FOX_HD, FOX_HD ** -0.5)
    u_n = _score_bound(nsa_q_norm_w, nsa_k_norm_w, NSA_HD, NSA_HD ** -0.5)
    u_d = _score_bound(diff_q_norm_w, diff_k_norm_w, DIFF_QK, DIFF_QK ** -0.5)
    shifts = SHIFT_HEADROOM - jnp.stack([u_f, u_n, u_d]).astype(F32)
    (fq, fk, fv, nqn, nqr, kvc, ks, vs, kw, vw, gates, dq, dk, dv, csum) = _prep(
        shifts, h, attn_norm_w, _split_w_in(w_in), tabs, fb, fox_q_norm_w, fox_k_norm_w, nsa_q_norm_w, nsa_k_norm_w, diff_q_norm_w, diff_k_norm_w)

    c_first_q = csum[0::Q_LANES, :FOX_HEADS]
    c_last_k = csum[TK_CAUSAL - 1::TK_CAUSAL, :FOX_HEADS]
    decay = (c_first_q.T[:, :, None] - c_last_k.T[:, None, :]).reshape(-1)
    o_fox = _flash(fq, fk, fv, None, decay, bound_ok=u_f <= SCORE_BOUND, groups=1, tq=Q_LANES, tk=TK_CAUSAL)

    tq_n = Q_LANES // NSA_G
    nsa_ok = u_n <= SCORE_BOUND
    kcx, vcx = _compress(kvc, nsa_cmp_w, nsa_cmp_pos, nsa_k_norm_w)
    o_c, sel = _nsa_cmp(nqn, kcx, vcx, _overlap_T(s), tq_n, nsa_ok)
    o_s = _flash(nqr, ks, vs, sel, bound_ok=nsa_ok, groups=NSA_G, tq=tq_n, tk=TK_CAUSAL)
    o_w = _flash(nqr, kw, vw, bound_ok=nsa_ok, groups=NSA_G, tq=tq_n, tk=512, window=WIN)

    o_d = _flash(dq, dk, dv, bound_ok=u_d <= SCORE_BOUND, groups=2, tq=Q_LANES // 2, tk=TK_CAUSAL)
    lv = diff_lambda.astype(F32)
    lam_init = 0.8 - 0.6 * math.exp(-0.3 * layer)
    lam = (jnp.exp(jnp.sum(lv[0] * lv[1])) - jnp.exp(jnp.sum(lv[2] * lv[3])) + lam_init).reshape(1)
    h1 = _mix_out(lam, h, o_fox, o_c, o_s, o_w, o_d, gates, diff_subln_w, w_out, 1.0 - lam_init)

    keysT = peer_sub_keys.reshape(2 * PEER_HEADS, PEER_NKEYS, PEER_DQ // 2).transpose(0, 2, 1).astype(BF16)
    xnT, scT = _peer_query(h1, ffn_norm_w, peer_w_q.astype(BF16), keysT)
    a, c, b, r = _peer_topk(scT)
    peer = _peer_main(xnT, peer_u.astype(BF16), peer_v.astype(BF16), a, c, b, r)

    return _ple(h1, peer, p_i, ple_norm_w, ple_w_gate.astype(BF16), ple_w_proj.astype(BF16))


def kernel(x, p, positions, attn_norm_w, w_in, fox_f_bias, fox_q_norm_w, fox_k_norm_w, nsa_q_norm_w, nsa_k_norm_w,
           nsa_cmp_pos, nsa_cmp_w, diff_q_norm_w, diff_k_norm_w, diff_lambda, diff_subln_w, w_out, ffn_norm_w,
           peer_w_q, peer_sub_keys, peer_u, peer_v, ple_norm_w, ple_w_gate, ple_w_proj):
    b, s, d = x.shape
    assert b == 1 and d == D_MODEL and s % Q_LANES == 0 and s % TK_CAUSAL == 0
    tabs = _rope_tables(positions)
    h = x.reshape(s, d)
    per_layer = (attn_norm_w, w_in, fox_f_bias, fox_q_norm_w, fox_k_norm_w, nsa_q_norm_w, nsa_k_norm_w, nsa_cmp_pos,
                 nsa_cmp_w, diff_q_norm_w, diff_k_norm_w, diff_lambda, diff_subln_w, w_out, ffn_norm_w, peer_w_q,
                 peer_sub_keys, peer_u, peer_v, ple_norm_w, ple_w_gate, ple_w_proj)
    for layer in range(attn_norm_w.shape[0]):
        h = _layer(h, p[layer, 0], tabs, layer, *(w[layer] for w in per_layer))
    return h.reshape(b, s, d)
```

```python
import functools
import math

import numpy as np
import jax
import jax.numpy as jnp
from jax import lax
from jax.experimental import pallas as pl
from jax.experimental.pallas import tpu as pltpu

F32 = jnp.float32
BF16 = jnp.bfloat16
HIGHEST = lax.Precision.HIGHEST

D_MODEL = 1024
PLE_DIM = 256
ROPE_THETA = 10000.0
EPS = 1e-6
FOX_HEADS, FOX_HD = 4, 64
NSA_HEADS, NSA_KV, NSA_HD = 8, 2, 64
NSA_G = NSA_HEADS // NSA_KV
CMP_LEN, CMP_STRIDE, SLC_LEN, SLC_TOPK, WIN = 32, 16, 64, 16, 512
FORCE_BONUS = 1.0e4
DIFF_HEADS, DIFF_QK, DIFF_V = 4, 32, 64
PEER_HEADS, PEER_NKEYS, PEER_DQ, PEER_TOPK = 8, 128, 256, 16
PEER_EXPERTS = PEER_NKEYS * PEER_NKEYS

LANES = 128
LOG2E = math.log2(math.e)
NEG_INIT = -1.0e30
MASKVAL = -2.0e30
VMEM_LIMIT = 56 * 1024 * 1024
TK_CAUSAL = 2048
Q_LANES = 2048
GATE_LANES = 512

_SEG = dict(fq=(0, 256), fk=(256, 256), fv=(512, 256), nq=(768, 512), nkc=(1280, 128), nvc=(1408, 128),
            nks=(1536, 128), nvs=(1664, 128), nkw=(1792, 128), nvw=(1920, 128),
            dq=(2048, 256), dk=(2304, 256), dv=(2560, 256), misc=(2816, 128))
PROJ_W = 2944


def _cparams(sem):
    return pltpu.CompilerParams(dimension_semantics=sem, vmem_limit_bytes=VMEM_LIMIT)


def _full(shape):
    n = len(shape)
    return pl.BlockSpec(shape, lambda *_: (0,) * n)


def _rope_tab_kernel(pos_ref, f64_ref, g64_ref, f32_ref, g32_ref, c64_o, s64_o, c32_o, s32_o):
    pos = pos_ref[...].astype(F32)
    a64 = pos * f64_ref[...]
    c64_o[...] = jnp.cos(a64)
    s64_o[...] = jnp.sin(a64) * g64_ref[...]
    a32 = pos * f32_ref[...]
    c32_o[...] = jnp.cos(a32)
    s32_o[...] = jnp.sin(a32) * g32_ref[...]


def _rope_tables(positions):
    s = positions.shape[-1]
    pos = positions.reshape(s, 1)
    lane = np.arange(LANES)

    def lanes(half):
        inv = ROPE_THETA ** (-jnp.arange(half, dtype=F32) / half)
        freq = inv[(lane % (2 * half)) % half].reshape(1, LANES)
        sign = np.where((lane % (2 * half)) < half, -1.0, 1.0).astype(np.float32).reshape(1, LANES)
        return freq, jnp.asarray(sign)

    f64, g64 = lanes(NSA_HD // 2)
    f32_, g32 = lanes(DIFF_QK // 2)
    tm = 512
    out = jax.ShapeDtypeStruct((s, LANES), F32)
    row = pl.BlockSpec((tm, LANES), lambda i: (i, 0))
    return pl.pallas_call(
        _rope_tab_kernel,
        grid=(s // tm,),
        in_specs=[pl.BlockSpec((tm, 1), lambda i: (i, 0))] + [_full((1, LANES))] * 4,
        out_specs=[row] * 4,
        out_shape=[out] * 4,
        compiler_params=_cparams(("parallel",)),
        name="rope_tables",
    )(pos, f64, g64, f32_, g32)


def _rms(x, w):
    return x * lax.rsqrt(jnp.mean(x * x, axis=-1, keepdims=True) + EPS) * w


def _seg_rms(x, bmat, seg):
    outs = []
    for c0 in range(0, x.shape[1], 256):
        w = min(256, x.shape[1] - c0)
        xc = x[:, c0:c0 + w]
        sq = xc * xc
        hi = sq.astype(BF16)
        lo = (sq - hi.astype(F32)).astype(BF16)
        b = bmat[:w, :w].astype(BF16)
        ss = jnp.dot(hi, b, preferred_element_type=F32) + jnp.dot(lo, b, preferred_element_type=F32)
        outs.append(xc * lax.rsqrt(ss * (1.0 / seg) + EPS))
    return outs[0] if len(outs) == 1 else jnp.concatenate(outs, axis=1)


def _tile_lanes(t, width):
    reps = width // LANES
    return t if reps == 1 else jnp.concatenate([t] * reps, axis=1)


def _rope(x, cos, sin, half):
    width = x.shape[1]
    left = pltpu.roll(x, width - half, 1)
    right = pltpu.roll(x, half, 1)
    lane = lax.broadcasted_iota(jnp.int32, x.shape, 1)
    swapped = jnp.where((lane & (2 * half - 1)) < half, left, right)
    return x * _tile_lanes(cos, width) + swapped * _tile_lanes(sin, width)


def _lane_fill(tm, cols):
    lane = lax.broadcasted_iota(jnp.int32, (tm, LANES), 1)
    out = jnp.zeros((tm, LANES), F32)
    for l, v in cols.items():
        out = jnp.where(lane == l, v, out)
    return out


def _slots(x, extras):
    lane = lax.broadcasted_iota(jnp.int32, (x.shape[0], LANES), 1)
    outs = []
    for h, ex in enumerate(extras):
        col = x[:, (h // 2) * LANES:(h // 2 + 1) * LANES]
        if h % 2:
            col = pltpu.roll(col, LANES // 2, 1)
        outs.append(jnp.where(lane < LANES // 2, col, ex))
    return jnp.concatenate(outs, axis=1)


def _prep_kernel(shift_ref, proj_ref, c64_ref, s64_ref, c32_ref, s32_ref, fb_ref, wfq_ref, wfk_ref, wnq_ref,
                 wnk_ref, wdq_ref, wdk_ref, b64_ref, b32_ref,
                 fq_o, fk_o, fv_o, nqn_o, nqr_o, kvc_o, ks_o, vs_o, kw_o, vw_o, gate_o, dq_o, dk_o, dv_o, cs_o,
                 carry_sc, *, tm):
    def seg(name):
        c0, w = _SEG[name]
        return proj_ref[:, c0:c0 + w]

    b64 = b64_ref[...]
    b32 = b32_ref[...]
    c64, s64, c32, s32 = c64_ref[...], s64_ref[...], c32_ref[...], s32_ref[...]
    one_ex = _lane_fill(tm, {COL_SHIFT: 1.0})

    misc = seg("misc")
    gate_o[...] = 1.0 / (1.0 + jnp.exp(-misc))
    t = misc + fb_ref[...]
    logf = jnp.minimum(t, 0.0) - jnp.log1p(jnp.exp(-jnp.abs(t)))

    @pl.when(pl.program_id(0) == 0)
    def _():
        carry_sc[...] = jnp.zeros_like(carry_sc)

    r = lax.broadcasted_iota(jnp.int32, (tm, tm), 0)
    c = lax.broadcasted_iota(jnp.int32, (tm, tm), 1)
    tri = (c <= r).astype(F32)
    csum = jnp.dot(tri, logf, precision=HIGHEST, preferred_element_type=F32) + carry_sc[0:1, :]
    carry_sc[...] = jnp.broadcast_to(csum[tm - 1:tm, :], carry_sc.shape)
    csum = csum * LOG2E
    hi = csum.astype(BF16).astype(F32)
    r1 = csum - hi
    mid = r1.astype(BF16).astype(F32)
    lo = (r1 - mid).astype(BF16).astype(F32)
    cs_o[...] = csum

    fq = _seg_rms(seg("fq"), b64, FOX_HD) * wfq_ref[...] * (FOX_HD ** -0.5 * LOG2E)
    fk = _seg_rms(seg("fk"), b64, FOX_HD) * wfk_ref[...]
    q_ex, k_ex = [], []
    for h in range(FOX_HEADS):
        ch, cm, cl = hi[:, h:h + 1], mid[:, h:h + 1], lo[:, h:h + 1]
        q_ex.append(_lane_fill(tm, {COL_SHIFT: shift_ref[0], 65: 1.0, 66: 1.0, 67: 1.0, 68: ch, 69: cm, 70: cl}))
        k_ex.append(_lane_fill(tm, {COL_SHIFT: 1.0, 65: -ch, 66: -cm, 67: -cl, 68: 1.0, 69: 1.0, 70: 1.0}))
    fq_o[...] = _slots(fq, q_ex)
    fk_o[...] = _slots(fk, k_ex).astype(BF16)
    fv_o[...] = _slots(seg("fv"), [one_ex] * FOX_HEADS).astype(BF16)

    n_ex = [_lane_fill(tm, {COL_SHIFT: shift_ref[1]})] * NSA_HEADS
    nqn = _seg_rms(seg("nq"), b64, NSA_HD) * wnq_ref[...]
    nqn_o[...] = _slots(nqn * (NSA_HD ** -0.5 * LOG2E), n_ex)
    nqr_o[...] = _slots(_rope(nqn, c64, s64, NSA_HD // 2) * (NSA_HD ** -0.5 * LOG2E), n_ex)
    kvc_o[:, 0:128] = seg("nkc").astype(BF16)
    kvc_o[:, 128:256] = seg("nvc").astype(BF16)
    wnk = wnk_ref[...]
    lane = lax.broadcasted_iota(jnp.int32, (tm, LANES), 1)
    row = pl.program_id(0) * tm + lax.broadcasted_iota(jnp.int32, (tm, LANES), 0)
    blk_in_tile = (row & (TK_CAUSAL - 1)) // SLC_LEN
    sel_ex = jnp.where(lane == COL_SEL + blk_in_tile, 1.0, one_ex)
    ks = _rope(_seg_rms(seg("nks"), b64, NSA_HD) * wnk, c64, s64, NSA_HD // 2)
    ks_o[...] = _slots(ks, [sel_ex] * NSA_KV).astype(BF16)
    vs_o[...] = _slots(seg("nvs"), [one_ex] * NSA_KV).astype(BF16)
    kw = _rope(_seg_rms(seg("nkw"), b64, NSA_HD) * wnk, c64, s64, NSA_HD // 2)
    kw_o[...] = _slots(kw, [one_ex] * NSA_KV).astype(BF16)
    vw_o[...] = _slots(seg("nvw"), [one_ex] * NSA_KV).astype(BF16)

    dq = _rope(_seg_rms(seg("dq"), b32, DIFF_QK) * wdq_ref[...], c32, s32, DIFF_QK // 2) * (DIFF_QK ** -0.5 * LOG2E)
    d_ex = _lane_fill(tm, {COL_SHIFT: shift_ref[2]})
    d_slots = []
    for h in range(DIFF_HEADS):
        col = dq[:, (h // 2) * LANES:(h // 2 + 1) * LANES]
        if h % 2:
            col = pltpu.roll(col, LANES // 2, 1)
        d_slots.append(jnp.where(lane < DIFF_QK, col, d_ex))
        d_slots.append(jnp.where((lane >= DIFF_QK) & (lane < 2 * DIFF_QK), col, d_ex))
    dq_o[...] = jnp.concatenate(d_slots, axis=1)
    dk = _rope(_seg_rms(seg("dk"), b32, DIFF_QK) * wdk_ref[...], c32, s32, DIFF_QK // 2)
    dk_o[...] = _slots(dk, [one_ex] * DIFF_HEADS).astype(BF16)
    dv_o[...] = _slots(seg("dv"), [one_ex] * DIFF_HEADS).astype(BF16)


def _block_diag_ones(n, seg):
    i = np.arange(n)
    return jnp.asarray((i[:, None] // seg == i[None, :] // seg).astype(np.float32))


def _proj_prep_kernel(shift_ref, x_ref, nw_ref, wa_ref, wb_ref, wc_ref, wm_ref, *rest, tm):
    proj_sc = rest[-1]
    xn = _rms(x_ref[...], nw_ref[...]).astype(BF16)
    c0 = 0
    for w_ref in (wa_ref, wb_ref, wc_ref, wm_ref):
        n = w_ref.shape[1]
        proj_sc[:, c0:c0 + n] = jnp.dot(xn, w_ref[...], preferred_element_type=F32)
        c0 += n
    _prep_kernel(shift_ref, proj_sc, *rest[:-1], tm=tm)


def _prep(shifts, x, nw, ws_bf16, tabs, fb, wfq, wfk, wnq, wnk, wdq, wdk, tm=256):
    s, d = x.shape
    c64, s64, c32, s32 = tabs
    b64 = _block_diag_ones(256, 64)
    b32 = _block_diag_ones(256, 32)

    def tiled(w, width):
        return jnp.tile(w.reshape(1, -1), (1, width // w.shape[-1]))

    consts = [fb, tiled(wfq, 256), tiled(wfk, 256), tiled(wnq, 512), tiled(wnk, 128), tiled(wdq, 256),
              tiled(wdk, 256), b64, b32]
    outs = [(FOX_HEADS, F32), (FOX_HEADS, BF16), (FOX_HEADS, BF16),
            (NSA_HEADS, F32), (NSA_HEADS, F32), (2, BF16),
            (NSA_KV, BF16), (NSA_KV, BF16), (NSA_KV, BF16), (NSA_KV, BF16), (1, F32),
            (2 * DIFF_HEADS, F32), (DIFF_HEADS, BF16), (DIFF_HEADS, BF16),
            (1, F32)]
    row = lambda w: pl.BlockSpec((tm, w), lambda i: (i, 0))
    assert sum(w.shape[1] for w in ws_bf16) == PROJ_W
    return pl.pallas_call(
        functools.partial(_proj_prep_kernel, tm=tm),
        grid=(s // tm,),
        in_specs=[pl.BlockSpec(memory_space=pltpu.SMEM), row(d), _full((1, d))] + [_full(w.shape) for w in ws_bf16]
                 + [row(LANES)] * 4 + [_full(c.shape) for c in consts],
        out_specs=[row(n * LANES) for n, _ in outs],
        out_shape=[jax.ShapeDtypeStruct((s, n * LANES), dt) for n, dt in outs],
        scratch_shapes=[pltpu.VMEM((8, LANES), F32), pltpu.VMEM((tm, PROJ_W), F32)],
        compiler_params=_cparams(("arbitrary",)),
        name="proj_head_prep",
    )(shifts, x, nw.reshape(1, d), *ws_bf16, c64, s64, c32, s32, *consts)


QK_DIM = 128
COL_SHIFT = 64
COL_SEL = 80
SCORE_BOUND = 60.0
SHIFT_HEADROOM = 60.0
EXP2_FLUSH = 200.0


def _flash_kernel(code_ref, q_ref, k_ref, v_ref, *rest, groups, tq, tk, window, has_sel, bounded, decay_tiles):
    rest = list(rest)
    sel_ref = rest.pop(0) if has_sel else None
    decay_ref = rest.pop(0) if decay_tiles else None
    o_ref = rest.pop(0)
    qT_sc = rest.pop(0)
    m_sc = None if bounded else rest.pop(0)
    acc_sc = rest.pop(0)
    code = code_ref[pl.program_id(1)]
    i = code & 0xFFF
    j = (code >> 12) & 0xFFF
    first = (code >> 24) & 1
    last = (code >> 25) & 1
    rows = groups * tq

    @pl.when(first == 1)
    def _():
        for g in range(groups):
            qT_sc[:, g * tq:(g + 1) * tq] = q_ref[:, g * QK_DIM:(g + 1) * QK_DIM].T.astype(BF16)
        if not bounded:
            m_sc[...] = jnp.full_like(m_sc, NEG_INIT)
        acc_sc[...] = jnp.zeros_like(acc_sc)

    def step(masked):
        if has_sel:
            qT_sc[COL_SEL:COL_SEL + tk // SLC_LEN, :] = _tile_lanes_any(sel_ref[0], groups)
        s = jnp.dot(k_ref[...], qT_sc[...], preferred_element_type=F32)
        if masked:
            kpos = j * tk + lax.broadcasted_iota(jnp.int32, (tk, rows), 0)
            lane = lax.broadcasted_iota(jnp.int32, (tk, rows), 1)
            qpos = i * tq + (lane & (tq - 1))
            keep = kpos <= qpos
            if window is not None:
                keep = keep & (kpos > qpos - window)
            s = jnp.where(keep, s, MASKVAL)
        v_t = (((0,), (0,)), ((), ()))
        v = v_ref[...]
        if bounded:
            acc_sc[...] += lax.dot_general(v, jnp.exp2(s).astype(BF16), v_t, preferred_element_type=F32)
        else:
            m_prev = m_sc[...]
            m_new = jnp.maximum(m_prev, jnp.max(s, axis=0, keepdims=True))
            p = jnp.exp2(s - m_new).astype(BF16)
            acc_sc[...] = (acc_sc[...] * jnp.exp2(m_prev - m_new)
                           + lax.dot_general(v, p, v_t, preferred_element_type=F32))
            m_sc[...] = m_new

    if window is not None:
        step(True)
    else:
        needs_mask = (j + 1) * tk - 1 > i * tq

        @pl.when(needs_mask)
        def _():
            step(True)

        @pl.when(jnp.logical_not(needs_mask))
        def _():
            if decay_tiles:
                nq, nk = decay_tiles
                bias_max = decay_ref[(pl.program_id(0) * nq + i) * nk + j]

                @pl.when(SHIFT_HEADROOM + bias_max > -EXP2_FLUSH)
                def _():
                    step(False)
            else:
                step(False)

    @pl.when(last == 1)
    def _():
        l = acc_sc[COL_SHIFT:COL_SHIFT + 1, :]
        o = acc_sc[...] / jnp.where(l > 0.0, l, 1.0)
        for g in range(groups):
            o_ref[:, g * QK_DIM:(g + 1) * QK_DIM] = o[:, g * tq:(g + 1) * tq].T


def _tile_lanes_any(t, reps):
    return t if reps == 1 else jnp.concatenate([t] * reps, axis=1)


def _flash_call(q, k, v, sel, decay, *, groups, tq, tk, window, bounded):
    s_len = q.shape[0]
    hkv = k.shape[1] // QK_DIM
    nq = s_len // tq
    rows = groups * tq
    codes = []
    for i in range(nq):
        q_lo, q_hi = i * tq, i * tq + tq - 1
        j_hi = q_hi // tk
        j_lo = 0 if window is None else max(0, (q_lo - window + 1) // tk)
        for j in range(j_lo, j_hi + 1):
            codes.append(i | (j << 12) | (int(j == j_lo) << 24) | (int(j == j_hi) << 25))
    codes = jnp.asarray(np.asarray(codes, dtype=np.int32))
    nsteps = codes.shape[0]

    def ti(c, s):
        return c[s] & 0xFFF

    def tj(c, s):
        return (c[s] >> 12) & 0xFFF

    in_specs = [
        pl.BlockSpec((tq, groups * QK_DIM), lambda h, s, c: (ti(c, s), h)),
        pl.BlockSpec((tk, QK_DIM), lambda h, s, c: (tj(c, s), h)),
        pl.BlockSpec((tk, QK_DIM), lambda h, s, c: (tj(c, s), h)),
    ]
    args = [q, k, v]
    if sel is not None:
        in_specs.append(pl.BlockSpec((1, tk // SLC_LEN, tq), lambda h, s, c: (h, tj(c, s), ti(c, s))))
        args.append(sel)
    use_decay = decay is not None and bounded
    if use_decay:
        in_specs.append(pl.BlockSpec(memory_space=pltpu.SMEM))
        args.append(decay)
    scratch = [pltpu.VMEM((QK_DIM, rows), BF16)]
    if not bounded:
        scratch.append(pltpu.VMEM((1, rows), F32))
    scratch.append(pltpu.VMEM((QK_DIM, rows), F32))
    kern = functools.partial(_flash_kernel, groups=groups, tq=tq, tk=tk, window=window, has_sel=sel is not None,
                             bounded=bounded, decay_tiles=(nq, s_len // tk) if use_decay else None)
    return pl.pallas_call(
        kern,
        grid_spec=pltpu.PrefetchScalarGridSpec(
            num_scalar_prefetch=1,
            grid=(hkv, nsteps),
            in_specs=in_specs,
            out_specs=pl.BlockSpec((tq, groups * QK_DIM), lambda h, s, c: (ti(c, s), h)),
            scratch_shapes=scratch,
        ),
        out_shape=jax.ShapeDtypeStruct(q.shape, F32),
        compiler_params=_cparams(("parallel", "arbitrary")),
        name="flash_bounded" if bounded else "flash_online",
    )(codes, *args)


def _flash(q, k, v, sel=None, decay=None, *, bound_ok, groups, tq, tk, window=None):
    call = functools.partial(_flash_call, q, k, v, sel, decay, groups=groups, tq=tq, tk=tk, window=window)
    return lax.cond(bound_ok, lambda: call(bounded=True), lambda: call(bounded=False))


def _score_bound(wq, wk, seg, scale):
    return 1.02 * seg * jnp.max(jnp.abs(wq)) * jnp.max(jnp.abs(wk)) * scale * LOG2E


def _compress_kernel(r_ref, wa_ref, wb_ref, pa_ref, pb_ref, nw_ref, b64_ref, kx_o, vx_o):
    n = r_ref.shape[0]
    r = r_ref[...]
    wa, wb = wa_ref[...], wb_ref[...]
    first = jnp.dot(r, wa.astype(BF16), preferred_element_type=F32)
    second = jnp.dot(r, wb.astype(BF16), preferred_element_type=F32)
    nxt = pltpu.roll(second, n - 1, 0)
    rowi = lax.broadcasted_iota(jnp.int32, nxt.shape, 0)
    nxt = jnp.where(rowi == n - 1, 0.0, nxt)
    const = (jnp.dot(pa_ref[...], wa, precision=HIGHEST, preferred_element_type=F32)
             + jnp.dot(pb_ref[...], wb, precision=HIGHEST, preferred_element_type=F32))
    out = first + nxt + const
    k, v = out[:, 0:LANES], out[:, LANES:2 * LANES]
    ss = jnp.dot(k * k, b64_ref[...], precision=HIGHEST, preferred_element_type=F32)
    kn = k * lax.rsqrt(ss * (1.0 / NSA_HD) + EPS) * nw_ref[...]
    one_ex = _lane_fill(n, {COL_SHIFT: 1.0})
    kx_o[...] = _slots(kn, [one_ex] * NSA_KV).astype(BF16)
    vx_o[...] = _slots(v, [one_ex] * NSA_KV).astype(BF16)


def _compress(kvc, cmp_w, cmp_pos, nw):
    s = kvc.shape[0]
    n = s // CMP_STRIDE
    width = CMP_STRIDE * 2 * NSA_KV * NSA_HD
    w = cmp_w.reshape(2, 2, CMP_STRIDE, NSA_HD, NSA_HD)
    eye = jnp.eye(2, dtype=w.dtype)
    big = jnp.einsum("khldD,kK,gG->hlkgdKGD", w, eye, eye).reshape(2, width, 2 * NSA_KV * NSA_HD)
    pos = cmp_pos.reshape(2, 2, CMP_STRIDE, 1, NSA_HD)
    pos = jnp.broadcast_to(pos, (2, 2, CMP_STRIDE, NSA_KV, NSA_HD)).transpose(1, 2, 0, 3, 4).reshape(2, 1, width)
    b64 = _block_diag_ones(LANES, NSA_HD)
    nwt = jnp.tile(nw.reshape(1, -1), (1, LANES // NSA_HD))
    out = jax.ShapeDtypeStruct((n, NSA_KV * LANES), BF16)
    return pl.pallas_call(
        _compress_kernel,
        grid=(1,),
        in_specs=[_full((n, width)), _full(big[0].shape), _full(big[1].shape), _full((1, width)), _full((1, width)),
                  _full((1, LANES)), _full((LANES, LANES))],
        out_specs=[_full((n, NSA_KV * LANES))] * 2,
        out_shape=[out, out],
        compiler_params=_cparams(("arbitrary",)),
        name="nsa_compress",
    )(kvc.reshape(n, width), big[0], big[1], pos[0], pos[1], nwt, b64)


def _nsa_cmp_kernel(q_ref, kc_ref, vc_ref, ov_ref, o_ref, sel_ref, *, tq, groups, bounded):
    i = pl.program_id(1)
    ncmp = kc_ref.shape[0]
    rows = groups * tq
    qT = jnp.concatenate([q_ref[:, g * QK_DIM:(g + 1) * QK_DIM].T for g in range(groups)], axis=1).astype(BF16)
    s = jnp.dot(kc_ref[...], qT, preferred_element_type=F32)
    n_idx = lax.broadcasted_iota(jnp.int32, (ncmp, rows), 0)
    lane = lax.broadcasted_iota(jnp.int32, (ncmp, rows), 1)
    qpos = i * tq + (lane & (tq - 1))
    keep = n_idx * CMP_STRIDE + (CMP_LEN - 1) <= qpos
    if bounded:
        e = jnp.where(keep, jnp.exp2(s), 0.0)
    else:
        sm = jnp.where(keep, s, NEG_INIT)
        e = jnp.where(keep, jnp.exp2(sm - jnp.max(sm, axis=0, keepdims=True)), 0.0)
    den = jnp.sum(e, axis=0, keepdims=True)
    p = e / jnp.where(den > 0.0, den, 1.0)
    o = lax.dot_general(vc_ref[...], p.astype(BF16), (((0,), (0,)), ((), ())), preferred_element_type=F32)
    for g in range(groups):
        o_ref[:, g * QK_DIM:(g + 1) * QK_DIM] = o[:, g * tq:(g + 1) * tq].T

    psum = p[:, 0:tq]
    for g in range(1, groups):
        psum = psum + p[:, g * tq:(g + 1) * tq]
    hi = psum.astype(BF16)
    lo = (psum - hi.astype(F32)).astype(BF16)
    ov = ov_ref[...]
    imp = (jnp.dot(ov, hi, preferred_element_type=F32) + jnp.dot(ov, lo, preferred_element_type=F32))
    ns = imp.shape[0]
    blk = lax.broadcasted_iota(jnp.int32, (ns, tq), 0)
    qp = i * tq + lax.broadcasted_iota(jnp.int32, (ns, tq), 1)
    cur = qp // SLC_LEN
    valid = blk <= cur
    forced = (blk == 0) | (blk == cur) | (blk == cur - 1)
    work = jnp.where(valid, imp + jnp.where(forced, FORCE_BONUS, 0.0), NEG_INIT)
    picked = jnp.zeros((ns, tq), dtype=jnp.bool_)
    for _ in range(SLC_TOPK):
        mx = jnp.max(work, axis=0, keepdims=True)
        idx = jnp.min(jnp.where(work == mx, blk, ns), axis=0, keepdims=True)
        pick = (blk == idx) & (mx > 0.5 * NEG_INIT)
        picked = picked | pick
        work = jnp.where(pick, NEG_INIT, work)
    sel_ref[0] = jnp.where(picked, 0.0, MASKVAL).astype(BF16)


def _nsa_cmp(qn, kcx, vcx, overlapT, tq, bound_ok):
    s = qn.shape[0]
    ncmp = kcx.shape[0]
    ns = overlapT.shape[0]
    groups = NSA_G

    def call(bounded):
        return pl.pallas_call(
            functools.partial(_nsa_cmp_kernel, tq=tq, groups=groups, bounded=bounded),
            grid=(NSA_KV, s // tq),
            in_specs=[pl.BlockSpec((tq, groups * QK_DIM), lambda h, i: (i, h)),
                      pl.BlockSpec((ncmp, QK_DIM), lambda h, i: (0, h)),
                      pl.BlockSpec((ncmp, QK_DIM), lambda h, i: (0, h)),
                      _full(overlapT.shape)],
            out_specs=[pl.BlockSpec((tq, groups * QK_DIM), lambda h, i: (i, h)),
                       pl.BlockSpec((1, ns, tq), lambda h, i: (h, 0, i))],
            out_shape=[jax.ShapeDtypeStruct(qn.shape, F32), jax.ShapeDtypeStruct((NSA_KV, ns, s), BF16)],
            compiler_params=_cparams(("parallel", "parallel")),
            name="nsa_cmp_bounded" if bounded else "nsa_cmp_max",
        )(qn, kcx, vcx, overlapT)

    return lax.cond(bound_ok, lambda: call(True), lambda: call(False))


def _mix_out_kernel(lam_ref, h_ref, fox_ref, oc_ref, os_ref, ow_ref, dd_ref, g_ref, sw_ref, wf_ref, wn_ref, wd_ref,
                    o_ref, *, diff_scale):
    lam = lam_ref[0]
    gates = g_ref[...]
    tm = gates.shape[0]
    lane = lax.broadcasted_iota(jnp.int32, (tm, LANES), 1)
    slot = lambda ref, n: ref[:, n * LANES:(n + 1) * LANES]
    nsa = []
    for h in range(NSA_HEADS):
        c0 = FOX_HEADS + 3 * h
        nsa.append(gates[:, c0:c0 + 1] * slot(oc_ref, h) + gates[:, c0 + 1:c0 + 2] * slot(os_ref, h)
                   + gates[:, c0 + 2:c0 + 3] * slot(ow_ref, h))
    diff = []
    for h in range(DIFF_HEADS):
        a = jnp.where(lane < DIFF_V, slot(dd_ref, 2 * h) - lam * slot(dd_ref, 2 * h + 1), 0.0)
        ss = jnp.sum(a * a, axis=-1, keepdims=True)
        diff.append(a * lax.rsqrt(ss * (1.0 / DIFF_V) + EPS) * sw_ref[...] * diff_scale)
    acc = h_ref[...]
    acc = acc + jnp.dot(fox_ref[...].astype(BF16), wf_ref[...], preferred_element_type=F32)
    acc = acc + jnp.dot(jnp.concatenate(nsa, axis=1).astype(BF16), wn_ref[...], preferred_element_type=F32)
    acc = acc + jnp.dot(jnp.concatenate(diff, axis=1).astype(BF16), wd_ref[...], preferred_element_type=F32)
    o_ref[...] = acc


def _slot_rows(w, heads):
    d = w.shape[1]
    w = w.reshape(heads, -1, d)
    return jnp.concatenate([w, jnp.zeros((heads, LANES - w.shape[1], d), w.dtype)], axis=1).reshape(heads * LANES, d)


def _mix_out(lam, h, fox, oc, os_, ow, dd, gates, subln_w, w_out, diff_scale, tm=512):
    s, d = h.shape
    nf, nn = FOX_HEADS * FOX_HD, NSA_HEADS * NSA_HD
    wf = _slot_rows(w_out[:nf], FOX_HEADS).astype(BF16)
    wn = _slot_rows(w_out[nf:nf + nn], NSA_HEADS).astype(BF16)
    wd = _slot_rows(w_out[nf + nn:], DIFF_HEADS).astype(BF16)
    sw = jnp.concatenate([subln_w.reshape(1, -1), jnp.zeros((1, LANES - DIFF_V), F32)], axis=1)
    row = lambda w: pl.BlockSpec((tm, w), lambda i: (i, 0))
    return pl.pallas_call(
        functools.partial(_mix_out_kernel, diff_scale=diff_scale),
        grid=(s // tm,),
        in_specs=[pl.BlockSpec(memory_space=pltpu.SMEM), row(d), row(fox.shape[1]), row(oc.shape[1]),
                  row(os_.shape[1]), row(ow.shape[1]), row(dd.shape[1]), row(LANES), _full((1, LANES)),
                  _full(wf.shape), _full(wn.shape), _full(wd.shape)],
        out_specs=row(d),
        out_shape=jax.ShapeDtypeStruct((s, d), F32),
        compiler_params=_cparams(("parallel",)),
        name="mix_out",
    )(lam, h, fox, oc, os_, ow, dd, gates, sw, wf, wn, wd)


def _peer_query_kernel(h_ref, nw_ref, wq_ref, keys_ref, xnT_o, scT_o):
    xn = _rms(h_ref[...], nw_ref[...])
    xnT_o[...] = xn.T.astype(BF16)
    q = jnp.dot(xn.astype(BF16), wq_ref[...], preferred_element_type=F32).astype(BF16)
    half = PEER_DQ // 2
    for b in range(2 * PEER_HEADS):
        sc = jnp.dot(q[:, b * half:(b + 1) * half], keys_ref[b], preferred_element_type=F32)
        scT_o[b] = sc.T


def _peer_query(h, nw, wq_bf16, keysT_bf16, tm=512):
    s, d = h.shape
    n = wq_bf16.shape[1]
    nb = 2 * PEER_HEADS
    return pl.pallas_call(
        _peer_query_kernel,
        grid=(s // tm,),
        in_specs=[pl.BlockSpec((tm, d), lambda i: (i, 0)), _full((1, d)), _full((d, n)), _full(keysT_bf16.shape)],
        out_specs=[pl.BlockSpec((d, tm), lambda i: (0, i)), pl.BlockSpec((nb, PEER_NKEYS, tm), lambda i: (0, 0, i))],
        out_shape=[jax.ShapeDtypeStruct((d, s), BF16), jax.ShapeDtypeStruct((nb, PEER_NKEYS, s), F32)],
        compiler_params=_cparams(("parallel",)),
        name="peer_query",
    )(h, nw.reshape(1, d), wq_bf16, keysT_bf16)


def _top16_rows(s, exact_ties):
    n = s.shape[0]
    row = lax.broadcasted_iota(jnp.int32, s.shape, 0)
    rank = jnp.full(s.shape, float(PEER_TOPK), dtype=F32)
    work = s
    vals = []
    for r in range(PEER_TOPK):
        mx = jnp.max(work, axis=0, keepdims=True)
        if exact_ties:
            pick = row == jnp.min(jnp.where(work == mx, row, n), axis=0, keepdims=True)
        else:
            pick = work == mx
        rank = jnp.where(pick, float(r), rank)
        work = jnp.where(pick, NEG_INIT, work)
        vals.append(mx)
    return rank, vals


def _peer_route(s1, s2, exact_ties):
    rank1, v1 = _top16_rows(s1, exact_ties)
    rank2, v2 = _top16_rows(s2, exact_ties)
    k = PEER_TOPK
    tm = s1.shape[1]
    r16 = lax.broadcasted_iota(jnp.int32, (k, tm), 0)
    v1m = jnp.zeros((k, tm), F32)
    v2m = jnp.zeros((k, tm), F32)
    for r in range(k):
        v1m = jnp.where(r16 == r, v1[r], v1m)
        v2m = jnp.where(r16 == r, v2[r], v2m)
    ea = jnp.exp(v1m - v1[0])
    eb = jnp.exp(v2m - v2[0])
    cands, gates = [v1m[0:1] + v2m], [ea[0:1] * eb]
    for r in range(1, 8):
        cands.append(v1m[r:r + 1] + v2m[0:8])
        gates.append(ea[r:r + 1] * eb[0:8])
    cands.append(v1m[8:16] + v2m[0:1])
    gates.append(ea[8:16] * eb[0:1])
    cand = jnp.concatenate(cands, axis=0)
    gate = jnp.concatenate(gates, axis=0)
    ncand = cand.shape[0]
    crow = lax.broadcasted_iota(jnp.int32, cand.shape, 0)
    for r in range(2, 8):
        start = 16 + 8 * (r - 1)
        cand = jnp.where((crow >= start + k // (r + 1)) & (crow < start + 8), NEG_INIT, cand)
    work = cand
    picked = jnp.zeros(cand.shape, dtype=jnp.bool_)
    for _ in range(k):
        mx = jnp.max(work, axis=0, keepdims=True)
        if exact_ties:
            pick = crow == jnp.min(jnp.where(work == mx, crow, ncand), axis=0, keepdims=True)
        else:
            pick = work == mx
        picked = picked | pick
        work = jnp.where(pick, NEG_INIT, work)
    pf = picked.astype(F32)
    z = jnp.sum(pf * gate, axis=0, keepdims=True)
    cnt = [jnp.sum(pf[0:16, :], axis=0, keepdims=True)]
    cnt += [jnp.sum(pf[8 + 8 * r:16 + 8 * r, :], axis=0, keepdims=True) for r in range(1, 8)]
    cnt += [pf[64 + r:65 + r, :] for r in range(8, 16)]
    cmap = jnp.zeros(s1.shape, dtype=F32)
    for r in range(k):
        cmap = jnp.where(rank1 == float(r), cnt[r], cmap)
    in1 = rank1 < float(k)
    in2 = rank2 < float(k)
    a = jnp.where(in1, jnp.exp(s1 - v1[0]), 0.0) / (2.0 * z)
    b = jnp.where(in2, jnp.exp(s2 - v2[0]), 0.0)
    excess = (jnp.abs(jnp.sum(in1.astype(F32), axis=0, keepdims=True) - k)
              + jnp.abs(jnp.sum(in2.astype(F32), axis=0, keepdims=True) - k)
              + jnp.abs(jnp.sum(pf, axis=0, keepdims=True) - k))
    return a, cmap, b, rank2, excess


def _peer_topk_kernel(sc_ref, a_o, c_o, b_o, r_o):
    def run(exact_ties):
        a, c, b, rank2, excess = _peer_route(sc_ref[0], sc_ref[1], exact_ties)
        a_o[0] = a
        c_o[0] = c
        b_o[0] = b.astype(BF16)
        r_o[0] = rank2.astype(BF16)
        return excess

    excess = run(False)

    @pl.when(jnp.max(excess) > 0.0)
    def _():
        run(True)


def _peer_topk(scT, tm=256):
    n2, nk, s = scT.shape
    heads = n2 // 2
    ospec = pl.BlockSpec((1, nk, tm), lambda h, t: (h, 0, t))
    return pl.pallas_call(
        _peer_topk_kernel,
        grid=(heads, s // tm),
        in_specs=[pl.BlockSpec((2, nk, tm), lambda h, t: (h, 0, t))],
        out_specs=[ospec] * 4,
        out_shape=[jax.ShapeDtypeStruct((heads, nk, s), dt) for dt in (F32, F32, BF16, BF16)],
        compiler_params=_cparams(("parallel", "parallel")),
        name="peer_topk",
    )(scT)


def _gelu_exact_x2(x):
    return x * (1.0 + lax.erf(x * (2.0 ** -0.5)))


def _peer_main_kernel(xT_ref, u_ref, v_ref, a_ref, c_ref, b_ref, r_ref, o_ref, w_sc, *, ec, sub):
    ci = pl.program_id(1)

    @pl.when(ci == 0)
    def _():
        o_ref[...] = jnp.zeros_like(o_ref)

    nk = PEER_NKEYS
    tm = xT_ref.shape[1]

    def row(ref, h, i1, lanes):
        x16 = jnp.broadcast_to(ref[h, pl.ds(i1, 1), :][:, lanes], (16, GATE_LANES)).astype(BF16)
        return jnp.concatenate([x16] * (nk // 16), axis=0)

    acc = None
    for sc in range(ec // sub):
        hid = jnp.dot(u_ref[sc * sub:(sc + 1) * sub, :], xT_ref[...], preferred_element_type=F32)
        act = _gelu_exact_x2(hid).astype(BF16)
        for ii in range(sub // nk):
            slab = sc * (sub // nk) + ii
            i1 = ci * (ec // nk) + slab
            for l0 in range(0, tm, GATE_LANES):
                lanes = slice(l0, l0 + GATE_LANES)
                w = None
                for h in range(PEER_HEADS):
                    keep = r_ref[h, :, lanes] < row(c_ref, h, i1, lanes)
                    term = jnp.where(keep, b_ref[h, :, lanes], 0.0) * row(a_ref, h, i1, lanes)
                    w = term if w is None else w + term
                w_sc[slab * nk:(slab + 1) * nk, lanes] = w * act[ii * nk:(ii + 1) * nk, lanes]
        part = jnp.dot(w_sc[sc * sub:(sc + 1) * sub, :].T, v_ref[sc * sub:(sc + 1) * sub, :],
                       preferred_element_type=F32)
        acc = part if acc is None else acc + part
    o_ref[...] += acc


def _peer_main(xnT, u_bf16, v_bf16, a, c, b, r, tm=GATE_LANES, ec=2048, sub=1024):
    d, s = xnT.shape
    e = u_bf16.shape[0]
    heads, nk, _ = a.shape
    rt = pl.BlockSpec((heads, nk, tm), lambda t, ci: (0, 0, t))
    wt = pl.BlockSpec((ec, d), lambda t, ci: (ci, 0))
    return pl.pallas_call(
        functools.partial(_peer_main_kernel, ec=ec, sub=sub),
        grid=(s // tm, e // ec),
        in_specs=[pl.BlockSpec((d, tm), lambda t, ci: (0, t)), wt, wt, rt, rt, rt, rt],
        out_specs=pl.BlockSpec((tm, d), lambda t, ci: (t, 0)),
        out_shape=jax.ShapeDtypeStruct((s, d), F32),
        scratch_shapes=[pltpu.VMEM((ec, tm), BF16)],
        compiler_params=_cparams(("parallel", "arbitrary")),
        name="peer_main",
    )(xnT, u_bf16, v_bf16, a, c, b, r)


def _ple_kernel(h_ref, peer_ref, p_ref, nw_ref, wg_ref, wp_ref, o_ref):
    h2 = h_ref[...] + peer_ref[...]
    xn = _rms(h2, nw_ref[...]).astype(BF16)
    gate = 1.0 / (1.0 + jnp.exp(-jnp.dot(xn, wg_ref[...], preferred_element_type=F32)))
    emb = jnp.dot(p_ref[...].astype(BF16), wp_ref[...], preferred_element_type=F32)
    o_ref[...] = h2 + gate * emb


def _ple(h, peer, p, nw, wg_bf16, wp_bf16, tm=512):
    s, d = h.shape
    pd = p.shape[1]
    row = lambda w: pl.BlockSpec((tm, w), lambda i: (i, 0))
    return pl.pallas_call(
        _ple_kernel,
        grid=(s // tm,),
        in_specs=[row(d), row(d), row(pd), _full((1, d)), _full((d, d)), _full((pd, d))],
        out_specs=row(d),
        out_shape=jax.ShapeDtypeStruct((s, d), F32),
        compiler_params=_cparams(("parallel",)),
        name="ple",
    )(h, peer, p, nw.reshape(1, d), wg_bf16, wp_bf16)


def _split_w_in(w):
    f0 = 3 * FOX_HEADS * FOX_HD
    n0 = f0 + FOX_HEADS
    g0 = n0 + _SEG["dq"][0] - _SEG["nq"][0]
    d0 = g0 + 3 * NSA_HEADS
    misc = jnp.concatenate([w[:, f0:n0], w[:, g0:d0], jnp.zeros((w.shape[0], LANES - (n0 - f0) - (d0 - g0)), w.dtype)],
                           axis=1)
    return [t.astype(BF16) for t in (w[:, :f0], w[:, n0:g0], w[:, d0:], misc)]


def _overlap_T(s):
    n = np.arange(s // CMP_STRIDE)[None, :] * CMP_STRIDE
    m = np.arange(s // SLC_LEN)[:, None] * SLC_LEN
    return jnp.asarray(((n < m + SLC_LEN) & (n + CMP_LEN > m)).astype(np.float32), dtype=BF16)


def _layer(h, p_i, tabs, layer, attn_norm_w, w_in, fox_f_bias, fox_q_norm_w, fox_k_norm_w, nsa_q_norm_w,
           nsa_k_norm_w, nsa_cmp_pos, nsa_cmp_w, diff_q_norm_w, diff_k_norm_w, diff_lambda, diff_subln_w,
           w_out, ffn_norm_w, peer_w_q, peer_sub_keys, peer_u, peer_v, ple_norm_w, ple_w_gate, ple_w_proj):
    s = h.shape[0]
    fb =jnp.zeros((1, LANES), F32).at[0, :FOX_HEADS].set(fox_f_bias)
    u_f = _score_bound(fox_q_norm_w, fox_k_norm_w, FOX_HD, FOX_HD ** -0.5)
    u_n = _score_bound(nsa_q_norm_w, nsa_k_norm_w, NSA_HD, NSA_HD ** -0.5)
    u_d = _score_bound(diff_q_norm_w, diff_k_norm_w, DIFF_QK, DIFF_QK ** -0.5)
    shifts = SHIFT_HEADROOM - jnp.stack([u_f, u_n, u_d]).astype(F32)
    (fq, fk, fv, nqn, nqr, kvc, ks, vs, kw, vw, gates, dq, dk, dv, csum) = _prep(
        shifts, h, attn_norm_w, _split_w_in(w_in), tabs, fb, fox_q_norm_w, fox_k_norm_w, nsa_q_norm_w, nsa_k_norm_w, diff_q_norm_w, diff_k_norm_w)

    c_first_q = csum[0::Q_LANES, :FOX_HEADS]
    c_last_k = csum[TK_CAUSAL - 1::TK_CAUSAL, :FOX_HEADS]
    decay = (c_first_q.T[:, :, None] - c_last_k.T[:, None, :]).reshape(-1)
    o_fox = _flash(fq, fk, fv, None, decay, bound_ok=u_f <= SCORE_BOUND, groups=1, tq=Q_LANES, tk=TK_CAUSAL)

    tq_n = Q_LANES // NSA_G
    nsa_ok = u_n <= SCORE_BOUND
    kcx, vcx = _compress(kvc, nsa_cmp_w, nsa_cmp_pos, nsa_k_norm_w)
    o_c, sel = _nsa_cmp(nqn, kcx, vcx, _overlap_T(s), tq_n, nsa_ok)
    o_s = _flash(nqr, ks, vs, sel, bound_ok=nsa_ok, groups=NSA_G, tq=tq_n, tk=TK_CAUSAL)
    o_w = _flash(nqr, kw, vw, bound_ok=nsa_ok, groups=NSA_G, tq=tq_n, tk=512, window=WIN)

    o_d = _flash(dq, dk, dv, bound_ok=u_d <= SCORE_BOUND, groups=2, tq=Q_LANES // 2, tk=TK_CAUSAL)
    lv = diff_lambda.astype(F32)
    lam_init = 0.8 - 0.6 * math.exp(-0.3 * layer)
    lam = (jnp.exp(jnp.sum(lv[0] * lv[1])) - jnp.exp(jnp.sum(lv[2] * lv[3])) + lam_init).reshape(1)
    h1 = _mix_out(lam, h, o_fox, o_c, o_s, o_w, o_d, gates, diff_subln_w, w_out, 1.0 - lam_init)

    keysT = peer_sub_keys.reshape(2 * PEER_HEADS, PEER_NKEYS, PEER_DQ // 2).transpose(0, 2, 1).astype(BF16)
    xnT, scT = _peer_query(h1, ffn_norm_w, peer_w_q.astype(BF16), keysT)
    a, c, b, r = _peer_topk(scT)
    peer = _peer_main(xnT, peer_u.astype(BF16), peer_v.astype(BF16), a, c, b, r)

    return _ple(h1, peer, p_i, ple_norm_w, ple_w_gate.astype(BF16), ple_w_proj.astype(BF16))


def kernel(x, p, positions, attn_norm_w, w_in, fox_f_bias, fox_q_norm_w, fox_k_norm_w, nsa_q_norm_w, nsa_k_norm_w,
           nsa_cmp_pos, nsa_cmp_w, diff_q_norm_w, diff_k_norm_w, diff_lambda, diff_subln_w, w_out, ffn_norm_w,
           peer_w_q, peer_sub_keys, peer_u, peer_v, ple_norm_w, ple_w_gate, ple_w_proj):
    b, s, d = x.shape
    assert b == 1 and d == D_MODEL and s % Q_LANES == 0 and s % TK_CAUSAL == 0
    tabs = _rope_tables(positions)
    h = x.reshape(s, d)
    per_layer = (attn_norm_w, w_in, fox_f_bias, fox_q_norm_w, fox_k_norm_w, nsa_q_norm_w, nsa_k_norm_w, nsa_cmp_pos,
                 nsa_cmp_w, diff_q_norm_w, diff_k_norm_w, diff_lambda, diff_subln_w, w_out, ffn_norm_w, peer_w_q,
                 peer_sub_keys, peer_u, peer_v, ple_norm_w, ple_w_gate, ple_w_proj)
    for layer in range(attn_norm_w.shape[0]):
        h = _layer(h, p[layer, 0], tabs, layer, *(w[layer] for w in per_layer))
    return h.reshape(b, s, d)
```

```python
import functools
import math

import numpy as np
import jax
import jax.numpy as jnp
from jax import lax
from jax.experimental import pallas as pl
from jax.experimental.pallas import tpu as pltpu

F32 = jnp.float32
BF16 = jnp.bfloat16
HIGHEST = lax.Precision.HIGHEST

D_MODEL = 1024
PLE_DIM = 256
ROPE_THETA = 10000.0
EPS = 1e-6
FOX_HEADS, FOX_HD = 4, 64
NSA_HEADS, NSA_KV, NSA_HD = 8, 2, 64
NSA_G = NSA_HEADS // NSA_KV
CMP_LEN, CMP_STRIDE, SLC_LEN, SLC_TOPK, WIN = 32, 16, 64, 16, 512
FORCE_BONUS = 1.0e4
assert FORCE_BONUS > NSA_G
DIFF_HEADS, DIFF_QK, DIFF_V = 4, 32, 64
PEER_HEADS, PEER_NKEYS, PEER_DQ, PEER_TOPK = 8, 128, 256, 16
PEER_EXPERTS = PEER_NKEYS * PEER_NKEYS

LANES = 128
LOG2E = math.log2(math.e)
NEG_INIT = -1.0e30
MASKVAL = -2.0e30
VMEM_LIMIT = 56 * 1024 * 1024
TK_CAUSAL = 2048
Q_LANES = 2048
GATE_LANES = 512

_SEG = dict(fq=(0, 256), fk=(256, 256), fv=(512, 256), nq=(768, 512), nkc=(1280, 128), nvc=(1408, 128),
            nks=(1536, 128), nvs=(1664, 128), nkw=(1792, 128), nvw=(1920, 128),
            dq=(2048, 256), dk=(2304, 256), dv=(2560, 256), misc=(2816, 128))
PROJ_W = 2944


def _cparams(sem):
    return pltpu.CompilerParams(dimension_semantics=sem, vmem_limit_bytes=VMEM_LIMIT)


def _full(shape):
    n = len(shape)
    return pl.BlockSpec(shape, lambda *_: (0,) * n)


def _rope_tab_kernel(pos_ref, f64_ref, g64_ref, f32_ref, g32_ref, c64_o, s64_o, c32_o, s32_o):
    pos = pos_ref[...].astype(F32)
    a64 = pos * f64_ref[...]
    c64_o[...] = jnp.cos(a64)
    s64_o[...] = jnp.sin(a64) * g64_ref[...]
    a32 = pos * f32_ref[...]
    c32_o[...] = jnp.cos(a32)
    s32_o[...] = jnp.sin(a32) * g32_ref[...]


def _rope_tables(positions):
    s = positions.shape[-1]
    pos = positions.reshape(s, 1)
    lane = np.arange(LANES)

    def lanes(half):
        inv = ROPE_THETA ** (-jnp.arange(half, dtype=F32) / half)
        freq = inv[(lane % (2 * half)) % half].reshape(1, LANES)
        sign = np.where((lane % (2 * half)) < half, -1.0, 1.0).astype(np.float32).reshape(1, LANES)
        return freq, jnp.asarray(sign)

    f64, g64 = lanes(NSA_HD // 2)
    f32_, g32 = lanes(DIFF_QK // 2)
    tm = 512
    out = jax.ShapeDtypeStruct((s, LANES), F32)
    row = pl.BlockSpec((tm, LANES), lambda i: (i, 0))
    return pl.pallas_call(
        _rope_tab_kernel,
        grid=(s // tm,),
        in_specs=[pl.BlockSpec((tm, 1), lambda i: (i, 0))] + [_full((1, LANES))] * 4,
        out_specs=[row] * 4,
        out_shape=[out] * 4,
        compiler_params=_cparams(("parallel",)),
        name="rope_tables",
    )(pos, f64, g64, f32_, g32)


def _rms(x, w):
    return x * lax.rsqrt(jnp.mean(x * x, axis=-1, keepdims=True) + EPS) * w


def _seg_rms(x, bmat, seg):
    outs = []
    for c0 in range(0, x.shape[1], 256):
        w = min(256, x.shape[1] - c0)
        xc = x[:, c0:c0 + w]
        sq = xc * xc
        hi = sq.astype(BF16)
        lo = (sq - hi.astype(F32)).astype(BF16)
        b = bmat[:w, :w].astype(BF16)
        ss = jnp.dot(hi, b, preferred_element_type=F32) + jnp.dot(lo, b, preferred_element_type=F32)
        outs.append(xc * lax.rsqrt(ss * (1.0 / seg) + EPS))
    return outs[0] if len(outs) == 1 else jnp.concatenate(outs, axis=1)


def _tile_lanes(t, width):
    reps = width // LANES
    return t if reps == 1 else jnp.concatenate([t] * reps, axis=1)


def _rope(x, cos, sin, half):
    width = x.shape[1]
    left = pltpu.roll(x, width - half, 1)
    right = pltpu.roll(x, half, 1)
    lane = lax.broadcasted_iota(jnp.int32, x.shape, 1)
    swapped = jnp.where((lane & (2 * half - 1)) < half, left, right)
    return x * _tile_lanes(cos, width) + swapped * _tile_lanes(sin, width)


def _lane_fill(tm, cols):
    lane = lax.broadcasted_iota(jnp.int32, (tm, LANES), 1)
    out = jnp.zeros((tm, LANES), F32)
    for l, v in cols.items():
        out = jnp.where(lane == l, v, out)
    return out


def _slots(x, extras):
    lane = lax.broadcasted_iota(jnp.int32, (x.shape[0], LANES), 1)
    outs = []
    for h, ex in enumerate(extras):
        col = x[:, (h // 2) * LANES:(h // 2 + 1) * LANES]
        if h % 2:
            col = pltpu.roll(col, LANES // 2, 1)
        outs.append(jnp.where(lane < LANES // 2, col, ex))
    return jnp.concatenate(outs, axis=1)


def _prep_kernel(shift_ref, proj_ref, c64_ref, s64_ref, c32_ref, s32_ref, fb_ref, wfq_ref, wfk_ref, wnq_ref,
                 wnk_ref, wdq_ref, wdk_ref, b64_ref, b32_ref,
                 fq_o, fk_o, fv_o, nqn_o, nqr_o, kvc_o, ks_o, vs_o, kw_o, vw_o, gate_o, dq_o, dk_o, dv_o, cs_o,
                 carry_sc, *, tm):
    def seg(name):
        c0, w = _SEG[name]
        return proj_ref[:, c0:c0 + w]

    b64 = b64_ref[...]
    b32 = b32_ref[...]
    c64, s64, c32, s32 = c64_ref[...], s64_ref[...], c32_ref[...], s32_ref[...]
    one_ex = _lane_fill(tm, {COL_SHIFT: 1.0})

    misc = seg("misc")
    gate_o[...] = 1.0 / (1.0 + jnp.exp(-misc))
    t = misc + fb_ref[...]
    logf = jnp.minimum(t, 0.0) - jnp.log1p(jnp.exp(-jnp.abs(t)))

    @pl.when(pl.program_id(0) == 0)
    def _():
        carry_sc[...] = jnp.zeros_like(carry_sc)

    r = lax.broadcasted_iota(jnp.int32, (tm, tm), 0)
    c = lax.broadcasted_iota(jnp.int32, (tm, tm), 1)
    tri = (c <= r).astype(F32)
    csum = jnp.dot(tri, logf, precision=HIGHEST, preferred_element_type=F32) + carry_sc[0:1, :]
    carry_sc[...] = jnp.broadcast_to(csum[tm - 1:tm, :], carry_sc.shape)
    csum = csum * LOG2E
    hi = csum.astype(BF16).astype(F32)
    r1 = csum - hi
    mid = r1.astype(BF16).astype(F32)
    lo = (r1 - mid).astype(BF16).astype(F32)
    cs_o[...] = csum

    fq = _seg_rms(seg("fq"), b64, FOX_HD) * wfq_ref[...] * (FOX_HD ** -0.5 * LOG2E)
    fk = _seg_rms(seg("fk"), b64, FOX_HD) * wfk_ref[...]
    q_ex, k_ex = [], []
    for h in range(FOX_HEADS):
        ch, cm, cl = hi[:, h:h + 1], mid[:, h:h + 1], lo[:, h:h + 1]
        q_ex.append(_lane_fill(tm, {COL_SHIFT: shift_ref[0], 65: 1.0, 66: 1.0, 67: 1.0, 68: ch, 69: cm, 70: cl}))
        k_ex.append(_lane_fill(tm, {COL_SHIFT: 1.0, 65: -ch, 66: -cm, 67: -cl, 68: 1.0, 69: 1.0, 70: 1.0}))
    fq_o[...] = _slots(fq, q_ex)
    fk_o[...] = _slots(fk, k_ex).astype(BF16)
    fv_o[...] = _slots(seg("fv"), [one_ex] * FOX_HEADS).astype(BF16)

    n_ex = [_lane_fill(tm, {COL_SHIFT: shift_ref[1]})] * NSA_HEADS
    nqn = _seg_rms(seg("nq"), b64, NSA_HD) * wnq_ref[...]
    nqn_o[...] = _slots(nqn * (NSA_HD ** -0.5 * LOG2E), n_ex)
    nqr_o[...] = _slots(_rope(nqn, c64, s64, NSA_HD // 2) * (NSA_HD ** -0.5 * LOG2E), n_ex)
    kvc_o[:, 0:128] = seg("nkc").astype(BF16)
    kvc_o[:, 128:256] = seg("nvc").astype(BF16)
    wnk = wnk_ref[...]
    lane = lax.broadcasted_iota(jnp.int32, (tm, LANES), 1)
    row = pl.program_id(0) * tm + lax.broadcasted_iota(jnp.int32, (tm, LANES), 0)
    blk_in_tile = (row & (TK_CAUSAL - 1)) // SLC_LEN
    sel_ex = jnp.where(lane == COL_SEL + blk_in_tile, 1.0, one_ex)
    ks = _rope(_seg_rms(seg("nks"), b64, NSA_HD) * wnk, c64, s64, NSA_HD // 2)
    ks_o[...] = _slots(ks, [sel_ex] * NSA_KV).astype(BF16)
    vs_o[...] = _slots(seg("nvs"), [one_ex] * NSA_KV).astype(BF16)
    kw = _rope(_seg_rms(seg("nkw"), b64, NSA_HD) * wnk, c64, s64, NSA_HD // 2)
    kw_o[...] = _slots(kw, [one_ex] * NSA_KV).astype(BF16)
    vw_o[...] = _slots(seg("nvw"), [one_ex] * NSA_KV).astype(BF16)

    dq = _rope(_seg_rms(seg("dq"), b32, DIFF_QK) * wdq_ref[...], c32, s32, DIFF_QK // 2) * (DIFF_QK ** -0.5 * LOG2E)
    d_ex = _lane_fill(tm, {COL_SHIFT: shift_ref[2]})
    d_slots = []
    for h in range(DIFF_HEADS):
        col = dq[:, (h // 2) * LANES:(h // 2 + 1) * LANES]
        if h % 2:
            col = pltpu.roll(col, LANES // 2, 1)
        d_slots.append(jnp.where(lane < DIFF_QK, col, d_ex))
        d_slots.append(jnp.where((lane >= DIFF_QK) & (lane < 2 * DIFF_QK), col, d_ex))
    dq_o[...] = jnp.concatenate(d_slots, axis=1)
    dk = _rope(_seg_rms(seg("dk"), b32, DIFF_QK) * wdk_ref[...], c32, s32, DIFF_QK // 2)
    dk_o[...] = _slots(dk, [one_ex] * DIFF_HEADS).astype(BF16)
    dv_o[...] = _slots(seg("dv"), [one_ex] * DIFF_HEADS).astype(BF16)


def _block_diag_ones(n, seg):
    i = np.arange(n)
    return jnp.asarray((i[:, None] // seg == i[None, :] // seg).astype(np.float32))


def _proj_prep_kernel(shift_ref, x_ref, nw_ref, wa_ref, wb_ref, wc_ref, wm_ref, *rest, tm):
    proj_sc = rest[-1]
    xn = _rms(x_ref[...], nw_ref[...]).astype(BF16)
    c0 = 0
    for w_ref in (wa_ref, wb_ref, wc_ref, wm_ref):
        n = w_ref.shape[1]
        proj_sc[:, c0:c0 + n] = jnp.dot(xn, w_ref[...], preferred_element_type=F32)
        c0 += n
    _prep_kernel(shift_ref, proj_sc, *rest[:-1], tm=tm)


def _prep(shifts, x, nw, ws_bf16, tabs, fb, wfq, wfk, wnq, wnk, wdq, wdk, tm=512):
    s, d = x.shape
    c64, s64, c32, s32 = tabs
    b64 = _block_diag_ones(256, 64)
    b32 = _block_diag_ones(256, 32)

    def tiled(w, width):
        return jnp.tile(w.reshape(1, -1), (1, width // w.shape[-1]))

    consts = [fb, tiled(wfq, 256), tiled(wfk, 256), tiled(wnq, 512), tiled(wnk, 128), tiled(wdq, 256),
              tiled(wdk, 256), b64, b32]
    outs = [(FOX_HEADS, F32), (FOX_HEADS, BF16), (FOX_HEADS, BF16),
            (NSA_HEADS, F32), (NSA_HEADS, F32), (2, BF16),
            (NSA_KV, BF16), (NSA_KV, BF16), (NSA_KV, BF16), (NSA_KV, BF16), (1, F32),
            (2 * DIFF_HEADS, F32), (DIFF_HEADS, BF16), (DIFF_HEADS, BF16),
            (1, F32)]
    row = lambda w: pl.BlockSpec((tm, w), lambda i: (i, 0))
    assert sum(w.shape[1] for w in ws_bf16) == PROJ_W
    return pl.pallas_call(
        functools.partial(_proj_prep_kernel, tm=tm),
        grid=(s // tm,),
        in_specs=[pl.BlockSpec(memory_space=pltpu.SMEM), row(d), _full((1, d))] + [_full(w.shape) for w in ws_bf16]
                 + [row(LANES)] * 4 + [_full(c.shape) for c in consts],
        out_specs=[row(n * LANES) for n, _ in outs],
        out_shape=[jax.ShapeDtypeStruct((s, n * LANES), dt) for n, dt in outs],
        scratch_shapes=[pltpu.VMEM((8, LANES), F32), pltpu.VMEM((tm, PROJ_W), F32)],
        compiler_params=_cparams(("arbitrary",)),
        name="proj_head_prep",
    )(shifts, x, nw.reshape(1, d), *ws_bf16, c64, s64, c32, s32, *consts)


QK_DIM = 128
COL_SHIFT = 64
COL_SEL = 80
SCORE_BOUND = 60.0
SHIFT_HEADROOM = 60.0
EXP2_FLUSH = 200.0


def _flash_kernel(code_ref, q_ref, k_ref, v_ref, *rest, groups, tq, tk, window, has_sel, bounded, decay_tiles):
    rest = list(rest)
    sel_ref = rest.pop(0) if has_sel else None
    decay_ref = rest.pop(0) if decay_tiles else None
    o_ref = rest.pop(0)
    qT_sc = rest.pop(0)
    m_sc = None if bounded else rest.pop(0)
    acc_sc = rest.pop(0)
    code = code_ref[pl.program_id(1)]
    i = code & 0xFFF
    j = (code >> 12) & 0xFFF
    first = (code >> 24) & 1
    last = (code >> 25) & 1
    rows = groups * tq

    @pl.when(first == 1)
    def _():
        for g in range(groups):
            qT_sc[:, g * tq:(g + 1) * tq] = q_ref[:, g * QK_DIM:(g + 1) * QK_DIM].T.astype(BF16)
        if not bounded:
            m_sc[...] = jnp.full_like(m_sc, NEG_INIT)
        acc_sc[...] = jnp.zeros_like(acc_sc)

    def step(masked):
        if has_sel:
            qT_sc[COL_SEL:COL_SEL + tk // SLC_LEN, :] = _tile_lanes_any(sel_ref[0], groups)
        s = jnp.dot(k_ref[...], qT_sc[...], preferred_element_type=F32)
        if masked:
            kpos = j * tk + lax.broadcasted_iota(jnp.int32, (tk, rows), 0)
            lane = lax.broadcasted_iota(jnp.int32, (tk, rows), 1)
            qpos = i * tq + (lane & (tq - 1))
            keep = kpos <= qpos
            if window is not None:
                keep = keep & (kpos > qpos - window)
            s = jnp.where(keep, s, MASKVAL)
        v_t = (((0,), (0,)), ((), ()))
        v = v_ref[...]
        if bounded:
            acc_sc[...] += lax.dot_general(v, jnp.exp2(s).astype(BF16), v_t, preferred_element_type=F32)
        else:
            m_prev = m_sc[...]
            m_new = jnp.maximum(m_prev, jnp.max(s, axis=0, keepdims=True))
            p = jnp.exp2(s - m_new).astype(BF16)
            acc_sc[...] = (acc_sc[...] * jnp.exp2(m_prev - m_new)
                           + lax.dot_general(v, p, v_t, preferred_element_type=F32))
            m_sc[...] = m_new

    if window is not None:
        step(True)
    else:
        needs_mask = (j + 1) * tk - 1 > i * tq

        @pl.when(needs_mask)
        def _():
            step(True)

        @pl.when(jnp.logical_not(needs_mask))
        def _():
            if decay_tiles:
                nq, nk = decay_tiles
                bias_max = decay_ref[(pl.program_id(0) * nq + i) * nk + j]

                @pl.when(SHIFT_HEADROOM + bias_max > -EXP2_FLUSH)
                def _():
                    step(False)
            else:
                step(False)

    @pl.when(last == 1)
    def _():
        l = acc_sc[COL_SHIFT:COL_SHIFT + 1, :]
        o = acc_sc[...] / jnp.where(l > 0.0, l, 1.0)
        for g in range(groups):
            o_ref[:, g * QK_DIM:(g + 1) * QK_DIM] = o[:, g * tq:(g + 1) * tq].T


def _tile_lanes_any(t, reps):
    return t if reps == 1 else jnp.concatenate([t] * reps, axis=1)


def _flash_call(q, k, v, sel, decay, *, groups, tq, tk, window, bounded):
    s_len = q.shape[0]
    hkv = k.shape[1] // QK_DIM
    nq = s_len // tq
    rows = groups * tq
    codes = []
    for i in range(nq):
        q_lo, q_hi = i * tq, i * tq + tq - 1
        j_hi = q_hi // tk
        j_lo = 0 if window is None else max(0, (q_lo - window + 1) // tk)
        for j in range(j_lo, j_hi + 1):
            codes.append(i | (j << 12) | (int(j == j_lo) << 24) | (int(j == j_hi) << 25))
    codes = jnp.asarray(np.asarray(codes, dtype=np.int32))
    nsteps = codes.shape[0]

    def ti(c, s):
        return c[s] & 0xFFF

    def tj(c, s):
        return (c[s] >> 12) & 0xFFF

    in_specs = [
        pl.BlockSpec((tq, groups * QK_DIM), lambda h, s, c: (ti(c, s), h)),
        pl.BlockSpec((tk, QK_DIM), lambda h, s, c: (tj(c, s), h)),
        pl.BlockSpec((tk, QK_DIM), lambda h, s, c: (tj(c, s), h)),
    ]
    args = [q, k, v]
    if sel is not None:
        in_specs.append(pl.BlockSpec((1, tk // SLC_LEN, tq), lambda h, s, c: (h, tj(c, s), ti(c, s))))
        args.append(sel)
    use_decay = decay is not None and bounded
    if use_decay:
        in_specs.append(pl.BlockSpec(memory_space=pltpu.SMEM))
        args.append(decay)
    scratch = [pltpu.VMEM((QK_DIM, rows), BF16)]
    if not bounded:
        scratch.append(pltpu.VMEM((1, rows), F32))
    scratch.append(pltpu.VMEM((QK_DIM, rows), F32))
    kern = functools.partial(_flash_kernel, groups=groups, tq=tq, tk=tk, window=window, has_sel=sel is not None,
                             bounded=bounded, decay_tiles=(nq, s_len // tk) if use_decay else None)
    return pl.pallas_call(
        kern,
        grid_spec=pltpu.PrefetchScalarGridSpec(
            num_scalar_prefetch=1,
            grid=(hkv, nsteps),
            in_specs=in_specs,
            out_specs=pl.BlockSpec((tq, groups * QK_DIM), lambda h, s, c: (ti(c, s), h)),
            scratch_shapes=scratch,
        ),
        out_shape=jax.ShapeDtypeStruct(q.shape, F32),
        compiler_params=_cparams(("parallel", "arbitrary")),
        name="flash_bounded" if bounded else "flash_online",
    )(codes, *args)


def _flash(q, k, v, sel=None, decay=None, *, bound_ok, groups, tq, tk, window=None):
    call = functools.partial(_flash_call, q, k, v, sel, decay, groups=groups, tq=tq, tk=tk, window=window)
    return lax.cond(bound_ok, lambda: call(bounded=True), lambda: call(bounded=False))


def _score_bound(wq, wk, seg, scale):
    return 1.02 * seg * jnp.max(jnp.abs(wq)) * jnp.max(jnp.abs(wk)) * scale * LOG2E


def _compress_kernel(r_ref, wa_ref, wb_ref, pa_ref, pb_ref, nw_ref, b64_ref, kx_o, vx_o):
    n = r_ref.shape[0]
    r = r_ref[...]
    wa, wb = wa_ref[...], wb_ref[...]
    first = jnp.dot(r, wa.astype(BF16), preferred_element_type=F32)
    second = jnp.dot(r, wb.astype(BF16), preferred_element_type=F32)
    nxt = pltpu.roll(second, n - 1, 0)
    rowi = lax.broadcasted_iota(jnp.int32, nxt.shape, 0)
    nxt = jnp.where(rowi == n - 1, 0.0, nxt)
    const = (jnp.dot(pa_ref[...], wa, precision=HIGHEST, preferred_element_type=F32)
             + jnp.dot(pb_ref[...], wb, precision=HIGHEST, preferred_element_type=F32))
    out = first + nxt + const
    k, v = out[:, 0:LANES], out[:, LANES:2 * LANES]
    ss = jnp.dot(k * k, b64_ref[...], precision=HIGHEST, preferred_element_type=F32)
    kn = k * lax.rsqrt(ss * (1.0 / NSA_HD) + EPS) * nw_ref[...]
    one_ex = _lane_fill(n, {COL_SHIFT: 1.0})
    kx_o[...] = _slots(kn, [one_ex] * NSA_KV).astype(BF16)
    vx_o[...] = _slots(v, [one_ex] * NSA_KV).astype(BF16)


def _compress(kvc, cmp_w, cmp_pos, nw):
    s = kvc.shape[0]
    n = s // CMP_STRIDE
    width = CMP_STRIDE * 2 * NSA_KV * NSA_HD
    w = cmp_w.reshape(2, 2, CMP_STRIDE, NSA_HD, NSA_HD)
    eye = jnp.eye(2, dtype=w.dtype)
    big = jnp.einsum("khldD,kK,gG->hlkgdKGD", w, eye, eye).reshape(2, width, 2 * NSA_KV * NSA_HD)
    pos = cmp_pos.reshape(2, 2, CMP_STRIDE, 1, NSA_HD)
    pos = jnp.broadcast_to(pos, (2, 2, CMP_STRIDE, NSA_KV, NSA_HD)).transpose(1, 2, 0, 3, 4).reshape(2, 1, width)
    b64 = _block_diag_ones(LANES, NSA_HD)
    nwt = jnp.tile(nw.reshape(1, -1), (1, LANES // NSA_HD))
    out = jax.ShapeDtypeStruct((n, NSA_KV * LANES), BF16)
    return pl.pallas_call(
        _compress_kernel,
        grid=(1,),
        in_specs=[_full((n, width)), _full(big[0].shape), _full(big[1].shape), _full((1, width)), _full((1, width)),
                  _full((1, LANES)), _full((LANES, LANES))],
        out_specs=[_full((n, NSA_KV * LANES))] * 2,
        out_shape=[out, out],
        compiler_params=_cparams(("arbitrary",)),
        name="nsa_compress",
    )(kvc.reshape(n, width), big[0], big[1], pos[0], pos[1], nwt, b64)


def _nsa_cmp_kernel(q_ref, kc_ref, vc_ref, ov_ref, o_ref, sel_ref, *, tq, groups, bounded):
    i = pl.program_id(1)
    ncmp = kc_ref.shape[0]
    rows = groups * tq
    qT = jnp.concatenate([q_ref[:, g * QK_DIM:(g + 1) * QK_DIM].T for g in range(groups)], axis=1).astype(BF16)
    s = jnp.dot(kc_ref[...], qT, preferred_element_type=F32)
    n_idx = lax.broadcasted_iota(jnp.int32, (ncmp, rows), 0)
    lane = lax.broadcasted_iota(jnp.int32, (ncmp, rows), 1)
    qpos = i * tq + (lane & (tq - 1))
    keep = n_idx * CMP_STRIDE + (CMP_LEN - 1) <= qpos
    if bounded:
        e = jnp.where(keep, jnp.exp2(s), 0.0)
    else:
        sm = jnp.where(keep, s, NEG_INIT)
        e = jnp.where(keep, jnp.exp2(sm - jnp.max(sm, axis=0, keepdims=True)), 0.0)
    den = jnp.sum(e, axis=0, keepdims=True)
    p = e / jnp.where(den > 0.0, den, 1.0)
    o = lax.dot_general(vc_ref[...], p.astype(BF16), (((0,), (0,)), ((), ())), preferred_element_type=F32)
    for g in range(groups):
        o_ref[:, g * QK_DIM:(g + 1) * QK_DIM] = o[:, g * tq:(g + 1) * tq].T

    psum = p[:, 0:tq]
    for g in range(1, groups):
        psum = psum + p[:, g * tq:(g + 1) * tq]
    hi = psum.astype(BF16)
    lo = (psum - hi.astype(F32)).astype(BF16)
    ov = ov_ref[...]
    imp = (jnp.dot(ov, hi, preferred_element_type=F32) + jnp.dot(ov, lo, preferred_element_type=F32))
    ns = imp.shape[0]
    blk = lax.broadcasted_iota(jnp.int32, (ns, tq), 0)
    qp = i * tq + lax.broadcasted_iota(jnp.int32, (ns, tq), 1)
    cur = qp // SLC_LEN
    valid = blk <= cur
    forced = valid & ((blk == 0) | (blk == cur) | (blk == cur - 1))
    n_forced = jnp.sum(forced.astype(F32), axis=0, keepdims=True)
    target = jnp.minimum(jnp.sum(valid.astype(F32), axis=0, keepdims=True), float(SLC_TOPK))
    work0 = jnp.where(valid & jnp.logical_not(forced), imp, NEG_INIT)

    def select(exact_ties):
        work, picked = work0, forced
        for it in range(SLC_TOPK - 1):
            mx = jnp.max(work, axis=0, keepdims=True)
            live = (mx > 0.5 * NEG_INIT) & (n_forced + float(it) < float(SLC_TOPK))
            if exact_ties:
                pick = blk == jnp.min(jnp.where(work == mx, blk, ns), axis=0, keepdims=True)
            else:
                pick = work == mx
            pick = pick & live
            picked = picked | pick
            work = jnp.where(pick, NEG_INIT, work)
        sel_ref[0] = jnp.where(picked, 0.0, MASKVAL).astype(BF16)
        return jnp.abs(jnp.sum(picked.astype(F32), axis=0, keepdims=True) - target)

    excess = select(False)

    @pl.when(jnp.max(excess) > 0.0)
    def _():
        select(True)


def _nsa_cmp(qn, kcx, vcx, overlapT, tq, bound_ok):
    s = qn.shape[0]
    ncmp = kcx.shape[0]
    ns = overlapT.shape[0]
    groups = NSA_G

    def call(bounded):
        return pl.pallas_call(
            functools.partial(_nsa_cmp_kernel, tq=tq, groups=groups, bounded=bounded),
            grid=(NSA_KV, s // tq),
            in_specs=[pl.BlockSpec((tq, groups * QK_DIM), lambda h, i: (i, h)),
                      pl.BlockSpec((ncmp, QK_DIM), lambda h, i: (0, h)),
                      pl.BlockSpec((ncmp, QK_DIM), lambda h, i: (0, h)),
                      _full(overlapT.shape)],
            out_specs=[pl.BlockSpec((tq, groups * QK_DIM), lambda h, i: (i, h)),
                       pl.BlockSpec((1, ns, tq), lambda h, i: (h, 0, i))],
            out_shape=[jax.ShapeDtypeStruct(qn.shape, F32), jax.ShapeDtypeStruct((NSA_KV, ns, s), BF16)],
            compiler_params=_cparams(("parallel", "parallel")),
            name="nsa_cmp_bounded" if bounded else "nsa_cmp_max",
        )(qn, kcx, vcx, overlapT)

    return lax.cond(bound_ok, lambda: call(True), lambda: call(False))


def _mix_out_kernel(lam_ref, h_ref, fox_ref, oc_ref, os_ref, ow_ref, dd_ref, g_ref, sw_ref, wf_ref, wn_ref, wd_ref,
                    o_ref, *, diff_scale):
    lam = lam_ref[0]
    gates = g_ref[...]
    tm = gates.shape[0]
    lane = lax.broadcasted_iota(jnp.int32, (tm, LANES), 1)
    slot = lambda ref, n: ref[:, n * LANES:(n + 1) * LANES]
    nsa = []
    for h in range(NSA_HEADS):
        c0 = FOX_HEADS + 3 * h
        nsa.append(gates[:, c0:c0 + 1] * slot(oc_ref, h) + gates[:, c0 + 1:c0 + 2] * slot(os_ref, h)
                   + gates[:, c0 + 2:c0 + 3] * slot(ow_ref, h))
    diff = []
    for h in range(DIFF_HEADS):
        a = jnp.where(lane < DIFF_V, slot(dd_ref, 2 * h) - lam * slot(dd_ref, 2 * h + 1), 0.0)
        ss = jnp.sum(a * a, axis=-1, keepdims=True)
        diff.append(a * lax.rsqrt(ss * (1.0 / DIFF_V) + EPS) * sw_ref[...] * diff_scale)
    acc = h_ref[...]
    acc = acc + jnp.dot(fox_ref[...].astype(BF16), wf_ref[...], preferred_element_type=F32)
    acc = acc + jnp.dot(jnp.concatenate(nsa, axis=1).astype(BF16), wn_ref[...], preferred_element_type=F32)
    acc = acc + jnp.dot(jnp.concatenate(diff, axis=1).astype(BF16), wd_ref[...], preferred_element_type=F32)
    o_ref[...] = acc


def _slot_rows(w, heads):
    d = w.shape[1]
    w = w.reshape(heads, -1, d)
    return jnp.concatenate([w, jnp.zeros((heads, LANES - w.shape[1], d), w.dtype)], axis=1).reshape(heads * LANES, d)


def _mix_out(lam, h, fox, oc, os_, ow, dd, gates, subln_w, w_out, diff_scale, tm=512):
    s, d = h.shape
    nf, nn = FOX_HEADS * FOX_HD, NSA_HEADS * NSA_HD
    wf = _slot_rows(w_out[:nf], FOX_HEADS).astype(BF16)
    wn = _slot_rows(w_out[nf:nf + nn], NSA_HEADS).astype(BF16)
    wd = _slot_rows(w_out[nf + nn:], DIFF_HEADS).astype(BF16)
    sw = jnp.concatenate([subln_w.reshape(1, -1), jnp.zeros((1, LANES - DIFF_V), F32)], axis=1)
    row = lambda w: pl.BlockSpec((tm, w), lambda i: (i, 0))
    return pl.pallas_call(
        functools.partial(_mix_out_kernel, diff_scale=diff_scale),
        grid=(s // tm,),
        in_specs=[pl.BlockSpec(memory_space=pltpu.SMEM), row(d), row(fox.shape[1]), row(oc.shape[1]),
                  row(os_.shape[1]), row(ow.shape[1]), row(dd.shape[1]), row(LANES), _full((1, LANES)),
                  _full(wf.shape), _full(wn.shape), _full(wd.shape)],
        out_specs=row(d),
        out_shape=jax.ShapeDtypeStruct((s, d), F32),
        compiler_params=_cparams(("parallel",)),
        name="mix_out",
    )(lam, h, fox, oc, os_, ow, dd, gates, sw, wf, wn, wd)


def _peer_query_kernel(h_ref, nw_ref, wq_ref, keys_ref, xnT_o, scT_o):
    xn = _rms(h_ref[...], nw_ref[...])
    xnT_o[...] = xn.T.astype(BF16)
    q = jnp.dot(xn.astype(BF16), wq_ref[...], preferred_element_type=F32).astype(BF16)
    half = PEER_DQ // 2
    for b in range(2 * PEER_HEADS):
        sc = jnp.dot(q[:, b * half:(b + 1) * half], keys_ref[b], preferred_element_type=F32)
        scT_o[b] = sc.T


def _peer_query(h, nw, wq_bf16, keysT_bf16, tm=512):
    s, d = h.shape
    n = wq_bf16.shape[1]
    nb = 2 * PEER_HEADS
    return pl.pallas_call(
        _peer_query_kernel,
        grid=(s // tm,),
        in_specs=[pl.BlockSpec((tm, d), lambda i: (i, 0)), _full((1, d)), _full((d, n)), _full(keysT_bf16.shape)],
        out_specs=[pl.BlockSpec((d, tm), lambda i: (0, i)), pl.BlockSpec((nb, PEER_NKEYS, tm), lambda i: (0, 0, i))],
        out_shape=[jax.ShapeDtypeStruct((d, s), BF16), jax.ShapeDtypeStruct((nb, PEER_NKEYS, s), F32)],
        compiler_params=_cparams(("parallel",)),
        name="peer_query",
    )(h, nw.reshape(1, d), wq_bf16, keysT_bf16)


def _top16_rows(s, exact_ties):
    n = s.shape[0]
    row = lax.broadcasted_iota(jnp.int32, s.shape, 0)
    rank = jnp.full(s.shape, float(PEER_TOPK), dtype=F32)
    work = s
    vals = []
    for r in range(PEER_TOPK):
        mx = jnp.max(work, axis=0, keepdims=True)
        if exact_ties:
            pick = row == jnp.min(jnp.where(work == mx, row, n), axis=0, keepdims=True)
        else:
            pick = work == mx
        rank = jnp.where(pick, float(r), rank)
        work = jnp.where(pick, NEG_INIT, work)
        vals.append(mx)
    return rank, vals


def _peer_route(s1, s2, exact_ties):
    rank1, v1 = _top16_rows(s1, exact_ties)
    rank2, v2 = _top16_rows(s2, exact_ties)
    k = PEER_TOPK
    tm = s1.shape[1]
    r16 = lax.broadcasted_iota(jnp.int32, (k, tm), 0)
    v1m = jnp.zeros((k, tm), F32)
    v2m = jnp.zeros((k, tm), F32)
    for r in range(k):
        v1m = jnp.where(r16 == r, v1[r], v1m)
        v2m = jnp.where(r16 == r, v2[r], v2m)
    ea = jnp.exp(v1m - v1[0])
    eb = jnp.exp(v2m - v2[0])
    cands, gates = [v1m[0:1] + v2m], [ea[0:1] * eb]
    for r in range(1, 8):
        cands.append(v1m[r:r + 1] + v2m[0:8])
        gates.append(ea[r:r + 1] * eb[0:8])
    cands.append(v1m[8:16] + v2m[0:1])
    gates.append(ea[8:16] * eb[0:1])
    cand = jnp.concatenate(cands, axis=0)
    gate = jnp.concatenate(gates, axis=0)
    ncand = cand.shape[0]
    crow = lax.broadcasted_iota(jnp.int32, cand.shape, 0)
    for r in range(2, 8):
        start = 16 + 8 * (r - 1)
        cand = jnp.where((crow >= start + k // (r + 1)) & (crow < start + 8), NEG_INIT, cand)
    work = cand
    picked = jnp.zeros(cand.shape, dtype=jnp.bool_)
    for _ in range(k):
        mx = jnp.max(work, axis=0, keepdims=True)
        if exact_ties:
            pick = crow == jnp.min(jnp.where(work == mx, crow, ncand), axis=0, keepdims=True)
        else:
            pick = work == mx
        picked = picked | pick
        work = jnp.where(pick, NEG_INIT, work)
    pf = picked.astype(F32)
    z = jnp.sum(pf * gate, axis=0, keepdims=True)
    cnt = [jnp.sum(pf[0:16, :], axis=0, keepdims=True)]
    cnt += [jnp.sum(pf[8 + 8 * r:16 + 8 * r, :], axis=0, keepdims=True) for r in range(1, 8)]
    cnt += [pf[64 + r:65 + r, :] for r in range(8, 16)]
    cmap = jnp.zeros(s1.shape, dtype=F32)
    for r in range(k):
        cmap = jnp.where(rank1 == float(r), cnt[r], cmap)
    in1 = rank1 < float(k)
    in2 = rank2 < float(k)
    a = jnp.where(in1, jnp.exp(s1 - v1[0]), 0.0) / (2.0 * z)
    b = jnp.where(in2, jnp.exp(s2 - v2[0]), 0.0)
    excess = (jnp.abs(jnp.sum(in1.astype(F32), axis=0, keepdims=True) - k)
              + jnp.abs(jnp.sum(in2.astype(F32), axis=0, keepdims=True) - k)
              + jnp.abs(jnp.sum(pf, axis=0, keepdims=True) - k))
    return a, cmap, b, rank2, excess


def _peer_topk_kernel(sc_ref, a_o, c_o, b_o, r_o):
    def run(exact_ties):
        a, c, b, rank2, excess = _peer_route(sc_ref[0], sc_ref[1], exact_ties)
        a_o[0] = a
        c_o[0] = c
        b_o[0] = b.astype(BF16)
        r_o[0] = rank2.astype(BF16)
        return excess

    excess = run(False)

    @pl.when(jnp.max(excess) > 0.0)
    def _():
        run(True)


def _peer_topk(scT, tm=256):
    n2, nk, s = scT.shape
    heads = n2 // 2
    ospec = pl.BlockSpec((1, nk, tm), lambda h, t: (h, 0, t))
    return pl.pallas_call(
        _peer_topk_kernel,
        grid=(heads, s // tm),
        in_specs=[pl.BlockSpec((2, nk, tm), lambda h, t: (h, 0, t))],
        out_specs=[ospec] * 4,
        out_shape=[jax.ShapeDtypeStruct((heads, nk, s), dt) for dt in (F32, F32, BF16, BF16)],
        compiler_params=_cparams(("parallel", "parallel")),
        name="peer_topk",
    )(scT)


def _gelu_exact_x2(x):
    return x * (1.0 + lax.erf(x * (2.0 ** -0.5)))


def _peer_main_kernel(xT_ref, u_ref, v_ref, a_ref, c_ref, b_ref, r_ref, o_ref, w_sc, *, ec, sub):
    ci = pl.program_id(1)

    @pl.when(ci == 0)
    def _():
        o_ref[...] = jnp.zeros_like(o_ref)

    nk = PEER_NKEYS
    tm = xT_ref.shape[1]

    def row(ref, h, i1, lanes):
        x16 = jnp.broadcast_to(ref[h, pl.ds(i1, 1), :][:, lanes], (16, GATE_LANES)).astype(BF16)
        return jnp.concatenate([x16] * (nk // 16), axis=0)

    acc = None
    for sc in range(ec // sub):
        hid = jnp.dot(u_ref[sc * sub:(sc + 1) * sub, :], xT_ref[...], preferred_element_type=F32)
        act = _gelu_exact_x2(hid).astype(BF16)
        for ii in range(sub // nk):
            slab = sc * (sub // nk) + ii
            i1 = ci * (ec // nk) + slab
            for l0 in range(0, tm, GATE_LANES):
                lanes = slice(l0, l0 + GATE_LANES)
                w = None
                for h in range(PEER_HEADS):
                    keep = r_ref[h, :, lanes] < row(c_ref, h, i1, lanes)
                    term = jnp.where(keep, b_ref[h, :, lanes], 0.0) * row(a_ref, h, i1, lanes)
                    w = term if w is None else w + term
                w_sc[slab * nk:(slab + 1) * nk, lanes] = w * act[ii * nk:(ii + 1) * nk, lanes]
        part = jnp.dot(w_sc[sc * sub:(sc + 1) * sub, :].T, v_ref[sc * sub:(sc + 1) * sub, :],
                       preferred_element_type=F32)
        acc = part if acc is None else acc + part
    o_ref[...] += acc


def _peer_main(xnT, u_bf16, v_bf16, a, c, b, r, tm=GATE_LANES, ec=2048, sub=1024):
    d, s = xnT.shape
    e = u_bf16.shape[0]
    heads, nk, _ = a.shape
    rt = pl.BlockSpec((heads, nk, tm), lambda t, ci: (0, 0, t))
    wt = pl.BlockSpec((ec, d), lambda t, ci: (ci, 0))
    return pl.pallas_call(
        functools.partial(_peer_main_kernel, ec=ec, sub=sub),
        grid=(s // tm, e // ec),
        in_specs=[pl.BlockSpec((d, tm), lambda t, ci: (0, t)), wt, wt, rt, rt, rt, rt],
        out_specs=pl.BlockSpec((tm, d), lambda t, ci: (t, 0)),
        out_shape=jax.ShapeDtypeStruct((s, d), F32),
        scratch_shapes=[pltpu.VMEM((ec, tm), BF16)],
        compiler_params=_cparams(("parallel", "arbitrary")),
        name="peer_main",
    )(xnT, u_bf16, v_bf16, a, c, b, r)


def _ple_kernel(h_ref, peer_ref, p_ref, nw_ref, wg_ref, wp_ref, o_ref):
    h2 = h_ref[...] + peer_ref[...]
    xn = _rms(h2, nw_ref[...]).astype(BF16)
    gate = 1.0 / (1.0 + jnp.exp(-jnp.dot(xn, wg_ref[...], preferred_element_type=F32)))
    emb = jnp.dot(p_ref[...].astype(BF16), wp_ref[...], preferred_element_type=F32)
    o_ref[...] = h2 + gate * emb


def _ple(h, peer, p, nw, wg_bf16, wp_bf16, tm=512):
    s, d = h.shape
    pd = p.shape[1]
    row = lambda w: pl.BlockSpec((tm, w), lambda i: (i, 0))
    return pl.pallas_call(
        _ple_kernel,
        grid=(s // tm,),
        in_specs=[row(d), row(d), row(pd), _full((1, d)), _full((d, d)), _full((pd, d))],
        out_specs=row(d),
        out_shape=jax.ShapeDtypeStruct((s, d), F32),
        compiler_params=_cparams(("parallel",)),
        name="ple",
    )(h, peer, p, nw.reshape(1, d), wg_bf16, wp_bf16)


def _split_w_in(w):
    f0 = 3 * FOX_HEADS * FOX_HD
    n0 = f0 + FOX_HEADS
    g0 = n0 + _SEG["dq"][0] - _SEG["nq"][0]
    d0 = g0 + 3 * NSA_HEADS
    misc = jnp.concatenate([w[:, f0:n0], w[:, g0:d0], jnp.zeros((w.shape[0], LANES - (n0 - f0) - (d0 - g0)), w.dtype)],
                           axis=1)
    return [t.astype(BF16) for t in (w[:, :f0], w[:, n0:g0], w[:, d0:], misc)]


def _overlap_T(s):
    n = np.arange(s // CMP_STRIDE)[None, :] * CMP_STRIDE
    m = np.arange(s // SLC_LEN)[:, None] * SLC_LEN
    return jnp.asarray(((n < m + SLC_LEN) & (n + CMP_LEN > m)).astype(np.float32), dtype=BF16)


def _layer(h, p_i, tabs, layer, attn_norm_w, w_in, fox_f_bias, fox_q_norm_w, fox_k_norm_w, nsa_q_norm_w,
           nsa_k_norm_w, nsa_cmp_pos, nsa_cmp_w, diff_q_norm_w, diff_k_norm_w, diff_lambda, diff_subln_w,
           w_out, ffn_norm_w, peer_w_q, peer_sub_keys, peer_u, peer_v, ple_norm_w, ple_w_gate, ple_w_proj):
    s = h.shape[0]
    fb =jnp.zeros((1, LANES), F32).at[0, :FOX_HEADS].set(fox_f_bias)
    u_f = _score_bound(fox_q_norm_w, fox_k_norm_w, FOX_HD, FOX_HD ** -0.5)
    u_n = _score_bound(nsa_q_norm_w, nsa_k_norm_w, NSA_HD, NSA_HD ** -0.5)
    u_d = _score_bound(diff_q_norm_w, diff_k_norm_w, DIFF_QK, DIFF_QK ** -0.5)
    shifts = SHIFT_HEADROOM - jnp.stack([u_f, u_n, u_d]).astype(F32)
    (fq, fk, fv, nqn, nqr, kvc, ks, vs, kw, vw, gates, dq, dk, dv, csum) = _prep(
        shifts, h, attn_norm_w, _split_w_in(w_in), tabs, fb, fox_q_norm_w, fox_k_norm_w, nsa_q_norm_w, nsa_k_norm_w, diff_q_norm_w, diff_k_norm_w)

    c_first_q = csum[0::Q_LANES, :FOX_HEADS]
    c_last_k = csum[TK_CAUSAL - 1::TK_CAUSAL, :FOX_HEADS]
    decay = (c_first_q.T[:, :, None] - c_last_k.T[:, None, :]).reshape(-1)
    o_fox = _flash(fq, fk, fv, None, decay, bound_ok=u_f <= SCORE_BOUND, groups=1, tq=Q_LANES, tk=TK_CAUSAL)

    tq_n = Q_LANES // NSA_G
    nsa_ok = u_n <= SCORE_BOUND
    kcx, vcx = _compress(kvc, nsa_cmp_w, nsa_cmp_pos, nsa_k_norm_w)
    o_c, sel = _nsa_cmp(nqn, kcx, vcx, _overlap_T(s), tq_n, nsa_ok)
    o_s = _flash(nqr, ks, vs, sel, bound_ok=nsa_ok, groups=NSA_G, tq=tq_n, tk=TK_CAUSAL)
    o_w = _flash(nqr, kw, vw, bound_ok=nsa_ok, groups=NSA_G, tq=tq_n, tk=512, window=WIN)

    o_d = _flash(dq, dk, dv, bound_ok=u_d <= SCORE_BOUND, groups=2, tq=Q_LANES // 2, tk=TK_CAUSAL)
    lv = diff_lambda.astype(F32)
    lam_init = 0.8 - 0.6 * math.exp(-0.3 * layer)
    lam = (jnp.exp(jnp.sum(lv[0] * lv[1])) - jnp.exp(jnp.sum(lv[2] * lv[3])) + lam_init).reshape(1)
    h1 = _mix_out(lam, h, o_fox, o_c, o_s, o_w, o_d, gates, diff_subln_w, w_out, 1.0 - lam_init)

    keysT = peer_sub_keys.reshape(2 * PEER_HEADS, PEER_NKEYS, PEER_DQ // 2).transpose(0, 2, 1).astype(BF16)
    xnT, scT = _peer_query(h1, ffn_norm_w, peer_w_q.astype(BF16), keysT)
    a, c, b, r = _peer_topk(scT)
    peer = _peer_main(xnT, peer_u.astype(BF16), peer_v.astype(BF16), a, c, b, r)

    return _ple(h1, peer, p_i, ple_norm_w, ple_w_gate.astype(BF16), ple_w_proj.astype(BF16))


def kernel(x, p, positions, attn_norm_w, w_in, fox_f_bias, fox_q_norm_w, fox_k_norm_w, nsa_q_norm_w, nsa_k_norm_w,
           nsa_cmp_pos, nsa_cmp_w, diff_q_norm_w, diff_k_norm_w, diff_lambda, diff_subln_w, w_out, ffn_norm_w,
           peer_w_q, peer_sub_keys, peer_u, peer_v, ple_norm_w, ple_w_gate, ple_w_proj):
    b, s, d = x.shape
    assert b == 1 and d == D_MODEL and s % Q_LANES == 0 and s % TK_CAUSAL == 0
    tabs = _rope_tables(positions)
    h = x.reshape(s, d)
    per_layer = (attn_norm_w, w_in, fox_f_bias, fox_q_norm_w, fox_k_norm_w, nsa_q_norm_w, nsa_k_norm_w, nsa_cmp_pos,
                 nsa_cmp_w, diff_q_norm_w, diff_k_norm_w, diff_lambda, diff_subln_w, w_out, ffn_norm_w, peer_w_q,
                 peer_sub_keys, peer_u, peer_v, ple_norm_w, ple_w_gate, ple_w_proj)
    for layer in range(attn_norm_w.shape[0]):
        h = _layer(h, p[layer, 0], tabs, layer, *(w[layer] for w in per_layer))
    return h.reshape(b, s, d)
```

```python
import functools
import math

import numpy as np
import jax
import jax.numpy as jnp
from jax import lax
from jax.experimental import pallas as pl
from jax.experimental.pallas import tpu as pltpu

F32 = jnp.float32
BF16 = jnp.bfloat16
HIGHEST = lax.Precision.HIGHEST

D_MODEL = 1024
PLE_DIM = 256
ROPE_THETA = 10000.0
EPS = 1e-6
FOX_HEADS, FOX_HD = 4, 64
NSA_HEADS, NSA_KV, NSA_HD = 8, 2, 64
NSA_G = NSA_HEADS // NSA_KV
CMP_LEN, CMP_STRIDE, SLC_LEN, SLC_TOPK, WIN = 32, 16, 64, 16, 512
FORCE_BONUS = 1.0e4
assert FORCE_BONUS > NSA_G
DIFF_HEADS, DIFF_QK, DIFF_V = 4, 32, 64
PEER_HEADS, PEER_NKEYS, PEER_DQ, PEER_TOPK = 8, 128, 256, 16
PEER_EXPERTS = PEER_NKEYS * PEER_NKEYS

LANES = 128
LOG2E = math.log2(math.e)
NEG_INIT = -1.0e30
MASKVAL = -2.0e30
VMEM_LIMIT = 56 * 1024 * 1024
TK_CAUSAL = 2048
Q_LANES = 2048
GATE_LANES = 512

_SEG = dict(fq=(0, 256), fk=(256, 256), fv=(512, 256), nq=(768, 512), nkc=(1280, 128), nvc=(1408, 128),
            nks=(1536, 128), nvs=(1664, 128), nkw=(1792, 128), nvw=(1920, 128),
            dq=(2048, 256), dk=(2304, 256), dv=(2560, 256), misc=(2816, 128))
PROJ_W = 2944


def _cparams(sem):
    return pltpu.CompilerParams(dimension_semantics=sem, vmem_limit_bytes=VMEM_LIMIT)


def _full(shape):
    n = len(shape)
    return pl.BlockSpec(shape, lambda *_: (0,) * n)


def _rope_tab_kernel(pos_ref, f64_ref, g64_ref, f32_ref, g32_ref, c64_o, s64_o, c32_o, s32_o):
    pos = pos_ref[...].astype(F32)
    a64 = pos * f64_ref[...]
    c64_o[...] = jnp.cos(a64)
    s64_o[...] = jnp.sin(a64) * g64_ref[...]
    a32 = pos * f32_ref[...]
    c32_o[...] = jnp.cos(a32)
    s32_o[...] = jnp.sin(a32) * g32_ref[...]


def _rope_tables(positions):
    s = positions.shape[-1]
    pos = positions.reshape(s, 1)
    lane = np.arange(LANES)

    def lanes(half):
        inv = ROPE_THETA ** (-jnp.arange(half, dtype=F32) / half)
        freq = inv[(lane % (2 * half)) % half].reshape(1, LANES)
        sign = np.where((lane % (2 * half)) < half, -1.0, 1.0).astype(np.float32).reshape(1, LANES)
        return freq, jnp.asarray(sign)

    f64, g64 = lanes(NSA_HD // 2)
    f32_, g32 = lanes(DIFF_QK // 2)
    tm = 512
    out = jax.ShapeDtypeStruct((s, LANES), F32)
    row = pl.BlockSpec((tm, LANES), lambda i: (i, 0))
    return pl.pallas_call(
        _rope_tab_kernel,
        grid=(s // tm,),
        in_specs=[pl.BlockSpec((tm, 1), lambda i: (i, 0))] + [_full((1, LANES))] * 4,
        out_specs=[row] * 4,
        out_shape=[out] * 4,
        compiler_params=_cparams(("parallel",)),
        name="rope_tables",
    )(pos, f64, g64, f32_, g32)


def _rms(x, w):
    return x * lax.rsqrt(jnp.mean(x * x, axis=-1, keepdims=True) + EPS) * w


def _seg_rms(x, bmat, seg):
    outs = []
    for c0 in range(0, x.shape[1], 256):
        w = min(256, x.shape[1] - c0)
        xc = x[:, c0:c0 + w]
        sq = xc * xc
        hi = sq.astype(BF16)
        lo = (sq - hi.astype(F32)).astype(BF16)
        b = bmat[:w, :w].astype(BF16)
        ss = jnp.dot(hi, b, preferred_element_type=F32) + jnp.dot(lo, b, preferred_element_type=F32)
        outs.append(xc * lax.rsqrt(ss * (1.0 / seg) + EPS))
    return outs[0] if len(outs) == 1 else jnp.concatenate(outs, axis=1)


def _tile_lanes(t, width):
    reps = width // LANES
    return t if reps == 1 else jnp.concatenate([t] * reps, axis=1)


def _rope(x, cos, sin, half):
    width = x.shape[1]
    left = pltpu.roll(x, width - half, 1)
    right = pltpu.roll(x, half, 1)
    lane = lax.broadcasted_iota(jnp.int32, x.shape, 1)
    swapped = jnp.where((lane & (2 * half - 1)) < half, left, right)
    return x * _tile_lanes(cos, width) + swapped * _tile_lanes(sin, width)


def _lane_fill(tm, cols):
    lane = lax.broadcasted_iota(jnp.int32, (tm, LANES), 1)
    out = jnp.zeros((tm, LANES), F32)
    for l, v in cols.items():
        out = jnp.where(lane == l, v, out)
    return out


def _slots(x, extras):
    lane = lax.broadcasted_iota(jnp.int32, (x.shape[0], LANES), 1)
    outs = []
    for h, ex in enumerate(extras):
        col = x[:, (h // 2) * LANES:(h // 2 + 1) * LANES]
        if h % 2:
            col = pltpu.roll(col, LANES // 2, 1)
        outs.append(jnp.where(lane < LANES // 2, col, ex))
    return jnp.concatenate(outs, axis=1)


def _prep_kernel(shift_ref, proj_ref, c64_ref, s64_ref, c32_ref, s32_ref, fb_ref, wfq_ref, wfk_ref, wnq_ref,
                 wnk_ref, wdq_ref, wdk_ref, b64_ref, b32_ref,
                 fq_o, fk_o, fv_o, nqn_o, nqr_o, kvc_o, ks_o, vs_o, kw_o, vw_o, gate_o, dq_o, dk_o, dv_o, cs_o,
                 carry_sc, *, tm):
    def seg(name):
        c0, w = _SEG[name]
        return proj_ref[:, c0:c0 + w]

    b64 = b64_ref[...]
    b32 = b32_ref[...]
    c64, s64, c32, s32 = c64_ref[...], s64_ref[...], c32_ref[...], s32_ref[...]
    one_ex = _lane_fill(tm, {COL_SHIFT: 1.0})

    misc = seg("misc")
    gate_o[...] = 1.0 / (1.0 + jnp.exp(-misc))
    t = misc + fb_ref[...]
    logf = jnp.minimum(t, 0.0) - jnp.log1p(jnp.exp(-jnp.abs(t)))

    @pl.when(pl.program_id(0) == 0)
    def _():
        carry_sc[...] = jnp.zeros_like(carry_sc)

    r = lax.broadcasted_iota(jnp.int32, (tm, tm), 0)
    c = lax.broadcasted_iota(jnp.int32, (tm, tm), 1)
    tri = (c <= r).astype(F32)
    csum = jnp.dot(tri, logf, precision=HIGHEST, preferred_element_type=F32) + carry_sc[0:1, :]
    carry_sc[...] = jnp.broadcast_to(csum[tm - 1:tm, :], carry_sc.shape)
    csum = csum * LOG2E
    hi = csum.astype(BF16).astype(F32)
    r1 = csum - hi
    mid = r1.astype(BF16).astype(F32)
    lo = (r1 - mid).astype(BF16).astype(F32)
    cs_o[...] = csum

    fq = _seg_rms(seg("fq"), b64, FOX_HD) * wfq_ref[...] * (FOX_HD ** -0.5 * LOG2E)
    fk = _seg_rms(seg("fk"), b64, FOX_HD) * wfk_ref[...]
    q_ex, k_ex = [], []
    for h in range(FOX_HEADS):
        ch, cm, cl = hi[:, h:h + 1], mid[:, h:h + 1], lo[:, h:h + 1]
        q_ex.append(_lane_fill(tm, {COL_SHIFT: shift_ref[0], 65: 1.0, 66: 1.0, 67: 1.0, 68: ch, 69: cm, 70: cl}))
        k_ex.append(_lane_fill(tm, {COL_SHIFT: 1.0, 65: -ch, 66: -cm, 67: -cl, 68: 1.0, 69: 1.0, 70: 1.0}))
    fq_o[...] = _slots(fq, q_ex)
    fk_o[...] = _slots(fk, k_ex).astype(BF16)
    fv_o[...] = _slots(seg("fv"), [one_ex] * FOX_HEADS).astype(BF16)

    n_ex = [_lane_fill(tm, {COL_SHIFT: shift_ref[1]})] * NSA_HEADS
    nqn = _seg_rms(seg("nq"), b64, NSA_HD) * wnq_ref[...]
    nqn_o[...] = _slots(nqn * (NSA_HD ** -0.5 * LOG2E), n_ex)
    nqr_o[...] = _slots(_rope(nqn, c64, s64, NSA_HD // 2) * (NSA_HD ** -0.5 * LOG2E), n_ex)
    kvc_o[:, 0:128] = seg("nkc").astype(BF16)
    kvc_o[:, 128:256] = seg("nvc").astype(BF16)
    wnk = wnk_ref[...]
    lane = lax.broadcasted_iota(jnp.int32, (tm, LANES), 1)
    row = pl.program_id(0) * tm + lax.broadcasted_iota(jnp.int32, (tm, LANES), 0)
    blk_in_tile = (row & (TK_CAUSAL - 1)) // SLC_LEN
    sel_ex = jnp.where(lane == COL_SEL + blk_in_tile, 1.0, one_ex)
    ks = _rope(_seg_rms(seg("nks"), b64, NSA_HD) * wnk, c64, s64, NSA_HD // 2)
    ks_o[...] = _slots(ks, [sel_ex] * NSA_KV).astype(BF16)
    vs_o[...] = _slots(seg("nvs"), [one_ex] * NSA_KV).astype(BF16)
    kw = _rope(_seg_rms(seg("nkw"), b64, NSA_HD) * wnk, c64, s64, NSA_HD // 2)
    kw_o[...] = _slots(kw, [one_ex] * NSA_KV).astype(BF16)
    vw_o[...] = _slots(seg("nvw"), [one_ex] * NSA_KV).astype(BF16)

    dq = _rope(_seg_rms(seg("dq"), b32, DIFF_QK) * wdq_ref[...], c32, s32, DIFF_QK // 2) * (DIFF_QK ** -0.5 * LOG2E)
    d_ex = _lane_fill(tm, {COL_SHIFT: shift_ref[2]})
    d_slots = []
    for h in range(DIFF_HEADS):
        col = dq[:, (h // 2) * LANES:(h // 2 + 1) * LANES]
        if h % 2:
            col = pltpu.roll(col, LANES // 2, 1)
        d_slots.append(jnp.where(lane < DIFF_QK, col, d_ex))
        d_slots.append(jnp.where((lane >= DIFF_QK) & (lane < 2 * DIFF_QK), col, d_ex))
    dq_o[...] = jnp.concatenate(d_slots, axis=1)
    dk = _rope(_seg_rms(seg("dk"), b32, DIFF_QK) * wdk_ref[...], c32, s32, DIFF_QK // 2)
    dk_o[...] = _slots(dk, [one_ex] * DIFF_HEADS).astype(BF16)
    dv_o[...] = _slots(seg("dv"), [one_ex] * DIFF_HEADS).astype(BF16)


def _block_diag_ones(n, seg):
    i = np.arange(n)
    return jnp.asarray((i[:, None] // seg == i[None, :] // seg).astype(np.float32))


def _proj_prep_kernel(shift_ref, x_ref, nw_ref, wa_ref, wb_ref, wc_ref, wm_ref, *rest, tm):
    proj_sc = rest[-1]
    xn = _rms(x_ref[...], nw_ref[...]).astype(BF16)
    c0 = 0
    for w_ref in (wa_ref, wb_ref, wc_ref, wm_ref):
        n = w_ref.shape[1]
        proj_sc[:, c0:c0 + n] = jnp.dot(xn, w_ref[...], preferred_element_type=F32)
        c0 += n
    _prep_kernel(shift_ref, proj_sc, *rest[:-1], tm=tm)


def _prep(shifts, x, nw, ws_bf16, tabs, fb, wfq, wfk, wnq, wnk, wdq, wdk, tm=512):
    s, d = x.shape
    c64, s64, c32, s32 = tabs
    b64 = _block_diag_ones(256, 64)
    b32 = _block_diag_ones(256, 32)

    def tiled(w, width):
        return jnp.tile(w.reshape(1, -1), (1, width // w.shape[-1]))

    consts = [fb, tiled(wfq, 256), tiled(wfk, 256), tiled(wnq, 512), tiled(wnk, 128), tiled(wdq, 256),
              tiled(wdk, 256), b64, b32]
    outs = [(FOX_HEADS, F32), (FOX_HEADS, BF16), (FOX_HEADS, BF16),
            (NSA_HEADS, F32), (NSA_HEADS, F32), (2, BF16),
            (NSA_KV, BF16), (NSA_KV, BF16), (NSA_KV, BF16), (NSA_KV, BF16), (1, F32),
            (2 * DIFF_HEADS, F32), (DIFF_HEADS, BF16), (DIFF_HEADS, BF16),
            (1, F32)]
    row = lambda w: pl.BlockSpec((tm, w), lambda i: (i, 0))
    assert sum(w.shape[1] for w in ws_bf16) == PROJ_W
    return pl.pallas_call(
        functools.partial(_proj_prep_kernel, tm=tm),
        grid=(s // tm,),
        in_specs=[pl.BlockSpec(memory_space=pltpu.SMEM), row(d), _full((1, d))] + [_full(w.shape) for w in ws_bf16]
                 + [row(LANES)] * 4 + [_full(c.shape) for c in consts],
        out_specs=[row(n * LANES) for n, _ in outs],
        out_shape=[jax.ShapeDtypeStruct((s, n * LANES), dt) for n, dt in outs],
        scratch_shapes=[pltpu.VMEM((8, LANES), F32), pltpu.VMEM((tm, PROJ_W), F32)],
        compiler_params=_cparams(("arbitrary",)),
        name="proj_head_prep",
    )(shifts, x, nw.reshape(1, d), *ws_bf16, c64, s64, c32, s32, *consts)


QK_DIM = 128
COL_SHIFT = 64
COL_SEL = 80
SCORE_BOUND = 60.0
SHIFT_HEADROOM = 60.0
EXP2_FLUSH = 200.0


def _flash_kernel(code_ref, q_ref, k_ref, v_ref, *rest, groups, tq, tk, window, has_sel, bounded, decay_tiles):
    rest = list(rest)
    sel_ref = rest.pop(0) if has_sel else None
    decay_ref = rest.pop(0) if decay_tiles else None
    o_ref = rest.pop(0)
    qT_sc = rest.pop(0)
    m_sc = None if bounded else rest.pop(0)
    acc_sc = rest.pop(0)
    code = code_ref[pl.program_id(1)]
    i = code & 0xFFF
    j = (code >> 12) & 0xFFF
    first = (code >> 24) & 1
    last = (code >> 25) & 1
    rows = groups * tq

    @pl.when(first == 1)
    def _():
        for g in range(groups):
            qT_sc[:, g * tq:(g + 1) * tq] = q_ref[:, g * QK_DIM:(g + 1) * QK_DIM].T.astype(BF16)
        if not bounded:
            m_sc[...] = jnp.full_like(m_sc, NEG_INIT)
        acc_sc[...] = jnp.zeros_like(acc_sc)

    def step(masked):
        if has_sel:
            qT_sc[COL_SEL:COL_SEL + tk // SLC_LEN, :] = _tile_lanes_any(sel_ref[0], groups)
        s = jnp.dot(k_ref[...], qT_sc[...], preferred_element_type=F32)
        if masked:
            kpos = j * tk + lax.broadcasted_iota(jnp.int32, (tk, rows), 0)
            lane = lax.broadcasted_iota(jnp.int32, (tk, rows), 1)
            qpos = i * tq + (lane & (tq - 1))
            keep = kpos <= qpos
            if window is not None:
                keep = keep & (kpos > qpos - window)
            s = jnp.where(keep, s, MASKVAL)
        v_t = (((0,), (0,)), ((), ()))
        v = v_ref[...]
        if bounded:
            acc_sc[...] += lax.dot_general(v, jnp.exp2(s).astype(BF16), v_t, preferred_element_type=F32)
        else:
            m_prev = m_sc[...]
            m_new = jnp.maximum(m_prev, jnp.max(s, axis=0, keepdims=True))
            p = jnp.exp2(s - m_new).astype(BF16)
            acc_sc[...] = (acc_sc[...] * jnp.exp2(m_prev - m_new)
                           + lax.dot_general(v, p, v_t, preferred_element_type=F32))
            m_sc[...] = m_new

    if window is not None:
        step(True)
    else:
        needs_mask = (j + 1) * tk - 1 > i * tq

        @pl.when(needs_mask)
        def _():
            step(True)

        @pl.when(jnp.logical_not(needs_mask))
        def _():
            if decay_tiles:
                nq, nk = decay_tiles
                bias_max = decay_ref[(pl.program_id(0) * nq + i) * nk + j]

                @pl.when(SHIFT_HEADROOM + bias_max > -EXP2_FLUSH)
                def _():
                    step(False)
            else:
                step(False)

    @pl.when(last == 1)
    def _():
        l = acc_sc[COL_SHIFT:COL_SHIFT + 1, :]
        o = acc_sc[...] / jnp.where(l > 0.0, l, 1.0)
        for g in range(groups):
            o_ref[:, g * QK_DIM:(g + 1) * QK_DIM] = o[:, g * tq:(g + 1) * tq].T


def _tile_lanes_any(t, reps):
    return t if reps == 1 else jnp.concatenate([t] * reps, axis=1)


def _flash_call(q, k, v, sel, decay, *, groups, tq, tk, window, bounded):
    s_len = q.shape[0]
    hkv = k.shape[1] // QK_DIM
    nq = s_len // tq
    rows = groups * tq
    codes = []
    for i in range(nq):
        q_lo, q_hi = i * tq, i * tq + tq - 1
        j_hi = q_hi // tk
        j_lo = 0 if window is None else max(0, (q_lo - window + 1) // tk)
        for j in range(j_lo, j_hi + 1):
            codes.append(i | (j << 12) | (int(j == j_lo) << 24) | (int(j == j_hi) << 25))
    codes = jnp.asarray(np.asarray(codes, dtype=np.int32))
    nsteps = codes.shape[0]

    def ti(c, s):
        return c[s] & 0xFFF

    def tj(c, s):
        return (c[s] >> 12) & 0xFFF

    in_specs = [
        pl.BlockSpec((tq, groups * QK_DIM), lambda h, s, c: (ti(c, s), h)),
        pl.BlockSpec((tk, QK_DIM), lambda h, s, c: (tj(c, s), h)),
        pl.BlockSpec((tk, QK_DIM), lambda h, s, c: (tj(c, s), h)),
    ]
    args = [q, k, v]
    if sel is not None:
        in_specs.append(pl.BlockSpec((1, tk // SLC_LEN, tq), lambda h, s, c: (h, tj(c, s), ti(c, s))))
        args.append(sel)
    use_decay = decay is not None and bounded
    if use_decay:
        in_specs.append(pl.BlockSpec(memory_space=pltpu.SMEM))
        args.append(decay)
    scratch = [pltpu.VMEM((QK_DIM, rows), BF16)]
    if not bounded:
        scratch.append(pltpu.VMEM((1, rows), F32))
    scratch.append(pltpu.VMEM((QK_DIM, rows), F32))
    kern = functools.partial(_flash_kernel, groups=groups, tq=tq, tk=tk, window=window, has_sel=sel is not None,
                             bounded=bounded, decay_tiles=(nq, s_len // tk) if use_decay else None)
    return pl.pallas_call(
        kern,
        grid_spec=pltpu.PrefetchScalarGridSpec(
            num_scalar_prefetch=1,
            grid=(hkv, nsteps),
            in_specs=in_specs,
            out_specs=pl.BlockSpec((tq, groups * QK_DIM), lambda h, s, c: (ti(c, s), h)),
            scratch_shapes=scratch,
        ),
        out_shape=jax.ShapeDtypeStruct(q.shape, F32),
        compiler_params=_cparams(("parallel", "arbitrary")),
        name="flash_bounded" if bounded else "flash_online",
    )(codes, *args)


def _flash(q, k, v, sel=None, decay=None, *, bound_ok, groups, tq, tk, window=None):
    call = functools.partial(_flash_call, q, k, v, sel, decay, groups=groups, tq=tq, tk=tk, window=window)
    return lax.cond(bound_ok, lambda: call(bounded=True), lambda: call(bounded=False))


def _score_bound(wq, wk, seg, scale):
    return 1.02 * seg * jnp.max(jnp.abs(wq)) * jnp.max(jnp.abs(wk)) * scale * LOG2E


def _compress_kernel(r_ref, wa_ref, wb_ref, pa_ref, pb_ref, nw_ref, b64_ref, kx_o, vx_o):
    n = r_ref.shape[0]
    r = r_ref[...]
    wa, wb = wa_ref[...], wb_ref[...]
    first = jnp.dot(r, wa.astype(BF16), preferred_element_type=F32)
    second = jnp.dot(r, wb.astype(BF16), preferred_element_type=F32)
    nxt = pltpu.roll(second, n - 1, 0)
    rowi = lax.broadcasted_iota(jnp.int32, nxt.shape, 0)
    nxt = jnp.where(rowi == n - 1, 0.0, nxt)
    const = (jnp.dot(pa_ref[...], wa, precision=HIGHEST, preferred_element_type=F32)
             + jnp.dot(pb_ref[...], wb, precision=HIGHEST, preferred_element_type=F32))
    out = first + nxt + const
    k, v = out[:, 0:LANES], out[:, LANES:2 * LANES]
    ss = jnp.dot(k * k, b64_ref[...], precision=HIGHEST, preferred_element_type=F32)
    kn = k * lax.rsqrt(ss * (1.0 / NSA_HD) + EPS) * nw_ref[...]
    one_ex = _lane_fill(n, {COL_SHIFT: 1.0})
    kx_o[...] = _slots(kn, [one_ex] * NSA_KV).astype(BF16)
    vx_o[...] = _slots(v, [one_ex] * NSA_KV).astype(BF16)


def _compress(kvc, cmp_w, cmp_pos, nw):
    s = kvc.shape[0]
    n = s // CMP_STRIDE
    width = CMP_STRIDE * 2 * NSA_KV * NSA_HD
    w = cmp_w.reshape(2, 2, CMP_STRIDE, NSA_HD, NSA_HD)
    eye = jnp.eye(2, dtype=w.dtype)
    big = jnp.einsum("khldD,kK,gG->hlkgdKGD", w, eye, eye).reshape(2, width, 2 * NSA_KV * NSA_HD)
    pos = cmp_pos.reshape(2, 2, CMP_STRIDE, 1, NSA_HD)
    pos = jnp.broadcast_to(pos, (2, 2, CMP_STRIDE, NSA_KV, NSA_HD)).transpose(1, 2, 0, 3, 4).reshape(2, 1, width)
    b64 = _block_diag_ones(LANES, NSA_HD)
    nwt = jnp.tile(nw.reshape(1, -1), (1, LANES // NSA_HD))
    out = jax.ShapeDtypeStruct((n, NSA_KV * LANES), BF16)
    return pl.pallas_call(
        _compress_kernel,
        grid=(1,),
        in_specs=[_full((n, width)), _full(big[0].shape), _full(big[1].shape), _full((1, width)), _full((1, width)),
                  _full((1, LANES)), _full((LANES, LANES))],
        out_specs=[_full((n, NSA_KV * LANES))] * 2,
        out_shape=[out, out],
        compiler_params=_cparams(("arbitrary",)),
        name="nsa_compress",
    )(kvc.reshape(n, width), big[0], big[1], pos[0], pos[1], nwt, b64)


def _nsa_cmp_kernel(q_ref, kc_ref, vc_ref, ov_ref, o_ref, sel_ref, *, tq, groups, bounded):
    i = pl.program_id(1)
    ncmp = kc_ref.shape[0]
    rows = groups * tq
    qT = jnp.concatenate([q_ref[:, g * QK_DIM:(g + 1) * QK_DIM].T for g in range(groups)], axis=1).astype(BF16)
    s = jnp.dot(kc_ref[...], qT, preferred_element_type=F32)
    n_idx = lax.broadcasted_iota(jnp.int32, (ncmp, rows), 0)
    lane = lax.broadcasted_iota(jnp.int32, (ncmp, rows), 1)
    qpos = i * tq + (lane & (tq - 1))
    keep = n_idx * CMP_STRIDE + (CMP_LEN - 1) <= qpos
    if bounded:
        e = jnp.where(keep, jnp.exp2(s), 0.0)
    else:
        sm = jnp.where(keep, s, NEG_INIT)
        e = jnp.where(keep, jnp.exp2(sm - jnp.max(sm, axis=0, keepdims=True)), 0.0)
    den = jnp.sum(e, axis=0, keepdims=True)
    p = e / jnp.where(den > 0.0, den, 1.0)
    o = lax.dot_general(vc_ref[...], p.astype(BF16), (((0,), (0,)), ((), ())), preferred_element_type=F32)
    for g in range(groups):
        o_ref[:, g * QK_DIM:(g + 1) * QK_DIM] = o[:, g * tq:(g + 1) * tq].T

    psum = p[:, 0:tq]
    for g in range(1, groups):
        psum = psum + p[:, g * tq:(g + 1) * tq]
    hi = psum.astype(BF16)
    lo = (psum - hi.astype(F32)).astype(BF16)
    ov = ov_ref[...]
    imp = (jnp.dot(ov, hi, preferred_element_type=F32) + jnp.dot(ov, lo, preferred_element_type=F32))
    ns = imp.shape[0]
    blk = lax.broadcasted_iota(jnp.int32, (ns, tq), 0)
    qp = i * tq + lax.broadcasted_iota(jnp.int32, (ns, tq), 1)
    cur = qp // SLC_LEN
    valid = blk <= cur
    forced = valid & ((blk == 0) | (blk == cur) | (blk == cur - 1))
    n_forced = jnp.sum(forced.astype(F32), axis=0, keepdims=True)
    target = jnp.minimum(jnp.sum(valid.astype(F32), axis=0, keepdims=True), float(SLC_TOPK))
    work0 = jnp.where(valid & jnp.logical_not(forced), imp, NEG_INIT)

    def select(exact_ties):
        work, picked = work0, forced
        for it in range(SLC_TOPK - 1):
            mx = jnp.max(work, axis=0, keepdims=True)
            live = (mx > 0.5 * NEG_INIT) & (n_forced + float(it) < float(SLC_TOPK))
            if exact_ties:
                pick = blk == jnp.min(jnp.where(work == mx, blk, ns), axis=0, keepdims=True)
            else:
                pick = work == mx
            pick = pick & live
            picked = picked | pick
            work = jnp.where(pick, NEG_INIT, work)
        sel_ref[0] = jnp.where(picked, 0.0, MASKVAL).astype(BF16)
        return jnp.abs(jnp.sum(picked.astype(F32), axis=0, keepdims=True) - target)

    excess = select(False)

    @pl.when(jnp.max(excess) > 0.0)
    def _():
        select(True)


def _nsa_cmp(qn, kcx, vcx, overlapT, tq, bound_ok):
    s = qn.shape[0]
    ncmp = kcx.shape[0]
    ns = overlapT.shape[0]
    groups = NSA_G

    def call(bounded):
        return pl.pallas_call(
            functools.partial(_nsa_cmp_kernel, tq=tq, groups=groups, bounded=bounded),
            grid=(NSA_KV, s // tq),
            in_specs=[pl.BlockSpec((tq, groups * QK_DIM), lambda h, i: (i, h)),
                      pl.BlockSpec((ncmp, QK_DIM), lambda h, i: (0, h)),
                      pl.BlockSpec((ncmp, QK_DIM), lambda h, i: (0, h)),
                      _full(overlapT.shape)],
            out_specs=[pl.BlockSpec((tq, groups * QK_DIM), lambda h, i: (i, h)),
                       pl.BlockSpec((1, ns, tq), lambda h, i: (h, 0, i))],
            out_shape=[jax.ShapeDtypeStruct(qn.shape, F32), jax.ShapeDtypeStruct((NSA_KV, ns, s), BF16)],
            compiler_params=_cparams(("parallel", "parallel")),
            name="nsa_cmp_bounded" if bounded else "nsa_cmp_max",
        )(qn, kcx, vcx, overlapT)

    return lax.cond(bound_ok, lambda: call(True), lambda: call(False))


def _mix_out_kernel(lam_ref, h_ref, fox_ref, oc_ref, os_ref, ow_ref, dd_ref, g_ref, sw_ref, wf_ref, wn_ref, wd_ref,
                    o_ref, *, diff_scale):
    lam = lam_ref[0]
    gates = g_ref[...]
    tm = gates.shape[0]
    lane = lax.broadcasted_iota(jnp.int32, (tm, LANES), 1)
    slot = lambda ref, n: ref[:, n * LANES:(n + 1) * LANES]
    nsa = []
    for h in range(NSA_HEADS):
        c0 = FOX_HEADS + 3 * h
        nsa.append(gates[:, c0:c0 + 1] * slot(oc_ref, h) + gates[:, c0 + 1:c0 + 2] * slot(os_ref, h)
                   + gates[:, c0 + 2:c0 + 3] * slot(ow_ref, h))
    diff = []
    for h in range(DIFF_HEADS):
        a = jnp.where(lane < DIFF_V, slot(dd_ref, 2 * h) - lam * slot(dd_ref, 2 * h + 1), 0.0)
        ss = jnp.sum(a * a, axis=-1, keepdims=True)
        diff.append(a * lax.rsqrt(ss * (1.0 / DIFF_V) + EPS) * sw_ref[...] * diff_scale)
    acc = h_ref[...]
    acc = acc + jnp.dot(fox_ref[...].astype(BF16), wf_ref[...], preferred_element_type=F32)
    acc = acc + jnp.dot(jnp.concatenate(nsa, axis=1).astype(BF16), wn_ref[...], preferred_element_type=F32)
    acc = acc + jnp.dot(jnp.concatenate(diff, axis=1).astype(BF16), wd_ref[...], preferred_element_type=F32)
    o_ref[...] = acc


def _slot_rows(w, heads):
    d = w.shape[1]
    w = w.reshape(heads, -1, d)
    return jnp.concatenate([w, jnp.zeros((heads, LANES - w.shape[1], d), w.dtype)], axis=1).reshape(heads * LANES, d)


def _mix_out(lam, h, fox, oc, os_, ow, dd, gates, subln_w, w_out, diff_scale, tm=512):
    s, d = h.shape
    nf, nn = FOX_HEADS * FOX_HD, NSA_HEADS * NSA_HD
    wf = _slot_rows(w_out[:nf], FOX_HEADS).astype(BF16)
    wn = _slot_rows(w_out[nf:nf + nn], NSA_HEADS).astype(BF16)
    wd = _slot_rows(w_out[nf + nn:], DIFF_HEADS).astype(BF16)
    sw = jnp.concatenate([subln_w.reshape(1, -1), jnp.zeros((1, LANES - DIFF_V), F32)], axis=1)
    row = lambda w: pl.BlockSpec((tm, w), lambda i: (i, 0))
    return pl.pallas_call(
        functools.partial(_mix_out_kernel, diff_scale=diff_scale),
        grid=(s // tm,),
        in_specs=[pl.BlockSpec(memory_space=pltpu.SMEM), row(d), row(fox.shape[1]), row(oc.shape[1]),
                  row(os_.shape[1]), row(ow.shape[1]), row(dd.shape[1]), row(LANES), _full((1, LANES)),
                  _full(wf.shape), _full(wn.shape), _full(wd.shape)],
        out_specs=row(d),
        out_shape=jax.ShapeDtypeStruct((s, d), F32),
        compiler_params=_cparams(("parallel",)),
        name="mix_out",
    )(lam, h, fox, oc, os_, ow, dd, gates, sw, wf, wn, wd)


def _peer_query_kernel(h_ref, nw_ref, wq_ref, keys_ref, xnT_o, scT_o):
    xn = _rms(h_ref[...], nw_ref[...])
    xnT_o[...] = xn.T.astype(BF16)
    q = jnp.dot(xn.astype(BF16), wq_ref[...], preferred_element_type=F32).astype(BF16)
    half = PEER_DQ // 2
    for b in range(2 * PEER_HEADS):
        sc = jnp.dot(q[:, b * half:(b + 1) * half], keys_ref[b], preferred_element_type=F32)
        scT_o[b] = sc.T


def _peer_query(h, nw, wq_bf16, keysT_bf16, tm=512):
    s, d = h.shape
    n = wq_bf16.shape[1]
    nb = 2 * PEER_HEADS
    return pl.pallas_call(
        _peer_query_kernel,
        grid=(s // tm,),
        in_specs=[pl.BlockSpec((tm, d), lambda i: (i, 0)), _full((1, d)), _full((d, n)), _full(keysT_bf16.shape)],
        out_specs=[pl.BlockSpec((d, tm), lambda i: (0, i)), pl.BlockSpec((nb, PEER_NKEYS, tm), lambda i: (0, 0, i))],
        out_shape=[jax.ShapeDtypeStruct((d, s), BF16), jax.ShapeDtypeStruct((nb, PEER_NKEYS, s), F32)],
        compiler_params=_cparams(("parallel",)),
        name="peer_query",
    )(h, nw.reshape(1, d), wq_bf16, keysT_bf16)


def _top16_rows(s, exact_ties):
    n = s.shape[0]
    row = lax.broadcasted_iota(jnp.int32, s.shape, 0)
    rank = jnp.full(s.shape, float(PEER_TOPK), dtype=F32)
    work = s
    vals = []
    for r in range(PEER_TOPK):
        mx = jnp.max(work, axis=0, keepdims=True)
        if exact_ties:
            pick = row == jnp.min(jnp.where(work == mx, row, n), axis=0, keepdims=True)
        else:
            pick = work == mx
        rank = jnp.where(pick, float(r), rank)
        work = jnp.where(pick, NEG_INIT, work)
        vals.append(mx)
    return rank, vals


def _peer_route(s1, s2, exact_ties):
    rank1, v1 = _top16_rows(s1, exact_ties)
    rank2, v2 = _top16_rows(s2, exact_ties)
    k = PEER_TOPK
    tm = s1.shape[1]
    r16 = lax.broadcasted_iota(jnp.int32, (k, tm), 0)
    v1m = jnp.zeros((k, tm), F32)
    v2m = jnp.zeros((k, tm), F32)
    for r in range(k):
        v1m = jnp.where(r16 == r, v1[r], v1m)
        v2m = jnp.where(r16 == r, v2[r], v2m)
    ea = jnp.exp(v1m - v1[0])
    eb = jnp.exp(v2m - v2[0])
    cands, gates = [v1m[0:1] + v2m], [ea[0:1] * eb]
    for r in range(1, 8):
        cands.append(v1m[r:r + 1] + v2m[0:8])
        gates.append(ea[r:r + 1] * eb[0:8])
    cands.append(v1m[8:16] + v2m[0:1])
    gates.append(ea[8:16] * eb[0:1])
    cand = jnp.concatenate(cands, axis=0)
    gate = jnp.concatenate(gates, axis=0)
    ncand = cand.shape[0]
    crow = lax.broadcasted_iota(jnp.int32, cand.shape, 0)
    for r in range(2, 8):
        start = 16 + 8 * (r - 1)
        cand = jnp.where((crow >= start + k // (r + 1)) & (crow < start + 8), NEG_INIT, cand)
    work = cand
    picked = jnp.zeros(cand.shape, dtype=jnp.bool_)
    for _ in range(k):
        mx = jnp.max(work, axis=0, keepdims=True)
        if exact_ties:
            pick = crow == jnp.min(jnp.where(work == mx, crow, ncand), axis=0, keepdims=True)
        else:
            pick = work == mx
        picked = picked | pick
        work = jnp.where(pick, NEG_INIT, work)
    pf = picked.astype(F32)
    z = jnp.sum(pf * gate, axis=0, keepdims=True)
    cnt = [jnp.sum(pf[0:16, :], axis=0, keepdims=True)]
    cnt += [jnp.sum(pf[8 + 8 * r:16 + 8 * r, :], axis=0, keepdims=True) for r in range(1, 8)]
    cnt += [pf[64 + r:65 + r, :] for r in range(8, 16)]
    cmap = jnp.zeros(s1.shape, dtype=F32)
    for r in range(k):
        cmap = jnp.where(rank1 == float(r), cnt[r], cmap)
    in1 = rank1 < float(k)
    in2 = rank2 < float(k)
    a = jnp.where(in1, jnp.exp(s1 - v1[0]), 0.0) / (2.0 * z)
    b = jnp.where(in2, jnp.exp(s2 - v2[0]), 0.0)
    excess = (jnp.abs(jnp.sum(in1.astype(F32), axis=0, keepdims=True) - k)
              + jnp.abs(jnp.sum(in2.astype(F32), axis=0, keepdims=True) - k)
              + jnp.abs(jnp.sum(pf, axis=0, keepdims=True) - k))
    return a, cmap, b, rank2, excess


def _peer_topk_kernel(sc_ref, a_o, c_o, b_o, r_o):
    def run(h, exact_ties):
        a, c, b, rank2, excess = _peer_route(sc_ref[2 * h], sc_ref[2 * h + 1], exact_ties)
        a_o[h] = a
        c_o[h] = c
        b_o[h] = b.astype(BF16)
        r_o[h] = rank2.astype(BF16)
        return excess

    for h in range(a_o.shape[0]):
        excess = run(h, False)

        @pl.when(jnp.max(excess) > 0.0)
        def _(h=h):
            run(h, True)


def _peer_topk(scT, tm=256, heads_per_step=2):
    n2, nk, s = scT.shape
    heads = n2 // 2
    hps = heads_per_step
    ospec = pl.BlockSpec((hps, nk, tm), lambda h, t: (h, 0, t))
    return pl.pallas_call(
        _peer_topk_kernel,
        grid=(heads // hps, s // tm),
        in_specs=[pl.BlockSpec((2 * hps, nk, tm), lambda h, t: (h, 0, t))],
        out_specs=[ospec] * 4,
        out_shape=[jax.ShapeDtypeStruct((heads, nk, s), dt) for dt in (F32, F32, BF16, BF16)],
        compiler_params=_cparams(("parallel", "parallel")),
        name="peer_topk",
    )(scT)


def _gelu_exact_x2(x):
    return x * (1.0 + lax.erf(x * (2.0 ** -0.5)))


def _peer_main_kernel(xT_ref, u_ref, v_ref, a_ref, c_ref, b_ref, r_ref, o_ref, w_sc, *, ec, sub):
    ci = pl.program_id(1)

    @pl.when(ci == 0)
    def _():
        o_ref[...] = jnp.zeros_like(o_ref)

    nk = PEER_NKEYS
    tm = xT_ref.shape[1]

    def row(ref, h, i1, lanes):
        x16 = jnp.broadcast_to(ref[h, pl.ds(i1, 1), :][:, lanes], (16, GATE_LANES)).astype(BF16)
        return jnp.concatenate([x16] * (nk // 16), axis=0)

    acc = None
    for sc in range(ec // sub):
        hid = jnp.dot(u_ref[sc * sub:(sc + 1) * sub, :], xT_ref[...], preferred_element_type=F32)
        act = _gelu_exact_x2(hid).astype(BF16)
        for ii in range(sub // nk):
            slab = sc * (sub // nk) + ii
            i1 = ci * (ec // nk) + slab
            for l0 in range(0, tm, GATE_LANES):
                lanes = slice(l0, l0 + GATE_LANES)
                w = None
                for h in range(PEER_HEADS):
                    keep = r_ref[h, :, lanes] < row(c_ref, h, i1, lanes)
                    term = jnp.where(keep, b_ref[h, :, lanes], 0.0) * row(a_ref, h, i1, lanes)
                    w = term if w is None else w + term
                w_sc[slab * nk:(slab + 1) * nk, lanes] = w * act[ii * nk:(ii + 1) * nk, lanes]
        part = jnp.dot(w_sc[sc * sub:(sc + 1) * sub, :].T, v_ref[sc * sub:(sc + 1) * sub, :],
                       preferred_element_type=F32)
        acc = part if acc is None else acc + part
    o_ref[...] += acc


def _peer_main(xnT, u_bf16, v_bf16, a, c, b, r, tm=GATE_LANES, ec=2048, sub=1024):
    d, s = xnT.shape
    e = u_bf16.shape[0]
    heads, nk, _ = a.shape
    rt = pl.BlockSpec((heads, nk, tm), lambda t, ci: (0, 0, t))
    wt = pl.BlockSpec((ec, d), lambda t, ci: (ci, 0))
    return pl.pallas_call(
        functools.partial(_peer_main_kernel, ec=ec, sub=sub),
        grid=(s // tm, e // ec),
        in_specs=[pl.BlockSpec((d, tm), lambda t, ci: (0, t)), wt, wt, rt, rt, rt, rt],
        out_specs=pl.BlockSpec((tm, d), lambda t, ci: (t, 0)),
        out_shape=jax.ShapeDtypeStruct((s, d), F32),
        scratch_shapes=[pltpu.VMEM((ec, tm), BF16)],
        compiler_params=_cparams(("parallel", "arbitrary")),
        name="peer_main",
    )(xnT, u_bf16, v_bf16, a, c, b, r)


def _ple_kernel(h_ref, peer_ref, p_ref, nw_ref, wg_ref, wp_ref, o_ref):
    h2 = h_ref[...] + peer_ref[...]
    xn = _rms(h2, nw_ref[...]).astype(BF16)
    gate = 1.0 / (1.0 + jnp.exp(-jnp.dot(xn, wg_ref[...], preferred_element_type=F32)))
    emb = jnp.dot(p_ref[...].astype(BF16), wp_ref[...], preferred_element_type=F32)
    o_ref[...] = h2 + gate * emb


def _ple(h, peer, p, nw, wg_bf16, wp_bf16, tm=512):
    s, d = h.shape
    pd = p.shape[1]
    row = lambda w: pl.BlockSpec((tm, w), lambda i: (i, 0))
    return pl.pallas_call(
        _ple_kernel,
        grid=(s // tm,),
        in_specs=[row(d), row(d), row(pd), _full((1, d)), _full((d, d)), _full((pd, d))],
        out_specs=row(d),
        out_shape=jax.ShapeDtypeStruct((s, d), F32),
        compiler_params=_cparams(("parallel",)),
        name="ple",
    )(h, peer, p, nw.reshape(1, d), wg_bf16, wp_bf16)


def _split_w_in(w):
    f0 = 3 * FOX_HEADS * FOX_HD
    n0 = f0 + FOX_HEADS
    g0 = n0 + _SEG["dq"][0] - _SEG["nq"][0]
    d0 = g0 + 3 * NSA_HEADS
    misc = jnp.concatenate([w[:, f0:n0], w[:, g0:d0], jnp.zeros((w.shape[0], LANES - (n0 - f0) - (d0 - g0)), w.dtype)],
                           axis=1)
    return [t.astype(BF16) for t in (w[:, :f0], w[:, n0:g0], w[:, d0:], misc)]


def _overlap_T(s):
    n = np.arange(s // CMP_STRIDE)[None, :] * CMP_STRIDE
    m = np.arange(s // SLC_LEN)[:, None] * SLC_LEN
    return jnp.asarray(((n < m + SLC_LEN) & (n + CMP_LEN > m)).astype(np.float32), dtype=BF16)


def _layer(h, p_i, tabs, layer, attn_norm_w, w_in, fox_f_bias, fox_q_norm_w, fox_k_norm_w, nsa_q_norm_w,
           nsa_k_norm_w, nsa_cmp_pos, nsa_cmp_w, diff_q_norm_w, diff_k_norm_w, diff_lambda, diff_subln_w,
           w_out, ffn_norm_w, peer_w_q, peer_sub_keys, peer_u, peer_v, ple_norm_w, ple_w_gate, ple_w_proj):
    s = h.shape[0]
    fb =jnp.zeros((1, LANES), F32).at[0, :FOX_HEADS].set(fox_f_bias)
    u_f = _score_bound(fox_q_norm_w, fox_k_norm_w, FOX_HD, FOX_HD ** -0.5)
    u_n = _score_bound(nsa_q_norm_w, nsa_k_norm_w, NSA_HD, NSA_HD ** -0.5)
    u_d = _score_bound(diff_q_norm_w, diff_k_norm_w, DIFF_QK, DIFF_QK ** -0.5)
    shifts = SHIFT_HEADROOM - jnp.stack([u_f, u_n, u_d]).astype(F32)
    (fq, fk, fv, nqn, nqr, kvc, ks, vs, kw, vw, gates, dq, dk, dv, csum) = _prep(
        shifts, h, attn_norm_w, _split_w_in(w_in), tabs, fb, fox_q_norm_w, fox_k_norm_w, nsa_q_norm_w, nsa_k_norm_w, diff_q_norm_w, diff_k_norm_w)

    c_first_q = csum[0::Q_LANES, :FOX_HEADS]
    c_last_k = csum[TK_CAUSAL - 1::TK_CAUSAL, :FOX_HEADS]
    decay = (c_first_q.T[:, :, None] - c_last_k.T[:, None, :]).reshape(-1)
    o_fox = _flash(fq, fk, fv, None, decay, bound_ok=u_f <= SCORE_BOUND, groups=1, tq=Q_LANES, tk=TK_CAUSAL)

    tq_n = Q_LANES // NSA_G
    nsa_ok = u_n <= SCORE_BOUND
    kcx, vcx = _compress(kvc, nsa_cmp_w, nsa_cmp_pos, nsa_k_norm_w)
    o_c, sel = _nsa_cmp(nqn, kcx, vcx, _overlap_T(s), tq_n, nsa_ok)
    o_s = _flash(nqr, ks, vs, sel, bound_ok=nsa_ok, groups=NSA_G, tq=tq_n, tk=TK_CAUSAL)
    o_w = _flash(nqr, kw, vw, bound_ok=nsa_ok, groups=NSA_G, tq=tq_n, tk=512, window=WIN)

    o_d = _flash(dq, dk, dv, bound_ok=u_d <= SCORE_BOUND, groups=2, tq=Q_LANES // 2, tk=TK_CAUSAL)
    lv = diff_lambda.astype(F32)
    lam_init = 0.8 - 0.6 * math.exp(-0.3 * layer)
    lam = (jnp.exp(jnp.sum(lv[0] * lv[1])) - jnp.exp(jnp.sum(lv[2] * lv[3])) + lam_init).reshape(1)
    h1 = _mix_out(lam, h, o_fox, o_c, o_s, o_w, o_d, gates, diff_subln_w, w_out, 1.0 - lam_init)

    keysT = peer_sub_keys.reshape(2 * PEER_HEADS, PEER_NKEYS, PEER_DQ // 2).transpose(0, 2, 1).astype(BF16)
    xnT, scT = _peer_query(h1, ffn_norm_w, peer_w_q.astype(BF16), keysT)
    a, c, b, r = _peer_topk(scT)
    peer = _peer_main(xnT, peer_u.astype(BF16), peer_v.astype(BF16), a, c, b, r)

    return _ple(h1, peer, p_i, ple_norm_w, ple_w_gate.astype(BF16), ple_w_proj.astype(BF16))


def kernel(x, p, positions, attn_norm_w, w_in, fox_f_bias, fox_q_norm_w, fox_k_norm_w, nsa_q_norm_w, nsa_k_norm_w,
           nsa_cmp_pos, nsa_cmp_w, diff_q_norm_w, diff_k_norm_w, diff_lambda, diff_subln_w, w_out, ffn_norm_w,
           peer_w_q, peer_sub_keys, peer_u, peer_v, ple_norm_w, ple_w_gate, ple_w_proj):
    b, s, d = x.shape
    assert b == 1 and d == D_MODEL and s % Q_LANES == 0 and s % TK_CAUSAL == 0
    tabs = _rope_tables(positions)
    h = x.reshape(s, d)
    per_layer = (attn_norm_w, w_in, fox_f_bias, fox_q_norm_w, fox_k_norm_w, nsa_q_norm_w, nsa_k_norm_w, nsa_cmp_pos,
                 nsa_cmp_w, diff_q_norm_w, diff_k_norm_w, diff_lambda, diff_subln_w, w_out, ffn_norm_w, peer_w_q,
                 peer_sub_keys, peer_u, peer_v, ple_norm_w, ple_w_gate, ple_w_proj)
    for layer in range(attn_norm_w.shape[0]):
        h = _layer(h, p[layer, 0], tabs, layer, *(w[layer] for w in per_layer))
    return h.reshape(b, s, d)
```

```python
import functools
import math

import numpy as np
import jax
import jax.numpy as jnp
from jax import lax
from jax.experimental import pallas as pl
from jax.experimental.pallas import tpu as pltpu

F32 = jnp.float32
BF16 = jnp.bfloat16
HIGHEST = lax.Precision.HIGHEST

D_MODEL = 1024
PLE_DIM = 256
ROPE_THETA = 10000.0
EPS = 1e-6
FOX_HEADS, FOX_HD = 4, 64
NSA_HEADS, NSA_KV, NSA_HD = 8, 2, 64
NSA_G = NSA_HEADS // NSA_KV
CMP_LEN, CMP_STRIDE, SLC_LEN, SLC_TOPK, WIN = 32, 16, 64, 16, 512
FORCE_BONUS = 1.0e4
assert FORCE_BONUS > NSA_G
DIFF_HEADS, DIFF_QK, DIFF_V = 4, 32, 64
PEER_HEADS, PEER_NKEYS, PEER_DQ, PEER_TOPK = 8, 128, 256, 16
PEER_EXPERTS = PEER_NKEYS * PEER_NKEYS

LANES = 128
LOG2E = math.log2(math.e)
NEG_INIT = -1.0e30
MASKVAL = -2.0e30
VMEM_LIMIT = 56 * 1024 * 1024
TK_CAUSAL = 2048
Q_LANES = 2048
GATE_LANES = 512

_SEG = dict(fq=(0, 256), fk=(256, 256), fv=(512, 256), nq=(768, 512), nkc=(1280, 128), nvc=(1408, 128),
            nks=(1536, 128), nvs=(1664, 128), nkw=(1792, 128), nvw=(1920, 128),
            dq=(2048, 256), dk=(2304, 256), dv=(2560, 256), misc=(2816, 128))
PROJ_W = 2944


def _cparams(sem):
    return pltpu.CompilerParams(dimension_semantics=sem, vmem_limit_bytes=VMEM_LIMIT)


def _full(shape):
    n = len(shape)
    return pl.BlockSpec(shape, lambda *_: (0,) * n)


def _rope_tab_kernel(pos_ref, f64_ref, g64_ref, f32_ref, g32_ref, c64_o, s64_o, c32_o, s32_o):
    pos = pos_ref[...].astype(F32)
    a64 = pos * f64_ref[...]
    c64_o[...] = jnp.cos(a64)
    s64_o[...] = jnp.sin(a64) * g64_ref[...]
    a32 = pos * f32_ref[...]
    c32_o[...] = jnp.cos(a32)
    s32_o[...] = jnp.sin(a32) * g32_ref[...]


def _rope_tables(positions):
    s = positions.shape[-1]
    pos = positions.reshape(s, 1)
    lane = np.arange(LANES)

    def lanes(half):
        inv = ROPE_THETA ** (-jnp.arange(half, dtype=F32) / half)
        freq = inv[(lane % (2 * half)) % half].reshape(1, LANES)
        sign = np.where((lane % (2 * half)) < half, -1.0, 1.0).astype(np.float32).reshape(1, LANES)
        return freq, jnp.asarray(sign)

    f64, g64 = lanes(NSA_HD // 2)
    f32_, g32 = lanes(DIFF_QK // 2)
    tm = 512
    out = jax.ShapeDtypeStruct((s, LANES), F32)
    row = pl.BlockSpec((tm, LANES), lambda i: (i, 0))
    return pl.pallas_call(
        _rope_tab_kernel,
        grid=(s // tm,),
        in_specs=[pl.BlockSpec((tm, 1), lambda i: (i, 0))] + [_full((1, LANES))] * 4,
        out_specs=[row] * 4,
        out_shape=[out] * 4,
        compiler_params=_cparams(("parallel",)),
        name="rope_tables",
    )(pos, f64, g64, f32_, g32)


def _rms(x, w):
    return x * lax.rsqrt(jnp.mean(x * x, axis=-1, keepdims=True) + EPS) * w


def _seg_rms(x, bmat, seg):
    outs = []
    for c0 in range(0, x.shape[1], 256):
        w = min(256, x.shape[1] - c0)
        xc = x[:, c0:c0 + w]
        sq = xc * xc
        hi = sq.astype(BF16)
        lo = (sq - hi.astype(F32)).astype(BF16)
        b = bmat[:w, :w].astype(BF16)
        ss = jnp.dot(hi, b, preferred_element_type=F32) + jnp.dot(lo, b, preferred_element_type=F32)
        outs.append(xc * lax.rsqrt(ss * (1.0 / seg) + EPS))
    return outs[0] if len(outs) == 1 else jnp.concatenate(outs, axis=1)


def _tile_lanes(t, width):
    reps = width // LANES
    return t if reps == 1 else jnp.concatenate([t] * reps, axis=1)


def _rope(x, cos, sin, half):
    width = x.shape[1]
    left = pltpu.roll(x, width - half, 1)
    right = pltpu.roll(x, half, 1)
    lane = lax.broadcasted_iota(jnp.int32, x.shape, 1)
    swapped = jnp.where((lane & (2 * half - 1)) < half, left, right)
    return x * _tile_lanes(cos, width) + swapped * _tile_lanes(sin, width)


def _lane_fill(tm, cols):
    lane = lax.broadcasted_iota(jnp.int32, (tm, LANES), 1)
    out = jnp.zeros((tm, LANES), F32)
    for l, v in cols.items():
        out = jnp.where(lane == l, v, out)
    return out


def _slots(x, extras):
    lane = lax.broadcasted_iota(jnp.int32, (x.shape[0], LANES), 1)
    outs = []
    for h, ex in enumerate(extras):
        col = x[:, (h // 2) * LANES:(h // 2 + 1) * LANES]
        if h % 2:
            col = pltpu.roll(col, LANES // 2, 1)
        outs.append(jnp.where(lane < LANES // 2, col, ex))
    return jnp.concatenate(outs, axis=1)


def _prep_kernel(shift_ref, proj_ref, c64_ref, s64_ref, c32_ref, s32_ref, fb_ref, wfq_ref, wfk_ref, wnq_ref,
                 wnk_ref, wdq_ref, wdk_ref, b64_ref, b32_ref,
                 fq_o, fk_o, fv_o, nqn_o, nqr_o, kvc_o, ks_o, vs_o, kw_o, vw_o, gate_o, dq_o, dk_o, dv_o, cs_o,
                 carry_sc, *, tm):
    def seg(name):
        c0, w = _SEG[name]
        return proj_ref[:, c0:c0 + w]

    b64 = b64_ref[...]
    b32 = b32_ref[...]
    c64, s64, c32, s32 = c64_ref[...], s64_ref[...], c32_ref[...], s32_ref[...]
    one_ex = _lane_fill(tm, {COL_SHIFT: 1.0})

    misc = seg("misc")
    gate_o[...] = 1.0 / (1.0 + jnp.exp(-misc))
    t = misc + fb_ref[...]
    logf = jnp.minimum(t, 0.0) - jnp.log1p(jnp.exp(-jnp.abs(t)))

    @pl.when(pl.program_id(0) == 0)
    def _():
        carry_sc[...] = jnp.zeros_like(carry_sc)

    r = lax.broadcasted_iota(jnp.int32, (tm, tm), 0)
    c = lax.broadcasted_iota(jnp.int32, (tm, tm), 1)
    tri = (c <= r).astype(F32)
    csum = jnp.dot(tri, logf, precision=HIGHEST, preferred_element_type=F32) + carry_sc[0:1, :]
    carry_sc[...] = jnp.broadcast_to(csum[tm - 1:tm, :], carry_sc.shape)
    csum = csum * LOG2E
    hi = csum.astype(BF16).astype(F32)
    r1 = csum - hi
    mid = r1.astype(BF16).astype(F32)
    lo = (r1 - mid).astype(BF16).astype(F32)
    cs_o[...] = csum

    fq = _seg_rms(seg("fq"), b64, FOX_HD) * wfq_ref[...] * (FOX_HD ** -0.5 * LOG2E)
    fk = _seg_rms(seg("fk"), b64, FOX_HD) * wfk_ref[...]
    q_ex, k_ex = [], []
    for h in range(FOX_HEADS):
        ch, cm, cl = hi[:, h:h + 1], mid[:, h:h + 1], lo[:, h:h + 1]
        q_ex.append(_lane_fill(tm, {COL_SHIFT: shift_ref[0], 65: 1.0, 66: 1.0, 67: 1.0, 68: ch, 69: cm, 70: cl}))
        k_ex.append(_lane_fill(tm, {COL_SHIFT: 1.0, 65: -ch, 66: -cm, 67: -cl, 68: 1.0, 69: 1.0, 70: 1.0}))
    fq_o[...] = _slots(fq, q_ex)
    fk_o[...] = _slots(fk, k_ex).astype(BF16)
    fv_o[...] = _slots(seg("fv"), [one_ex] * FOX_HEADS).astype(BF16)

    n_ex = [_lane_fill(tm, {COL_SHIFT: shift_ref[1]})] * NSA_HEADS
    nqn = _seg_rms(seg("nq"), b64, NSA_HD) * wnq_ref[...]
    nqn_o[...] = _slots(nqn * (NSA_HD ** -0.5 * LOG2E), n_ex)
    nqr_o[...] = _slots(_rope(nqn, c64, s64, NSA_HD // 2) * (NSA_HD ** -0.5 * LOG2E), n_ex)
    kvc_o[:, 0:128] = seg("nkc").astype(BF16)
    kvc_o[:, 128:256] = seg("nvc").astype(BF16)
    wnk = wnk_ref[...]
    lane = lax.broadcasted_iota(jnp.int32, (tm, LANES), 1)
    row = pl.program_id(0) * tm + lax.broadcasted_iota(jnp.int32, (tm, LANES), 0)
    blk_in_tile = (row & (TK_CAUSAL - 1)) // SLC_LEN
    sel_ex = jnp.where(lane == COL_SEL + blk_in_tile, 1.0, one_ex)
    ks = _rope(_seg_rms(seg("nks"), b64, NSA_HD) * wnk, c64, s64, NSA_HD // 2)
    ks_o[...] = _slots(ks, [sel_ex] * NSA_KV).astype(BF16)
    vs_o[...] = _slots(seg("nvs"), [one_ex] * NSA_KV).astype(BF16)
    kw = _rope(_seg_rms(seg("nkw"), b64, NSA_HD) * wnk, c64, s64, NSA_HD // 2)
    kw_o[...] = _slots(kw, [one_ex] * NSA_KV).astype(BF16)
    vw_o[...] = _slots(seg("nvw"), [one_ex] * NSA_KV).astype(BF16)

    dq = _rope(_seg_rms(seg("dq"), b32, DIFF_QK) * wdq_ref[...], c32, s32, DIFF_QK // 2) * (DIFF_QK ** -0.5 * LOG2E)
    d_ex = _lane_fill(tm, {COL_SHIFT: shift_ref[2]})
    d_slots = []
    for h in range(DIFF_HEADS):
        col = dq[:, (h // 2) * LANES:(h // 2 + 1) * LANES]
        if h % 2:
            col = pltpu.roll(col, LANES // 2, 1)
        d_slots.append(jnp.where(lane < DIFF_QK, col, d_ex))
        d_slots.append(jnp.where((lane >= DIFF_QK) & (lane < 2 * DIFF_QK), col, d_ex))
    dq_o[...] = jnp.concatenate(d_slots, axis=1)
    dk = _rope(_seg_rms(seg("dk"), b32, DIFF_QK) * wdk_ref[...], c32, s32, DIFF_QK // 2)
    dk_o[...] = _slots(dk, [one_ex] * DIFF_HEADS).astype(BF16)
    dv_o[...] = _slots(seg("dv"), [one_ex] * DIFF_HEADS).astype(BF16)


def _block_diag_ones(n, seg):
    i = np.arange(n)
    return jnp.asarray((i[:, None] // seg == i[None, :] // seg).astype(np.float32))


def _proj_prep_kernel(shift_ref, x_ref, nw_ref, wa_ref, wb_ref, wc_ref, wm_ref, *rest, tm):
    proj_sc = rest[-1]
    xn = _rms(x_ref[...], nw_ref[...]).astype(BF16)
    c0 = 0
    for w_ref in (wa_ref, wb_ref, wc_ref, wm_ref):
        n = w_ref.shape[1]
        proj_sc[:, c0:c0 + n] = jnp.dot(xn, w_ref[...], preferred_element_type=F32)
        c0 += n
    _prep_kernel(shift_ref, proj_sc, *rest[:-1], tm=tm)


def _prep(shifts, x, nw, ws_bf16, tabs, fb, wfq, wfk, wnq, wnk, wdq, wdk, tm=512):
    s, d = x.shape
    c64, s64, c32, s32 = tabs
    b64 = _block_diag_ones(256, 64)
    b32 = _block_diag_ones(256, 32)

    def tiled(w, width):
        return jnp.tile(w.reshape(1, -1), (1, width // w.shape[-1]))

    consts = [fb, tiled(wfq, 256), tiled(wfk, 256), tiled(wnq, 512), tiled(wnk, 128), tiled(wdq, 256),
              tiled(wdk, 256), b64, b32]
    outs = [(FOX_HEADS, F32), (FOX_HEADS, BF16), (FOX_HEADS, BF16),
            (NSA_HEADS, F32), (NSA_HEADS, F32), (2, BF16),
            (NSA_KV, BF16), (NSA_KV, BF16), (NSA_KV, BF16), (NSA_KV, BF16), (1, F32),
            (2 * DIFF_HEADS, F32), (DIFF_HEADS, BF16), (DIFF_HEADS, BF16),
            (1, F32)]
    row = lambda w: pl.BlockSpec((tm, w), lambda i: (i, 0))
    assert sum(w.shape[1] for w in ws_bf16) == PROJ_W
    return pl.pallas_call(
        functools.partial(_proj_prep_kernel, tm=tm),
        grid=(s // tm,),
        in_specs=[pl.BlockSpec(memory_space=pltpu.SMEM), row(d), _full((1, d))] + [_full(w.shape) for w in ws_bf16]
                 + [row(LANES)] * 4 + [_full(c.shape) for c in consts],
        out_specs=[row(n * LANES) for n, _ in outs],
        out_shape=[jax.ShapeDtypeStruct((s, n * LANES), dt) for n, dt in outs],
        scratch_shapes=[pltpu.VMEM((8, LANES), F32), pltpu.VMEM((tm, PROJ_W), F32)],
        compiler_params=_cparams(("arbitrary",)),
        name="proj_head_prep",
    )(shifts, x, nw.reshape(1, d), *ws_bf16, c64, s64, c32, s32, *consts)


QK_DIM = 128
COL_SHIFT = 64
COL_SEL = 80
SCORE_BOUND = 60.0
SHIFT_HEADROOM = 60.0
EXP2_FLUSH = 200.0


def _flash_kernel(code_ref, q_ref, k_ref, v_ref, *rest, groups, tq, tk, window, has_sel, bounded, decay_tiles):
    rest = list(rest)
    sel_ref = rest.pop(0) if has_sel else None
    decay_ref = rest.pop(0) if decay_tiles else None
    o_ref = rest.pop(0)
    qT_sc = rest.pop(0)
    m_sc = None if bounded else rest.pop(0)
    acc_sc = rest.pop(0)
    code = code_ref[pl.program_id(1)]
    i = code & 0xFFF
    j = (code >> 12) & 0xFFF
    first = (code >> 24) & 1
    last = (code >> 25) & 1
    rows = groups * tq

    @pl.when(first == 1)
    def _():
        for g in range(groups):
            qT_sc[:, g * tq:(g + 1) * tq] = q_ref[:, g * QK_DIM:(g + 1) * QK_DIM].T.astype(BF16)
        if not bounded:
            m_sc[...] = jnp.full_like(m_sc, NEG_INIT)
        acc_sc[...] = jnp.zeros_like(acc_sc)

    def step(masked):
        if has_sel:
            qT_sc[COL_SEL:COL_SEL + tk // SLC_LEN, :] = _tile_lanes_any(sel_ref[0], groups)
        s = jnp.dot(k_ref[...], qT_sc[...], preferred_element_type=F32)
        if masked:
            kpos = j * tk + lax.broadcasted_iota(jnp.int32, (tk, rows), 0)
            lane = lax.broadcasted_iota(jnp.int32, (tk, rows), 1)
            qpos = i * tq + (lane & (tq - 1))
            keep = kpos <= qpos
            if window is not None:
                keep = keep & (kpos > qpos - window)
            s = jnp.where(keep, s, MASKVAL)
        v_t = (((0,), (0,)), ((), ()))
        v = v_ref[...]
        if bounded:
            acc_sc[...] += lax.dot_general(v, jnp.exp2(s).astype(BF16), v_t, preferred_element_type=F32)
        else:
            m_prev = m_sc[...]
            m_new = jnp.maximum(m_prev, jnp.max(s, axis=0, keepdims=True))
            p = jnp.exp2(s - m_new).astype(BF16)
            acc_sc[...] = (acc_sc[...] * jnp.exp2(m_prev - m_new)
                           + lax.dot_general(v, p, v_t, preferred_element_type=F32))
            m_sc[...] = m_new

    if window is not None:
        step(True)
    else:
        needs_mask = (j + 1) * tk - 1 > i * tq

        @pl.when(needs_mask)
        def _():
            step(True)

        @pl.when(jnp.logical_not(needs_mask))
        def _():
            if decay_tiles:
                nq, nk = decay_tiles
                bias_max = decay_ref[(pl.program_id(0) * nq + i) * nk + j]

                @pl.when(SHIFT_HEADROOM + bias_max > -EXP2_FLUSH)
                def _():
                    step(False)
            else:
                step(False)

    @pl.when(last == 1)
    def _():
        l = acc_sc[COL_SHIFT:COL_SHIFT + 1, :]
        o = acc_sc[...] / jnp.where(l > 0.0, l, 1.0)
        for g in range(groups):
            o_ref[:, g * QK_DIM:(g + 1) * QK_DIM] = o[:, g * tq:(g + 1) * tq].T


def _tile_lanes_any(t, reps):
    return t if reps == 1 else jnp.concatenate([t] * reps, axis=1)


def _flash_call(q, k, v, sel, decay, *, groups, tq, tk, window, bounded):
    s_len = q.shape[0]
    hkv = k.shape[1] // QK_DIM
    nq = s_len // tq
    rows = groups * tq
    codes = []
    for i in range(nq):
        q_lo, q_hi = i * tq, i * tq + tq - 1
        j_hi = q_hi // tk
        j_lo = 0 if window is None else max(0, (q_lo - window + 1) // tk)
        for j in range(j_lo, j_hi + 1):
            codes.append(i | (j << 12) | (int(j == j_lo) << 24) | (int(j == j_hi) << 25))
    codes = jnp.asarray(np.asarray(codes, dtype=np.int32))
    nsteps = codes.shape[0]

    def ti(c, s):
        return c[s] & 0xFFF

    def tj(c, s):
        return (c[s] >> 12) & 0xFFF

    in_specs = [
        pl.BlockSpec((tq, groups * QK_DIM), lambda h, s, c: (ti(c, s), h)),
        pl.BlockSpec((tk, QK_DIM), lambda h, s, c: (tj(c, s), h)),
        pl.BlockSpec((tk, QK_DIM), lambda h, s, c: (tj(c, s), h)),
    ]
    args = [q, k, v]
    if sel is not None:
        in_specs.append(pl.BlockSpec((1, tk // SLC_LEN, tq), lambda h, s, c: (h, tj(c, s), ti(c, s))))
        args.append(sel)
    use_decay = decay is not None and bounded
    if use_decay:
        in_specs.append(pl.BlockSpec(memory_space=pltpu.SMEM))
        args.append(decay)
    scratch = [pltpu.VMEM((QK_DIM, rows), BF16)]
    if not bounded:
        scratch.append(pltpu.VMEM((1, rows), F32))
    scratch.append(pltpu.VMEM((QK_DIM, rows), F32))
    kern = functools.partial(_flash_kernel, groups=groups, tq=tq, tk=tk, window=window, has_sel=sel is not None,
                             bounded=bounded, decay_tiles=(nq, s_len // tk) if use_decay else None)
    return pl.pallas_call(
        kern,
        grid_spec=pltpu.PrefetchScalarGridSpec(
            num_scalar_prefetch=1,
            grid=(hkv, nsteps),
            in_specs=in_specs,
            out_specs=pl.BlockSpec((tq, groups * QK_DIM), lambda h, s, c: (ti(c, s), h)),
            scratch_shapes=scratch,
        ),
        out_shape=jax.ShapeDtypeStruct(q.shape, F32),
        compiler_params=_cparams(("parallel", "arbitrary")),
        name="flash_bounded" if bounded else "flash_online",
    )(codes, *args)


def _flash(q, k, v, sel=None, decay=None, *, bound_ok, groups, tq, tk, window=None):
    call = functools.partial(_flash_call, q, k, v, sel, decay, groups=groups, tq=tq, tk=tk, window=window)
    return lax.cond(bound_ok, lambda: call(bounded=True), lambda: call(bounded=False))


def _score_bound(wq, wk, seg, scale):
    return 1.02 * seg * jnp.max(jnp.abs(wq)) * jnp.max(jnp.abs(wk)) * scale * LOG2E


def _compress_kernel(r_ref, wa_ref, wb_ref, pa_ref, pb_ref, nw_ref, b64_ref, kx_o, vx_o):
    n = r_ref.shape[0]
    r = r_ref[...]
    wa, wb = wa_ref[...], wb_ref[...]
    first = jnp.dot(r, wa.astype(BF16), preferred_element_type=F32)
    second = jnp.dot(r, wb.astype(BF16), preferred_element_type=F32)
    nxt = pltpu.roll(second, n - 1, 0)
    rowi = lax.broadcasted_iota(jnp.int32, nxt.shape, 0)
    nxt = jnp.where(rowi == n - 1, 0.0, nxt)
    const = (jnp.dot(pa_ref[...], wa, precision=HIGHEST, preferred_element_type=F32)
             + jnp.dot(pb_ref[...], wb, precision=HIGHEST, preferred_element_type=F32))
    out = first + nxt + const
    k, v = out[:, 0:LANES], out[:, LANES:2 * LANES]
    ss = jnp.dot(k * k, b64_ref[...], precision=HIGHEST, preferred_element_type=F32)
    kn = k * lax.rsqrt(ss * (1.0 / NSA_HD) + EPS) * nw_ref[...]
    one_ex = _lane_fill(n, {COL_SHIFT: 1.0})
    kx_o[...] = _slots(kn, [one_ex] * NSA_KV).astype(BF16)
    vx_o[...] = _slots(v, [one_ex] * NSA_KV).astype(BF16)


def _compress(kvc, cmp_w, cmp_pos, nw):
    s = kvc.shape[0]
    n = s // CMP_STRIDE
    width = CMP_STRIDE * 2 * NSA_KV * NSA_HD
    w = cmp_w.reshape(2, 2, CMP_STRIDE, NSA_HD, NSA_HD)
    eye = jnp.eye(2, dtype=w.dtype)
    big = jnp.einsum("khldD,kK,gG->hlkgdKGD", w, eye, eye).reshape(2, width, 2 * NSA_KV * NSA_HD)
    pos = cmp_pos.reshape(2, 2, CMP_STRIDE, 1, NSA_HD)
    pos = jnp.broadcast_to(pos, (2, 2, CMP_STRIDE, NSA_KV, NSA_HD)).transpose(1, 2, 0, 3, 4).reshape(2, 1, width)
    b64 = _block_diag_ones(LANES, NSA_HD)
    nwt = jnp.tile(nw.reshape(1, -1), (1, LANES // NSA_HD))
    out = jax.ShapeDtypeStruct((n, NSA_KV * LANES), BF16)
    return pl.pallas_call(
        _compress_kernel,
        grid=(1,),
        in_specs=[_full((n, width)), _full(big[0].shape), _full(big[1].shape), _full((1, width)), _full((1, width)),
                  _full((1, LANES)), _full((LANES, LANES))],
        out_specs=[_full((n, NSA_KV * LANES))] * 2,
        out_shape=[out, out],
        compiler_params=_cparams(("arbitrary",)),
        name="nsa_compress",
    )(kvc.reshape(n, width), big[0], big[1], pos[0], pos[1], nwt, b64)


def _nsa_cmp_kernel(q_ref, kc_ref, vc_ref, ov_ref, o_ref, sel_ref, *, tq, groups, bounded):
    i = pl.program_id(1)
    ncmp = kc_ref.shape[0]
    rows = groups * tq
    qT = jnp.concatenate([q_ref[:, g * QK_DIM:(g + 1) * QK_DIM].T for g in range(groups)], axis=1).astype(BF16)
    s = jnp.dot(kc_ref[...], qT, preferred_element_type=F32)
    n_idx = lax.broadcasted_iota(jnp.int32, (ncmp, rows), 0)
    lane = lax.broadcasted_iota(jnp.int32, (ncmp, rows), 1)
    qpos = i * tq + (lane & (tq - 1))
    keep = n_idx * CMP_STRIDE + (CMP_LEN - 1) <= qpos
    if bounded:
        e = jnp.where(keep, jnp.exp2(s), 0.0)
    else:
        sm = jnp.where(keep, s, NEG_INIT)
        e = jnp.where(keep, jnp.exp2(sm - jnp.max(sm, axis=0, keepdims=True)), 0.0)
    den = jnp.sum(e, axis=0, keepdims=True)
    p = e / jnp.where(den > 0.0, den, 1.0)
    o = lax.dot_general(vc_ref[...], p.astype(BF16), (((0,), (0,)), ((), ())), preferred_element_type=F32)
    for g in range(groups):
        o_ref[:, g * QK_DIM:(g + 1) * QK_DIM] = o[:, g * tq:(g + 1) * tq].T

    psum = p[:, 0:tq]
    for g in range(1, groups):
        psum = psum + p[:, g * tq:(g + 1) * tq]
    hi = psum.astype(BF16)
    lo = (psum - hi.astype(F32)).astype(BF16)
    ov = ov_ref[...]
    imp = (jnp.dot(ov, hi, preferred_element_type=F32) + jnp.dot(ov, lo, preferred_element_type=F32))
    ns = imp.shape[0]
    blk = lax.broadcasted_iota(jnp.int32, (ns, tq), 0)
    qp = i * tq + lax.broadcasted_iota(jnp.int32, (ns, tq), 1)
    cur = qp // SLC_LEN
    valid = blk <= cur
    forced = valid & ((blk == 0) | (blk == cur) | (blk == cur - 1))
    n_forced = jnp.sum(forced.astype(F32), axis=0, keepdims=True)
    target = jnp.minimum(jnp.sum(valid.astype(F32), axis=0, keepdims=True), float(SLC_TOPK))
    work0 = jnp.where(valid & jnp.logical_not(forced), imp, NEG_INIT)

    def select(exact_ties):
        work, picked = work0, forced
        for it in range(SLC_TOPK - 1):
            mx = jnp.max(work, axis=0, keepdims=True)
            live = (mx > 0.5 * NEG_INIT) & (n_forced + float(it) < float(SLC_TOPK))
            if exact_ties:
                pick = blk == jnp.min(jnp.where(work == mx, blk, ns), axis=0, keepdims=True)
            else:
                pick = work == mx
            pick = pick & live
            picked = picked | pick
            work = jnp.where(pick, NEG_INIT, work)
        sel_ref[0] = jnp.where(picked, 0.0, MASKVAL).astype(BF16)
        return jnp.abs(jnp.sum(picked.astype(F32), axis=0, keepdims=True) - target)

    excess = select(False)

    @pl.when(jnp.max(excess) > 0.0)
    def _():
        select(True)


def _nsa_cmp(qn, kcx, vcx, overlapT, tq, bound_ok):
    s = qn.shape[0]
    ncmp = kcx.shape[0]
    ns = overlapT.shape[0]
    groups = NSA_G

    def call(bounded):
        return pl.pallas_call(
            functools.partial(_nsa_cmp_kernel, tq=tq, groups=groups, bounded=bounded),
            grid=(NSA_KV, s // tq),
            in_specs=[pl.BlockSpec((tq, groups * QK_DIM), lambda h, i: (i, h)),
                      pl.BlockSpec((ncmp, QK_DIM), lambda h, i: (0, h)),
                      pl.BlockSpec((ncmp, QK_DIM), lambda h, i: (0, h)),
                      _full(overlapT.shape)],
            out_specs=[pl.BlockSpec((tq, groups * QK_DIM), lambda h, i: (i, h)),
                       pl.BlockSpec((1, ns, tq), lambda h, i: (h, 0, i))],
            out_shape=[jax.ShapeDtypeStruct(qn.shape, F32), jax.ShapeDtypeStruct((NSA_KV, ns, s), BF16)],
            compiler_params=_cparams(("parallel", "parallel")),
            name="nsa_cmp_bounded" if bounded else "nsa_cmp_max",
        )(qn, kcx, vcx, overlapT)

    return lax.cond(bound_ok, lambda: call(True), lambda: call(False))


def _mix_out_kernel(lam_ref, h_ref, fox_ref, oc_ref, os_ref, ow_ref, dd_ref, g_ref, sw_ref, wf_ref, wn_ref, wd_ref,
                    o_ref, *, diff_scale):
    lam = lam_ref[0]
    gates = g_ref[...]
    tm = gates.shape[0]
    lane = lax.broadcasted_iota(jnp.int32, (tm, LANES), 1)
    slot = lambda ref, n: ref[:, n * LANES:(n + 1) * LANES]
    nsa = []
    for h in range(NSA_HEADS):
        c0 = FOX_HEADS + 3 * h
        nsa.append(gates[:, c0:c0 + 1] * slot(oc_ref, h) + gates[:, c0 + 1:c0 + 2] * slot(os_ref, h)
                   + gates[:, c0 + 2:c0 + 3] * slot(ow_ref, h))
    diff = []
    for h in range(DIFF_HEADS):
        a = jnp.where(lane < DIFF_V, slot(dd_ref, 2 * h) - lam * slot(dd_ref, 2 * h + 1), 0.0)
        ss = jnp.sum(a * a, axis=-1, keepdims=True)
        diff.append(a * lax.rsqrt(ss * (1.0 / DIFF_V) + EPS) * sw_ref[...] * diff_scale)
    acc = h_ref[...]
    acc = acc + jnp.dot(fox_ref[...].astype(BF16), wf_ref[...], preferred_element_type=F32)
    acc = acc + jnp.dot(jnp.concatenate(nsa, axis=1).astype(BF16), wn_ref[...], preferred_element_type=F32)
    acc = acc + jnp.dot(jnp.concatenate(diff, axis=1).astype(BF16), wd_ref[...], preferred_element_type=F32)
    o_ref[...] = acc


def _slot_rows(w, heads):
    d = w.shape[1]
    w = w.reshape(heads, -1, d)
    return jnp.concatenate([w, jnp.zeros((heads, LANES - w.shape[1], d), w.dtype)], axis=1).reshape(heads * LANES, d)


def _mix_out(lam, h, fox, oc, os_, ow, dd, gates, subln_w, w_out, diff_scale, tm=512):
    s, d = h.shape
    nf, nn = FOX_HEADS * FOX_HD, NSA_HEADS * NSA_HD
    wf = _slot_rows(w_out[:nf], FOX_HEADS).astype(BF16)
    wn = _slot_rows(w_out[nf:nf + nn], NSA_HEADS).astype(BF16)
    wd = _slot_rows(w_out[nf + nn:], DIFF_HEADS).astype(BF16)
    sw = jnp.concatenate([subln_w.reshape(1, -1), jnp.zeros((1, LANES - DIFF_V), F32)], axis=1)
    row = lambda w: pl.BlockSpec((tm, w), lambda i: (i, 0))
    return pl.pallas_call(
        functools.partial(_mix_out_kernel, diff_scale=diff_scale),
        grid=(s // tm,),
        in_specs=[pl.BlockSpec(memory_space=pltpu.SMEM), row(d), row(fox.shape[1]), row(oc.shape[1]),
                  row(os_.shape[1]), row(ow.shape[1]), row(dd.shape[1]), row(LANES), _full((1, LANES)),
                  _full(wf.shape), _full(wn.shape), _full(wd.shape)],
        out_specs=row(d),
        out_shape=jax.ShapeDtypeStruct((s, d), F32),
        compiler_params=_cparams(("parallel",)),
        name="mix_out",
    )(lam, h, fox, oc, os_, ow, dd, gates, sw, wf, wn, wd)


def _peer_query_kernel(h_ref, nw_ref, wq_ref, keys_ref, xnT_o, scT_o):
    xn = _rms(h_ref[...], nw_ref[...])
    xnT_o[...] = xn.T.astype(BF16)
    q = jnp.dot(xn.astype(BF16), wq_ref[...], preferred_element_type=F32).astype(BF16)
    half = PEER_DQ // 2
    for b in range(2 * PEER_HEADS):
        sc = jnp.dot(q[:, b * half:(b + 1) * half], keys_ref[b], preferred_element_type=F32)
        scT_o[b] = sc.T


def _mix_query_kernel(lam_ref, h_ref, fox_ref, oc_ref, os_ref, ow_ref, dd_ref, g_ref, sw_ref, wf_ref, wn_ref, wd_ref,
                      nw_ref, wq_ref, keys_ref, h1_o, xnT_o, scT_o, *, diff_scale):
    _mix_out_kernel(lam_ref, h_ref, fox_ref, oc_ref, os_ref, ow_ref, dd_ref, g_ref, sw_ref, wf_ref, wn_ref, wd_ref,
                    h1_o, diff_scale=diff_scale)
    _peer_query_kernel(h1_o, nw_ref, wq_ref, keys_ref, xnT_o, scT_o)


def _mix_query(lam, h, fox, oc, os_, ow, dd, gates, subln_w, w_out, diff_scale, nw, wq_bf16, keysT_bf16, tm=256):
    s, d = h.shape
    nf, nn = FOX_HEADS * FOX_HD, NSA_HEADS * NSA_HD
    wf = _slot_rows(w_out[:nf], FOX_HEADS).astype(BF16)
    wn = _slot_rows(w_out[nf:nf + nn], NSA_HEADS).astype(BF16)
    wd = _slot_rows(w_out[nf + nn:], DIFF_HEADS).astype(BF16)
    sw = jnp.concatenate([subln_w.reshape(1, -1), jnp.zeros((1, LANES - DIFF_V), F32)], axis=1)
    nb = 2 * PEER_HEADS
    row = lambda w: pl.BlockSpec((tm, w), lambda i: (i, 0))
    return pl.pallas_call(
        functools.partial(_mix_query_kernel, diff_scale=diff_scale),
        grid=(s // tm,),
        in_specs=[pl.BlockSpec(memory_space=pltpu.SMEM), row(d), row(fox.shape[1]), row(oc.shape[1]),
                  row(os_.shape[1]), row(ow.shape[1]), row(dd.shape[1]), row(LANES), _full((1, LANES)),
                  _full(wf.shape), _full(wn.shape), _full(wd.shape),
                  _full((1, d)), _full(wq_bf16.shape), _full(keysT_bf16.shape)],
        out_specs=[row(d), pl.BlockSpec((d, tm), lambda i: (0, i)),
                   pl.BlockSpec((nb, PEER_NKEYS, tm), lambda i: (0, 0, i))],
        out_shape=[jax.ShapeDtypeStruct((s, d), F32), jax.ShapeDtypeStruct((d, s), BF16),
                   jax.ShapeDtypeStruct((nb, PEER_NKEYS, s), F32)],
        compiler_params=_cparams(("parallel",)),
        name="mix_out_peer_query",
    )(lam, h, fox, oc, os_, ow, dd, gates, sw, wf, wn, wd, nw.reshape(1, d), wq_bf16, keysT_bf16)


def _peer_query(h, nw, wq_bf16, keysT_bf16, tm=512):
    s, d = h.shape
    n = wq_bf16.shape[1]
    nb = 2 * PEER_HEADS
    return pl.pallas_call(
        _peer_query_kernel,
        grid=(s // tm,),
        in_specs=[pl.BlockSpec((tm, d), lambda i: (i, 0)), _full((1, d)), _full((d, n)), _full(keysT_bf16.shape)],
        out_specs=[pl.BlockSpec((d, tm), lambda i: (0, i)), pl.BlockSpec((nb, PEER_NKEYS, tm), lambda i: (0, 0, i))],
        out_shape=[jax.ShapeDtypeStruct((d, s), BF16), jax.ShapeDtypeStruct((nb, PEER_NKEYS, s), F32)],
        compiler_params=_cparams(("parallel",)),
        name="peer_query",
    )(h, nw.reshape(1, d), wq_bf16, keysT_bf16)


def _top16_rows(s, exact_ties):
    n = s.shape[0]
    row = lax.broadcasted_iota(jnp.int32, s.shape, 0)
    rank = jnp.full(s.shape, float(PEER_TOPK), dtype=F32)
    work = s
    vals = []
    for r in range(PEER_TOPK):
        mx = jnp.max(work, axis=0, keepdims=True)
        if exact_ties:
            pick = row == jnp.min(jnp.where(work == mx, row, n), axis=0, keepdims=True)
        else:
            pick = work == mx
        rank = jnp.where(pick, float(r), rank)
        work = jnp.where(pick, NEG_INIT, work)
        vals.append(mx)
    return rank, vals


def _peer_route(s1, s2, exact_ties):
    rank1, v1 = _top16_rows(s1, exact_ties)
    rank2, v2 = _top16_rows(s2, exact_ties)
    k = PEER_TOPK
    tm = s1.shape[1]
    r16 = lax.broadcasted_iota(jnp.int32, (k, tm), 0)
    v1m = jnp.zeros((k, tm), F32)
    v2m = jnp.zeros((k, tm), F32)
    for r in range(k):
        v1m = jnp.where(r16 == r, v1[r], v1m)
        v2m = jnp.where(r16 == r, v2[r], v2m)
    ea = jnp.exp(v1m - v1[0])
    eb = jnp.exp(v2m - v2[0])
    cands, gates = [v1m[0:1] + v2m], [ea[0:1] * eb]
    for r in range(1, 8):
        cands.append(v1m[r:r + 1] + v2m[0:8])
        gates.append(ea[r:r + 1] * eb[0:8])
    cands.append(v1m[8:16] + v2m[0:1])
    gates.append(ea[8:16] * eb[0:1])
    cand = jnp.concatenate(cands, axis=0)
    gate = jnp.concatenate(gates, axis=0)
    ncand = cand.shape[0]
    crow = lax.broadcasted_iota(jnp.int32, cand.shape, 0)
    for r in range(2, 8):
        start = 16 + 8 * (r - 1)
        cand = jnp.where((crow >= start + k // (r + 1)) & (crow < start + 8), NEG_INIT, cand)
    work = cand
    picked = jnp.zeros(cand.shape, dtype=jnp.bool_)
    for _ in range(k):
        mx = jnp.max(work, axis=0, keepdims=True)
        if exact_ties:
            pick = crow == jnp.min(jnp.where(work == mx, crow, ncand), axis=0, keepdims=True)
        else:
            pick = work == mx
        picked = picked | pick
        work = jnp.where(pick, NEG_INIT, work)
    pf = picked.astype(F32)
    z = jnp.sum(pf * gate, axis=0, keepdims=True)
    cnt = [jnp.sum(pf[0:16, :], axis=0, keepdims=True)]
    cnt += [jnp.sum(pf[8 + 8 * r:16 + 8 * r, :], axis=0, keepdims=True) for r in range(1, 8)]
    cnt += [pf[64 + r:65 + r, :] for r in range(8, 16)]
    cmap = jnp.zeros(s1.shape, dtype=F32)
    for r in range(k):
        cmap = jnp.where(rank1 == float(r), cnt[r], cmap)
    in1 = rank1 < float(k)
    in2 = rank2 < float(k)
    a = jnp.where(in1, jnp.exp(s1 - v1[0]), 0.0) / (2.0 * z)
    b = jnp.where(in2, jnp.exp(s2 - v2[0]), 0.0)
    excess = (jnp.abs(jnp.sum(in1.astype(F32), axis=0, keepdims=True) - k)
              + jnp.abs(jnp.sum(in2.astype(F32), axis=0, keepdims=True) - k)
              + jnp.abs(jnp.sum(pf, axis=0, keepdims=True) - k))
    return a, cmap, b, rank2, excess


def _peer_topk_kernel(sc_ref, a_o, c_o, b_o, r_o):
    def run(h, exact_ties):
        a, c, b, rank2, excess = _peer_route(sc_ref[2 * h], sc_ref[2 * h + 1], exact_ties)
        a_o[h] = a
        c_o[h] = c
        b_o[h] = b.astype(BF16)
        r_o[h] = rank2.astype(BF16)
        return excess

    for h in range(a_o.shape[0]):
        excess = run(h, False)

        @pl.when(jnp.max(excess) > 0.0)
        def _(h=h):
            run(h, True)


def _peer_topk(scT, tm=256, heads_per_step=2):
    n2, nk, s = scT.shape
    heads = n2 // 2
    hps = heads_per_step
    ospec = pl.BlockSpec((hps, nk, tm), lambda h, t: (h, 0, t))
    return pl.pallas_call(
        _peer_topk_kernel,
        grid=(heads // hps, s // tm),
        in_specs=[pl.BlockSpec((2 * hps, nk, tm), lambda h, t: (h, 0, t))],
        out_specs=[ospec] * 4,
        out_shape=[jax.ShapeDtypeStruct((heads, nk, s), dt) for dt in (F32, F32, BF16, BF16)],
        compiler_params=_cparams(("parallel", "parallel")),
        name="peer_topk",
    )(scT)


def _gelu_exact_x2(x):
    return x * (1.0 + lax.erf(x * (2.0 ** -0.5)))


def _peer_main_kernel(xT_ref, u_ref, v_ref, a_ref, c_ref, b_ref, r_ref, o_ref, w_sc, *, ec, sub):
    ci = pl.program_id(1)

    @pl.when(ci == 0)
    def _():
        o_ref[...] = jnp.zeros_like(o_ref)

    nk = PEER_NKEYS
    tm = xT_ref.shape[1]

    def row(ref, h, i1, lanes):
        x16 = jnp.broadcast_to(ref[h, pl.ds(i1, 1), :][:, lanes], (16, GATE_LANES)).astype(BF16)
        return jnp.concatenate([x16] * (nk // 16), axis=0)

    acc = None
    for sc in range(ec // sub):
        hid = jnp.dot(u_ref[sc * sub:(sc + 1) * sub, :], xT_ref[...], preferred_element_type=F32)
        act = _gelu_exact_x2(hid).astype(BF16)
        for ii in range(sub // nk):
            slab = sc * (sub // nk) + ii
            i1 = ci * (ec // nk) + slab
            for l0 in range(0, tm, GATE_LANES):
                lanes = slice(l0, l0 + GATE_LANES)
                w = None
                for h in range(PEER_HEADS):
                    keep = r_ref[h, :, lanes] < row(c_ref, h, i1, lanes)
                    term = jnp.where(keep, b_ref[h, :, lanes], 0.0) * row(a_ref, h, i1, lanes)
                    w = term if w is None else w + term
                w_sc[slab * nk:(slab + 1) * nk, lanes] = w * act[ii * nk:(ii + 1) * nk, lanes]
        part = jnp.dot(w_sc[sc * sub:(sc + 1) * sub, :].T, v_ref[sc * sub:(sc + 1) * sub, :],
                       preferred_element_type=F32)
        acc = part if acc is None else acc + part
    o_ref[...] += acc


def _peer_main(xnT, u_bf16, v_bf16, a, c, b, r, tm=GATE_LANES, ec=2048, sub=1024):
    d, s = xnT.shape
    e = u_bf16.shape[0]
    heads, nk, _ = a.shape
    rt = pl.BlockSpec((heads, nk, tm), lambda t, ci: (0, 0, t))
    wt = pl.BlockSpec((ec, d), lambda t, ci: (ci, 0))
    return pl.pallas_call(
        functools.partial(_peer_main_kernel, ec=ec, sub=sub),
        grid=(s // tm, e // ec),
        in_specs=[pl.BlockSpec((d, tm), lambda t, ci: (0, t)), wt, wt, rt, rt, rt, rt],
        out_specs=pl.BlockSpec((tm, d), lambda t, ci: (t, 0)),
        out_shape=jax.ShapeDtypeStruct((s, d), F32),
        scratch_shapes=[pltpu.VMEM((ec, tm), BF16)],
        compiler_params=_cparams(("parallel", "arbitrary")),
        name="peer_main",
    )(xnT, u_bf16, v_bf16, a, c, b, r)


def _ple_kernel(h_ref, peer_ref, p_ref, nw_ref, wg_ref, wp_ref, o_ref):
    h2 = h_ref[...] + peer_ref[...]
    xn = _rms(h2, nw_ref[...]).astype(BF16)
    gate = 1.0 / (1.0 + jnp.exp(-jnp.dot(xn, wg_ref[...], preferred_element_type=F32)))
    emb = jnp.dot(p_ref[...].astype(BF16), wp_ref[...], preferred_element_type=F32)
    o_ref[...] = h2 + gate * emb


def _ple(h, peer, p, nw, wg_bf16, wp_bf16, tm=512):
    s, d = h.shape
    pd = p.shape[1]
    row = lambda w: pl.BlockSpec((tm, w), lambda i: (i, 0))
    return pl.pallas_call(
        _ple_kernel,
        grid=(s // tm,),
        in_specs=[row(d), row(d), row(pd), _full((1, d)), _full((d, d)), _full((pd, d))],
        out_specs=row(d),
        out_shape=jax.ShapeDtypeStruct((s, d), F32),
        compiler_params=_cparams(("parallel",)),
        name="ple",
    )(h, peer, p, nw.reshape(1, d), wg_bf16, wp_bf16)


def _split_w_in(w):
    f0 = 3 * FOX_HEADS * FOX_HD
    n0 = f0 + FOX_HEADS
    g0 = n0 + _SEG["dq"][0] - _SEG["nq"][0]
    d0 = g0 + 3 * NSA_HEADS
    misc = jnp.concatenate([w[:, f0:n0], w[:, g0:d0], jnp.zeros((w.shape[0], LANES - (n0 - f0) - (d0 - g0)), w.dtype)],
                           axis=1)
    return [t.astype(BF16) for t in (w[:, :f0], w[:, n0:g0], w[:, d0:], misc)]


def _overlap_T(s):
    n = np.arange(s // CMP_STRIDE)[None, :] * CMP_STRIDE
    m = np.arange(s // SLC_LEN)[:, None] * SLC_LEN
    return jnp.asarray(((n < m + SLC_LEN) & (n + CMP_LEN > m)).astype(np.float32), dtype=BF16)


def _layer(h, p_i, tabs, layer, attn_norm_w, w_in, fox_f_bias, fox_q_norm_w, fox_k_norm_w, nsa_q_norm_w,
           nsa_k_norm_w, nsa_cmp_pos, nsa_cmp_w, diff_q_norm_w, diff_k_norm_w, diff_lambda, diff_subln_w,
           w_out, ffn_norm_w, peer_w_q, peer_sub_keys, peer_u, peer_v, ple_norm_w, ple_w_gate, ple_w_proj):
    s = h.shape[0]
    fb =jnp.zeros((1, LANES), F32).at[0, :FOX_HEADS].set(fox_f_bias)
    u_f = _score_bound(fox_q_norm_w, fox_k_norm_w, FOX_HD, FOX_HD ** -0.5)
    u_n = _score_bound(nsa_q_norm_w, nsa_k_norm_w, NSA_HD, NSA_HD ** -0.5)
    u_d = _score_bound(diff_q_norm_w, diff_k_norm_w, DIFF_QK, DIFF_QK ** -0.5)
    shifts = SHIFT_HEADROOM - jnp.stack([u_f, u_n, u_d]).astype(F32)
    (fq, fk, fv, nqn, nqr, kvc, ks, vs, kw, vw, gates, dq, dk, dv, csum) = _prep(
        shifts, h, attn_norm_w, _split_w_in(w_in), tabs, fb, fox_q_norm_w, fox_k_norm_w, nsa_q_norm_w, nsa_k_norm_w, diff_q_norm_w, diff_k_norm_w)

    c_first_q = csum[0::Q_LANES, :FOX_HEADS]
    c_last_k = csum[TK_CAUSAL - 1::TK_CAUSAL, :FOX_HEADS]
    decay = (c_first_q.T[:, :, None] - c_last_k.T[:, None, :]).reshape(-1)
    o_fox = _flash(fq, fk, fv, None, decay, bound_ok=u_f <= SCORE_BOUND, groups=1, tq=Q_LANES, tk=TK_CAUSAL)

    tq_n = Q_LANES // NSA_G
    nsa_ok = u_n <= SCORE_BOUND
    kcx, vcx = _compress(kvc, nsa_cmp_w, nsa_cmp_pos, nsa_k_norm_w)
    o_c, sel = _nsa_cmp(nqn, kcx, vcx, _overlap_T(s), tq_n, nsa_ok)
    o_s = _flash(nqr, ks, vs, sel, bound_ok=nsa_ok, groups=NSA_G, tq=tq_n, tk=TK_CAUSAL)
    o_w = _flash(nqr, kw, vw, bound_ok=nsa_ok, groups=NSA_G, tq=tq_n, tk=512, window=WIN)

    o_d = _flash(dq, dk, dv, bound_ok=u_d <= SCORE_BOUND, groups=2, tq=Q_LANES // 2, tk=TK_CAUSAL)
    lv = diff_lambda.astype(F32)
    lam_init = 0.8 - 0.6 * math.exp(-0.3 * layer)
    lam = (jnp.exp(jnp.sum(lv[0] * lv[1])) - jnp.exp(jnp.sum(lv[2] * lv[3])) + lam_init).reshape(1)
    keysT = peer_sub_keys.reshape(2 * PEER_HEADS, PEER_NKEYS, PEER_DQ // 2).transpose(0, 2, 1).astype(BF16)
    h1, xnT, scT = _mix_query(lam, h, o_fox, o_c, o_s, o_w, o_d, gates, diff_subln_w, w_out, 1.0 - lam_init,
                              ffn_norm_w, peer_w_q.astype(BF16), keysT)
    a, c, b, r = _peer_topk(scT)
    peer = _peer_main(xnT, peer_u.astype(BF16), peer_v.astype(BF16), a, c, b, r)

    return _ple(h1, peer, p_i, ple_norm_w, ple_w_gate.astype(BF16), ple_w_proj.astype(BF16))


def kernel(x, p, positions, attn_norm_w, w_in, fox_f_bias, fox_q_norm_w, fox_k_norm_w, nsa_q_norm_w, nsa_k_norm_w,
           nsa_cmp_pos, nsa_cmp_w, diff_q_norm_w, diff_k_norm_w, diff_lambda, diff_subln_w, w_out, ffn_norm_w,
           peer_w_q, peer_sub_keys, peer_u, peer_v, ple_norm_w, ple_w_gate, ple_w_proj):
    b, s, d = x.shape
    assert b == 1 and d == D_MODEL and s % Q_LANES == 0 and s % TK_CAUSAL == 0
    tabs = _rope_tables(positions)
    h = x.reshape(s, d)
    per_layer = (attn_norm_w, w_in, fox_f_bias, fox_q_norm_w, fox_k_norm_w, nsa_q_norm_w, nsa_k_norm_w, nsa_cmp_pos,
                 nsa_cmp_w, diff_q_norm_w, diff_k_norm_w, diff_lambda, diff_subln_w, w_out, ffn_norm_w, peer_w_q,
                 peer_sub_keys, peer_u, peer_v, ple_norm_w, ple_w_gate, ple_w_proj)
    for layer in range(attn_norm_w.shape[0]):
        h = _layer(h, p[layer, 0], tabs, layer, *(w[layer] for w in per_layer))
    return h.reshape(b, s, d)
```

```python
import functools
import math

import numpy as np
import jax
import jax.numpy as jnp
from jax import lax
from jax.experimental import pallas as pl
from jax.experimental.pallas import tpu as pltpu

F32 = jnp.float32
BF16 = jnp.bfloat16
HIGHEST = lax.Precision.HIGHEST

D_MODEL = 1024
PLE_DIM = 256
ROPE_THETA = 10000.0
EPS = 1e-6
FOX_HEADS, FOX_HD = 4, 64
NSA_HEADS, NSA_KV, NSA_HD = 8, 2, 64
NSA_G = NSA_HEADS // NSA_KV
CMP_LEN, CMP_STRIDE, SLC_LEN, SLC_TOPK, WIN = 32, 16, 64, 16, 512
FORCE_BONUS = 1.0e4
assert FORCE_BONUS > NSA_G
DIFF_HEADS, DIFF_QK, DIFF_V = 4, 32, 64
PEER_HEADS, PEER_NKEYS, PEER_DQ, PEER_TOPK = 8, 128, 256, 16
PEER_EXPERTS = PEER_NKEYS * PEER_NKEYS

LANES = 128
LOG2E = math.log2(math.e)
NEG_INIT = -1.0e30
MASKVAL = -2.0e30
VMEM_LIMIT = 56 * 1024 * 1024
TK_CAUSAL = 2048
Q_LANES = 2048
GATE_LANES = 512

_SEG = dict(fq=(0, 256), fk=(256, 256), fv=(512, 256), nq=(768, 512), nkc=(1280, 128), nvc=(1408, 128),
            nks=(1536, 128), nvs=(1664, 128), nkw=(1792, 128), nvw=(1920, 128),
            dq=(2048, 256), dk=(2304, 256), dv=(2560, 256), misc=(2816, 128))
PROJ_W = 2944


def _cparams(sem):
    return pltpu.CompilerParams(dimension_semantics=sem, vmem_limit_bytes=VMEM_LIMIT)


def _full(shape):
    n = len(shape)
    return pl.BlockSpec(shape, lambda *_: (0,) * n)


def _rope_tab_kernel(pos_ref, f64_ref, g64_ref, f32_ref, g32_ref, c64_o, s64_o, c32_o, s32_o):
    pos = pos_ref[...].astype(F32)
    a64 = pos * f64_ref[...]
    c64_o[...] = jnp.cos(a64)
    s64_o[...] = jnp.sin(a64) * g64_ref[...]
    a32 = pos * f32_ref[...]
    c32_o[...] = jnp.cos(a32)
    s32_o[...] = jnp.sin(a32) * g32_ref[...]


def _rope_tables(positions):
    s = positions.shape[-1]
    pos = positions.reshape(s, 1)
    lane = np.arange(LANES)

    def lanes(half):
        inv = ROPE_THETA ** (-jnp.arange(half, dtype=F32) / half)
        freq = inv[(lane % (2 * half)) % half].reshape(1, LANES)
        sign = np.where((lane % (2 * half)) < half, -1.0, 1.0).astype(np.float32).reshape(1, LANES)
        return freq, jnp.asarray(sign)

    f64, g64 = lanes(NSA_HD // 2)
    f32_, g32 = lanes(DIFF_QK // 2)
    tm = 512
    out = jax.ShapeDtypeStruct((s, LANES), F32)
    row = pl.BlockSpec((tm, LANES), lambda i: (i, 0))
    return pl.pallas_call(
        _rope_tab_kernel,
        grid=(s // tm,),
        in_specs=[pl.BlockSpec((tm, 1), lambda i: (i, 0))] + [_full((1, LANES))] * 4,
        out_specs=[row] * 4,
        out_shape=[out] * 4,
        compiler_params=_cparams(("parallel",)),
        name="rope_tables",
    )(pos, f64, g64, f32_, g32)


def _rms(x, w):
    return x * lax.rsqrt(jnp.mean(x * x, axis=-1, keepdims=True) + EPS) * w


def _seg_rms(x, bmat, seg):
    outs = []
    for c0 in range(0, x.shape[1], 256):
        w = min(256, x.shape[1] - c0)
        xc = x[:, c0:c0 + w]
        sq = xc * xc
        hi = sq.astype(BF16)
        lo = (sq - hi.astype(F32)).astype(BF16)
        b = bmat[:w, :w].astype(BF16)
        ss = jnp.dot(hi, b, preferred_element_type=F32) + jnp.dot(lo, b, preferred_element_type=F32)
        outs.append(xc * lax.rsqrt(ss * (1.0 / seg) + EPS))
    return outs[0] if len(outs) == 1 else jnp.concatenate(outs, axis=1)


def _tile_lanes(t, width):
    reps = width // LANES
    return t if reps == 1 else jnp.concatenate([t] * reps, axis=1)


def _rope(x, cos, sin, half):
    width = x.shape[1]
    left = pltpu.roll(x, width - half, 1)
    right = pltpu.roll(x, half, 1)
    lane = lax.broadcasted_iota(jnp.int32, x.shape, 1)
    swapped = jnp.where((lane & (2 * half - 1)) < half, left, right)
    return x * _tile_lanes(cos, width) + swapped * _tile_lanes(sin, width)


def _lane_fill(tm, cols):
    lane = lax.broadcasted_iota(jnp.int32, (tm, LANES), 1)
    out = jnp.zeros((tm, LANES), F32)
    for l, v in cols.items():
        out = jnp.where(lane == l, v, out)
    return out


def _slots(x, extras):
    lane = lax.broadcasted_iota(jnp.int32, (x.shape[0], LANES), 1)
    outs = []
    for h, ex in enumerate(extras):
        col = x[:, (h // 2) * LANES:(h // 2 + 1) * LANES]
        if h % 2:
            col = pltpu.roll(col, LANES // 2, 1)
        outs.append(jnp.where(lane < LANES // 2, col, ex))
    return jnp.concatenate(outs, axis=1)


def _prep_kernel(shift_ref, proj_ref, c64_ref, s64_ref, c32_ref, s32_ref, fb_ref, wfq_ref, wfk_ref, wnq_ref,
                 wnk_ref, wdq_ref, wdk_ref, b64_ref, b32_ref,
                 fq_o, fk_o, fv_o, nqn_o, nqr_o, kvc_o, ks_o, vs_o, kw_o, vw_o, gate_o, dq_o, dk_o, dv_o, cs_o,
                 carry_sc, *, tm):
    def seg(name):
        c0, w = _SEG[name]
        return proj_ref[:, c0:c0 + w]

    b64 = b64_ref[...]
    b32 = b32_ref[...]
    c64, s64, c32, s32 = c64_ref[...], s64_ref[...], c32_ref[...], s32_ref[...]
    one_ex = _lane_fill(tm, {COL_SHIFT: 1.0})

    misc = seg("misc")
    gate_o[...] = 1.0 / (1.0 + jnp.exp(-misc))
    t = misc + fb_ref[...]
    logf = jnp.minimum(t, 0.0) - jnp.log1p(jnp.exp(-jnp.abs(t)))

    @pl.when(pl.program_id(0) == 0)
    def _():
        carry_sc[...] = jnp.zeros_like(carry_sc)

    r = lax.broadcasted_iota(jnp.int32, (tm, tm), 0)
    c = lax.broadcasted_iota(jnp.int32, (tm, tm), 1)
    tri = (c <= r).astype(F32)
    csum = jnp.dot(tri, logf, precision=HIGHEST, preferred_element_type=F32) + carry_sc[0:1, :]
    carry_sc[...] = jnp.broadcast_to(csum[tm - 1:tm, :], carry_sc.shape)
    csum = csum * LOG2E
    hi = csum.astype(BF16).astype(F32)
    r1 = csum - hi
    mid = r1.astype(BF16).astype(F32)
    lo = (r1 - mid).astype(BF16).astype(F32)
    cs_o[...] = csum

    fq = _seg_rms(seg("fq"), b64, FOX_HD) * wfq_ref[...] * (FOX_HD ** -0.5 * LOG2E)
    fk = _seg_rms(seg("fk"), b64, FOX_HD) * wfk_ref[...]
    q_ex, k_ex = [], []
    for h in range(FOX_HEADS):
        ch, cm, cl = hi[:, h:h + 1], mid[:, h:h + 1], lo[:, h:h + 1]
        q_ex.append(_lane_fill(tm, {COL_SHIFT: shift_ref[0], 65: 1.0, 66: 1.0, 67: 1.0, 68: ch, 69: cm, 70: cl}))
        k_ex.append(_lane_fill(tm, {COL_SHIFT: 1.0, 65: -ch, 66: -cm, 67: -cl, 68: 1.0, 69: 1.0, 70: 1.0}))
    fq_o[...] = _slots(fq, q_ex)
    fk_o[...] = _slots(fk, k_ex).astype(BF16)
    fv_o[...] = _slots(seg("fv"), [one_ex] * FOX_HEADS).astype(BF16)

    n_ex = [_lane_fill(tm, {COL_SHIFT: shift_ref[1]})] * NSA_HEADS
    nqn = _seg_rms(seg("nq"), b64, NSA_HD) * wnq_ref[...]
    nqn_o[...] = _slots(nqn * (NSA_HD ** -0.5 * LOG2E), n_ex)
    nqr_o[...] = _slots(_rope(nqn, c64, s64, NSA_HD // 2) * (NSA_HD ** -0.5 * LOG2E), n_ex)
    kvc_o[:, 0:128] = seg("nkc").astype(BF16)
    kvc_o[:, 128:256] = seg("nvc").astype(BF16)
    wnk = wnk_ref[...]
    lane = lax.broadcasted_iota(jnp.int32, (tm, LANES), 1)
    row = pl.program_id(0) * tm + lax.broadcasted_iota(jnp.int32, (tm, LANES), 0)
    blk_in_tile = (row & (TK_CAUSAL - 1)) // SLC_LEN
    sel_ex = jnp.where(lane == COL_SEL + blk_in_tile, 1.0, one_ex)
    ks = _rope(_seg_rms(seg("nks"), b64, NSA_HD) * wnk, c64, s64, NSA_HD // 2)
    ks_o[...] = _slots(ks, [sel_ex] * NSA_KV).astype(BF16)
    vs_o[...] = _slots(seg("nvs"), [one_ex] * NSA_KV).astype(BF16)
    kw = _rope(_seg_rms(seg("nkw"), b64, NSA_HD) * wnk, c64, s64, NSA_HD // 2)
    kw_o[...] = _slots(kw, [one_ex] * NSA_KV).astype(BF16)
    vw_o[...] = _slots(seg("nvw"), [one_ex] * NSA_KV).astype(BF16)

    dq = _rope(_seg_rms(seg("dq"), b32, DIFF_QK) * wdq_ref[...], c32, s32, DIFF_QK // 2) * (DIFF_QK ** -0.5 * LOG2E)
    d_ex = _lane_fill(tm, {COL_SHIFT: shift_ref[2]})
    d_slots = []
    for h in range(DIFF_HEADS):
        col = dq[:, (h // 2) * LANES:(h // 2 + 1) * LANES]
        if h % 2:
            col = pltpu.roll(col, LANES // 2, 1)
        d_slots.append(jnp.where(lane < DIFF_QK, col, d_ex))
        d_slots.append(jnp.where((lane >= DIFF_QK) & (lane < 2 * DIFF_QK), col, d_ex))
    dq_o[...] = jnp.concatenate(d_slots, axis=1)
    dk = _rope(_seg_rms(seg("dk"), b32, DIFF_QK) * wdk_ref[...], c32, s32, DIFF_QK // 2)
    dk_o[...] = _slots(dk, [one_ex] * DIFF_HEADS).astype(BF16)
    dv_o[...] = _slots(seg("dv"), [one_ex] * DIFF_HEADS).astype(BF16)


def _block_diag_ones(n, seg):
    i = np.arange(n)
    return jnp.asarray((i[:, None] // seg == i[None, :] // seg).astype(np.float32))


def _proj_prep_kernel(shift_ref, x_ref, nw_ref, wa_ref, wb_ref, wc_ref, wm_ref, *rest, tm):
    proj_sc = rest[-1]
    xn = _rms(x_ref[...], nw_ref[...]).astype(BF16)
    c0 = 0
    for w_ref in (wa_ref, wb_ref, wc_ref, wm_ref):
        n = w_ref.shape[1]
        proj_sc[:, c0:c0 + n] = jnp.dot(xn, w_ref[...], preferred_element_type=F32)
        c0 += n
    _prep_kernel(shift_ref, proj_sc, *rest[:-1], tm=tm)


def _prep(shifts, x, nw, ws_bf16, tabs, fb, wfq, wfk, wnq, wnk, wdq, wdk, tm=512):
    s, d = x.shape
    c64, s64, c32, s32 = tabs
    b64 = _block_diag_ones(256, 64)
    b32 = _block_diag_ones(256, 32)

    def tiled(w, width):
        return jnp.tile(w.reshape(1, -1), (1, width // w.shape[-1]))

    consts = [fb, tiled(wfq, 256), tiled(wfk, 256), tiled(wnq, 512), tiled(wnk, 128), tiled(wdq, 256),
              tiled(wdk, 256), b64, b32]
    outs = [(FOX_HEADS, F32), (FOX_HEADS, BF16), (FOX_HEADS, BF16),
            (NSA_HEADS, F32), (NSA_HEADS, F32), (2, BF16),
            (NSA_KV, BF16), (NSA_KV, BF16), (NSA_KV, BF16), (NSA_KV, BF16), (1, F32),
            (2 * DIFF_HEADS, F32), (DIFF_HEADS, BF16), (DIFF_HEADS, BF16),
            (1, F32)]
    row = lambda w: pl.BlockSpec((tm, w), lambda i: (i, 0))
    assert sum(w.shape[1] for w in ws_bf16) == PROJ_W
    return pl.pallas_call(
        functools.partial(_proj_prep_kernel, tm=tm),
        grid=(s // tm,),
        in_specs=[pl.BlockSpec(memory_space=pltpu.SMEM), row(d), _full((1, d))] + [_full(w.shape) for w in ws_bf16]
                 + [row(LANES)] * 4 + [_full(c.shape) for c in consts],
        out_specs=[row(n * LANES) for n, _ in outs],
        out_shape=[jax.ShapeDtypeStruct((s, n * LANES), dt) for n, dt in outs],
        scratch_shapes=[pltpu.VMEM((8, LANES), F32), pltpu.VMEM((tm, PROJ_W), F32)],
        compiler_params=_cparams(("arbitrary",)),
        name="proj_head_prep",
    )(shifts, x, nw.reshape(1, d), *ws_bf16, c64, s64, c32, s32, *consts)


QK_DIM = 128
COL_SHIFT = 64
COL_SEL = 80
SCORE_BOUND = 60.0
SHIFT_HEADROOM = 60.0
EXP2_FLUSH = 200.0


def _flash_kernel(code_ref, q_ref, k_ref, v_ref, *rest, groups, tq, tk, window, has_sel, bounded, decay_tiles):
    rest = list(rest)
    sel_ref = rest.pop(0) if has_sel else None
    decay_ref = rest.pop(0) if decay_tiles else None
    o_ref = rest.pop(0)
    qT_sc = rest.pop(0)
    m_sc = None if bounded else rest.pop(0)
    acc_sc = rest.pop(0)
    code = code_ref[pl.program_id(1)]
    i = code & 0xFFF
    j = (code >> 12) & 0xFFF
    first = (code >> 24) & 1
    last = (code >> 25) & 1
    rows = groups * tq

    @pl.when(first == 1)
    def _():
        for g in range(groups):
            qT_sc[:, g * tq:(g + 1) * tq] = q_ref[:, g * QK_DIM:(g + 1) * QK_DIM].T.astype(BF16)
        if not bounded:
            m_sc[...] = jnp.full_like(m_sc, NEG_INIT)
        acc_sc[...] = jnp.zeros_like(acc_sc)

    def step(masked):
        if has_sel:
            qT_sc[COL_SEL:COL_SEL + tk // SLC_LEN, :] = _tile_lanes_any(sel_ref[0], groups)
        s = jnp.dot(k_ref[...], qT_sc[...], preferred_element_type=F32)
        if masked:
            kpos = j * tk + lax.broadcasted_iota(jnp.int32, (tk, rows), 0)
            lane = lax.broadcasted_iota(jnp.int32, (tk, rows), 1)
            qpos = i * tq + (lane & (tq - 1))
            keep = kpos <= qpos
            if window is not None:
                keep = keep & (kpos > qpos - window)
            s = jnp.where(keep, s, MASKVAL)
        v_t = (((0,), (0,)), ((), ()))
        v = v_ref[...]
        if bounded:
            acc_sc[...] += lax.dot_general(v, jnp.exp2(s).astype(BF16), v_t, preferred_element_type=F32)
        else:
            m_prev = m_sc[...]
            m_new = jnp.maximum(m_prev, jnp.max(s, axis=0, keepdims=True))
            p = jnp.exp2(s - m_new).astype(BF16)
            acc_sc[...] = (acc_sc[...] * jnp.exp2(m_prev - m_new)
                           + lax.dot_general(v, p, v_t, preferred_element_type=F32))
            m_sc[...] = m_new

    if window is not None:
        step(True)
    else:
        needs_mask = (j + 1) * tk - 1 > i * tq

        @pl.when(needs_mask)
        def _():
            step(True)

        @pl.when(jnp.logical_not(needs_mask))
        def _():
            if decay_tiles:
                nq, nk = decay_tiles
                bias_max = decay_ref[(pl.program_id(0) * nq + i) * nk + j]

                @pl.when(SHIFT_HEADROOM + bias_max > -EXP2_FLUSH)
                def _():
                    step(False)
            else:
                step(False)

    @pl.when(last == 1)
    def _():
        l = acc_sc[COL_SHIFT:COL_SHIFT + 1, :]
        o = acc_sc[...] / jnp.where(l > 0.0, l, 1.0)
        for g in range(groups):
            o_ref[:, g * QK_DIM:(g + 1) * QK_DIM] = o[:, g * tq:(g + 1) * tq].T


def _tile_lanes_any(t, reps):
    return t if reps == 1 else jnp.concatenate([t] * reps, axis=1)


def _flash_call(q, k, v, sel, decay, *, groups, tq, tk, window, bounded):
    s_len = q.shape[0]
    hkv = k.shape[1] // QK_DIM
    nq = s_len // tq
    rows = groups * tq
    codes = []
    for i in range(nq):
        q_lo, q_hi = i * tq, i * tq + tq - 1
        j_hi = q_hi // tk
        j_lo = 0 if window is None else max(0, (q_lo - window + 1) // tk)
        for j in range(j_lo, j_hi + 1):
            codes.append(i | (j << 12) | (int(j == j_lo) << 24) | (int(j == j_hi) << 25))
    codes = jnp.asarray(np.asarray(codes, dtype=np.int32))
    nsteps = codes.shape[0]

    def ti(c, s):
        return c[s] & 0xFFF

    def tj(c, s):
        return (c[s] >> 12) & 0xFFF

    in_specs = [
        pl.BlockSpec((tq, groups * QK_DIM), lambda h, s, c: (ti(c, s), h)),
        pl.BlockSpec((tk, QK_DIM), lambda h, s, c: (tj(c, s), h)),
        pl.BlockSpec((tk, QK_DIM), lambda h, s, c: (tj(c, s), h)),
    ]
    args = [q, k, v]
    if sel is not None:
        in_specs.append(pl.BlockSpec((1, tk // SLC_LEN, tq), lambda h, s, c: (h, tj(c, s), ti(c, s))))
        args.append(sel)
    use_decay = decay is not None and bounded
    if use_decay:
        in_specs.append(pl.BlockSpec(memory_space=pltpu.SMEM))
        args.append(decay)
    scratch = [pltpu.VMEM((QK_DIM, rows), BF16)]
    if not bounded:
        scratch.append(pltpu.VMEM((1, rows), F32))
    scratch.append(pltpu.VMEM((QK_DIM, rows), F32))
    kern = functools.partial(_flash_kernel, groups=groups, tq=tq, tk=tk, window=window, has_sel=sel is not None,
                             bounded=bounded, decay_tiles=(nq, s_len // tk) if use_decay else None)
    return pl.pallas_call(
        kern,
        grid_spec=pltpu.PrefetchScalarGridSpec(
            num_scalar_prefetch=1,
            grid=(hkv, nsteps),
            in_specs=in_specs,
            out_specs=pl.BlockSpec((tq, groups * QK_DIM), lambda h, s, c: (ti(c, s), h)),
            scratch_shapes=scratch,
        ),
        out_shape=jax.ShapeDtypeStruct(q.shape, F32),
        compiler_params=_cparams(("parallel", "arbitrary")),
        name="flash_bounded" if bounded else "flash_online",
    )(codes, *args)


def _flash(q, k, v, sel=None, decay=None, *, bound_ok, groups, tq, tk, window=None):
    call = functools.partial(_flash_call, q, k, v, sel, decay, groups=groups, tq=tq, tk=tk, window=window)
    return lax.cond(bound_ok, lambda: call(bounded=True), lambda: call(bounded=False))


def _score_bound(wq, wk, seg, scale):
    return 1.02 * seg * jnp.max(jnp.abs(wq)) * jnp.max(jnp.abs(wk)) * scale * LOG2E


def _compress_kernel(r_ref, wa_ref, wb_ref, pa_ref, pb_ref, nw_ref, b64_ref, kx_o, vx_o):
    n = r_ref.shape[0]
    r = r_ref[...]
    wa, wb = wa_ref[...], wb_ref[...]
    first = jnp.dot(r, wa.astype(BF16), preferred_element_type=F32)
    second = jnp.dot(r, wb.astype(BF16), preferred_element_type=F32)
    nxt = pltpu.roll(second, n - 1, 0)
    rowi = lax.broadcasted_iota(jnp.int32, nxt.shape, 0)
    nxt = jnp.where(rowi == n - 1, 0.0, nxt)
    const = (jnp.dot(pa_ref[...], wa, precision=HIGHEST, preferred_element_type=F32)
             + jnp.dot(pb_ref[...], wb, precision=HIGHEST, preferred_element_type=F32))
    out = first + nxt + const
    k, v = out[:, 0:LANES], out[:, LANES:2 * LANES]
    ss = jnp.dot(k * k, b64_ref[...], precision=HIGHEST, preferred_element_type=F32)
    kn = k * lax.rsqrt(ss * (1.0 / NSA_HD) + EPS) * nw_ref[...]
    one_ex = _lane_fill(n, {COL_SHIFT: 1.0})
    kx_o[...] = _slots(kn, [one_ex] * NSA_KV).astype(BF16)
    vx_o[...] = _slots(v, [one_ex] * NSA_KV).astype(BF16)


def _compress(kvc, cmp_w, cmp_pos, nw):
    s = kvc.shape[0]
    n = s // CMP_STRIDE
    width = CMP_STRIDE * 2 * NSA_KV * NSA_HD
    w = cmp_w.reshape(2, 2, CMP_STRIDE, NSA_HD, NSA_HD)
    eye = jnp.eye(2, dtype=w.dtype)
    big = jnp.einsum("khldD,kK,gG->hlkgdKGD", w, eye, eye).reshape(2, width, 2 * NSA_KV * NSA_HD)
    pos = cmp_pos.reshape(2, 2, CMP_STRIDE, 1, NSA_HD)
    pos = jnp.broadcast_to(pos, (2, 2, CMP_STRIDE, NSA_KV, NSA_HD)).transpose(1, 2, 0, 3, 4).reshape(2, 1, width)
    b64 = _block_diag_ones(LANES, NSA_HD)
    nwt = jnp.tile(nw.reshape(1, -1), (1, LANES // NSA_HD))
    out = jax.ShapeDtypeStruct((n, NSA_KV * LANES), BF16)
    return pl.pallas_call(
        _compress_kernel,
        grid=(1,),
        in_specs=[_full((n, width)), _full(big[0].shape), _full(big[1].shape), _full((1, width)), _full((1, width)),
                  _full((1, LANES)), _full((LANES, LANES))],
        out_specs=[_full((n, NSA_KV * LANES))] * 2,
        out_shape=[out, out],
        compiler_params=_cparams(("arbitrary",)),
        name="nsa_compress",
    )(kvc.reshape(n, width), big[0], big[1], pos[0], pos[1], nwt, b64)


def _nsa_cmp_kernel(q_ref, kc_ref, vc_ref, ov_ref, o_ref, sel_ref, *, tq, groups, bounded):
    i = pl.program_id(1)
    ncmp = kc_ref.shape[0]
    rows = groups * tq
    qT = jnp.concatenate([q_ref[:, g * QK_DIM:(g + 1) * QK_DIM].T for g in range(groups)], axis=1).astype(BF16)
    s = jnp.dot(kc_ref[...], qT, preferred_element_type=F32)
    n_idx = lax.broadcasted_iota(jnp.int32, (ncmp, rows), 0)
    lane = lax.broadcasted_iota(jnp.int32, (ncmp, rows), 1)
    qpos = i * tq + (lane & (tq - 1))
    keep = n_idx * CMP_STRIDE + (CMP_LEN - 1) <= qpos
    if bounded:
        e = jnp.where(keep, jnp.exp2(s), 0.0)
    else:
        sm = jnp.where(keep, s, NEG_INIT)
        e = jnp.where(keep, jnp.exp2(sm - jnp.max(sm, axis=0, keepdims=True)), 0.0)
    den = jnp.sum(e, axis=0, keepdims=True)
    p = e / jnp.where(den > 0.0, den, 1.0)
    o = lax.dot_general(vc_ref[...], p.astype(BF16), (((0,), (0,)), ((), ())), preferred_element_type=F32)
    for g in range(groups):
        o_ref[:, g * QK_DIM:(g + 1) * QK_DIM] = o[:, g * tq:(g + 1) * tq].T

    psum = p[:, 0:tq]
    for g in range(1, groups):
        psum = psum + p[:, g * tq:(g + 1) * tq]
    hi = psum.astype(BF16)
    lo = (psum - hi.astype(F32)).astype(BF16)
    ov = ov_ref[...]
    imp = (jnp.dot(ov, hi, preferred_element_type=F32) + jnp.dot(ov, lo, preferred_element_type=F32))
    ns = imp.shape[0]
    blk = lax.broadcasted_iota(jnp.int32, (ns, tq), 0)
    qp = i * tq + lax.broadcasted_iota(jnp.int32, (ns, tq), 1)
    cur = qp // SLC_LEN
    valid = blk <= cur
    forced = valid & ((blk == 0) | (blk == cur) | (blk == cur - 1))
    n_forced = jnp.sum(forced.astype(F32), axis=0, keepdims=True)
    target = jnp.minimum(jnp.sum(valid.astype(F32), axis=0, keepdims=True), float(SLC_TOPK))
    work0 = jnp.where(valid & jnp.logical_not(forced), imp, NEG_INIT)

    def select(exact_ties):
        work, picked = work0, forced
        for it in range(SLC_TOPK - 1):
            mx = jnp.max(work, axis=0, keepdims=True)
            live = (mx > 0.5 * NEG_INIT) & (n_forced + float(it) < float(SLC_TOPK))
            if exact_ties:
                pick = blk == jnp.min(jnp.where(work == mx, blk, ns), axis=0, keepdims=True)
            else:
                pick = work == mx
            pick = pick & live
            picked = picked | pick
            work = jnp.where(pick, NEG_INIT, work)
        sel_ref[0] = jnp.where(picked, 0.0, MASKVAL).astype(BF16)
        return jnp.abs(jnp.sum(picked.astype(F32), axis=0, keepdims=True) - target)

    excess = select(False)

    @pl.when(jnp.max(excess) > 0.0)
    def _():
        select(True)


def _nsa_cmp(qn, kcx, vcx, overlapT, tq, bound_ok):
    s = qn.shape[0]
    ncmp = kcx.shape[0]
    ns = overlapT.shape[0]
    groups = NSA_G

    def call(bounded):
        return pl.pallas_call(
            functools.partial(_nsa_cmp_kernel, tq=tq, groups=groups, bounded=bounded),
            grid=(NSA_KV, s // tq),
            in_specs=[pl.BlockSpec((tq, groups * QK_DIM), lambda h, i: (i, h)),
                      pl.BlockSpec((ncmp, QK_DIM), lambda h, i: (0, h)),
                      pl.BlockSpec((ncmp, QK_DIM), lambda h, i: (0, h)),
                      _full(overlapT.shape)],
            out_specs=[pl.BlockSpec((tq, groups * QK_DIM), lambda h, i: (i, h)),
                       pl.BlockSpec((1, ns, tq), lambda h, i: (h, 0, i))],
            out_shape=[jax.ShapeDtypeStruct(qn.shape, F32), jax.ShapeDtypeStruct((NSA_KV, ns, s), BF16)],
            compiler_params=_cparams(("parallel", "parallel")),
            name="nsa_cmp_bounded" if bounded else "nsa_cmp_max",
        )(qn, kcx, vcx, overlapT)

    return lax.cond(bound_ok, lambda: call(True), lambda: call(False))


def _mix_out_kernel(lam_ref, h_ref, fox_ref, oc_ref, os_ref, ow_ref, dd_ref, g_ref, sw_ref, wf_ref, wn_ref, wd_ref,
                    o_ref, *, diff_scale):
    lam = lam_ref[0]
    gates = g_ref[...]
    tm = gates.shape[0]
    lane = lax.broadcasted_iota(jnp.int32, (tm, LANES), 1)
    slot = lambda ref, n: ref[:, n * LANES:(n + 1) * LANES]
    nsa = []
    for h in range(NSA_HEADS):
        c0 = FOX_HEADS + 3 * h
        nsa.append(gates[:, c0:c0 + 1] * slot(oc_ref, h) + gates[:, c0 + 1:c0 + 2] * slot(os_ref, h)
                   + gates[:, c0 + 2:c0 + 3] * slot(ow_ref, h))
    diff = []
    for h in range(DIFF_HEADS):
        a = jnp.where(lane < DIFF_V, slot(dd_ref, 2 * h) - lam * slot(dd_ref, 2 * h + 1), 0.0)
        ss = jnp.sum(a * a, axis=-1, keepdims=True)
        diff.append(a * lax.rsqrt(ss * (1.0 / DIFF_V) + EPS) * sw_ref[...] * diff_scale)
    acc = h_ref[...]
    acc = acc + jnp.dot(fox_ref[...].astype(BF16), wf_ref[...], preferred_element_type=F32)
    acc = acc + jnp.dot(jnp.concatenate(nsa, axis=1).astype(BF16), wn_ref[...], preferred_element_type=F32)
    acc = acc + jnp.dot(jnp.concatenate(diff, axis=1).astype(BF16), wd_ref[...], preferred_element_type=F32)
    o_ref[...] = acc


def _slot_rows(w, heads):
    d = w.shape[1]
    w = w.reshape(heads, -1, d)
    return jnp.concatenate([w, jnp.zeros((heads, LANES - w.shape[1], d), w.dtype)], axis=1).reshape(heads * LANES, d)


def _peer_query_kernel(h_ref, nw_ref, wq_ref, keys_ref, xnT_o, scT_o):
    xn = _rms(h_ref[...], nw_ref[...])
    xnT_o[...] = xn.T.astype(BF16)
    q = jnp.dot(xn.astype(BF16), wq_ref[...], preferred_element_type=F32).astype(BF16)
    half = PEER_DQ // 2
    for b in range(2 * PEER_HEADS):
        sc = jnp.dot(q[:, b * half:(b + 1) * half], keys_ref[b], preferred_element_type=F32)
        scT_o[b] = sc.T


def _mix_query_kernel(lam_ref, h_ref, fox_ref, oc_ref, os_ref, ow_ref, dd_ref, g_ref, sw_ref, wf_ref, wn_ref, wd_ref,
                      nw_ref, wq_ref, keys_ref, h1_o, xnT_o, scT_o, *, diff_scale):
    _mix_out_kernel(lam_ref, h_ref, fox_ref, oc_ref, os_ref, ow_ref, dd_ref, g_ref, sw_ref, wf_ref, wn_ref, wd_ref,
                    h1_o, diff_scale=diff_scale)
    _peer_query_kernel(h1_o, nw_ref, wq_ref, keys_ref, xnT_o, scT_o)


def _mix_query(lam, h, fox, oc, os_, ow, dd, gates, subln_w, w_out, diff_scale, nw, wq_bf16, keysT_bf16, tm=256):
    s, d = h.shape
    nf, nn = FOX_HEADS * FOX_HD, NSA_HEADS * NSA_HD
    wf = _slot_rows(w_out[:nf], FOX_HEADS).astype(BF16)
    wn = _slot_rows(w_out[nf:nf + nn], NSA_HEADS).astype(BF16)
    wd = _slot_rows(w_out[nf + nn:], DIFF_HEADS).astype(BF16)
    sw = jnp.concatenate([subln_w.reshape(1, -1), jnp.zeros((1, LANES - DIFF_V), F32)], axis=1)
    nb = 2 * PEER_HEADS
    row = lambda w: pl.BlockSpec((tm, w), lambda i: (i, 0))
    return pl.pallas_call(
        functools.partial(_mix_query_kernel, diff_scale=diff_scale),
        grid=(s // tm,),
        in_specs=[pl.BlockSpec(memory_space=pltpu.SMEM), row(d), row(fox.shape[1]), row(oc.shape[1]),
                  row(os_.shape[1]), row(ow.shape[1]), row(dd.shape[1]), row(LANES), _full((1, LANES)),
                  _full(wf.shape), _full(wn.shape), _full(wd.shape),
                  _full((1, d)), _full(wq_bf16.shape), _full(keysT_bf16.shape)],
        out_specs=[row(d), pl.BlockSpec((d, tm), lambda i: (0, i)),
                   pl.BlockSpec((nb, PEER_NKEYS, tm), lambda i: (0, 0, i))],
        out_shape=[jax.ShapeDtypeStruct((s, d), F32), jax.ShapeDtypeStruct((d, s), BF16),
                   jax.ShapeDtypeStruct((nb, PEER_NKEYS, s), F32)],
        compiler_params=_cparams(("parallel",)),
        name="mix_out_peer_query",
    )(lam, h, fox, oc, os_, ow, dd, gates, sw, wf, wn, wd, nw.reshape(1, d), wq_bf16, keysT_bf16)


def _top16_rows(s, exact_ties):
    n = s.shape[0]
    row = lax.broadcasted_iota(jnp.int32, s.shape, 0)
    rank = jnp.full(s.shape, float(PEER_TOPK), dtype=F32)
    work = s
    vals = []
    for r in range(PEER_TOPK):
        mx = jnp.max(work, axis=0, keepdims=True)
        if exact_ties:
            pick = row == jnp.min(jnp.where(work == mx, row, n), axis=0, keepdims=True)
        else:
            pick = work == mx
        rank = jnp.where(pick, float(r), rank)
        work = jnp.where(pick, NEG_INIT, work)
        vals.append(mx)
    return rank, vals


def _peer_route(s1, s2, exact_ties):
    rank1, v1 = _top16_rows(s1, exact_ties)
    rank2, v2 = _top16_rows(s2, exact_ties)
    k = PEER_TOPK
    tm = s1.shape[1]
    r16 = lax.broadcasted_iota(jnp.int32, (k, tm), 0)
    v1m = jnp.zeros((k, tm), F32)
    v2m = jnp.zeros((k, tm), F32)
    for r in range(k):
        v1m = jnp.where(r16 == r, v1[r], v1m)
        v2m = jnp.where(r16 == r, v2[r], v2m)
    ea = jnp.exp(v1m - v1[0])
    eb = jnp.exp(v2m - v2[0])
    cands, gates = [v1m[0:1] + v2m], [ea[0:1] * eb]
    for r in range(1, 8):
        cands.append(v1m[r:r + 1] + v2m[0:8])
        gates.append(ea[r:r + 1] * eb[0:8])
    cands.append(v1m[8:16] + v2m[0:1])
    gates.append(ea[8:16] * eb[0:1])
    cand = jnp.concatenate(cands, axis=0)
    gate = jnp.concatenate(gates, axis=0)
    ncand = cand.shape[0]
    crow = lax.broadcasted_iota(jnp.int32, cand.shape, 0)
    for r in range(2, 8):
        start = 16 + 8 * (r - 1)
        cand = jnp.where((crow >= start + k // (r + 1)) & (crow < start + 8), NEG_INIT, cand)
    work = cand
    picked = jnp.zeros(cand.shape, dtype=jnp.bool_)
    for _ in range(k):
        mx = jnp.max(work, axis=0, keepdims=True)
        if exact_ties:
            pick = crow == jnp.min(jnp.where(work == mx, crow, ncand), axis=0, keepdims=True)
        else:
            pick = work == mx
        picked = picked | pick
        work = jnp.where(pick, NEG_INIT, work)
    pf = picked.astype(F32)
    z = jnp.sum(pf * gate, axis=0, keepdims=True)
    cnt = [jnp.sum(pf[0:16, :], axis=0, keepdims=True)]
    cnt += [jnp.sum(pf[8 + 8 * r:16 + 8 * r, :], axis=0, keepdims=True) for r in range(1, 8)]
    cnt += [pf[64 + r:65 + r, :] for r in range(8, 16)]
    cmap = jnp.zeros(s1.shape, dtype=F32)
    for r in range(k):
        cmap = jnp.where(rank1 == float(r), cnt[r], cmap)
    in1 = rank1 < float(k)
    in2 = rank2 < float(k)
    a = jnp.where(in1, jnp.exp(s1 - v1[0]), 0.0) / (2.0 * z)
    b = jnp.where(in2, jnp.exp(s2 - v2[0]), 0.0)
    excess = (jnp.abs(jnp.sum(in1.astype(F32), axis=0, keepdims=True) - k)
              + jnp.abs(jnp.sum(in2.astype(F32), axis=0, keepdims=True) - k)
              + jnp.abs(jnp.sum(pf, axis=0, keepdims=True) - k))
    return a, cmap, b, rank2, excess


def _peer_topk_kernel(sc_ref, a_o, c_o, b_o, r_o):
    def run(h, exact_ties):
        a, c, b, rank2, excess = _peer_route(sc_ref[2 * h], sc_ref[2 * h + 1], exact_ties)
        a_o[h] = a
        c_o[h] = c
        b_o[h] = b.astype(BF16)
        r_o[h] = rank2.astype(BF16)
        return excess

    for h in range(a_o.shape[0]):
        excess = run(h, False)

        @pl.when(jnp.max(excess) > 0.0)
        def _(h=h):
            run(h, True)


def _peer_topk(scT, tm=256, heads_per_step=2):
    n2, nk, s = scT.shape
    heads = n2 // 2
    hps = heads_per_step
    ospec = pl.BlockSpec((hps, nk, tm), lambda h, t: (h, 0, t))
    return pl.pallas_call(
        _peer_topk_kernel,
        grid=(heads // hps, s // tm),
        in_specs=[pl.BlockSpec((2 * hps, nk, tm), lambda h, t: (h, 0, t))],
        out_specs=[ospec] * 4,
        out_shape=[jax.ShapeDtypeStruct((heads, nk, s), dt) for dt in (F32, F32, BF16, BF16)],
        compiler_params=_cparams(("parallel", "parallel")),
        name="peer_topk",
    )(scT)


def _gelu_exact_x2(x):
    return x * (1.0 + lax.erf(x * (2.0 ** -0.5)))


def _peer_main_kernel(xT_ref, u_ref, v_ref, a_ref, c_ref, b_ref, r_ref, o_ref, w_sc, *, ec, sub):
    ci = pl.program_id(1)

    @pl.when(ci == 0)
    def _():
        o_ref[...] = jnp.zeros_like(o_ref)

    nk = PEER_NKEYS
    tm = xT_ref.shape[1]

    def row(ref, h, i1, lanes):
        x16 = jnp.broadcast_to(ref[h, pl.ds(i1, 1), :][:, lanes], (16, GATE_LANES)).astype(BF16)
        return jnp.concatenate([x16] * (nk // 16), axis=0)

    acc = None
    for sc in range(ec // sub):
        hid = jnp.dot(u_ref[sc * sub:(sc + 1) * sub, :], xT_ref[...], preferred_element_type=F32)
        act = _gelu_exact_x2(hid).astype(BF16)
        for ii in range(sub // nk):
            slab = sc * (sub // nk) + ii
            i1 = ci * (ec // nk) + slab
            for l0 in range(0, tm, GATE_LANES):
                lanes = slice(l0, l0 + GATE_LANES)
                w = None
                for h in range(PEER_HEADS):
                    keep = r_ref[h, :, lanes] < row(c_ref, h, i1, lanes)
                    term = jnp.where(keep, b_ref[h, :, lanes], 0.0) * row(a_ref, h, i1, lanes)
                    w = term if w is None else w + term
                w_sc[slab * nk:(slab + 1) * nk, lanes] = w * act[ii * nk:(ii + 1) * nk, lanes]
        part = jnp.dot(w_sc[sc * sub:(sc + 1) * sub, :].T, v_ref[sc * sub:(sc + 1) * sub, :],
                       preferred_element_type=F32)
        acc = part if acc is None else acc + part
    o_ref[...] += acc


def _peer_main(xnT, u_bf16, v_bf16, a, c, b, r, tm=GATE_LANES, ec=2048, sub=1024):
    d, s = xnT.shape
    e = u_bf16.shape[0]
    heads, nk, _ = a.shape
    rt = pl.BlockSpec((heads, nk, tm), lambda t, ci: (0, 0, t))
    wt = pl.BlockSpec((ec, d), lambda t, ci: (ci, 0))
    return pl.pallas_call(
        functools.partial(_peer_main_kernel, ec=ec, sub=sub),
        grid=(s // tm, e // ec),
        in_specs=[pl.BlockSpec((d, tm), lambda t, ci: (0, t)), wt, wt, rt, rt, rt, rt],
        out_specs=pl.BlockSpec((tm, d), lambda t, ci: (t, 0)),
        out_shape=jax.ShapeDtypeStruct((s, d), F32),
        scratch_shapes=[pltpu.VMEM((ec, tm), BF16)],
        compiler_params=_cparams(("parallel", "arbitrary")),
        name="peer_main",
    )(xnT, u_bf16, v_bf16, a, c, b, r)


def _ple_kernel(h_ref, peer_ref, p_ref, nw_ref, wg_ref, wp_ref, o_ref):
    h2 = h_ref[...] + peer_ref[...]
    xn = _rms(h2, nw_ref[...]).astype(BF16)
    gate = 1.0 / (1.0 + jnp.exp(-jnp.dot(xn, wg_ref[...], preferred_element_type=F32)))
    emb = jnp.dot(p_ref[...].astype(BF16), wp_ref[...], preferred_element_type=F32)
    o_ref[...] = h2 + gate * emb


def _ple(h, peer, p, nw, wg_bf16, wp_bf16, tm=512):
    s, d = h.shape
    pd = p.shape[1]
    row = lambda w: pl.BlockSpec((tm, w), lambda i: (i, 0))
    return pl.pallas_call(
        _ple_kernel,
        grid=(s // tm,),
        in_specs=[row(d), row(d), row(pd), _full((1, d)), _full((d, d)), _full((pd, d))],
        out_specs=row(d),
        out_shape=jax.ShapeDtypeStruct((s, d), F32),
        compiler_params=_cparams(("parallel",)),
        name="ple",
    )(h, peer, p, nw.reshape(1, d), wg_bf16, wp_bf16)


def _split_w_in(w):
    f0 = 3 * FOX_HEADS * FOX_HD
    n0 = f0 + FOX_HEADS
    g0 = n0 + _SEG["dq"][0] - _SEG["nq"][0]
    d0 = g0 + 3 * NSA_HEADS
    misc = jnp.concatenate([w[:, f0:n0], w[:, g0:d0], jnp.zeros((w.shape[0], LANES - (n0 - f0) - (d0 - g0)), w.dtype)],
                           axis=1)
    return [t.astype(BF16) for t in (w[:, :f0], w[:, n0:g0], w[:, d0:], misc)]


def _overlap_T(s):
    n = np.arange(s // CMP_STRIDE)[None, :] * CMP_STRIDE
    m = np.arange(s // SLC_LEN)[:, None] * SLC_LEN
    return jnp.asarray(((n < m + SLC_LEN) & (n + CMP_LEN > m)).astype(np.float32), dtype=BF16)


def _layer(h, p_i, tabs, layer, attn_norm_w, w_in, fox_f_bias, fox_q_norm_w, fox_k_norm_w, nsa_q_norm_w,
           nsa_k_norm_w, nsa_cmp_pos, nsa_cmp_w, diff_q_norm_w, diff_k_norm_w, diff_lambda, diff_subln_w,
           w_out, ffn_norm_w, peer_w_q, peer_sub_keys, peer_u, peer_v, ple_norm_w, ple_w_gate, ple_w_proj):
    s = h.shape[0]
    fb =jnp.zeros((1, LANES), F32).at[0, :FOX_HEADS].set(fox_f_bias)
    u_f = _score_bound(fox_q_norm_w, fox_k_norm_w, FOX_HD, FOX_HD ** -0.5)
    u_n = _score_bound(nsa_q_norm_w, nsa_k_norm_w, NSA_HD, NSA_HD ** -0.5)
    u_d = _score_bound(diff_q_norm_w, diff_k_norm_w, DIFF_QK, DIFF_QK ** -0.5)
    shifts = SHIFT_HEADROOM - jnp.stack([u_f, u_n, u_d]).astype(F32)
    (fq, fk, fv, nqn, nqr, kvc, ks, vs, kw, vw, gates, dq, dk, dv, csum) = _prep(
        shifts, h, attn_norm_w, _split_w_in(w_in), tabs, fb, fox_q_norm_w, fox_k_norm_w, nsa_q_norm_w, nsa_k_norm_w, diff_q_norm_w, diff_k_norm_w)

    c_first_q = csum[0::Q_LANES, :FOX_HEADS]
    c_last_k = csum[TK_CAUSAL - 1::TK_CAUSAL, :FOX_HEADS]
    decay = (c_first_q.T[:, :, None] - c_last_k.T[:, None, :]).reshape(-1)
    o_fox = _flash(fq, fk, fv, None, decay, bound_ok=u_f <= SCORE_BOUND, groups=1, tq=Q_LANES, tk=TK_CAUSAL)

    tq_n = Q_LANES // NSA_G
    nsa_ok = u_n <= SCORE_BOUND
    kcx, vcx = _compress(kvc, nsa_cmp_w, nsa_cmp_pos, nsa_k_norm_w)
    o_c, sel = _nsa_cmp(nqn, kcx, vcx, _overlap_T(s), tq_n, nsa_ok)
    o_s = _flash(nqr, ks, vs, sel, bound_ok=nsa_ok, groups=NSA_G, tq=tq_n, tk=TK_CAUSAL)
    o_w = _flash(nqr, kw, vw, bound_ok=nsa_ok, groups=NSA_G, tq=tq_n, tk=512, window=WIN)

    o_d = _flash(dq, dk, dv, bound_ok=u_d <= SCORE_BOUND, groups=2, tq=Q_LANES // 2, tk=TK_CAUSAL)
    lv = diff_lambda.astype(F32)
    lam_init = 0.8 - 0.6 * math.exp(-0.3 * layer)
    lam = (jnp.exp(jnp.sum(lv[0] * lv[1])) - jnp.exp(jnp.sum(lv[2] * lv[3])) + lam_init).reshape(1)
    keysT = peer_sub_keys.reshape(2 * PEER_HEADS, PEER_NKEYS, PEER_DQ // 2).transpose(0, 2, 1).astype(BF16)
    h1, xnT, scT = _mix_query(lam, h, o_fox, o_c, o_s, o_w, o_d, gates, diff_subln_w, w_out, 1.0 - lam_init,
                              ffn_norm_w, peer_w_q.astype(BF16), keysT)
    a, c, b, r = _peer_topk(scT)
    peer = _peer_main(xnT, peer_u.astype(BF16), peer_v.astype(BF16), a, c, b, r)

    return _ple(h1, peer, p_i, ple_norm_w, ple_w_gate.astype(BF16), ple_w_proj.astype(BF16))


def kernel(x, p, positions, attn_norm_w, w_in, fox_f_bias, fox_q_norm_w, fox_k_norm_w, nsa_q_norm_w, nsa_k_norm_w,
           nsa_cmp_pos, nsa_cmp_w, diff_q_norm_w, diff_k_norm_w, diff_lambda, diff_subln_w, w_out, ffn_norm_w,
           peer_w_q, peer_sub_keys, peer_u, peer_v, ple_norm_w, ple_w_gate, ple_w_proj):
    b, s, d = x.shape
    assert b == 1 and d == D_MODEL and s % Q_LANES == 0 and s % TK_CAUSAL == 0
    tabs = _rope_tables(positions)
    h = x.reshape(s, d)
    per_layer = (attn_norm_w, w_in, fox_f_bias, fox_q_norm_w, fox_k_norm_w, nsa_q_norm_w, nsa_k_norm_w, nsa_cmp_pos,
                 nsa_cmp_w, diff_q_norm_w, diff_k_norm_w, diff_lambda, diff_subln_w, w_out, ffn_norm_w, peer_w_q,
                 peer_sub_keys, peer_u, peer_v, ple_norm_w, ple_w_gate, ple_w_proj)
    for layer in range(attn_norm_w.shape[0]):
        h = _layer(h, p[layer, 0], tabs, layer, *(w[layer] for w in per_layer))
    return h.reshape(b, s, d)
```
